```python
import math
import jax, jax.numpy as jnp
from jax import lax
import numpy as np

D_MODEL = 1024
BATCH = 8
SEQ = 2048
DEPTH = 2

ATT_HEADS = 8
ATT_HEAD_DIM = 64
ATT_QK_WIDTH = ATT_HEADS * 2 * ATT_HEAD_DIM
ATT_V_WIDTH = ATT_HEADS * 2 * ATT_HEAD_DIM
BLOCK_Q = 128
SSM_WIDTH = 512
SSM_GROUP = 16
SSM_GROUPS = SSM_WIDTH // SSM_GROUP
SSM_STATE = 64
DT_MIN = 0.001
DT_MAX = 0.1
N_BRANCH = 2
IN_WIDTH = 2 * ATT_QK_WIDTH + ATT_V_WIDTH + SSM_WIDTH + N_BRANCH * D_MODEL
N_GROUPS = 4
EXPERTS_PER_GROUP = 8
N_EXPERTS = N_GROUPS * EXPERTS_PER_GROUP
EXPERT_FF = 256
TOP_K_INNER = 2
PLE_DIM = 256
EPS = 1e-6

kernel_name = 'hybrid_s5_diffattn_hiermoe_ple'


def rms_norm(x, g):
    xf = x.astype(jnp.float32)
    y = xf * lax.rsqrt(jnp.mean(xf * xf, axis=-1, keepdims=True) + EPS)
    return (y * g.astype(jnp.float32)).astype(x.dtype)


def lambda_init_for(layer_idx):
    return 0.8 - 0.6 * math.exp(-0.3 * layer_idx)


def diff_attention(q, k, v, lam):
    s = q.shape[1]
    scale = ATT_HEAD_DIM ** -0.5
    outs = []
    for blk in range(s // BLOCK_Q):
        q0 = blk * BLOCK_Q
        kv_len = q0 + BLOCK_Q
        qb = q[:, q0:kv_len]
        kb = k[:, :kv_len]
        vb = v[:, :kv_len]
        scores = jnp.einsum('bqhmd,bkhmd->bhmqk', qb, kb).astype(jnp.float32) * scale
        q_pos = q0 + jnp.arange(BLOCK_Q)[:, None]
        k_pos = jnp.arange(kv_len)[None, :]
        scores = jnp.where(k_pos <= q_pos, scores, -jnp.inf)
        probs = jax.nn.softmax(scores, axis=-1)
        w = probs[:, :, 0] - lam * probs[:, :, 1]
        outs.append(jnp.einsum('bhqk,bkhe->bqhe', w.astype(v.dtype), vb))
    return jnp.concatenate(outs, axis=1)


def _complex_linear_combine(e1, e2):
    a1r, a1i, b1r, b1i = e1
    a2r, a2i, b2r, b2i = e2
    ar = a2r * a1r - a2i * a1i
    ai = a2r * a1i + a2i * a1r
    br = a2r * b1r - a2i * b1i + b2r
    bi = a2r * b1i + a2i * b1r + b2i
    return (ar, ai, br, bi)


def s5_ssm(u, lam_re, lam_im, log_step, b_re, b_im, c_re, c_im, d_skip):
    f32 = jnp.float32
    bsz, s, _ = u.shape
    uf = u.astype(f32).reshape(bsz, s, SSM_GROUPS, SSM_GROUP)
    lr = lam_re.astype(f32)
    li = lam_im.astype(f32)
    dt = jnp.exp(log_step.astype(f32))[:, None]
    mag = jnp.exp(lr * dt)
    ab_re = mag * jnp.cos(li * dt)
    ab_im = mag * jnp.sin(li * dt)
    den = lr * lr + li * li
    nr = ab_re - 1.0
    coef_re = (nr * lr + ab_im * li) / den
    coef_im = (ab_im * lr - nr * li) / den
    br = b_re.astype(f32)
    bi = b_im.astype(f32)
    bb_re = coef_re[..., None] * br - coef_im[..., None] * bi
    bb_im = coef_re[..., None] * bi + coef_im[..., None] * br
    bu_re = jnp.einsum('bsgh,gph->bsgp', uf, bb_re)
    bu_im = jnp.einsum('bsgh,gph->bsgp', uf, bb_im)
    a_re = jnp.broadcast_to(ab_re, bu_re.shape)
    a_im = jnp.broadcast_to(ab_im, bu_im.shape)
    _, _, st_re, st_im = lax.associative_scan(_complex_linear_combine, (a_re, a_im, bu_re, bu_im), axis=1)
    y = (jnp.einsum('bsgp,ghp->bsgh', st_re, c_re.astype(f32))
         - jnp.einsum('bsgp,ghp->bsgh', st_im, c_im.astype(f32))
         + d_skip.astype(f32) * uf)
    return y.reshape(bsz, s, SSM_WIDTH).astype(u.dtype)


def hier_moe(h, w_rg, b_rg, w_re, b_re, w_g, w_u, w_d):
    f32 = jnp.float32
    bsz, s, d = h.shape
    t = h.reshape(-1, d)
    logits_grp = (t @ w_rg + b_rg).astype(f32)
    probs_grp = jax.nn.softmax(logits_grp, axis=-1)
    grp = jnp.argmax(logits_grp, axis=-1)
    p_grp = jnp.take_along_axis(probs_grp, grp[:, None], axis=-1)
    logits_exp = (jnp.einsum('td,nde->tne', t, w_re) + b_re).astype(f32)
    logits_sel = jnp.take_along_axis(logits_exp, grp[:, None, None], axis=1)[:, 0]
    top_vals, top_idx = lax.top_k(logits_sel, TOP_K_INNER)
    top_w = jax.nn.softmax(top_vals, axis=-1) * p_grp
    w_local = jnp.einsum('tk,tke->te', top_w, jax.nn.one_hot(top_idx, EXPERTS_PER_GROUP, dtype=f32))
    out = jnp.zeros_like(t)
    for n in range(N_GROUPS):
        w_n = jnp.where((grp == n)[:, None], w_local, 0.0).astype(t.dtype)
        hg = jnp.einsum('td,edf->tef', t, w_g[n])
        hu = jnp.einsum('td,edf->tef', t, w_u[n])
        act = jax.nn.silu(hg) * hu * w_n[:, :, None]
        out = out + jnp.einsum('tef,efd->td', act, w_d[n])
    return out.reshape(bsz, s, d)


def setup_inputs(seed: int = 0) -> dict:
    key = jax.random.key(seed)
    ks = jax.random.split(key, 40)
    f32 = jnp.float32

    def nrm(k, shape, scale):
        return jax.random.normal(k, shape, f32) * scale

    def gain(k, shape):
        return 1.0 + 0.02 * jax.random.normal(k, shape, f32)

    G, P, H = SSM_GROUPS, SSM_STATE, SSM_GROUP
    lam_im_base = jnp.pi * jnp.arange(P, dtype=f32)
    return {
        'x': nrm(ks[0], (BATCH, SEQ, D_MODEL), 1.0),
        'p': nrm(ks[1], (DEPTH, BATCH, SEQ, PLE_DIM), 1.0),
        'attn_norm_g': gain(ks[2], (DEPTH, D_MODEL)),
        'w_in': nrm(ks[3], (DEPTH, D_MODEL, IN_WIDTH), D_MODEL ** -0.5),
        'b_gate': nrm(ks[4], (DEPTH, N_BRANCH * D_MODEL), 0.01),
        'q_norm_g': gain(ks[5], (DEPTH, 2, ATT_HEAD_DIM)),
        'k_norm_g': gain(ks[6], (DEPTH, 2, ATT_HEAD_DIM)),
        'lam_qk': nrm(ks[7], (DEPTH, 4, ATT_HEAD_DIM), 0.1),
        'subln_g': gain(ks[8], (DEPTH, 2 * ATT_HEAD_DIM)),
        'w_attn_up': nrm(ks[9], (DEPTH, ATT_V_WIDTH, D_MODEL), ATT_V_WIDTH ** -0.5),
        'ssm_lam_re': -0.5 + 0.01 * jax.random.normal(ks[10], (DEPTH, G, P), f32),
        'ssm_lam_im': lam_im_base + 0.01 * jax.random.normal(ks[11], (DEPTH, G, P), f32),
        'ssm_log_step': jax.random.uniform(ks[12], (DEPTH, G), f32, math.log(DT_MIN), math.log(DT_MAX)),
        'ssm_b_re': nrm(ks[13], (DEPTH, G, P, H), (2 * H) ** -0.5),
        'ssm_b_im': nrm(ks[14], (DEPTH, G, P, H), (2 * H) ** -0.5),
        'ssm_c_re': nrm(ks[15], (DEPTH, G, H, P), (2 * P) ** -0.5),
        'ssm_c_im': nrm(ks[16], (DEPTH, G, H, P), (2 * P) ** -0.5),
        'ssm_d': nrm(ks[17], (DEPTH, G, H), 1.0),
        'w_glu': nrm(ks[18], (DEPTH, SSM_WIDTH, SSM_WIDTH), SSM_WIDTH ** -0.5),
        'b_glu': nrm(ks[19], (DEPTH, SSM_WIDTH), 0.01),
        'w_ssm_up': nrm(ks[20], (DEPTH, SSM_WIDTH, D_MODEL), SSM_WIDTH ** -0.5),
        'w_out': nrm(ks[21], (DEPTH, D_MODEL, D_MODEL), D_MODEL ** -0.5),
        'ffn_norm_g': gain(ks[22], (DEPTH, D_MODEL)),
        'w_router_group': nrm(ks[23], (DEPTH, D_MODEL, N_GROUPS), D_MODEL ** -0.5),
        'b_router_group': nrm(ks[24], (DEPTH, N_GROUPS), 0.01),
        'w_router_expert': nrm(ks[25], (DEPTH, N_GROUPS, D_MODEL, EXPERTS_PER_GROUP), D_MODEL ** -0.5),
        'b_router_expert': nrm(ks[26], (DEPTH, N_GROUPS, EXPERTS_PER_GROUP), 0.01),
        'w_exp_gate': nrm(ks[27], (DEPTH, N_GROUPS, EXPERTS_PER_GROUP, D_MODEL, EXPERT_FF), D_MODEL ** -0.5),
        'w_exp_up': nrm(ks[28], (DEPTH, N_GROUPS, EXPERTS_PER_GROUP, D_MODEL, EXPERT_FF), D_MODEL ** -0.5),
        'w_exp_down': nrm(ks[29], (DEPTH, N_GROUPS, EXPERTS_PER_GROUP, EXPERT_FF, D_MODEL), EXPERT_FF ** -0.5),
        'ple_norm_g': gain(ks[30], (DEPTH, D_MODEL)),
        'w_ple_gate': nrm(ks[31], (DEPTH, D_MODEL, D_MODEL), D_MODEL ** -0.5),
        'w_ple': nrm(ks[32], (DEPTH, PLE_DIM, D_MODEL), PLE_DIM ** -0.5),
    }


def reference(x, p, attn_norm_g, w_in, b_gate, q_norm_g, k_norm_g, lam_qk, subln_g, w_attn_up,
              ssm_lam_re, ssm_lam_im, ssm_log_step, ssm_b_re, ssm_b_im, ssm_c_re, ssm_c_im, ssm_d,
              w_glu, b_glu, w_ssm_up, w_out, ffn_norm_g, w_router_group, b_router_group,
              w_router_expert, b_router_expert, w_exp_gate, w_exp_up, w_exp_down,
              ple_norm_g, w_ple_gate, w_ple):
    bsz, s, _ = x.shape
    split_points = [ATT_QK_WIDTH, 2 * ATT_QK_WIDTH, 2 * ATT_QK_WIDTH + ATT_V_WIDTH,
                    2 * ATT_QK_WIDTH + ATT_V_WIDTH + SSM_WIDTH]
    for i in range(DEPTH):
        lam_init = lambda_init_for(i)
        h = rms_norm(x, attn_norm_g[i])
        proj = h @ w_in[i]
        q, k, v, u, g = jnp.split(proj, split_points, axis=-1)
        q = rms_norm(q.reshape(bsz, s, ATT_HEADS, 2, ATT_HEAD_DIM), q_norm_g[i])
        k = rms_norm(k.reshape(bsz, s, ATT_HEADS, 2, ATT_HEAD_DIM), k_norm_g[i])
        v = v.reshape(bsz, s, ATT_HEADS, 2 * ATT_HEAD_DIM)
        lq = lam_qk[i].astype(jnp.float32)
        lam = jnp.exp(jnp.sum(lq[0] * lq[1])) - jnp.exp(jnp.sum(lq[2] * lq[3])) + lam_init
        o = diff_attention(q, k, v, lam)
        o = rms_norm(o, subln_g[i]) * (1.0 - lam_init)
        y_att = o.reshape(bsz, s, ATT_V_WIDTH) @ w_attn_up[i]

        y_s = s5_ssm(u, ssm_lam_re[i], ssm_lam_im[i], ssm_log_step[i], ssm_b_re[i], ssm_b_im[i],
                     ssm_c_re[i], ssm_c_im[i], ssm_d[i])
        z = jax.nn.gelu(y_s)
        z = z * jax.nn.sigmoid(z @ w_glu[i] + b_glu[i])
        y_ssm = z @ w_ssm_up[i]

        gates = jax.nn.sigmoid(g + b_gate[i]).reshape(bsz, s, N_BRANCH, D_MODEL)
        mixed = gates[:, :, 0] * y_att + gates[:, :, 1] * y_ssm
        x = x + mixed @ w_out[i]
        x = x + hier_moe(rms_norm(x, ffn_norm_g[i]), w_router_group[i], b_router_group[i],
                         w_router_expert[i], b_router_expert[i], w_exp_gate[i], w_exp_up[i],
                         w_exp_down[i])
        ple_gate = jax.nn.sigmoid(rms_norm(x, ple_norm_g[i]) @ w_ple_gate[i])
        x = x + ple_gate * (p[i] @ w_ple[i])
    return x
```

```python
import functools
import math

import jax
import jax.numpy as jnp
from jax import lax
from jax.experimental import pallas as pl
from jax.experimental.pallas import tpu as pltpu

F32 = jnp.float32
BF16 = jnp.bfloat16

D_MODEL = 1024
ATT_HEADS = 8
ATT_HEAD_DIM = 64
HEAD_W = 2 * ATT_HEAD_DIM
ATT_QK_WIDTH = ATT_HEADS * HEAD_W
ATT_V_WIDTH = ATT_HEADS * HEAD_W
SSM_WIDTH = 512
SSM_GROUP = 16
SSM_GROUPS = SSM_WIDTH // SSM_GROUP
SSM_STATE = 64
N_STATE = SSM_GROUPS * SSM_STATE
N_BRANCH = 2
IN_WIDTH = 2 * ATT_QK_WIDTH + ATT_V_WIDTH + SSM_WIDTH + N_BRANCH * D_MODEL
N_GROUPS = 4
EXPERTS_PER_GROUP = 8
N_EXPERTS = N_GROUPS * EXPERTS_PER_GROUP
EXPERT_FF = 256
PLE_DIM = 256
EPS = 1e-6

LANES = 128
SUBLANES = 8
MXU_N = 256
VMEM_LIMIT = 56 * 1024 * 1024

TOK_TILE = 512
MOE_TILE = 1024
COL_CHUNK = 512
ATT_TILE = 256
SCAN_T = 64
SCAN_LANES = 512
ROUTE_W = LANES
ROUTE_OFF = N_GROUPS


def _lambda_init(layer_idx):
    return 0.8 - 0.6 * math.exp(-0.3 * layer_idx)


def _dot(a, b):
    return jnp.dot(a, b, preferred_element_type=F32)


def _const_spec(shape):
    nd = len(shape)
    return pl.BlockSpec(shape, lambda *_: (0,) * nd, pipeline_mode=pl.Buffered(1))


def _params(*sem):
    return pltpu.CompilerParams(dimension_semantics=sem, vmem_limit_bytes=VMEM_LIMIT)


def _prep_kernel(lre_ref, lim_ref, lstep_ref, bre_ref, bim_ref, lqk_ref,
                 are_ref, aim_ref, bbre_ref, bbim_ref, lam_ref):
    lr = lre_ref[...]
    li = lim_ref[...]
    dt = jnp.exp(lstep_ref[...])
    mag = jnp.exp(lr * dt)
    ab_re = mag * jnp.cos(li * dt)
    ab_im = mag * jnp.sin(li * dt)
    den = lr * lr + li * li
    nr = ab_re - 1.0
    coef_re = (nr * lr + ab_im * li) / den
    coef_im = (ab_im * lr - nr * li) / den
    br = bre_ref[...]
    bi = bim_ref[...]
    are_ref[...] = ab_re
    aim_ref[...] = ab_im
    bbre_ref[...] = coef_re * br - coef_im * bi
    bbim_ref[...] = coef_re * bi + coef_im * br
    lq = lqk_ref[...]
    s01 = jnp.sum(lq[0:1] * lq[1:2], axis=-1, keepdims=True)
    s23 = jnp.sum(lq[2:3] * lq[3:4], axis=-1, keepdims=True)
    lam_ref[...] = jnp.broadcast_to(jnp.exp(s01) - jnp.exp(s23), lam_ref.shape)


def _prep(lam_re, lam_im, log_step, b_re, b_im, lam_qk):
    depth = lam_re.shape[0]
    rows = SSM_WIDTH
    rep = lambda a: jnp.repeat(a, SSM_GROUP, axis=1)
    lstep = jnp.broadcast_to(log_step[:, :, None], lam_re.shape)
    tr = lambda b: jnp.transpose(b, (0, 1, 3, 2)).reshape(depth, rows, SSM_STATE)
    blk = pl.BlockSpec((None, rows, SSM_STATE), lambda i: (i, 0, 0))
    out = jax.ShapeDtypeStruct((depth, rows, SSM_STATE), F32)
    return pl.pallas_call(
        _prep_kernel,
        grid=(depth,),
        in_specs=[blk, blk, blk, blk, blk, pl.BlockSpec((None, 4, ATT_HEAD_DIM), lambda i: (i, 0, 0))],
        out_specs=[blk, blk, blk, blk, pl.BlockSpec((None, SUBLANES, LANES), lambda i: (i, 0, 0))],
        out_shape=[out, out, out, out, jax.ShapeDtypeStruct((depth, SUBLANES, LANES), F32)],
        compiler_params=_params("arbitrary"),
        name="prep",
    )(rep(lam_re), rep(lam_im), rep(lstep), tr(b_re), tr(b_im), lam_qk)


def _block_diag(a):
    g, r, c = a.shape
    eye = jnp.eye(g, dtype=a.dtype)
    return jnp.einsum('grc,gk->grkc', a, eye).reshape(g * r, g * c)


def _ssm_matrices(bb_re, bb_im, c_re, c_im):
    g = SSM_GROUPS
    n_re_tiles = N_STATE // MXU_N
    tiles = []
    for full in (_block_diag(bb_re.reshape(g, SSM_GROUP, SSM_STATE)),
                 _block_diag(bb_im.reshape(g, SSM_GROUP, SSM_STATE))):
        for j in range(n_re_tiles):
            k0 = LANES * ((j * MXU_N // SSM_STATE * SSM_GROUP) // LANES)
            tiles.append(full[k0:k0 + LANES, j * MXU_N:(j + 1) * MXU_N])
    b_tiles = jnp.stack(tiles).astype(BF16)
    c_out = []
    for c in (c_re, c_im):
        full = _block_diag(jnp.transpose(c, (0, 2, 1)))
        half = N_STATE // 2
        c_out.append(jnp.stack([full[n * half:(n + 1) * half, n * MXU_N:(n + 1) * MXU_N]
                                for n in range(2)]).astype(BF16))
    return b_tiles, c_out[0], c_out[1]


def _inproj_kernel(x_ref, g_ref, w_ref, gsum_ref, qg_ref, kg_ref, bg_ref,
                   q_ref, k_ref, v_ref, u_ref, gate_ref, *, q_scale):
    xf = x_ref[...]
    ms = jnp.mean(xf * xf, axis=-1, keepdims=True)
    h = (xf * lax.rsqrt(ms + EPS) * g_ref[...]).astype(BF16)

    def head_norm(y, gain):
        sq = y * y
        hi = sq.astype(BF16)
        lo = (sq - hi.astype(F32)).astype(BF16)
        gs = _dot(hi, gsum_ref[...]) + _dot(lo, gsum_ref[...])
        return y * lax.rsqrt(gs * (1.0 / ATT_HEAD_DIM) + EPS) * gain

    n_qk = ATT_QK_WIDTH // COL_CHUNK
    n_v = ATT_V_WIDTH // COL_CHUNK
    for c in range(IN_WIDTH // COL_CHUNK):
        y = _dot(h, w_ref[:, c * COL_CHUNK:(c + 1) * COL_CHUNK])
        if c < n_qk:
            q_ref[:, c * COL_CHUNK:(c + 1) * COL_CHUNK] = (head_norm(y, qg_ref[...]) * q_scale).astype(BF16)
        elif c < 2 * n_qk:
            cc = c - n_qk
            k_ref[:, cc * COL_CHUNK:(cc + 1) * COL_CHUNK] = head_norm(y, kg_ref[...]).astype(BF16)
        elif c < 2 * n_qk + n_v:
            cc = c - 2 * n_qk
            v_ref[:, cc * COL_CHUNK:(cc + 1) * COL_CHUNK] = y.astype(BF16)
        elif c == 2 * n_qk + n_v:
            u_ref[...] = y
        else:
            cc = c - (2 * n_qk + n_v + 1)
            b = bg_ref[:, cc * COL_CHUNK:(cc + 1) * COL_CHUNK]
            gate_ref[:, cc * COL_CHUNK:(cc + 1) * COL_CHUNK] = jax.nn.sigmoid(y + b).astype(BF16)


def _inproj(x2d, norm_g, w_in, gsum, qg, kg, b_gate, *, seq):
    t = x2d.shape[0]
    n_sb = seq // TOK_TILE
    bsz = t // seq
    row = lambda w: pl.BlockSpec((TOK_TILE, w), lambda i: (i, 0))
    q_scale = ATT_HEAD_DIM ** -0.5 * math.log2(math.e)
    return pl.pallas_call(
        functools.partial(_inproj_kernel, q_scale=q_scale),
        grid=(t // TOK_TILE,),
        in_specs=[row(D_MODEL), _const_spec((1, D_MODEL)), _const_spec((D_MODEL, IN_WIDTH)),
                  _const_spec((COL_CHUNK, COL_CHUNK)), _const_spec((1, COL_CHUNK)),
                  _const_spec((1, COL_CHUNK)), _const_spec((1, N_BRANCH * D_MODEL))],
        out_specs=[row(ATT_QK_WIDTH), row(ATT_QK_WIDTH), row(ATT_V_WIDTH),
                   pl.BlockSpec((TOK_TILE, SSM_WIDTH), lambda i: (i % n_sb, i // n_sb)),
                   row(N_BRANCH * D_MODEL)],
        out_shape=[jax.ShapeDtypeStruct((t, ATT_QK_WIDTH), BF16),
                   jax.ShapeDtypeStruct((t, ATT_QK_WIDTH), BF16),
                   jax.ShapeDtypeStruct((t, ATT_V_WIDTH), BF16),
                   jax.ShapeDtypeStruct((seq, bsz * SSM_WIDTH), F32),
                   jax.ShapeDtypeStruct((t, N_BRANCH * D_MODEL), BF16)],
        compiler_params=_params("parallel"),
        name="inproj",
    )(x2d, norm_g, w_in, gsum, qg, kg, b_gate)


def _attn_kernel(lam_ref, q_ref, k_ref, v_ref, g_ref, o_ref, vt_scr, m_scr, l_scr, acc_scr, *, lam_init):
    tq = ATT_TILE
    seq = k_ref.shape[0]
    nq = seq // tq
    lam = lam_ref[0, 0] + lam_init
    for c in range(nq):
        vt_scr[c] = v_ref[c * tq:(c + 1) * tq, :].astype(F32).T.astype(BF16)
    head_row = lax.broadcasted_iota(jnp.int32, (HEAD_W, tq), 0)
    key_pos = lax.broadcasted_iota(jnp.int32, (tq, 2 * tq), 0)
    qry_pos = lax.broadcasted_iota(jnp.int32, (tq, 2 * tq), 1) & (tq - 1)
    causal = key_pos <= qry_pos

    for qi in range(nq):
        qt = q_ref[qi * tq:(qi + 1) * tq, :].astype(F32).T
        qst = jnp.concatenate([jnp.where(head_row < ATT_HEAD_DIM, qt, 0.0),
                               jnp.where(head_row >= ATT_HEAD_DIM, qt, 0.0)], axis=1).astype(BF16)
        m_scr[...] = jnp.full(m_scr.shape, -jnp.inf, F32)
        l_scr[...] = jnp.zeros(l_scr.shape, F32)
        acc_scr[...] = jnp.zeros(acc_scr.shape, F32)

        def chunk(c, masked, qst=qst):
            k0 = pl.multiple_of(c * tq, tq)
            st = _dot(k_ref[pl.ds(k0, tq), :], qst)
            if masked:
                st = jnp.where(causal, st, -jnp.inf)
            m_old = m_scr[...]
            m_new = jnp.maximum(m_old, jnp.max(st, axis=0, keepdims=True))
            p = jnp.exp2(st - m_new)
            alpha = jnp.exp2(m_old - m_new)
            l_scr[...] = alpha * l_scr[...] + jnp.sum(p, axis=0, keepdims=True)
            acc_scr[...] = alpha * acc_scr[...] + _dot(vt_scr[c], p.astype(BF16))
            m_scr[...] = m_new

        if qi > 0:
            def body(c, carry):
                chunk(c, False)
                return carry
            lax.fori_loop(0, qi, body, 0)
        chunk(qi, True)

        accn = acc_scr[...] * (1.0 / l_scr[...])
        ot = accn[:, :tq] - lam * accn[:, tq:]
        ms = jnp.mean(ot * ot, axis=0, keepdims=True)
        o = (ot * lax.rsqrt(ms + EPS)).T * g_ref[...] * (1.0 - lam_init)
        o_ref[qi * tq:(qi + 1) * tq, :] = o.astype(BF16)


def _attention(lam, q, k, v, subln_g, *, seq, lam_init):
    t = q.shape[0]
    bsz = t // seq
    blk = pl.BlockSpec((seq, HEAD_W), lambda b, h: (b, h))
    return pl.pallas_call(
        functools.partial(_attn_kernel, lam_init=lam_init),
        grid=(bsz, ATT_HEADS),
        in_specs=[pl.BlockSpec(memory_space=pltpu.SMEM), blk, blk, blk, _const_spec((1, HEAD_W))],
        out_specs=blk,
        out_shape=jax.ShapeDtypeStruct((t, ATT_V_WIDTH), BF16),
        scratch_shapes=[pltpu.VMEM((seq // ATT_TILE, HEAD_W, ATT_TILE), BF16),
                        pltpu.VMEM((1, 2 * ATT_TILE), F32),
                        pltpu.VMEM((1, 2 * ATT_TILE), F32),
                        pltpu.VMEM((HEAD_W, 2 * ATT_TILE), F32)],
        compiler_params=_params("parallel", "parallel"),
        name="attn",
    )(lam, q, k, v, subln_g)


def _ssm_kernel(u_ref, bt_ref, cre_ref, cim_ref, are_ref, aim_ref, d_ref, z_ref, st_scr, state_scr):
    rows = u_ref.shape[0]
    bsz = state_scr.shape[0]

    @pl.when(pl.program_id(0) == 0)
    def _():
        state_scr[...] = jnp.zeros(state_scr.shape, F32)

    u = u_ref[...]
    ub = u.astype(BF16)
    n_tiles = 2 * N_STATE // MXU_N
    for j in range(n_tiles):
        k0 = LANES * (((j % (n_tiles // 2)) * MXU_N // SSM_STATE * SSM_GROUP) // LANES)
        st_scr[:, j * MXU_N:(j + 1) * MXU_N] = _dot(ub[:, k0:k0 + LANES], bt_ref[j])

    for cc in range(N_STATE // SCAN_LANES):
        lo = cc * SCAN_LANES
        hi = N_STATE + lo
        a_r = jnp.broadcast_to(are_ref[:, lo:lo + SCAN_LANES], (bsz, SCAN_LANES))
        a_i = jnp.broadcast_to(aim_ref[:, lo:lo + SCAN_LANES], (bsz, SCAN_LANES))

        def step(t, carry, lo=lo, hi=hi, a_r=a_r, a_i=a_i):
            s_r, s_i = carry
            r0 = pl.multiple_of(t * bsz, bsz)
            n_r = a_r * s_r - a_i * s_i + st_scr[pl.ds(r0, bsz), lo:lo + SCAN_LANES]
            n_i = a_r * s_i + a_i * s_r + st_scr[pl.ds(r0, bsz), hi:hi + SCAN_LANES]
            st_scr[pl.ds(r0, bsz), lo:lo + SCAN_LANES] = n_r
            st_scr[pl.ds(r0, bsz), hi:hi + SCAN_LANES] = n_i
            return n_r, n_i

        s_r, s_i = lax.fori_loop(0, rows // bsz, step,
                                 (state_scr[:, lo:lo + SCAN_LANES], state_scr[:, hi:hi + SCAN_LANES]),
                                 unroll=8)
        state_scr[:, lo:lo + SCAN_LANES] = s_r
        state_scr[:, hi:hi + SCAN_LANES] = s_i

    half = N_STATE // 2
    for n in range(2):
        s_re = st_scr[:, n * half:(n + 1) * half].astype(BF16)
        s_im = st_scr[:, N_STATE + n * half:N_STATE + (n + 1) * half].astype(BF16)
        cols = slice(n * MXU_N, (n + 1) * MXU_N)
        y = _dot(s_re, cre_ref[n]) - _dot(s_im, cim_ref[n]) + d_ref[:, cols] * u[:, cols]
        z_ref[:, cols] = jax.nn.gelu(y).astype(BF16)


def _ssm(u_tm, b_tiles, c_re_tiles, c_im_tiles, a_re, a_im, d_skip, *, bsz):
    rows_total = u_tm.shape[0]
    rows = SCAN_T * bsz
    blk = pl.BlockSpec((rows, SSM_WIDTH), lambda i: (i, 0))
    return pl.pallas_call(
        _ssm_kernel,
        grid=(rows_total // rows,),
        in_specs=[blk, _const_spec(b_tiles.shape), _const_spec(c_re_tiles.shape),
                  _const_spec(c_im_tiles.shape), _const_spec((1, N_STATE)), _const_spec((1, N_STATE)),
                  _const_spec((1, SSM_WIDTH))],
        out_specs=blk,
        out_shape=jax.ShapeDtypeStruct((rows_total, SSM_WIDTH), BF16),
        scratch_shapes=[pltpu.VMEM((rows, 2 * N_STATE), F32), pltpu.VMEM((bsz, 2 * N_STATE), F32)],
        compiler_params=_params("arbitrary"),
        name="ssm",
    )(u_tm, b_tiles, c_re_tiles, c_im_tiles, a_re, a_im, d_skip)


def _route(logits):
    lane = lax.broadcasted_iota(jnp.int32, logits.shape, 1)
    lanef = lane.astype(F32)
    big = float(ROUTE_W)
    neg = -jnp.inf
    gl = jnp.where(lane < N_GROUPS, logits, neg)
    gmax = jnp.max(gl, axis=-1, keepdims=True)
    grp = jnp.min(jnp.where(gl == gmax, lanef, big), axis=-1, keepdims=True)
    p_grp = 1.0 / jnp.sum(jnp.exp(gl - gmax), axis=-1, keepdims=True)
    first = ROUTE_OFF + grp * EXPERTS_PER_GROUP
    in_grp = (lanef >= first) & (lanef < first + EXPERTS_PER_GROUP)
    el = jnp.where(in_grp, logits, neg)
    v1 = jnp.max(el, axis=-1, keepdims=True)
    i1 = jnp.min(jnp.where(el == v1, lanef, big), axis=-1, keepdims=True)
    el2 = jnp.where(lanef == i1, neg, el)
    v2 = jnp.max(el2, axis=-1, keepdims=True)
    i2 = jnp.min(jnp.where(el2 == v2, lanef, big), axis=-1, keepdims=True)
    e21 = jnp.exp(v2 - v1)
    w1 = p_grp / (1.0 + e21)
    w2 = p_grp * (e21 / (1.0 + e21))
    return jnp.where(lanef == i1, w1, 0.0) + jnp.where(lanef == i2, w2, 0.0)


def _mix_kernel(x_ref, o_ref, z_ref, gate_ref, wau_ref, wglu_ref, bglu_ref, wsu_ref, wout_ref,
                fg_ref, wrh_ref, wrl_ref, br_ref, x1_ref, hn_ref, route_ref):
    y_att = _dot(o_ref[...], wau_ref[...])
    z = z_ref[...].astype(F32)
    z = z * jax.nn.sigmoid(_dot(z_ref[...], wglu_ref[...]) + bglu_ref[...])
    y_ssm = _dot(z.astype(BF16), wsu_ref[...])
    mixed = (gate_ref[:, :D_MODEL].astype(F32) * y_att + gate_ref[:, D_MODEL:].astype(F32) * y_ssm)
    x1 = x_ref[...] + _dot(mixed.astype(BF16), wout_ref[...])
    x1_ref[...] = x1
    ms = jnp.mean(x1 * x1, axis=-1, keepdims=True)
    hn = x1 * lax.rsqrt(ms + EPS) * fg_ref[...]
    hn_hi = hn.astype(BF16)
    hn_ref[...] = hn_hi
    hn_lo = (hn - hn_hi.astype(F32)).astype(BF16)
    logits = (_dot(hn_hi, wrh_ref[...]) + _dot(hn_lo, wrh_ref[...]) + _dot(hn_hi, wrl_ref[...])
              + br_ref[...])
    route_ref[...] = _route(logits)


def _mix(x2d, o, z_sm, gates, wau, wglu, bglu, wsu, wout, ffn_g, wr_hi, wr_lo, br, *, seq):
    t = x2d.shape[0]
    n_sb = seq // TOK_TILE
    row = lambda w: pl.BlockSpec((TOK_TILE, w), lambda i: (i, 0))
    return pl.pallas_call(
        _mix_kernel,
        grid=(t // TOK_TILE,),
        in_specs=[row(D_MODEL), row(ATT_V_WIDTH),
                  pl.BlockSpec((TOK_TILE, SSM_WIDTH), lambda i: (i % n_sb, i // n_sb)),
                  row(N_BRANCH * D_MODEL),
                  _const_spec(wau.shape), _const_spec(wglu.shape), _const_spec(bglu.shape),
                  _const_spec(wsu.shape), _const_spec(wout.shape), _const_spec(ffn_g.shape),
                  _const_spec(wr_hi.shape), _const_spec(wr_lo.shape), _const_spec(br.shape)],
        out_specs=[row(D_MODEL), row(D_MODEL), row(ROUTE_W)],
        out_shape=[jax.ShapeDtypeStruct((t, D_MODEL), F32),
                   jax.ShapeDtypeStruct((t, D_MODEL), BF16),
                   jax.ShapeDtypeStruct((t, ROUTE_W), F32)],
        compiler_params=_params("parallel"),
        name="mix",
    )(x2d, o, z_sm, gates, wau, wglu, bglu, wsu, wout, ffn_g, wr_hi, wr_lo, br)


def _moe_kernel(x1_ref, hn_ref, route_ref, p_ref, wg_ref, wu_ref, wd_ref, pg_ref, wpg_ref, wple_ref,
                out_ref, acc_scr):
    e = pl.program_id(1)

    @pl.when(e == 0)
    def _():
        acc_scr[...] = jnp.zeros(acc_scr.shape, F32)

    h = hn_ref[...]
    route = route_ref[...]
    lane = lax.broadcasted_iota(jnp.int32, route.shape, 1)
    w_e = jnp.sum(jnp.where(lane == e + ROUTE_OFF, route, 0.0), axis=-1, keepdims=True)
    act = jax.nn.silu(_dot(h, wg_ref[...])) * _dot(h, wu_ref[...]) * w_e
    acc_scr[...] += _dot(act.astype(BF16), wd_ref[...])

    @pl.when(e == N_EXPERTS - 1)
    def _():
        x2 = x1_ref[...] + acc_scr[...]
        ms = jnp.mean(x2 * x2, axis=-1, keepdims=True)
        hp = (x2 * lax.rsqrt(ms + EPS) * pg_ref[...]).astype(BF16)
        gate = jax.nn.sigmoid(_dot(hp, wpg_ref[...]))
        out_ref[...] = x2 + gate * _dot(p_ref[...].astype(BF16), wple_ref[...])


def _moe(x1, hn, route, p2d, wg, wu, wd, ple_g, wpg, wple):
    t = x1.shape[0]
    row = lambda w: pl.BlockSpec((MOE_TILE, w), lambda i, e: (i, 0))
    return pl.pallas_call(
        _moe_kernel,
        grid=(t // MOE_TILE, N_EXPERTS),
        in_specs=[row(D_MODEL), row(D_MODEL), row(ROUTE_W), row(PLE_DIM),
                  pl.BlockSpec((None, D_MODEL, EXPERT_FF), lambda i, e: (e, 0, 0)),
                  pl.BlockSpec((None, D_MODEL, EXPERT_FF), lambda i, e: (e, 0, 0)),
                  pl.BlockSpec((None, EXPERT_FF, D_MODEL), lambda i, e: (e, 0, 0)),
                  _const_spec(ple_g.shape), _const_spec(wpg.shape), _const_spec(wple.shape)],
        out_specs=row(D_MODEL),
        out_shape=jax.ShapeDtypeStruct((t, D_MODEL), F32),
        scratch_shapes=[pltpu.VMEM((MOE_TILE, D_MODEL), F32)],
        compiler_params=_params("parallel", "arbitrary"),
        name="moe",
    )(x1, hn, route, p2d, wg, wu, wd, ple_g, wpg, wple)


def kernel(x, p, attn_norm_g, w_in, b_gate, q_norm_g, k_norm_g, lam_qk, subln_g, w_attn_up, ssm_lam_re, ssm_lam_im, ssm_log_step, ssm_b_re, ssm_b_im, ssm_c_re, ssm_c_im, ssm_d, w_glu, b_glu, w_ssm_up, w_out, ffn_norm_g, w_router_group, b_router_group, w_router_expert, b_router_expert, w_exp_gate, w_exp_up, w_exp_down, ple_norm_g, w_ple_gate, w_ple):
    bsz, seq, d = x.shape
    depth = w_in.shape[0]
    t = bsz * seq
    assert d == D_MODEL and seq % TOK_TILE == 0 and t % MOE_TILE == 0 and seq % ATT_TILE == 0
    assert seq % SCAN_T == 0 and bsz == SUBLANES

    a_re, a_im, bb_re, bb_im, lam_dyn = _prep(ssm_lam_re, ssm_lam_im, ssm_log_step, ssm_b_re, ssm_b_im, lam_qk)
    head_ids = jnp.arange(COL_CHUNK) // ATT_HEAD_DIM
    gsum = (head_ids[:, None] == head_ids[None, :]).astype(BF16)

    x2d = x.reshape(t, d)
    for i in range(depth):
        lam_init = _lambda_init(i)
        tile_gain = lambda g: jnp.tile(g.reshape(1, HEAD_W), (1, COL_CHUNK // HEAD_W))
        q, k, v, u_sm, gates = _inproj(
            x2d, attn_norm_g[i].reshape(1, d), w_in[i].astype(BF16), gsum,
            tile_gain(q_norm_g[i]), tile_gain(k_norm_g[i]), b_gate[i].reshape(1, -1), seq=seq)

        o = _attention(lam_dyn[i, :1, :1], q, k, v, subln_g[i].reshape(1, HEAD_W), seq=seq, lam_init=lam_init)

        b_tiles, c_re_tiles, c_im_tiles = _ssm_matrices(bb_re[i], bb_im[i], ssm_c_re[i], ssm_c_im[i])
        z_tm = _ssm(u_sm.reshape(seq * bsz, SSM_WIDTH), b_tiles, c_re_tiles, c_im_tiles,
                    a_re[i, ::SSM_GROUP].reshape(1, N_STATE), a_im[i, ::SSM_GROUP].reshape(1, N_STATE),
                    ssm_d[i].reshape(1, SSM_WIDTH), bsz=bsz)

        w_r = jnp.concatenate(
            [w_router_group[i], jnp.transpose(w_router_expert[i], (1, 0, 2)).reshape(d, N_EXPERTS),
             jnp.zeros((d, ROUTE_W - N_GROUPS - N_EXPERTS), F32)], axis=1)
        w_r_hi = w_r.astype(BF16)
        w_r_lo = (w_r - w_r_hi.astype(F32)).astype(BF16)
        b_r = jnp.concatenate([b_router_group[i], b_router_expert[i].reshape(-1),
                               jnp.zeros((ROUTE_W - N_GROUPS - N_EXPERTS,), F32)]).reshape(1, ROUTE_W)
        x1, hn, route = _mix(
            x2d, o, z_tm.reshape(seq, bsz * SSM_WIDTH), gates,
            w_attn_up[i].astype(BF16), w_glu[i].astype(BF16), b_glu[i].reshape(1, -1),
            w_ssm_up[i].astype(BF16), w_out[i].astype(BF16), ffn_norm_g[i].reshape(1, d),
            w_r_hi, w_r_lo, b_r, seq=seq)

        x2d = _moe(
            x1, hn, route, p[i].reshape(t, PLE_DIM),
            w_exp_gate[i].reshape(N_EXPERTS, d, EXPERT_FF).astype(BF16),
            w_exp_up[i].reshape(N_EXPERTS, d, EXPERT_FF).astype(BF16),
            w_exp_down[i].reshape(N_EXPERTS, EXPERT_FF, d).astype(BF16),
            ple_norm_g[i].reshape(1, d), w_ple_gate[i].astype(BF16), w_ple[i].astype(BF16))
    return x2d.reshape(bsz, seq, d)
```

```python
import functools
import math

import jax
import jax.numpy as jnp
from jax import lax
from jax.experimental import pallas as pl
from jax.experimental.pallas import tpu as pltpu

F32 = jnp.float32
BF16 = jnp.bfloat16

D_MODEL = 1024
ATT_HEADS = 8
ATT_HEAD_DIM = 64
HEAD_W = 2 * ATT_HEAD_DIM
ATT_QK_WIDTH = ATT_HEADS * HEAD_W
ATT_V_WIDTH = ATT_HEADS * HEAD_W
SSM_WIDTH = 512
SSM_GROUP = 16
SSM_GROUPS = SSM_WIDTH // SSM_GROUP
SSM_STATE = 64
N_STATE = SSM_GROUPS * SSM_STATE
N_BRANCH = 2
IN_WIDTH = 2 * ATT_QK_WIDTH + ATT_V_WIDTH + SSM_WIDTH + N_BRANCH * D_MODEL
N_GROUPS = 4
EXPERTS_PER_GROUP = 8
N_EXPERTS = N_GROUPS * EXPERTS_PER_GROUP
EXPERT_FF = 256
PLE_DIM = 256
EPS = 1e-6

LANES = 128
SUBLANES = 8
MXU_N = 256
VMEM_LIMIT = 56 * 1024 * 1024

TOK_TILE = 512
MOE_TILE = 1024
COL_CHUNK = 512
ATT_TILE = 256
SCAN_T = 64
SCAN_LANES = 512
ROUTE_W = LANES
ROUTE_OFF = N_GROUPS


def _lambda_init(layer_idx):
    return 0.8 - 0.6 * math.exp(-0.3 * layer_idx)


def _dot(a, b):
    return jnp.dot(a, b, preferred_element_type=F32)


def _const_spec(shape):
    nd = len(shape)
    return pl.BlockSpec(shape, lambda *_: (0,) * nd, pipeline_mode=pl.Buffered(1))


def _params(*sem):
    return pltpu.CompilerParams(dimension_semantics=sem, vmem_limit_bytes=VMEM_LIMIT)


def _prep_kernel(lre_ref, lim_ref, lstep_ref, bre_ref, bim_ref, lqk_ref,
                 are_ref, aim_ref, bbre_ref, bbim_ref, lam_ref):
    lr = lre_ref[...]
    li = lim_ref[...]
    dt = jnp.exp(lstep_ref[...])
    mag = jnp.exp(lr * dt)
    ab_re = mag * jnp.cos(li * dt)
    ab_im = mag * jnp.sin(li * dt)
    den = lr * lr + li * li
    nr = ab_re - 1.0
    coef_re = (nr * lr + ab_im * li) / den
    coef_im = (ab_im * lr - nr * li) / den
    br = bre_ref[...]
    bi = bim_ref[...]
    are_ref[...] = ab_re
    aim_ref[...] = ab_im
    bbre_ref[...] = coef_re * br - coef_im * bi
    bbim_ref[...] = coef_re * bi + coef_im * br
    lq = lqk_ref[...]
    s01 = jnp.sum(lq[0:1] * lq[1:2], axis=-1, keepdims=True)
    s23 = jnp.sum(lq[2:3] * lq[3:4], axis=-1, keepdims=True)
    lam_ref[...] = jnp.broadcast_to(jnp.exp(s01) - jnp.exp(s23), lam_ref.shape)


def _prep(lam_re, lam_im, log_step, b_re, b_im, lam_qk):
    depth = lam_re.shape[0]
    rows = SSM_WIDTH
    rep = lambda a: jnp.repeat(a, SSM_GROUP, axis=1)
    lstep = jnp.broadcast_to(log_step[:, :, None], lam_re.shape)
    tr = lambda b: jnp.transpose(b, (0, 1, 3, 2)).reshape(depth, rows, SSM_STATE)
    blk = pl.BlockSpec((None, rows, SSM_STATE), lambda i: (i, 0, 0))
    out = jax.ShapeDtypeStruct((depth, rows, SSM_STATE), F32)
    return pl.pallas_call(
        _prep_kernel,
        grid=(depth,),
        in_specs=[blk, blk, blk, blk, blk, pl.BlockSpec((None, 4, ATT_HEAD_DIM), lambda i: (i, 0, 0))],
        out_specs=[blk, blk, blk, blk, pl.BlockSpec((None, SUBLANES, LANES), lambda i: (i, 0, 0))],
        out_shape=[out, out, out, out, jax.ShapeDtypeStruct((depth, SUBLANES, LANES), F32)],
        compiler_params=_params("arbitrary"),
        name="prep",
    )(rep(lam_re), rep(lam_im), rep(lstep), tr(b_re), tr(b_im), lam_qk)


def _block_diag(a):
    g, r, c = a.shape
    eye = jnp.eye(g, dtype=a.dtype)
    return jnp.einsum('grc,gk->grkc', a, eye).reshape(g * r, g * c)


def _ssm_matrices(bb_re, bb_im, c_re, c_im):
    g = SSM_GROUPS
    n_re_tiles = N_STATE // MXU_N
    tiles = []
    for full in (_block_diag(bb_re.reshape(g, SSM_GROUP, SSM_STATE)),
                 _block_diag(bb_im.reshape(g, SSM_GROUP, SSM_STATE))):
        for j in range(n_re_tiles):
            k0 = LANES * ((j * MXU_N // SSM_STATE * SSM_GROUP) // LANES)
            tiles.append(full[k0:k0 + LANES, j * MXU_N:(j + 1) * MXU_N])
    b_tiles = jnp.stack(tiles).astype(BF16)
    c_out = []
    for c in (c_re, c_im):
        full = _block_diag(jnp.transpose(c, (0, 2, 1)))
        half = N_STATE // 2
        c_out.append(jnp.stack([full[n * half:(n + 1) * half, n * MXU_N:(n + 1) * MXU_N]
                                for n in range(2)]).astype(BF16))
    return b_tiles, c_out[0], c_out[1]


def _inproj_kernel(x_ref, g_ref, w_ref, gsum_ref, qg_ref, kg_ref, bg_ref,
                   q_ref, k_ref, v_ref, u_ref, gate_ref, *, q_scale):
    xf = x_ref[...]
    ms = jnp.mean(xf * xf, axis=-1, keepdims=True)
    h = (xf * lax.rsqrt(ms + EPS) * g_ref[...]).astype(BF16)

    def head_norm(y, gain):
        sq = y * y
        hi = sq.astype(BF16)
        lo = (sq - hi.astype(F32)).astype(BF16)
        gs = _dot(hi, gsum_ref[...]) + _dot(lo, gsum_ref[...])
        return y * lax.rsqrt(gs * (1.0 / ATT_HEAD_DIM) + EPS) * gain

    n_qk = ATT_QK_WIDTH // COL_CHUNK
    n_v = ATT_V_WIDTH // COL_CHUNK
    for c in range(IN_WIDTH // COL_CHUNK):
        y = _dot(h, w_ref[:, c * COL_CHUNK:(c + 1) * COL_CHUNK])
        if c < n_qk:
            q_ref[:, c * COL_CHUNK:(c + 1) * COL_CHUNK] = (head_norm(y, qg_ref[...]) * q_scale).astype(BF16)
        elif c < 2 * n_qk:
            cc = c - n_qk
            k_ref[:, cc * COL_CHUNK:(cc + 1) * COL_CHUNK] = head_norm(y, kg_ref[...]).astype(BF16)
        elif c < 2 * n_qk + n_v:
            cc = c - 2 * n_qk
            v_ref[:, cc * COL_CHUNK:(cc + 1) * COL_CHUNK] = y.astype(BF16)
        elif c == 2 * n_qk + n_v:
            u_ref[...] = y
        else:
            cc = c - (2 * n_qk + n_v + 1)
            b = bg_ref[:, cc * COL_CHUNK:(cc + 1) * COL_CHUNK]
            gate_ref[:, cc * COL_CHUNK:(cc + 1) * COL_CHUNK] = jax.nn.sigmoid(y + b).astype(BF16)


def _inproj(x2d, norm_g, w_in, gsum, qg, kg, b_gate, *, seq):
    t = x2d.shape[0]
    n_sb = seq // TOK_TILE
    bsz = t // seq
    row = lambda w: pl.BlockSpec((TOK_TILE, w), lambda i: (i, 0))
    q_scale = ATT_HEAD_DIM ** -0.5 * math.log2(math.e)
    return pl.pallas_call(
        functools.partial(_inproj_kernel, q_scale=q_scale),
        grid=(t // TOK_TILE,),
        in_specs=[row(D_MODEL), _const_spec((1, D_MODEL)), _const_spec((D_MODEL, IN_WIDTH)),
                  _const_spec((COL_CHUNK, COL_CHUNK)), _const_spec((1, COL_CHUNK)),
                  _const_spec((1, COL_CHUNK)), _const_spec((1, N_BRANCH * D_MODEL))],
        out_specs=[row(ATT_QK_WIDTH), row(ATT_QK_WIDTH), row(ATT_V_WIDTH),
                   pl.BlockSpec((TOK_TILE, SSM_WIDTH), lambda i: (i % n_sb, i // n_sb)),
                   row(N_BRANCH * D_MODEL)],
        out_shape=[jax.ShapeDtypeStruct((t, ATT_QK_WIDTH), BF16),
                   jax.ShapeDtypeStruct((t, ATT_QK_WIDTH), BF16),
                   jax.ShapeDtypeStruct((t, ATT_V_WIDTH), BF16),
                   jax.ShapeDtypeStruct((seq, bsz * SSM_WIDTH), F32),
                   jax.ShapeDtypeStruct((t, N_BRANCH * D_MODEL), BF16)],
        compiler_params=_params("parallel"),
        name="inproj",
    )(x2d, norm_g, w_in, gsum, qg, kg, b_gate)


def _attn_kernel(lam_ref, q_ref, k_ref, v_ref, g_ref, o_ref, vt_scr, *, lam_init):
    tq = ATT_TILE
    seq = k_ref.shape[0]
    nq = seq // tq
    lam = lam_ref[0, 0] + lam_init
    for c in range(nq):
        vt_scr[:, c * tq:(c + 1) * tq] = v_ref[c * tq:(c + 1) * tq, :].astype(F32).T.astype(BF16)
    head_row = lax.broadcasted_iota(jnp.int32, (HEAD_W, tq), 0)
    key_pos = lax.broadcasted_iota(jnp.int32, (tq, 2 * tq), 0)
    qry_pos = lax.broadcasted_iota(jnp.int32, (tq, 2 * tq), 1) & (tq - 1)
    causal = key_pos <= qry_pos

    for qi in range(nq):
        qt = q_ref[qi * tq:(qi + 1) * tq, :].astype(F32).T
        qst = jnp.concatenate([jnp.where(head_row < ATT_HEAD_DIM, qt, 0.0),
                               jnp.where(head_row >= ATT_HEAD_DIM, qt, 0.0)], axis=1).astype(BF16)
        nk = qi * tq
        s_d = jnp.where(causal, _dot(k_ref[nk:nk + tq, :], qst), -jnp.inf)
        m = jnp.max(s_d, axis=0, keepdims=True)
        if qi > 0:
            s_m = _dot(k_ref[0:nk, :], qst)
            m = jnp.maximum(m, jnp.max(s_m, axis=0, keepdims=True))
        p_d = jnp.exp2(s_d - m)
        l = jnp.sum(p_d, axis=0, keepdims=True)
        acc = _dot(vt_scr[:, nk:nk + tq], p_d.astype(BF16))
        if qi > 0:
            p_m = jnp.exp2(s_m - m)
            l = l + jnp.sum(p_m, axis=0, keepdims=True)
            acc = acc + _dot(vt_scr[:, 0:nk], p_m.astype(BF16))
        accn = acc * (1.0 / l)
        ot = accn[:, :tq] - lam * accn[:, tq:]
        ms = jnp.mean(ot * ot, axis=0, keepdims=True)
        o = (ot * lax.rsqrt(ms + EPS)).T * g_ref[...] * (1.0 - lam_init)
        o_ref[qi * tq:(qi + 1) * tq, :] = o.astype(BF16)


def _attention(lam, q, k, v, subln_g, *, seq, lam_init):
    t = q.shape[0]
    bsz = t // seq
    blk = pl.BlockSpec((seq, HEAD_W), lambda b, h: (b, h))
    return pl.pallas_call(
        functools.partial(_attn_kernel, lam_init=lam_init),
        grid=(bsz, ATT_HEADS),
        in_specs=[pl.BlockSpec(memory_space=pltpu.SMEM), blk, blk, blk, _const_spec((1, HEAD_W))],
        out_specs=blk,
        out_shape=jax.ShapeDtypeStruct((t, ATT_V_WIDTH), BF16),
        scratch_shapes=[pltpu.VMEM((HEAD_W, seq), BF16)],
        compiler_params=_params("parallel", "parallel"),
        name="attn",
    )(lam, q, k, v, subln_g)


def _ssm_kernel(u_ref, bt_ref, cre_ref, cim_ref, are_ref, aim_ref, d_ref, z_ref, st_scr, state_scr):
    rows = u_ref.shape[0]
    bsz = state_scr.shape[0]

    @pl.when(pl.program_id(0) == 0)
    def _():
        state_scr[...] = jnp.zeros(state_scr.shape, F32)

    u = u_ref[...]
    ub = u.astype(BF16)
    n_tiles = 2 * N_STATE // MXU_N
    for j in range(n_tiles):
        k0 = LANES * (((j % (n_tiles // 2)) * MXU_N // SSM_STATE * SSM_GROUP) // LANES)
        st_scr[:, j * MXU_N:(j + 1) * MXU_N] = _dot(ub[:, k0:k0 + LANES], bt_ref[j])

    for cc in range(N_STATE // SCAN_LANES):
        lo = cc * SCAN_LANES
        hi = N_STATE + lo
        a_r = jnp.broadcast_to(are_ref[:, lo:lo + SCAN_LANES], (bsz, SCAN_LANES))
        a_i = jnp.broadcast_to(aim_ref[:, lo:lo + SCAN_LANES], (bsz, SCAN_LANES))

        def step(t, carry, lo=lo, hi=hi, a_r=a_r, a_i=a_i):
            s_r, s_i = carry
            r0 = pl.multiple_of(t * bsz, bsz)
            n_r = a_r * s_r - a_i * s_i + st_scr[pl.ds(r0, bsz), lo:lo + SCAN_LANES]
            n_i = a_r * s_i + a_i * s_r + st_scr[pl.ds(r0, bsz), hi:hi + SCAN_LANES]
            st_scr[pl.ds(r0, bsz), lo:lo + SCAN_LANES] = n_r
            st_scr[pl.ds(r0, bsz), hi:hi + SCAN_LANES] = n_i
            return n_r, n_i

        s_r, s_i = lax.fori_loop(0, rows // bsz, step,
                                 (state_scr[:, lo:lo + SCAN_LANES], state_scr[:, hi:hi + SCAN_LANES]),
                                 unroll=8)
        state_scr[:, lo:lo + SCAN_LANES] = s_r
        state_scr[:, hi:hi + SCAN_LANES] = s_i

    half = N_STATE // 2
    for n in range(2):
        s_re = st_scr[:, n * half:(n + 1) * half].astype(BF16)
        s_im = st_scr[:, N_STATE + n * half:N_STATE + (n + 1) * half].astype(BF16)
        cols = slice(n * MXU_N, (n + 1) * MXU_N)
        y = _dot(s_re, cre_ref[n]) - _dot(s_im, cim_ref[n]) + d_ref[:, cols] * u[:, cols]
        z_ref[:, cols] = jax.nn.gelu(y).astype(BF16)


def _ssm(u_tm, b_tiles, c_re_tiles, c_im_tiles, a_re, a_im, d_skip, *, bsz):
    rows_total = u_tm.shape[0]
    rows = SCAN_T * bsz
    blk = pl.BlockSpec((rows, SSM_WIDTH), lambda i: (i, 0))
    return pl.pallas_call(
        _ssm_kernel,
        grid=(rows_total // rows,),
        in_specs=[blk, _const_spec(b_tiles.shape), _const_spec(c_re_tiles.shape),
                  _const_spec(c_im_tiles.shape), _const_spec((1, N_STATE)), _const_spec((1, N_STATE)),
                  _const_spec((1, SSM_WIDTH))],
        out_specs=blk,
        out_shape=jax.ShapeDtypeStruct((rows_total, SSM_WIDTH), BF16),
        scratch_shapes=[pltpu.VMEM((rows, 2 * N_STATE), F32), pltpu.VMEM((bsz, 2 * N_STATE), F32)],
        compiler_params=_params("arbitrary"),
        name="ssm",
    )(u_tm, b_tiles, c_re_tiles, c_im_tiles, a_re, a_im, d_skip)


def _route(logits):
    lane = lax.broadcasted_iota(jnp.int32, logits.shape, 1)
    lanef = lane.astype(F32)
    big = float(ROUTE_W)
    neg = -jnp.inf
    gl = jnp.where(lane < N_GROUPS, logits, neg)
    gmax = jnp.max(gl, axis=-1, keepdims=True)
    grp = jnp.min(jnp.where(gl == gmax, lanef, big), axis=-1, keepdims=True)
    p_grp = 1.0 / jnp.sum(jnp.exp(gl - gmax), axis=-1, keepdims=True)
    first = ROUTE_OFF + grp * EXPERTS_PER_GROUP
    in_grp = (lanef >= first) & (lanef < first + EXPERTS_PER_GROUP)
    el = jnp.where(in_grp, logits, neg)
    v1 = jnp.max(el, axis=-1, keepdims=True)
    i1 = jnp.min(jnp.where(el == v1, lanef, big), axis=-1, keepdims=True)
    el2 = jnp.where(lanef == i1, neg, el)
    v2 = jnp.max(el2, axis=-1, keepdims=True)
    i2 = jnp.min(jnp.where(el2 == v2, lanef, big), axis=-1, keepdims=True)
    e21 = jnp.exp(v2 - v1)
    w1 = p_grp / (1.0 + e21)
    w2 = p_grp * (e21 / (1.0 + e21))
    return jnp.where(lanef == i1, w1, 0.0) + jnp.where(lanef == i2, w2, 0.0)


def _mix_kernel(x_ref, o_ref, z_ref, gate_ref, wau_ref, wglu_ref, bglu_ref, wsu_ref, wout_ref,
                fg_ref, wrh_ref, wrl_ref, br_ref, x1_ref, hn_ref, route_ref):
    y_att = _dot(o_ref[...], wau_ref[...])
    z = z_ref[...].astype(F32)
    z = z * jax.nn.sigmoid(_dot(z_ref[...], wglu_ref[...]) + bglu_ref[...])
    y_ssm = _dot(z.astype(BF16), wsu_ref[...])
    mixed = (gate_ref[:, :D_MODEL].astype(F32) * y_att + gate_ref[:, D_MODEL:].astype(F32) * y_ssm)
    x1 = x_ref[...] + _dot(mixed.astype(BF16), wout_ref[...])
    x1_ref[...] = x1
    ms = jnp.mean(x1 * x1, axis=-1, keepdims=True)
    hn = x1 * lax.rsqrt(ms + EPS) * fg_ref[...]
    hn_hi = hn.astype(BF16)
    hn_ref[...] = hn_hi
    hn_lo = (hn - hn_hi.astype(F32)).astype(BF16)
    logits = (_dot(hn_hi, wrh_ref[...]) + _dot(hn_lo, wrh_ref[...]) + _dot(hn_hi, wrl_ref[...])
              + br_ref[...])
    route_ref[...] = _route(logits)


def _mix(x2d, o, z_sm, gates, wau, wglu, bglu, wsu, wout, ffn_g, wr_hi, wr_lo, br, *, seq):
    t = x2d.shape[0]
    n_sb = seq // TOK_TILE
    row = lambda w: pl.BlockSpec((TOK_TILE, w), lambda i: (i, 0))
    return pl.pallas_call(
        _mix_kernel,
        grid=(t // TOK_TILE,),
        in_specs=[row(D_MODEL), row(ATT_V_WIDTH),
                  pl.BlockSpec((TOK_TILE, SSM_WIDTH), lambda i: (i % n_sb, i // n_sb)),
                  row(N_BRANCH * D_MODEL),
                  _const_spec(wau.shape), _const_spec(wglu.shape), _const_spec(bglu.shape),
                  _const_spec(wsu.shape), _const_spec(wout.shape), _const_spec(ffn_g.shape),
                  _const_spec(wr_hi.shape), _const_spec(wr_lo.shape), _const_spec(br.shape)],
        out_specs=[row(D_MODEL), row(D_MODEL), row(ROUTE_W)],
        out_shape=[jax.ShapeDtypeStruct((t, D_MODEL), F32),
                   jax.ShapeDtypeStruct((t, D_MODEL), BF16),
                   jax.ShapeDtypeStruct((t, ROUTE_W), F32)],
        compiler_params=_params("parallel"),
        name="mix",
    )(x2d, o, z_sm, gates, wau, wglu, bglu, wsu, wout, ffn_g, wr_hi, wr_lo, br)


def _moe_kernel(x1_ref, hn_ref, route_ref, p_ref, wg_ref, wu_ref, wd_ref, pg_ref, wpg_ref, wple_ref,
                out_ref, acc_scr):
    e = pl.program_id(1)

    @pl.when(e == 0)
    def _():
        acc_scr[...] = jnp.zeros(acc_scr.shape, F32)

    h = hn_ref[...]
    route = route_ref[...]
    lane = lax.broadcasted_iota(jnp.int32, route.shape, 1)
    w_e = jnp.sum(jnp.where(lane == e + ROUTE_OFF, route, 0.0), axis=-1, keepdims=True)
    act = jax.nn.silu(_dot(h, wg_ref[...])) * _dot(h, wu_ref[...]) * w_e
    acc_scr[...] += _dot(act.astype(BF16), wd_ref[...])

    @pl.when(e == N_EXPERTS - 1)
    def _():
        x2 = x1_ref[...] + acc_scr[...]
        ms = jnp.mean(x2 * x2, axis=-1, keepdims=True)
        hp = (x2 * lax.rsqrt(ms + EPS) * pg_ref[...]).astype(BF16)
        gate = jax.nn.sigmoid(_dot(hp, wpg_ref[...]))
        out_ref[...] = x2 + gate * _dot(p_ref[...].astype(BF16), wple_ref[...])


def _moe(x1, hn, route, p2d, wg, wu, wd, ple_g, wpg, wple):
    t = x1.shape[0]
    row = lambda w: pl.BlockSpec((MOE_TILE, w), lambda i, e: (i, 0))
    return pl.pallas_call(
        _moe_kernel,
        grid=(t // MOE_TILE, N_EXPERTS),
        in_specs=[row(D_MODEL), row(D_MODEL), row(ROUTE_W), row(PLE_DIM),
                  pl.BlockSpec((None, D_MODEL, EXPERT_FF), lambda i, e: (e, 0, 0)),
                  pl.BlockSpec((None, D_MODEL, EXPERT_FF), lambda i, e: (e, 0, 0)),
                  pl.BlockSpec((None, EXPERT_FF, D_MODEL), lambda i, e: (e, 0, 0)),
                  _const_spec(ple_g.shape), _const_spec(wpg.shape), _const_spec(wple.shape)],
        out_specs=row(D_MODEL),
        out_shape=jax.ShapeDtypeStruct((t, D_MODEL), F32),
        scratch_shapes=[pltpu.VMEM((MOE_TILE, D_MODEL), F32)],
        compiler_params=_params("parallel", "arbitrary"),
        name="moe",
    )(x1, hn, route, p2d, wg, wu, wd, ple_g, wpg, wple)


def kernel(x, p, attn_norm_g, w_in, b_gate, q_norm_g, k_norm_g, lam_qk, subln_g, w_attn_up, ssm_lam_re, ssm_lam_im, ssm_log_step, ssm_b_re, ssm_b_im, ssm_c_re, ssm_c_im, ssm_d, w_glu, b_glu, w_ssm_up, w_out, ffn_norm_g, w_router_group, b_router_group, w_router_expert, b_router_expert, w_exp_gate, w_exp_up, w_exp_down, ple_norm_g, w_ple_gate, w_ple):
    bsz, seq, d = x.shape
    depth = w_in.shape[0]
    t = bsz * seq
    assert d == D_MODEL and seq % TOK_TILE == 0 and t % MOE_TILE == 0 and seq % ATT_TILE == 0
    assert seq % SCAN_T == 0 and bsz == SUBLANES

    a_re, a_im, bb_re, bb_im, lam_dyn = _prep(ssm_lam_re, ssm_lam_im, ssm_log_step, ssm_b_re, ssm_b_im, lam_qk)
    head_ids = jnp.arange(COL_CHUNK) // ATT_HEAD_DIM
    gsum = (head_ids[:, None] == head_ids[None, :]).astype(BF16)

    x2d = x.reshape(t, d)
    for i in range(depth):
        lam_init = _lambda_init(i)
        tile_gain = lambda g: jnp.tile(g.reshape(1, HEAD_W), (1, COL_CHUNK // HEAD_W))
        q, k, v, u_sm, gates = _inproj(
            x2d, attn_norm_g[i].reshape(1, d), w_in[i].astype(BF16), gsum,
            tile_gain(q_norm_g[i]), tile_gain(k_norm_g[i]), b_gate[i].reshape(1, -1), seq=seq)

        o = _attention(lam_dyn[i, :1, :1], q, k, v, subln_g[i].reshape(1, HEAD_W), seq=seq, lam_init=lam_init)

        b_tiles, c_re_tiles, c_im_tiles = _ssm_matrices(bb_re[i], bb_im[i], ssm_c_re[i], ssm_c_im[i])
        z_tm = _ssm(u_sm.reshape(seq * bsz, SSM_WIDTH), b_tiles, c_re_tiles, c_im_tiles,
                    a_re[i, ::SSM_GROUP].reshape(1, N_STATE), a_im[i, ::SSM_GROUP].reshape(1, N_STATE),
                    ssm_d[i].reshape(1, SSM_WIDTH), bsz=bsz)

        w_r = jnp.concatenate(
            [w_router_group[i], jnp.transpose(w_router_expert[i], (1, 0, 2)).reshape(d, N_EXPERTS),
             jnp.zeros((d, ROUTE_W - N_GROUPS - N_EXPERTS), F32)], axis=1)
        w_r_hi = w_r.astype(BF16)
        w_r_lo = (w_r - w_r_hi.astype(F32)).astype(BF16)
        b_r = jnp.concatenate([b_router_group[i], b_router_expert[i].reshape(-1),
                               jnp.zeros((ROUTE_W - N_GROUPS - N_EXPERTS,), F32)]).reshape(1, ROUTE_W)
        x1, hn, route = _mix(
            x2d, o, z_tm.reshape(seq, bsz * SSM_WIDTH), gates,
            w_attn_up[i].astype(BF16), w_glu[i].astype(BF16), b_glu[i].reshape(1, -1),
            w_ssm_up[i].astype(BF16), w_out[i].astype(BF16), ffn_norm_g[i].reshape(1, d),
            w_r_hi, w_r_lo, b_r, seq=seq)

        x2d = _moe(
            x1, hn, route, p[i].reshape(t, PLE_DIM),
            w_exp_gate[i].reshape(N_EXPERTS, d, EXPERT_FF).astype(BF16),
            w_exp_up[i].reshape(N_EXPERTS, d, EXPERT_FF).astype(BF16),
            w_exp_down[i].reshape(N_EXPERTS, EXPERT_FF, d).astype(BF16),
            ple_norm_g[i].reshape(1, d), w_ple_gate[i].astype(BF16), w_ple[i].astype(BF16))
    return x2d.reshape(bsz, seq, d)
```

```python
import functools
import math

import jax
import jax.numpy as jnp
from jax import lax
from jax.experimental import pallas as pl
from jax.experimental.pallas import tpu as pltpu

F32 = jnp.float32
BF16 = jnp.bfloat16

D_MODEL = 1024
ATT_HEADS = 8
ATT_HEAD_DIM = 64
HEAD_W = 2 * ATT_HEAD_DIM
ATT_QK_WIDTH = ATT_HEADS * HEAD_W
ATT_V_WIDTH = ATT_HEADS * HEAD_W
SSM_WIDTH = 512
SSM_GROUP = 16
SSM_GROUPS = SSM_WIDTH // SSM_GROUP
SSM_STATE = 64
N_STATE = SSM_GROUPS * SSM_STATE
N_BRANCH = 2
IN_WIDTH = 2 * ATT_QK_WIDTH + ATT_V_WIDTH + SSM_WIDTH + N_BRANCH * D_MODEL
N_GROUPS = 4
EXPERTS_PER_GROUP = 8
N_EXPERTS = N_GROUPS * EXPERTS_PER_GROUP
EXPERT_FF = 256
PLE_DIM = 256
EPS = 1e-6

LANES = 128
SUBLANES = 8
MXU_N = 256
VMEM_LIMIT = 56 * 1024 * 1024

TOK_TILE = 512
MOE_CAP = 160
COL_CHUNK = 512
ATT_TILE = 256
SCAN_T = 64
SCAN_LANES = 512
ROUTE_W = LANES
ROUTE_OFF = N_GROUPS
GRP_LANE = 0
RANK_LANE = 1


def _lambda_init(layer_idx):
    return 0.8 - 0.6 * math.exp(-0.3 * layer_idx)


def _dot(a, b):
    return jnp.dot(a, b, preferred_element_type=F32)


def _const_spec(shape):
    nd = len(shape)
    return pl.BlockSpec(shape, lambda *_: (0,) * nd, pipeline_mode=pl.Buffered(1))


def _params(*sem):
    return pltpu.CompilerParams(dimension_semantics=sem, vmem_limit_bytes=VMEM_LIMIT)


def _prep_kernel(lre_ref, lim_ref, lstep_ref, bre_ref, bim_ref, lqk_ref,
                 are_ref, aim_ref, bbre_ref, bbim_ref, lam_ref):
    lr = lre_ref[...]
    li = lim_ref[...]
    dt = jnp.exp(lstep_ref[...])
    mag = jnp.exp(lr * dt)
    ab_re = mag * jnp.cos(li * dt)
    ab_im = mag * jnp.sin(li * dt)
    den = lr * lr + li * li
    nr = ab_re - 1.0
    coef_re = (nr * lr + ab_im * li) / den
    coef_im = (ab_im * lr - nr * li) / den
    br = bre_ref[...]
    bi = bim_ref[...]
    are_ref[...] = ab_re
    aim_ref[...] = ab_im
    bbre_ref[...] = coef_re * br - coef_im * bi
    bbim_ref[...] = coef_re * bi + coef_im * br
    lq = lqk_ref[...]
    s01 = jnp.sum(lq[0:1] * lq[1:2], axis=-1, keepdims=True)
    s23 = jnp.sum(lq[2:3] * lq[3:4], axis=-1, keepdims=True)
    lam_ref[...] = jnp.broadcast_to(jnp.exp(s01) - jnp.exp(s23), lam_ref.shape)


def _prep(lam_re, lam_im, log_step, b_re, b_im, lam_qk):
    depth = lam_re.shape[0]
    rows = SSM_WIDTH
    rep = lambda a: jnp.repeat(a, SSM_GROUP, axis=1)
    lstep = jnp.broadcast_to(log_step[:, :, None], lam_re.shape)
    tr = lambda b: jnp.transpose(b, (0, 1, 3, 2)).reshape(depth, rows, SSM_STATE)
    blk = pl.BlockSpec((None, rows, SSM_STATE), lambda i: (i, 0, 0))
    out = jax.ShapeDtypeStruct((depth, rows, SSM_STATE), F32)
    return pl.pallas_call(
        _prep_kernel,
        grid=(depth,),
        in_specs=[blk, blk, blk, blk, blk, pl.BlockSpec((None, 4, ATT_HEAD_DIM), lambda i: (i, 0, 0))],
        out_specs=[blk, blk, blk, blk, pl.BlockSpec((None, SUBLANES, LANES), lambda i: (i, 0, 0))],
        out_shape=[out, out, out, out, jax.ShapeDtypeStruct((depth, SUBLANES, LANES), F32)],
        compiler_params=_params("arbitrary"),
        name="prep",
    )(rep(lam_re), rep(lam_im), rep(lstep), tr(b_re), tr(b_im), lam_qk)


def _block_diag(a):
    g, r, c = a.shape
    eye = jnp.eye(g, dtype=a.dtype)
    return jnp.einsum('grc,gk->grkc', a, eye).reshape(g * r, g * c)


def _ssm_matrices(bb_re, bb_im, c_re, c_im):
    g = SSM_GROUPS
    n_re_tiles = N_STATE // MXU_N
    tiles = []
    for full in (_block_diag(bb_re.reshape(g, SSM_GROUP, SSM_STATE)),
                 _block_diag(bb_im.reshape(g, SSM_GROUP, SSM_STATE))):
        for j in range(n_re_tiles):
            k0 = LANES * ((j * MXU_N // SSM_STATE * SSM_GROUP) // LANES)
            tiles.append(full[k0:k0 + LANES, j * MXU_N:(j + 1) * MXU_N])
    b_tiles = jnp.stack(tiles).astype(BF16)
    c_out = []
    for c in (c_re, c_im):
        full = _block_diag(jnp.transpose(c, (0, 2, 1)))
        half = N_STATE // 2
        c_out.append(jnp.stack([full[n * half:(n + 1) * half, n * MXU_N:(n + 1) * MXU_N]
                                for n in range(2)]).astype(BF16))
    return b_tiles, c_out[0], c_out[1]


def _inproj_kernel(x_ref, g_ref, w_ref, gsum_ref, qg_ref, kg_ref, bg_ref,
                   q_ref, k_ref, v_ref, u_ref, gate_ref, *, q_scale):
    xf = x_ref[...]
    ms = jnp.mean(xf * xf, axis=-1, keepdims=True)
    h = (xf * lax.rsqrt(ms + EPS) * g_ref[...]).astype(BF16)

    def head_norm(y, gain):
        sq = y * y
        hi = sq.astype(BF16)
        lo = (sq - hi.astype(F32)).astype(BF16)
        gs = _dot(hi, gsum_ref[...]) + _dot(lo, gsum_ref[...])
        return y * lax.rsqrt(gs * (1.0 / ATT_HEAD_DIM) + EPS) * gain

    n_qk = ATT_QK_WIDTH // COL_CHUNK
    n_v = ATT_V_WIDTH // COL_CHUNK
    for c in range(IN_WIDTH // COL_CHUNK):
        y = _dot(h, w_ref[:, c * COL_CHUNK:(c + 1) * COL_CHUNK])
        if c < n_qk:
            q_ref[:, c * COL_CHUNK:(c + 1) * COL_CHUNK] = (head_norm(y, qg_ref[...]) * q_scale).astype(BF16)
        elif c < 2 * n_qk:
            cc = c - n_qk
            k_ref[:, cc * COL_CHUNK:(cc + 1) * COL_CHUNK] = head_norm(y, kg_ref[...]).astype(BF16)
        elif c < 2 * n_qk + n_v:
            cc = c - 2 * n_qk
            v_ref[:, cc * COL_CHUNK:(cc + 1) * COL_CHUNK] = y.astype(BF16)
        elif c == 2 * n_qk + n_v:
            u_ref[...] = y
        else:
            cc = c - (2 * n_qk + n_v + 1)
            b = bg_ref[:, cc * COL_CHUNK:(cc + 1) * COL_CHUNK]
            gate_ref[:, cc * COL_CHUNK:(cc + 1) * COL_CHUNK] = jax.nn.sigmoid(y + b).astype(BF16)


def _inproj(x2d, norm_g, w_in, gsum, qg, kg, b_gate, *, seq):
    t = x2d.shape[0]
    n_sb = seq // TOK_TILE
    bsz = t // seq
    row = lambda w: pl.BlockSpec((TOK_TILE, w), lambda i: (i, 0))
    q_scale = ATT_HEAD_DIM ** -0.5 * math.log2(math.e)
    return pl.pallas_call(
        functools.partial(_inproj_kernel, q_scale=q_scale),
        grid=(t // TOK_TILE,),
        in_specs=[row(D_MODEL), _const_spec((1, D_MODEL)), _const_spec((D_MODEL, IN_WIDTH)),
                  _const_spec((COL_CHUNK, COL_CHUNK)), _const_spec((1, COL_CHUNK)),
                  _const_spec((1, COL_CHUNK)), _const_spec((1, N_BRANCH * D_MODEL))],
        out_specs=[row(ATT_QK_WIDTH), row(ATT_QK_WIDTH), row(ATT_V_WIDTH),
                   pl.BlockSpec((TOK_TILE, SSM_WIDTH), lambda i: (i % n_sb, i // n_sb)),
                   row(N_BRANCH * D_MODEL)],
        out_shape=[jax.ShapeDtypeStruct((t, ATT_QK_WIDTH), BF16),
                   jax.ShapeDtypeStruct((t, ATT_QK_WIDTH), BF16),
                   jax.ShapeDtypeStruct((t, ATT_V_WIDTH), BF16),
                   jax.ShapeDtypeStruct((seq, bsz * SSM_WIDTH), F32),
                   jax.ShapeDtypeStruct((t, N_BRANCH * D_MODEL), BF16)],
        compiler_params=_params("parallel"),
        name="inproj",
    )(x2d, norm_g, w_in, gsum, qg, kg, b_gate)


def _attn_kernel(lam_ref, q_ref, k_ref, v_ref, g_ref, o_ref, vt_scr, *, lam_init):
    tq = ATT_TILE
    seq = k_ref.shape[0]
    nq = seq // tq
    lam = lam_ref[0, 0] + lam_init
    for c in range(nq):
        vt_scr[:, c * tq:(c + 1) * tq] = v_ref[c * tq:(c + 1) * tq, :].astype(F32).T.astype(BF16)
    head_row = lax.broadcasted_iota(jnp.int32, (HEAD_W, tq), 0)
    key_pos = lax.broadcasted_iota(jnp.int32, (tq, 2 * tq), 0)
    qry_pos = lax.broadcasted_iota(jnp.int32, (tq, 2 * tq), 1) & (tq - 1)
    causal = key_pos <= qry_pos

    for qi in range(nq):
        qt = q_ref[qi * tq:(qi + 1) * tq, :].astype(F32).T
        qst = jnp.concatenate([jnp.where(head_row < ATT_HEAD_DIM, qt, 0.0),
                               jnp.where(head_row >= ATT_HEAD_DIM, qt, 0.0)], axis=1).astype(BF16)
        nk = qi * tq
        s_d = jnp.where(causal, _dot(k_ref[nk:nk + tq, :], qst), -jnp.inf)
        m = jnp.max(s_d, axis=0, keepdims=True)
        if qi > 0:
            s_m = _dot(k_ref[0:nk, :], qst)
            m = jnp.maximum(m, jnp.max(s_m, axis=0, keepdims=True))
        p_d = jnp.exp2(s_d - m)
        l = jnp.sum(p_d, axis=0, keepdims=True)
        acc = _dot(vt_scr[:, nk:nk + tq], p_d.astype(BF16))
        if qi > 0:
            p_m = jnp.exp2(s_m - m)
            l = l + jnp.sum(p_m, axis=0, keepdims=True)
            acc = acc + _dot(vt_scr[:, 0:nk], p_m.astype(BF16))
        accn = acc * (1.0 / l)
        ot = accn[:, :tq] - lam * accn[:, tq:]
        ms = jnp.mean(ot * ot, axis=0, keepdims=True)
        o = (ot * lax.rsqrt(ms + EPS)).T * g_ref[...] * (1.0 - lam_init)
        o_ref[qi * tq:(qi + 1) * tq, :] = o.astype(BF16)


def _attention(lam, q, k, v, subln_g, *, seq, lam_init):
    t = q.shape[0]
    bsz = t // seq
    blk = pl.BlockSpec((seq, HEAD_W), lambda b, h: (b, h))
    return pl.pallas_call(
        functools.partial(_attn_kernel, lam_init=lam_init),
        grid=(bsz, ATT_HEADS),
        in_specs=[pl.BlockSpec(memory_space=pltpu.SMEM), blk, blk, blk, _const_spec((1, HEAD_W))],
        out_specs=blk,
        out_shape=jax.ShapeDtypeStruct((t, ATT_V_WIDTH), BF16),
        scratch_shapes=[pltpu.VMEM((HEAD_W, seq), BF16)],
        compiler_params=_params("parallel", "parallel"),
        name="attn",
    )(lam, q, k, v, subln_g)


def _ssm_kernel(u_ref, bt_ref, cre_ref, cim_ref, are_ref, aim_ref, d_ref, z_ref, st_scr, state_scr):
    rows = u_ref.shape[0]
    bsz = state_scr.shape[0]

    @pl.when(pl.program_id(0) == 0)
    def _():
        state_scr[...] = jnp.zeros(state_scr.shape, F32)

    u = u_ref[...]
    ub = u.astype(BF16)
    n_tiles = 2 * N_STATE // MXU_N
    for j in range(n_tiles):
        k0 = LANES * (((j % (n_tiles // 2)) * MXU_N // SSM_STATE * SSM_GROUP) // LANES)
        st_scr[:, j * MXU_N:(j + 1) * MXU_N] = _dot(ub[:, k0:k0 + LANES], bt_ref[j])

    for cc in range(N_STATE // SCAN_LANES):
        lo = cc * SCAN_LANES
        hi = N_STATE + lo
        a_r = jnp.broadcast_to(are_ref[:, lo:lo + SCAN_LANES], (bsz, SCAN_LANES))
        a_i = jnp.broadcast_to(aim_ref[:, lo:lo + SCAN_LANES], (bsz, SCAN_LANES))

        def step(t, carry, lo=lo, hi=hi, a_r=a_r, a_i=a_i):
            s_r, s_i = carry
            r0 = pl.multiple_of(t * bsz, bsz)
            n_r = a_r * s_r - a_i * s_i + st_scr[pl.ds(r0, bsz), lo:lo + SCAN_LANES]
            n_i = a_r * s_i + a_i * s_r + st_scr[pl.ds(r0, bsz), hi:hi + SCAN_LANES]
            st_scr[pl.ds(r0, bsz), lo:lo + SCAN_LANES] = n_r
            st_scr[pl.ds(r0, bsz), hi:hi + SCAN_LANES] = n_i
            return n_r, n_i

        s_r, s_i = lax.fori_loop(0, rows // bsz, step,
                                 (state_scr[:, lo:lo + SCAN_LANES], state_scr[:, hi:hi + SCAN_LANES]),
                                 unroll=8)
        state_scr[:, lo:lo + SCAN_LANES] = s_r
        state_scr[:, hi:hi + SCAN_LANES] = s_i

    half = N_STATE // 2
    for n in range(2):
        s_re = st_scr[:, n * half:(n + 1) * half].astype(BF16)
        s_im = st_scr[:, N_STATE + n * half:N_STATE + (n + 1) * half].astype(BF16)
        cols = slice(n * MXU_N, (n + 1) * MXU_N)
        y = _dot(s_re, cre_ref[n]) - _dot(s_im, cim_ref[n]) + d_ref[:, cols] * u[:, cols]
        z_ref[:, cols] = jax.nn.gelu(y).astype(BF16)


def _ssm(u_tm, b_tiles, c_re_tiles, c_im_tiles, a_re, a_im, d_skip, *, bsz):
    rows_total = u_tm.shape[0]
    rows = SCAN_T * bsz
    blk = pl.BlockSpec((rows, SSM_WIDTH), lambda i: (i, 0))
    return pl.pallas_call(
        _ssm_kernel,
        grid=(rows_total // rows,),
        in_specs=[blk, _const_spec(b_tiles.shape), _const_spec(c_re_tiles.shape),
                  _const_spec(c_im_tiles.shape), _const_spec((1, N_STATE)), _const_spec((1, N_STATE)),
                  _const_spec((1, SSM_WIDTH))],
        out_specs=blk,
        out_shape=jax.ShapeDtypeStruct((rows_total, SSM_WIDTH), BF16),
        scratch_shapes=[pltpu.VMEM((rows, 2 * N_STATE), F32), pltpu.VMEM((bsz, 2 * N_STATE), F32)],
        compiler_params=_params("arbitrary"),
        name="ssm",
    )(u_tm, b_tiles, c_re_tiles, c_im_tiles, a_re, a_im, d_skip)


def _route(logits, ltri):
    lane = lax.broadcasted_iota(jnp.int32, logits.shape, 1)
    lanef = lane.astype(F32)
    big = float(ROUTE_W)
    neg = -jnp.inf
    gl = jnp.where(lane < N_GROUPS, logits, neg)
    gmax = jnp.max(gl, axis=-1, keepdims=True)
    grp = jnp.min(jnp.where(gl == gmax, lanef, big), axis=-1, keepdims=True)
    p_grp = 1.0 / jnp.sum(jnp.exp(gl - gmax), axis=-1, keepdims=True)
    first = ROUTE_OFF + grp * EXPERTS_PER_GROUP
    in_grp = (lanef >= first) & (lanef < first + EXPERTS_PER_GROUP)
    el = jnp.where(in_grp, logits, neg)
    v1 = jnp.max(el, axis=-1, keepdims=True)
    i1 = jnp.min(jnp.where(el == v1, lanef, big), axis=-1, keepdims=True)
    el2 = jnp.where(lanef == i1, neg, el)
    v2 = jnp.max(el2, axis=-1, keepdims=True)
    i2 = jnp.min(jnp.where(el2 == v2, lanef, big), axis=-1, keepdims=True)
    e21 = jnp.exp(v2 - v1)
    w1 = p_grp / (1.0 + e21)
    w2 = p_grp * (e21 / (1.0 + e21))
    member = lanef == grp
    before = _dot(ltri, jnp.where(member, 1.0, 0.0).astype(BF16))
    rank = jnp.sum(jnp.where(member, before, 0.0), axis=-1, keepdims=True)
    return (jnp.where(lanef == i1, w1, 0.0) + jnp.where(lanef == i2, w2, 0.0)
            + jnp.where(lane == GRP_LANE, grp, 0.0) + jnp.where(lane == RANK_LANE, rank, 0.0))


def _mix_kernel(x_ref, o_ref, z_ref, gate_ref, wau_ref, wglu_ref, bglu_ref, wsu_ref, wout_ref,
                fg_ref, wrh_ref, wrl_ref, br_ref, ltri_ref, x1_ref, hn_ref, route_ref, route_t_ref):
    y_att = _dot(o_ref[...], wau_ref[...])
    z = z_ref[...].astype(F32)
    z = z * jax.nn.sigmoid(_dot(z_ref[...], wglu_ref[...]) + bglu_ref[...])
    y_ssm = _dot(z.astype(BF16), wsu_ref[...])
    mixed = (gate_ref[:, :D_MODEL].astype(F32) * y_att + gate_ref[:, D_MODEL:].astype(F32) * y_ssm)
    x1 = x_ref[...] + _dot(mixed.astype(BF16), wout_ref[...])
    x1_ref[...] = x1
    ms = jnp.mean(x1 * x1, axis=-1, keepdims=True)
    hn = x1 * lax.rsqrt(ms + EPS) * fg_ref[...]
    hn_hi = hn.astype(BF16)
    hn_ref[...] = hn_hi
    hn_lo = (hn - hn_hi.astype(F32)).astype(BF16)
    logits = (_dot(hn_hi, wrh_ref[...]) + _dot(hn_lo, wrh_ref[...]) + _dot(hn_hi, wrl_ref[...])
              + br_ref[...])
    route = _route(logits, ltri_ref[...])
    route_ref[...] = route
    route_t_ref[...] = route.T[:SUBLANES, :]


def _mix(x2d, o, z_sm, gates, wau, wglu, bglu, wsu, wout, ffn_g, wr_hi, wr_lo, br, ltri, *, seq):
    t = x2d.shape[0]
    n_sb = seq // TOK_TILE
    row = lambda w: pl.BlockSpec((TOK_TILE, w), lambda i: (i, 0))
    return pl.pallas_call(
        _mix_kernel,
        grid=(t // TOK_TILE,),
        in_specs=[row(D_MODEL), row(ATT_V_WIDTH),
                  pl.BlockSpec((TOK_TILE, SSM_WIDTH), lambda i: (i % n_sb, i // n_sb)),
                  row(N_BRANCH * D_MODEL),
                  _const_spec(wau.shape), _const_spec(wglu.shape), _const_spec(bglu.shape),
                  _const_spec(wsu.shape), _const_spec(wout.shape), _const_spec(ffn_g.shape),
                  _const_spec(wr_hi.shape), _const_spec(wr_lo.shape), _const_spec(br.shape),
                  _const_spec(ltri.shape)],
        out_specs=[row(D_MODEL), row(D_MODEL), row(ROUTE_W),
                   pl.BlockSpec((SUBLANES, TOK_TILE), lambda i: (0, i))],
        out_shape=[jax.ShapeDtypeStruct((t, D_MODEL), F32),
                   jax.ShapeDtypeStruct((t, D_MODEL), BF16),
                   jax.ShapeDtypeStruct((t, ROUTE_W), F32),
                   jax.ShapeDtypeStruct((SUBLANES, t), F32)],
        compiler_params=_params("parallel"),
        name="mix",
    )(x2d, o, z_sm, gates, wau, wglu, bglu, wsu, wout, ffn_g, wr_hi, wr_lo, br, ltri)


def _moe_kernel(hn_ref, route_ref, route_t_ref, wg_ref, wu_ref, wd_ref, out_ref, acc_scr):
    n = pl.program_id(0)
    tm = hn_ref.shape[0]
    nf = n.astype(F32)
    route = route_ref[...]
    lane = lax.broadcasted_iota(jnp.int32, route.shape, 1)
    grp_c = jnp.sum(jnp.where(lane == GRP_LANE, route, 0.0), axis=-1, keepdims=True)
    rank_c = jnp.sum(jnp.where(lane == RANK_LANE, route, 0.0), axis=-1, keepdims=True)
    grp_r = route_t_ref[GRP_LANE:GRP_LANE + 1, :]
    rank_r = route_t_ref[RANK_LANE:RANK_LANE + 1, :]
    r1 = route.astype(BF16)
    r2 = (route - r1.astype(F32)).astype(BF16)
    r3 = (route - r1.astype(F32) - r2.astype(F32)).astype(BF16)
    pieces = jnp.concatenate([r1, r2, r3], axis=1)

    acc_scr[...] = jnp.zeros(acc_scr.shape, F32)
    n_rows = jnp.max(jnp.where(grp_c == nf, rank_c + 1.0, 0.0))
    n_pass = lax.div(n_rows.astype(jnp.int32) + (MOE_CAP - 1), MOE_CAP)

    def one_pass(k, carry):
        base = (k * MOE_CAP).astype(F32)
        slot_r = lax.broadcasted_iota(jnp.int32, (MOE_CAP, tm), 0).astype(F32) + base
        take = jnp.where((grp_r == nf) & (rank_r == slot_r), 1.0, 0.0).astype(BF16)
        slot_c = lax.broadcasted_iota(jnp.int32, (tm, MOE_CAP), 1).astype(F32) + base
        put = jnp.where((grp_c == nf) & (rank_c == slot_c), 1.0, 0.0).astype(BF16)
        xc = _dot(take, hn_ref[...]).astype(BF16)
        rc = _dot(take, pieces)
        rc = rc[:, :ROUTE_W] + rc[:, ROUTE_W:2 * ROUTE_W] + rc[:, 2 * ROUTE_W:]
        lane_c = lax.broadcasted_iota(jnp.int32, rc.shape, 1)
        first = ROUTE_OFF + n * EXPERTS_PER_GROUP
        act = jax.nn.silu(_dot(xc, wg_ref[...])) * _dot(xc, wu_ref[...])
        blocks = []
        for j in range(EXPERTS_PER_GROUP):
            w_j = jnp.sum(jnp.where(lane_c == first + j, rc, 0.0), axis=-1, keepdims=True)
            blocks.append((act[:, j * EXPERT_FF:(j + 1) * EXPERT_FF] * w_j).astype(BF16))
        y = _dot(jnp.concatenate(blocks, axis=1), wd_ref[...])
        acc_scr[...] += _dot(put, y.astype(BF16))
        return carry

    lax.fori_loop(0, n_pass, one_pass, 0)
    out_ref[...] = acc_scr[...].astype(BF16)


def _moe(hn, route, route_t, wg, wu, wd):
    t = hn.shape[0]
    row = lambda w: pl.BlockSpec((TOK_TILE, w), lambda n, i: (i, 0))
    grp_w = lambda shape: pl.BlockSpec((None,) + shape, lambda n, i: (n, 0, 0), pipeline_mode=pl.Buffered(1))
    ff = EXPERTS_PER_GROUP * EXPERT_FF
    return pl.pallas_call(
        _moe_kernel,
        grid=(N_GROUPS, t // TOK_TILE),
        in_specs=[row(D_MODEL), row(ROUTE_W), pl.BlockSpec((SUBLANES, TOK_TILE), lambda n, i: (0, i)),
                  grp_w((D_MODEL, ff)), grp_w((D_MODEL, ff)), grp_w((ff, D_MODEL))],
        out_specs=pl.BlockSpec((None, TOK_TILE, D_MODEL), lambda n, i: (n, i, 0)),
        out_shape=jax.ShapeDtypeStruct((N_GROUPS, t, D_MODEL), BF16),
        scratch_shapes=[pltpu.VMEM((TOK_TILE, D_MODEL), F32)],
        compiler_params=_params("arbitrary", "parallel"),
        name="moe",
    )(hn, route, route_t, wg, wu, wd)


def _ple_kernel(x1_ref, c_ref, p_ref, pg_ref, wpg_ref, wple_ref, out_ref):
    x2 = x1_ref[...]
    for n in range(N_GROUPS):
        x2 = x2 + c_ref[n].astype(F32)
    ms = jnp.mean(x2 * x2, axis=-1, keepdims=True)
    hp = (x2 * lax.rsqrt(ms + EPS) * pg_ref[...]).astype(BF16)
    gate = jax.nn.sigmoid(_dot(hp, wpg_ref[...]))
    out_ref[...] = x2 + gate * _dot(p_ref[...].astype(BF16), wple_ref[...])


def _ple(x1, contrib, p2d, ple_g, wpg, wple):
    t = x1.shape[0]
    row = lambda w: pl.BlockSpec((TOK_TILE, w), lambda i: (i, 0))
    return pl.pallas_call(
        _ple_kernel,
        grid=(t // TOK_TILE,),
        in_specs=[row(D_MODEL), pl.BlockSpec((N_GROUPS, TOK_TILE, D_MODEL), lambda i: (0, i, 0)),
                  row(PLE_DIM), _const_spec(ple_g.shape), _const_spec(wpg.shape), _const_spec(wple.shape)],
        out_specs=row(D_MODEL),
        out_shape=jax.ShapeDtypeStruct((t, D_MODEL), F32),
        compiler_params=_params("parallel"),
        name="ple",
    )(x1, contrib, p2d, ple_g, wpg, wple)


def kernel(x, p, attn_norm_g, w_in, b_gate, q_norm_g, k_norm_g, lam_qk, subln_g, w_attn_up, ssm_lam_re, ssm_lam_im, ssm_log_step, ssm_b_re, ssm_b_im, ssm_c_re, ssm_c_im, ssm_d, w_glu, b_glu, w_ssm_up, w_out, ffn_norm_g, w_router_group, b_router_group, w_router_expert, b_router_expert, w_exp_gate, w_exp_up, w_exp_down, ple_norm_g, w_ple_gate, w_ple):
    bsz, seq, d = x.shape
    depth = w_in.shape[0]
    t = bsz * seq
    assert d == D_MODEL and seq % TOK_TILE == 0 and seq % ATT_TILE == 0
    assert seq % SCAN_T == 0 and bsz == SUBLANES
    ff = EXPERTS_PER_GROUP * EXPERT_FF
    tok = jnp.arange(TOK_TILE)
    ltri = (tok[None, :] < tok[:, None]).astype(BF16)

    a_re, a_im, bb_re, bb_im, lam_dyn = _prep(ssm_lam_re, ssm_lam_im, ssm_log_step, ssm_b_re, ssm_b_im, lam_qk)
    head_ids = jnp.arange(COL_CHUNK) // ATT_HEAD_DIM
    gsum = (head_ids[:, None] == head_ids[None, :]).astype(BF16)

    x2d = x.reshape(t, d)
    for i in range(depth):
        lam_init = _lambda_init(i)
        tile_gain = lambda g: jnp.tile(g.reshape(1, HEAD_W), (1, COL_CHUNK // HEAD_W))
        q, k, v, u_sm, gates = _inproj(
            x2d, attn_norm_g[i].reshape(1, d), w_in[i].astype(BF16), gsum,
            tile_gain(q_norm_g[i]), tile_gain(k_norm_g[i]), b_gate[i].reshape(1, -1), seq=seq)

        o = _attention(lam_dyn[i, :1, :1], q, k, v, subln_g[i].reshape(1, HEAD_W), seq=seq, lam_init=lam_init)

        b_tiles, c_re_tiles, c_im_tiles = _ssm_matrices(bb_re[i], bb_im[i], ssm_c_re[i], ssm_c_im[i])
        z_tm = _ssm(u_sm.reshape(seq * bsz, SSM_WIDTH), b_tiles, c_re_tiles, c_im_tiles,
                    a_re[i, ::SSM_GROUP].reshape(1, N_STATE), a_im[i, ::SSM_GROUP].reshape(1, N_STATE),
                    ssm_d[i].reshape(1, SSM_WIDTH), bsz=bsz)

        w_r = jnp.concatenate(
            [w_router_group[i], jnp.transpose(w_router_expert[i], (1, 0, 2)).reshape(d, N_EXPERTS),
             jnp.zeros((d, ROUTE_W - N_GROUPS - N_EXPERTS), F32)], axis=1)
        w_r_hi = w_r.astype(BF16)
        w_r_lo = (w_r - w_r_hi.astype(F32)).astype(BF16)
        b_r = jnp.concatenate([b_router_group[i], b_router_expert[i].reshape(-1),
                               jnp.zeros((ROUTE_W - N_GROUPS - N_EXPERTS,), F32)]).reshape(1, ROUTE_W)
        x1, hn, route, route_t = _mix(
            x2d, o, z_tm.reshape(seq, bsz * SSM_WIDTH), gates,
            w_attn_up[i].astype(BF16), w_glu[i].astype(BF16), b_glu[i].reshape(1, -1),
            w_ssm_up[i].astype(BF16), w_out[i].astype(BF16), ffn_norm_g[i].reshape(1, d),
            w_r_hi, w_r_lo, b_r, ltri, seq=seq)

        wide = lambda w: jnp.transpose(w.astype(BF16), (0, 2, 1, 3)).reshape(N_GROUPS, d, ff)
        contrib = _moe(hn, route, route_t, wide(w_exp_gate[i]), wide(w_exp_up[i]),
                       w_exp_down[i].astype(BF16).reshape(N_GROUPS, ff, d))
        x2d = _ple(x1, contrib, p[i].reshape(t, PLE_DIM), ple_norm_g[i].reshape(1, d),
                   w_ple_gate[i].astype(BF16), w_ple[i].astype(BF16))
    return x2d.reshape(bsz, seq, d)
```

```python
import functools
import math

import jax
import jax.numpy as jnp
from jax import lax
from jax.experimental import pallas as pl
from jax.experimental.pallas import tpu as pltpu

F32 = jnp.float32
BF16 = jnp.bfloat16

D_MODEL = 1024
ATT_HEADS = 8
ATT_HEAD_DIM = 64
HEAD_W = 2 * ATT_HEAD_DIM
ATT_QK_WIDTH = ATT_HEADS * HEAD_W
ATT_V_WIDTH = ATT_HEADS * HEAD_W
SSM_WIDTH = 512
SSM_GROUP = 16
SSM_GROUPS = SSM_WIDTH // SSM_GROUP
SSM_STATE = 64
N_STATE = SSM_GROUPS * SSM_STATE
N_BRANCH = 2
IN_WIDTH = 2 * ATT_QK_WIDTH + ATT_V_WIDTH + SSM_WIDTH + N_BRANCH * D_MODEL
N_GROUPS = 4
EXPERTS_PER_GROUP = 8
N_EXPERTS = N_GROUPS * EXPERTS_PER_GROUP
EXPERT_FF = 256
PLE_DIM = 256
EPS = 1e-6

LANES = 128
SUBLANES = 8
MXU_N = 256
VMEM_LIMIT = 56 * 1024 * 1024

TOK_TILE = 512
MOE_CAP = 160
COL_CHUNK = 512
ATT_TILE = 256
SCAN_T = 64
SCAN_LANES = 512
ROUTE_W = LANES
ROUTE_OFF = N_GROUPS
GRP_LANE = 0
RANK_LANE = 1


def _lambda_init(layer_idx):
    return 0.8 - 0.6 * math.exp(-0.3 * layer_idx)


def _dot(a, b):
    return jnp.dot(a, b, preferred_element_type=F32)


def _const_spec(shape):
    nd = len(shape)
    return pl.BlockSpec(shape, lambda *_: (0,) * nd, pipeline_mode=pl.Buffered(1))


def _params(*sem):
    return pltpu.CompilerParams(dimension_semantics=sem, vmem_limit_bytes=VMEM_LIMIT)


def _prep_kernel(lre_ref, lim_ref, lstep_ref, bre_ref, bim_ref, lqk_ref,
                 are_ref, aim_ref, bbre_ref, bbim_ref, lam_ref):
    lr = lre_ref[...]
    li = lim_ref[...]
    dt = jnp.exp(lstep_ref[...])
    mag = jnp.exp(lr * dt)
    ab_re = mag * jnp.cos(li * dt)
    ab_im = mag * jnp.sin(li * dt)
    den = lr * lr + li * li
    nr = ab_re - 1.0
    coef_re = (nr * lr + ab_im * li) / den
    coef_im = (ab_im * lr - nr * li) / den
    br = bre_ref[...]
    bi = bim_ref[...]
    are_ref[...] = ab_re
    aim_ref[...] = ab_im
    bbre_ref[...] = coef_re * br - coef_im * bi
    bbim_ref[...] = coef_re * bi + coef_im * br
    lq = lqk_ref[...]
    s01 = jnp.sum(lq[0:1] * lq[1:2], axis=-1, keepdims=True)
    s23 = jnp.sum(lq[2:3] * lq[3:4], axis=-1, keepdims=True)
    lam_ref[...] = jnp.broadcast_to(jnp.exp(s01) - jnp.exp(s23), lam_ref.shape)


def _prep(lam_re, lam_im, log_step, b_re, b_im, lam_qk):
    depth = lam_re.shape[0]
    rows = SSM_WIDTH
    rep = lambda a: jnp.repeat(a, SSM_GROUP, axis=1)
    lstep = jnp.broadcast_to(log_step[:, :, None], lam_re.shape)
    tr = lambda b: jnp.transpose(b, (0, 1, 3, 2)).reshape(depth, rows, SSM_STATE)
    blk = pl.BlockSpec((None, rows, SSM_STATE), lambda i: (i, 0, 0))
    out = jax.ShapeDtypeStruct((depth, rows, SSM_STATE), F32)
    return pl.pallas_call(
        _prep_kernel,
        grid=(depth,),
        in_specs=[blk, blk, blk, blk, blk, pl.BlockSpec((None, 4, ATT_HEAD_DIM), lambda i: (i, 0, 0))],
        out_specs=[blk, blk, blk, blk, pl.BlockSpec((None, SUBLANES, LANES), lambda i: (i, 0, 0))],
        out_shape=[out, out, out, out, jax.ShapeDtypeStruct((depth, SUBLANES, LANES), F32)],
        compiler_params=_params("arbitrary"),
        name="prep",
    )(rep(lam_re), rep(lam_im), rep(lstep), tr(b_re), tr(b_im), lam_qk)


def _block_diag(a):
    g, r, c = a.shape
    eye = jnp.eye(g, dtype=a.dtype)
    return jnp.einsum('grc,gk->grkc', a, eye).reshape(g * r, g * c)


def _ssm_matrices(bb_re, bb_im, c_re, c_im):
    g = SSM_GROUPS
    n_re_tiles = N_STATE // MXU_N
    tiles = []
    for full in (_block_diag(bb_re.reshape(g, SSM_GROUP, SSM_STATE)),
                 _block_diag(bb_im.reshape(g, SSM_GROUP, SSM_STATE))):
        for j in range(n_re_tiles):
            k0 = LANES * ((j * MXU_N // SSM_STATE * SSM_GROUP) // LANES)
            tiles.append(full[k0:k0 + LANES, j * MXU_N:(j + 1) * MXU_N])
    b_tiles = jnp.stack(tiles).astype(BF16)
    c_out = []
    for c in (c_re, c_im):
        full = _block_diag(jnp.transpose(c, (0, 2, 1)))
        half = N_STATE // 2
        c_out.append(jnp.stack([full[n * half:(n + 1) * half, n * MXU_N:(n + 1) * MXU_N]
                                for n in range(2)]).astype(BF16))
    return b_tiles, c_out[0], c_out[1]


def _inproj_kernel(x_ref, g_ref, w_ref, gsum_ref, qg_ref, kg_ref, bg_ref,
                   q_ref, k_ref, v_ref, u_ref, gate_ref, *, q_scale):
    xf = x_ref[...]
    ms = jnp.mean(xf * xf, axis=-1, keepdims=True)
    h = (xf * lax.rsqrt(ms + EPS) * g_ref[...]).astype(BF16)

    def head_norm(y, gain):
        sq = y * y
        hi = sq.astype(BF16)
        lo = (sq - hi.astype(F32)).astype(BF16)
        gs = _dot(hi, gsum_ref[...]) + _dot(lo, gsum_ref[...])
        return y * lax.rsqrt(gs * (1.0 / ATT_HEAD_DIM) + EPS) * gain

    n_qk = ATT_QK_WIDTH // COL_CHUNK
    n_v = ATT_V_WIDTH // COL_CHUNK
    for c in range(IN_WIDTH // COL_CHUNK):
        y = _dot(h, w_ref[:, c * COL_CHUNK:(c + 1) * COL_CHUNK])
        if c < n_qk:
            q_ref[:, c * COL_CHUNK:(c + 1) * COL_CHUNK] = (head_norm(y, qg_ref[...]) * q_scale).astype(BF16)
        elif c < 2 * n_qk:
            cc = c - n_qk
            k_ref[:, cc * COL_CHUNK:(cc + 1) * COL_CHUNK] = head_norm(y, kg_ref[...]).astype(BF16)
        elif c < 2 * n_qk + n_v:
            cc = c - 2 * n_qk
            v_ref[:, cc * COL_CHUNK:(cc + 1) * COL_CHUNK] = y.astype(BF16)
        elif c == 2 * n_qk + n_v:
            u_ref[...] = y
        else:
            cc = c - (2 * n_qk + n_v + 1)
            b = bg_ref[:, cc * COL_CHUNK:(cc + 1) * COL_CHUNK]
            gate_ref[:, cc * COL_CHUNK:(cc + 1) * COL_CHUNK] = jax.nn.sigmoid(y + b).astype(BF16)


def _inproj(x2d, norm_g, w_in, gsum, qg, kg, b_gate, *, seq):
    t = x2d.shape[0]
    n_sb = seq // TOK_TILE
    bsz = t // seq
    row = lambda w: pl.BlockSpec((TOK_TILE, w), lambda i: (i, 0))
    q_scale = ATT_HEAD_DIM ** -0.5 * math.log2(math.e)
    return pl.pallas_call(
        functools.partial(_inproj_kernel, q_scale=q_scale),
        grid=(t // TOK_TILE,),
        in_specs=[row(D_MODEL), _const_spec((1, D_MODEL)), _const_spec((D_MODEL, IN_WIDTH)),
                  _const_spec((COL_CHUNK, COL_CHUNK)), _const_spec((1, COL_CHUNK)),
                  _const_spec((1, COL_CHUNK)), _const_spec((1, N_BRANCH * D_MODEL))],
        out_specs=[row(ATT_QK_WIDTH), row(ATT_QK_WIDTH), row(ATT_V_WIDTH),
                   pl.BlockSpec((TOK_TILE, SSM_WIDTH), lambda i: (i % n_sb, i // n_sb)),
                   row(N_BRANCH * D_MODEL)],
        out_shape=[jax.ShapeDtypeStruct((t, ATT_QK_WIDTH), BF16),
                   jax.ShapeDtypeStruct((t, ATT_QK_WIDTH), BF16),
                   jax.ShapeDtypeStruct((t, ATT_V_WIDTH), BF16),
                   jax.ShapeDtypeStruct((seq, bsz * SSM_WIDTH), F32),
                   jax.ShapeDtypeStruct((t, N_BRANCH * D_MODEL), BF16)],
        compiler_params=_params("parallel"),
        name="inproj",
    )(x2d, norm_g, w_in, gsum, qg, kg, b_gate)


def _attn_kernel(lam_ref, q_ref, k_ref, v_ref, g_ref, o_ref, vt_scr, *, lam_init):
    tq = ATT_TILE
    seq = k_ref.shape[0]
    nq = seq // tq
    lam = lam_ref[0, 0] + lam_init
    for c in range(nq):
        vt_scr[:, c * tq:(c + 1) * tq] = v_ref[c * tq:(c + 1) * tq, :].astype(F32).T.astype(BF16)
    head_row = lax.broadcasted_iota(jnp.int32, (HEAD_W, tq), 0)
    key_pos = lax.broadcasted_iota(jnp.int32, (tq, 2 * tq), 0)
    qry_pos = lax.broadcasted_iota(jnp.int32, (tq, 2 * tq), 1) & (tq - 1)
    causal = key_pos <= qry_pos

    for qi in range(nq):
        qt = q_ref[qi * tq:(qi + 1) * tq, :].astype(F32).T
        qst = jnp.concatenate([jnp.where(head_row < ATT_HEAD_DIM, qt, 0.0),
                               jnp.where(head_row >= ATT_HEAD_DIM, qt, 0.0)], axis=1).astype(BF16)
        nk = qi * tq
        s_d = jnp.where(causal, _dot(k_ref[nk:nk + tq, :], qst), -jnp.inf)
        m = jnp.max(s_d, axis=0, keepdims=True)
        if qi > 0:
            s_m = _dot(k_ref[0:nk, :], qst)
            m = jnp.maximum(m, jnp.max(s_m, axis=0, keepdims=True))
        p_d = jnp.exp2(s_d - m)
        l = jnp.sum(p_d, axis=0, keepdims=True)
        acc = _dot(vt_scr[:, nk:nk + tq], p_d.astype(BF16))
        if qi > 0:
            p_m = jnp.exp2(s_m - m)
            l = l + jnp.sum(p_m, axis=0, keepdims=True)
            acc = acc + _dot(vt_scr[:, 0:nk], p_m.astype(BF16))
        accn = acc * (1.0 / l)
        ot = accn[:, :tq] - lam * accn[:, tq:]
        ms = jnp.mean(ot * ot, axis=0, keepdims=True)
        o = (ot * lax.rsqrt(ms + EPS)).T * g_ref[...] * (1.0 - lam_init)
        o_ref[qi * tq:(qi + 1) * tq, :] = o.astype(BF16)


def _attention(lam, q, k, v, subln_g, *, seq, lam_init):
    t = q.shape[0]
    bsz = t // seq
    blk = pl.BlockSpec((seq, HEAD_W), lambda b, h: (b, h))
    return pl.pallas_call(
        functools.partial(_attn_kernel, lam_init=lam_init),
        grid=(bsz, ATT_HEADS),
        in_specs=[pl.BlockSpec(memory_space=pltpu.SMEM), blk, blk, blk, _const_spec((1, HEAD_W))],
        out_specs=blk,
        out_shape=jax.ShapeDtypeStruct((t, ATT_V_WIDTH), BF16),
        scratch_shapes=[pltpu.VMEM((HEAD_W, seq), BF16)],
        compiler_params=_params("parallel", "parallel"),
        name="attn",
    )(lam, q, k, v, subln_g)


def _ssm_kernel(u_ref, bt_ref, cre_ref, cim_ref, are_ref, aim_ref, d_ref, z_ref, st_scr, state_scr):
    rows = u_ref.shape[0]
    bsz = state_scr.shape[0]

    @pl.when(pl.program_id(0) == 0)
    def _():
        state_scr[...] = jnp.zeros(state_scr.shape, F32)

    u = u_ref[...]
    ub = u.astype(BF16)
    n_tiles = 2 * N_STATE // MXU_N
    for j in range(n_tiles):
        k0 = LANES * (((j % (n_tiles // 2)) * MXU_N // SSM_STATE * SSM_GROUP) // LANES)
        st_scr[:, j * MXU_N:(j + 1) * MXU_N] = _dot(ub[:, k0:k0 + LANES], bt_ref[j])

    for cc in range(N_STATE // SCAN_LANES):
        lo = cc * SCAN_LANES
        hi = N_STATE + lo
        a_r = jnp.broadcast_to(are_ref[:, lo:lo + SCAN_LANES], (bsz, SCAN_LANES))
        a_i = jnp.broadcast_to(aim_ref[:, lo:lo + SCAN_LANES], (bsz, SCAN_LANES))

        def step(t, carry, lo=lo, hi=hi, a_r=a_r, a_i=a_i):
            s_r, s_i = carry
            r0 = pl.multiple_of(t * bsz, bsz)
            n_r = a_r * s_r - a_i * s_i + st_scr[pl.ds(r0, bsz), lo:lo + SCAN_LANES]
            n_i = a_r * s_i + a_i * s_r + st_scr[pl.ds(r0, bsz), hi:hi + SCAN_LANES]
            st_scr[pl.ds(r0, bsz), lo:lo + SCAN_LANES] = n_r
            st_scr[pl.ds(r0, bsz), hi:hi + SCAN_LANES] = n_i
            return n_r, n_i

        s_r, s_i = lax.fori_loop(0, rows // bsz, step,
                                 (state_scr[:, lo:lo + SCAN_LANES], state_scr[:, hi:hi + SCAN_LANES]),
                                 unroll=8)
        state_scr[:, lo:lo + SCAN_LANES] = s_r
        state_scr[:, hi:hi + SCAN_LANES] = s_i

    half = N_STATE // 2
    for n in range(2):
        s_re = st_scr[:, n * half:(n + 1) * half].astype(BF16)
        s_im = st_scr[:, N_STATE + n * half:N_STATE + (n + 1) * half].astype(BF16)
        cols = slice(n * MXU_N, (n + 1) * MXU_N)
        y = _dot(s_re, cre_ref[n]) - _dot(s_im, cim_ref[n]) + d_ref[:, cols] * u[:, cols]
        z_ref[:, cols] = jax.nn.gelu(y).astype(BF16)


def _ssm(u_tm, b_tiles, c_re_tiles, c_im_tiles, a_re, a_im, d_skip, *, bsz):
    rows_total = u_tm.shape[0]
    rows = SCAN_T * bsz
    blk = pl.BlockSpec((rows, SSM_WIDTH), lambda i: (i, 0))
    return pl.pallas_call(
        _ssm_kernel,
        grid=(rows_total // rows,),
        in_specs=[blk, _const_spec(b_tiles.shape), _const_spec(c_re_tiles.shape),
                  _const_spec(c_im_tiles.shape), _const_spec((1, N_STATE)), _const_spec((1, N_STATE)),
                  _const_spec((1, SSM_WIDTH))],
        out_specs=blk,
        out_shape=jax.ShapeDtypeStruct((rows_total, SSM_WIDTH), BF16),
        scratch_shapes=[pltpu.VMEM((rows, 2 * N_STATE), F32), pltpu.VMEM((bsz, 2 * N_STATE), F32)],
        compiler_params=_params("arbitrary"),
        name="ssm",
    )(u_tm, b_tiles, c_re_tiles, c_im_tiles, a_re, a_im, d_skip)


def _route(logits, ltri):
    lane = lax.broadcasted_iota(jnp.int32, logits.shape, 1)
    lanef = lane.astype(F32)
    big = float(ROUTE_W)
    neg = -jnp.inf
    gl = jnp.where(lane < N_GROUPS, logits, neg)
    gmax = jnp.max(gl, axis=-1, keepdims=True)
    grp = jnp.min(jnp.where(gl == gmax, lanef, big), axis=-1, keepdims=True)
    p_grp = 1.0 / jnp.sum(jnp.exp(gl - gmax), axis=-1, keepdims=True)
    first = ROUTE_OFF + grp * EXPERTS_PER_GROUP
    in_grp = (lanef >= first) & (lanef < first + EXPERTS_PER_GROUP)
    el = jnp.where(in_grp, logits, neg)
    v1 = jnp.max(el, axis=-1, keepdims=True)
    i1 = jnp.min(jnp.where(el == v1, lanef, big), axis=-1, keepdims=True)
    el2 = jnp.where(lanef == i1, neg, el)
    v2 = jnp.max(el2, axis=-1, keepdims=True)
    i2 = jnp.min(jnp.where(el2 == v2, lanef, big), axis=-1, keepdims=True)
    e21 = jnp.exp(v2 - v1)
    w1 = p_grp / (1.0 + e21)
    w2 = p_grp * (e21 / (1.0 + e21))
    member = lanef == grp
    before = _dot(ltri, jnp.where(member, 1.0, 0.0).astype(BF16))
    rank = jnp.sum(jnp.where(member, before, 0.0), axis=-1, keepdims=True)
    return (jnp.where(lanef == i1, w1, 0.0) + jnp.where(lanef == i2, w2, 0.0)
            + jnp.where(lane == GRP_LANE, grp, 0.0) + jnp.where(lane == RANK_LANE, rank, 0.0))


def _mix_kernel(x_ref, o_ref, z_ref, gate_ref, wau_ref, wglu_ref, bglu_ref, wsu_ref, wout_ref,
                fg_ref, wrh_ref, wrl_ref, br_ref, ltri_ref, x1_ref, hn_ref, route_ref, route_t_ref):
    y_att = _dot(o_ref[...], wau_ref[...])
    z = z_ref[...].astype(F32)
    z = z * jax.nn.sigmoid(_dot(z_ref[...], wglu_ref[...]) + bglu_ref[...])
    y_ssm = _dot(z.astype(BF16), wsu_ref[...])
    mixed = (gate_ref[:, :D_MODEL].astype(F32) * y_att + gate_ref[:, D_MODEL:].astype(F32) * y_ssm)
    x1 = x_ref[...] + _dot(mixed.astype(BF16), wout_ref[...])
    x1_ref[...] = x1
    ms = jnp.mean(x1 * x1, axis=-1, keepdims=True)
    hn = x1 * lax.rsqrt(ms + EPS) * fg_ref[...]
    hn_hi = hn.astype(BF16)
    hn_ref[...] = hn_hi
    hn_lo = (hn - hn_hi.astype(F32)).astype(BF16)
    logits = (_dot(hn_hi, wrh_ref[...]) + _dot(hn_lo, wrh_ref[...]) + _dot(hn_hi, wrl_ref[...])
              + br_ref[...])
    route = _route(logits, ltri_ref[...])
    route_ref[...] = route
    route_t_ref[...] = route.T[:SUBLANES, :]


def _mix(x2d, o, z_sm, gates, wau, wglu, bglu, wsu, wout, ffn_g, wr_hi, wr_lo, br, ltri, *, seq):
    t = x2d.shape[0]
    n_sb = seq // TOK_TILE
    row = lambda w: pl.BlockSpec((TOK_TILE, w), lambda i: (i, 0))
    return pl.pallas_call(
        _mix_kernel,
        grid=(t // TOK_TILE,),
        in_specs=[row(D_MODEL), row(ATT_V_WIDTH),
                  pl.BlockSpec((TOK_TILE, SSM_WIDTH), lambda i: (i % n_sb, i // n_sb)),
                  row(N_BRANCH * D_MODEL),
                  _const_spec(wau.shape), _const_spec(wglu.shape), _const_spec(bglu.shape),
                  _const_spec(wsu.shape), _const_spec(wout.shape), _const_spec(ffn_g.shape),
                  _const_spec(wr_hi.shape), _const_spec(wr_lo.shape), _const_spec(br.shape),
                  _const_spec(ltri.shape)],
        out_specs=[row(D_MODEL), row(D_MODEL), row(ROUTE_W),
                   pl.BlockSpec((SUBLANES, TOK_TILE), lambda i: (0, i))],
        out_shape=[jax.ShapeDtypeStruct((t, D_MODEL), F32),
                   jax.ShapeDtypeStruct((t, D_MODEL), BF16),
                   jax.ShapeDtypeStruct((t, ROUTE_W), F32),
                   jax.ShapeDtypeStruct((SUBLANES, t), F32)],
        compiler_params=_params("parallel"),
        name="mix",
    )(x2d, o, z_sm, gates, wau, wglu, bglu, wsu, wout, ffn_g, wr_hi, wr_lo, br, ltri)


def _moe_kernel(hn_ref, route_ref, route_t_ref, wg_ref, wu_ref, wd_ref, out_ref, acc_scr, wg_scr, wu_scr, wd_scr):
    n = pl.program_id(0)

    @pl.when(pl.program_id(1) == 0)
    def _():
        for j in range(EXPERTS_PER_GROUP):
            cols = slice(j * EXPERT_FF, (j + 1) * EXPERT_FF)
            wg_scr[:, cols] = wg_ref[j].astype(BF16)
            wu_scr[:, cols] = wu_ref[j].astype(BF16)
            wd_scr[cols, :] = wd_ref[j].astype(BF16)

    tm = hn_ref.shape[0]
    nf = n.astype(F32)
    route = route_ref[...]
    lane = lax.broadcasted_iota(jnp.int32, route.shape, 1)
    grp_c = jnp.sum(jnp.where(lane == GRP_LANE, route, 0.0), axis=-1, keepdims=True)
    rank_c = jnp.sum(jnp.where(lane == RANK_LANE, route, 0.0), axis=-1, keepdims=True)
    grp_r = route_t_ref[GRP_LANE:GRP_LANE + 1, :]
    rank_r = route_t_ref[RANK_LANE:RANK_LANE + 1, :]
    r1 = route.astype(BF16)
    r2 = (route - r1.astype(F32)).astype(BF16)
    r3 = (route - r1.astype(F32) - r2.astype(F32)).astype(BF16)
    pieces = jnp.concatenate([r1, r2, r3], axis=1)

    acc_scr[...] = jnp.zeros(acc_scr.shape, F32)
    n_rows = jnp.max(jnp.where(grp_c == nf, rank_c + 1.0, 0.0))
    n_pass = lax.div(n_rows.astype(jnp.int32) + (MOE_CAP - 1), MOE_CAP)

    def one_pass(k, carry):
        base = (k * MOE_CAP).astype(F32)
        slot_r = lax.broadcasted_iota(jnp.int32, (MOE_CAP, tm), 0).astype(F32) + base
        take = jnp.where((grp_r == nf) & (rank_r == slot_r), 1.0, 0.0).astype(BF16)
        slot_c = lax.broadcasted_iota(jnp.int32, (tm, MOE_CAP), 1).astype(F32) + base
        put = jnp.where((grp_c == nf) & (rank_c == slot_c), 1.0, 0.0).astype(BF16)
        xc = _dot(take, hn_ref[...]).astype(BF16)
        rc = _dot(take, pieces)
        rc = rc[:, :ROUTE_W] + rc[:, ROUTE_W:2 * ROUTE_W] + rc[:, 2 * ROUTE_W:]
        lane_c = lax.broadcasted_iota(jnp.int32, rc.shape, 1)
        first = ROUTE_OFF + n * EXPERTS_PER_GROUP
        act = jax.nn.silu(_dot(xc, wg_scr[...])) * _dot(xc, wu_scr[...])
        blocks = []
        for j in range(EXPERTS_PER_GROUP):
            w_j = jnp.sum(jnp.where(lane_c == first + j, rc, 0.0), axis=-1, keepdims=True)
            blocks.append((act[:, j * EXPERT_FF:(j + 1) * EXPERT_FF] * w_j).astype(BF16))
        y = _dot(jnp.concatenate(blocks, axis=1), wd_scr[...])
        acc_scr[...] += _dot(put, y.astype(BF16))
        return carry

    lax.fori_loop(0, n_pass, one_pass, 0)
    out_ref[...] = acc_scr[...].astype(BF16)


def _moe(hn, route, route_t, w_gate, w_up, w_down, *, layer):
    t = hn.shape[0]
    row = lambda w: pl.BlockSpec((TOK_TILE, w), lambda n, i: (i, 0))
    grp_w = lambda shape: pl.BlockSpec((None, None) + shape, lambda n, i: (layer, n, 0, 0, 0),
                                       pipeline_mode=pl.Buffered(1))
    up_shape = (EXPERTS_PER_GROUP, D_MODEL, EXPERT_FF)
    down_shape = (EXPERTS_PER_GROUP, EXPERT_FF, D_MODEL)
    ff = EXPERTS_PER_GROUP * EXPERT_FF
    return pl.pallas_call(
        _moe_kernel,
        grid=(N_GROUPS, t // TOK_TILE),
        in_specs=[row(D_MODEL), row(ROUTE_W), pl.BlockSpec((SUBLANES, TOK_TILE), lambda n, i: (0, i)),
                  grp_w(up_shape), grp_w(up_shape), grp_w(down_shape)],
        out_specs=pl.BlockSpec((None, TOK_TILE, D_MODEL), lambda n, i: (n, i, 0)),
        out_shape=jax.ShapeDtypeStruct((N_GROUPS, t, D_MODEL), BF16),
        scratch_shapes=[pltpu.VMEM((TOK_TILE, D_MODEL), F32), pltpu.VMEM((D_MODEL, ff), BF16),
                        pltpu.VMEM((D_MODEL, ff), BF16), pltpu.VMEM((ff, D_MODEL), BF16)],
        compiler_params=_params("arbitrary", "arbitrary"),
        name="moe",
    )(hn, route, route_t, w_gate, w_up, w_down)


def _ple_kernel(x1_ref, c_ref, p_ref, pg_ref, wpg_ref, wple_ref, out_ref):
    x2 = x1_ref[...]
    for n in range(N_GROUPS):
        x2 = x2 + c_ref[n].astype(F32)
    ms = jnp.mean(x2 * x2, axis=-1, keepdims=True)
    hp = (x2 * lax.rsqrt(ms + EPS) * pg_ref[...]).astype(BF16)
    gate = jax.nn.sigmoid(_dot(hp, wpg_ref[...]))
    out_ref[...] = x2 + gate * _dot(p_ref[...].astype(BF16), wple_ref[...])


def _ple(x1, contrib, p2d, ple_g, wpg, wple):
    t = x1.shape[0]
    row = lambda w: pl.BlockSpec((TOK_TILE, w), lambda i: (i, 0))
    return pl.pallas_call(
        _ple_kernel,
        grid=(t // TOK_TILE,),
        in_specs=[row(D_MODEL), pl.BlockSpec((N_GROUPS, TOK_TILE, D_MODEL), lambda i: (0, i, 0)),
                  row(PLE_DIM), _const_spec(ple_g.shape), _const_spec(wpg.shape), _const_spec(wple.shape)],
        out_specs=row(D_MODEL),
        out_shape=jax.ShapeDtypeStruct((t, D_MODEL), F32),
        compiler_params=_params("parallel"),
        name="ple",
    )(x1, contrib, p2d, ple_g, wpg, wple)


def kernel(x, p, attn_norm_g, w_in, b_gate, q_norm_g, k_norm_g, lam_qk, subln_g, w_attn_up, ssm_lam_re, ssm_lam_im, ssm_log_step, ssm_b_re, ssm_b_im, ssm_c_re, ssm_c_im, ssm_d, w_glu, b_glu, w_ssm_up, w_out, ffn_norm_g, w_router_group, b_router_group, w_router_expert, b_router_expert, w_exp_gate, w_exp_up, w_exp_down, ple_norm_g, w_ple_gate, w_ple):
    bsz, seq, d = x.shape
    depth = w_in.shape[0]
    t = bsz * seq
    assert d == D_MODEL and seq % TOK_TILE == 0 and seq % ATT_TILE == 0
    assert seq % SCAN_T == 0 and bsz == SUBLANES
    tok = jnp.arange(TOK_TILE)
    ltri = (tok[None, :] < tok[:, None]).astype(BF16)

    a_re, a_im, bb_re, bb_im, lam_dyn = _prep(ssm_lam_re, ssm_lam_im, ssm_log_step, ssm_b_re, ssm_b_im, lam_qk)
    head_ids = jnp.arange(COL_CHUNK) // ATT_HEAD_DIM
    gsum = (head_ids[:, None] == head_ids[None, :]).astype(BF16)

    x2d = x.reshape(t, d)
    for i in range(depth):
        lam_init = _lambda_init(i)
        tile_gain = lambda g: jnp.tile(g.reshape(1, HEAD_W), (1, COL_CHUNK // HEAD_W))
        q, k, v, u_sm, gates = _inproj(
            x2d, attn_norm_g[i].reshape(1, d), w_in[i].astype(BF16), gsum,
            tile_gain(q_norm_g[i]), tile_gain(k_norm_g[i]), b_gate[i].reshape(1, -1), seq=seq)

        o = _attention(lam_dyn[i, :1, :1], q, k, v, subln_g[i].reshape(1, HEAD_W), seq=seq, lam_init=lam_init)

        b_tiles, c_re_tiles, c_im_tiles = _ssm_matrices(bb_re[i], bb_im[i], ssm_c_re[i], ssm_c_im[i])
        z_tm = _ssm(u_sm.reshape(seq * bsz, SSM_WIDTH), b_tiles, c_re_tiles, c_im_tiles,
                    a_re[i, ::SSM_GROUP].reshape(1, N_STATE), a_im[i, ::SSM_GROUP].reshape(1, N_STATE),
                    ssm_d[i].reshape(1, SSM_WIDTH), bsz=bsz)

        w_r = jnp.concatenate(
            [w_router_group[i], jnp.transpose(w_router_expert[i], (1, 0, 2)).reshape(d, N_EXPERTS),
             jnp.zeros((d, ROUTE_W - N_GROUPS - N_EXPERTS), F32)], axis=1)
        w_r_hi = w_r.astype(BF16)
        w_r_lo = (w_r - w_r_hi.astype(F32)).astype(BF16)
        b_r = jnp.concatenate([b_router_group[i], b_router_expert[i].reshape(-1),
                               jnp.zeros((ROUTE_W - N_GROUPS - N_EXPERTS,), F32)]).reshape(1, ROUTE_W)
        x1, hn, route, route_t = _mix(
            x2d, o, z_tm.reshape(seq, bsz * SSM_WIDTH), gates,
            w_attn_up[i].astype(BF16), w_glu[i].astype(BF16), b_glu[i].reshape(1, -1),
            w_ssm_up[i].astype(BF16), w_out[i].astype(BF16), ffn_norm_g[i].reshape(1, d),
            w_r_hi, w_r_lo, b_r, ltri, seq=seq)

        contrib = _moe(hn, route, route_t, w_exp_gate, w_exp_up, w_exp_down, layer=i)
        x2d = _ple(x1, contrib, p[i].reshape(t, PLE_DIM), ple_norm_g[i].reshape(1, d),
                   w_ple_gate[i].astype(BF16), w_ple[i].astype(BF16))
    return x2d.reshape(bsz, seq, d)
```

```python
import functools
import math

import jax
import jax.numpy as jnp
from jax import lax
from jax.experimental import pallas as pl
from jax.experimental.pallas import tpu as pltpu

F32 = jnp.float32
BF16 = jnp.bfloat16

D_MODEL = 1024
ATT_HEADS = 8
ATT_HEAD_DIM = 64
HEAD_W = 2 * ATT_HEAD_DIM
ATT_QK_WIDTH = ATT_HEADS * HEAD_W
ATT_V_WIDTH = ATT_HEADS * HEAD_W
SSM_WIDTH = 512
SSM_GROUP = 16
SSM_GROUPS = SSM_WIDTH // SSM_GROUP
SSM_STATE = 64
N_STATE = SSM_GROUPS * SSM_STATE
N_BRANCH = 2
IN_WIDTH = 2 * ATT_QK_WIDTH + ATT_V_WIDTH + SSM_WIDTH + N_BRANCH * D_MODEL
N_GROUPS = 4
EXPERTS_PER_GROUP = 8
N_EXPERTS = N_GROUPS * EXPERTS_PER_GROUP
EXPERT_FF = 256
PLE_DIM = 256
EPS = 1e-6

LANES = 128
SUBLANES = 8
MXU_N = 256
VMEM_LIMIT = 56 * 1024 * 1024

TOK_TILE = 512
MOE_CAP = 160
COL_CHUNK = 512
ATT_TILE = 256
Q_SCALE = ATT_HEAD_DIM ** -0.5 * math.log2(math.e)
LAM_ROW = 0
BOUND_ROW = 1
ATT_BOUND_SLACK = 1.01
ATT_BOUND_LIMIT = 60.0
SCAN_T = 64
SCAN_LANES = 512
ROUTE_W = LANES
ROUTE_OFF = N_GROUPS
GRP_LANE = 0
RANK_LANE = 1


def _lambda_init(layer_idx):
    return 0.8 - 0.6 * math.exp(-0.3 * layer_idx)


def _dot(a, b):
    return jnp.dot(a, b, preferred_element_type=F32)


def _const_spec(shape):
    nd = len(shape)
    return pl.BlockSpec(shape, lambda *_: (0,) * nd, pipeline_mode=pl.Buffered(1))


def _params(*sem):
    return pltpu.CompilerParams(dimension_semantics=sem, vmem_limit_bytes=VMEM_LIMIT)


def _prep_kernel(lre_ref, lim_ref, lstep_ref, bre_ref, bim_ref, lqk_ref, qg_ref, kg_ref,
                 are_ref, aim_ref, bbre_ref, bbim_ref, lam_ref):
    lr = lre_ref[...]
    li = lim_ref[...]
    dt = jnp.exp(lstep_ref[...])
    mag = jnp.exp(lr * dt)
    ab_re = mag * jnp.cos(li * dt)
    ab_im = mag * jnp.sin(li * dt)
    den = lr * lr + li * li
    nr = ab_re - 1.0
    coef_re = (nr * lr + ab_im * li) / den
    coef_im = (ab_im * lr - nr * li) / den
    br = bre_ref[...]
    bi = bim_ref[...]
    are_ref[...] = ab_re
    aim_ref[...] = ab_im
    bbre_ref[...] = coef_re * br - coef_im * bi
    bbim_ref[...] = coef_re * bi + coef_im * br
    lq = lqk_ref[...]
    s01 = jnp.sum(lq[0:1] * lq[1:2], axis=-1, keepdims=True)
    s23 = jnp.sum(lq[2:3] * lq[3:4], axis=-1, keepdims=True)
    lam_dyn = jnp.exp(s01) - jnp.exp(s23)
    g_max = jnp.max(jnp.abs(qg_ref[...]), keepdims=True) * jnp.max(jnp.abs(kg_ref[...]), keepdims=True)
    bound = g_max * (ATT_HEAD_DIM * Q_SCALE * ATT_BOUND_SLACK)
    row = lax.broadcasted_iota(jnp.int32, lam_ref.shape, 0)
    lam_ref[...] = jnp.where(row == LAM_ROW, lam_dyn, bound)


def _prep(lam_re, lam_im, log_step, b_re, b_im, lam_qk, q_gain, k_gain):
    depth = lam_re.shape[0]
    rows = SSM_WIDTH
    rep = lambda a: jnp.repeat(a, SSM_GROUP, axis=1)
    lstep = jnp.broadcast_to(log_step[:, :, None], lam_re.shape)
    tr = lambda b: jnp.transpose(b, (0, 1, 3, 2)).reshape(depth, rows, SSM_STATE)
    blk = pl.BlockSpec((None, rows, SSM_STATE), lambda i: (i, 0, 0))
    out = jax.ShapeDtypeStruct((depth, rows, SSM_STATE), F32)
    return pl.pallas_call(
        _prep_kernel,
        grid=(depth,),
        in_specs=[blk, blk, blk, blk, blk, pl.BlockSpec((None, 4, ATT_HEAD_DIM), lambda i: (i, 0, 0)),
                  pl.BlockSpec((None, 2, ATT_HEAD_DIM), lambda i: (i, 0, 0)),
                  pl.BlockSpec((None, 2, ATT_HEAD_DIM), lambda i: (i, 0, 0))],
        out_specs=[blk, blk, blk, blk, pl.BlockSpec((None, SUBLANES, LANES), lambda i: (i, 0, 0))],
        out_shape=[out, out, out, out, jax.ShapeDtypeStruct((depth, SUBLANES, LANES), F32)],
        compiler_params=_params("arbitrary"),
        name="prep",
    )(rep(lam_re), rep(lam_im), rep(lstep), tr(b_re), tr(b_im), lam_qk, q_gain, k_gain)


def _block_diag(a):
    g, r, c = a.shape
    eye = jnp.eye(g, dtype=a.dtype)
    return jnp.einsum('grc,gk->grkc', a, eye).reshape(g * r, g * c)


def _ssm_matrices(bb_re, bb_im, c_re, c_im):
    g = SSM_GROUPS
    n_re_tiles = N_STATE // MXU_N
    tiles = []
    for full in (_block_diag(bb_re.reshape(g, SSM_GROUP, SSM_STATE)),
                 _block_diag(bb_im.reshape(g, SSM_GROUP, SSM_STATE))):
        for j in range(n_re_tiles):
            k0 = LANES * ((j * MXU_N // SSM_STATE * SSM_GROUP) // LANES)
            tiles.append(full[k0:k0 + LANES, j * MXU_N:(j + 1) * MXU_N])
    b_tiles = jnp.stack(tiles).astype(BF16)
    c_out = []
    for c in (c_re, c_im):
        full = _block_diag(jnp.transpose(c, (0, 2, 1)))
        half = N_STATE // 2
        c_out.append(jnp.stack([full[n * half:(n + 1) * half, n * MXU_N:(n + 1) * MXU_N]
                                for n in range(2)]).astype(BF16))
    return b_tiles, c_out[0], c_out[1]


def _inproj_kernel(x_ref, g_ref, w_ref, gsum_ref, qg_ref, kg_ref, bg_ref,
                   qt_ref, k_ref, vt_ref, u_ref, gate_ref):
    xf = x_ref[...]
    ms = jnp.mean(xf * xf, axis=-1, keepdims=True)
    h = (xf * lax.rsqrt(ms + EPS) * g_ref[...]).astype(BF16)

    def head_norm(y, gain):
        sq = y * y
        hi = sq.astype(BF16)
        lo = (sq - hi.astype(F32)).astype(BF16)
        gs = _dot(hi, gsum_ref[...]) + _dot(lo, gsum_ref[...])
        return y * lax.rsqrt(gs * (1.0 / ATT_HEAD_DIM) + EPS) * gain

    n_qk = ATT_QK_WIDTH // COL_CHUNK
    n_v = ATT_V_WIDTH // COL_CHUNK
    for c in range(IN_WIDTH // COL_CHUNK):
        y = _dot(h, w_ref[:, c * COL_CHUNK:(c + 1) * COL_CHUNK])
        if c < n_qk:
            qt_ref[c * COL_CHUNK:(c + 1) * COL_CHUNK, :] = (head_norm(y, qg_ref[...]) * Q_SCALE).T.astype(BF16)
        elif c < 2 * n_qk:
            cc = c - n_qk
            k_ref[:, cc * COL_CHUNK:(cc + 1) * COL_CHUNK] = head_norm(y, kg_ref[...]).astype(BF16)
        elif c < 2 * n_qk + n_v:
            cc = c - 2 * n_qk
            vt_ref[cc * COL_CHUNK:(cc + 1) * COL_CHUNK, :] = y.T.astype(BF16)
        elif c == 2 * n_qk + n_v:
            u_ref[...] = y
        else:
            cc = c - (2 * n_qk + n_v + 1)
            b = bg_ref[:, cc * COL_CHUNK:(cc + 1) * COL_CHUNK]
            gate_ref[:, cc * COL_CHUNK:(cc + 1) * COL_CHUNK] = jax.nn.sigmoid(y + b).astype(BF16)


def _inproj(x2d, norm_g, w_in, gsum, qg, kg, b_gate, *, seq):
    t = x2d.shape[0]
    n_sb = seq // TOK_TILE
    bsz = t // seq
    row = lambda w: pl.BlockSpec((TOK_TILE, w), lambda i: (i, 0))
    col = lambda w: pl.BlockSpec((w, TOK_TILE), lambda i: (i // n_sb, i % n_sb))
    return pl.pallas_call(
        _inproj_kernel,
        grid=(t // TOK_TILE,),
        in_specs=[row(D_MODEL), _const_spec((1, D_MODEL)), _const_spec((D_MODEL, IN_WIDTH)),
                  _const_spec((COL_CHUNK, COL_CHUNK)), _const_spec((1, COL_CHUNK)),
                  _const_spec((1, COL_CHUNK)), _const_spec((1, N_BRANCH * D_MODEL))],
        out_specs=[col(ATT_QK_WIDTH), row(ATT_QK_WIDTH), col(ATT_V_WIDTH),
                   pl.BlockSpec((TOK_TILE, SSM_WIDTH), lambda i: (i % n_sb, i // n_sb)),
                   row(N_BRANCH * D_MODEL)],
        out_shape=[jax.ShapeDtypeStruct((bsz * ATT_QK_WIDTH, seq), BF16),
                   jax.ShapeDtypeStruct((t, ATT_QK_WIDTH), BF16),
                   jax.ShapeDtypeStruct((bsz * ATT_V_WIDTH, seq), BF16),
                   jax.ShapeDtypeStruct((seq, bsz * SSM_WIDTH), F32),
                   jax.ShapeDtypeStruct((t, N_BRANCH * D_MODEL), BF16)],
        compiler_params=_params("parallel"),
        name="inproj",
    )(x2d, norm_g, w_in, gsum, qg, kg, b_gate)


def _attn_kernel(sc_ref, qt_ref, k_ref, vt_ref, g_ref, o_ref, *, lam_init):
    tq = ATT_TILE
    seq = k_ref.shape[0]
    nq = seq // tq
    lam = sc_ref[LAM_ROW, 0] + lam_init
    bound = sc_ref[BOUND_ROW, 0]
    head_row = lax.broadcasted_iota(jnp.int32, (HEAD_W, tq), 0)
    key_pos = lax.broadcasted_iota(jnp.int32, (tq, 2 * tq), 0)
    qry_pos = lax.broadcasted_iota(jnp.int32, (tq, 2 * tq), 1) & (tq - 1)
    causal = key_pos <= qry_pos
    zero = jnp.zeros((HEAD_W, tq), BF16)

    def tile(qi, shift_by_bound):
        qt = qt_ref[:, qi * tq:(qi + 1) * tq]
        qst = jnp.concatenate([jnp.where(head_row < ATT_HEAD_DIM, qt, zero),
                               jnp.where(head_row >= ATT_HEAD_DIM, qt, zero)], axis=1)
        nk = qi * tq
        s_d = jnp.where(causal, _dot(k_ref[nk:nk + tq, :], qst), -jnp.inf)
        if qi > 0:
            s_m = _dot(k_ref[0:nk, :], qst)
        if shift_by_bound:
            m = bound
        else:
            m = jnp.max(s_d, axis=0, keepdims=True)
            if qi > 0:
                m = jnp.maximum(m, jnp.max(s_m, axis=0, keepdims=True))
        p_d = jnp.exp2(s_d - m)
        l = jnp.sum(p_d, axis=0, keepdims=True)
        acc = _dot(vt_ref[:, nk:nk + tq], p_d.astype(BF16))
        if qi > 0:
            p_m = jnp.exp2(s_m - m)
            l = l + jnp.sum(p_m, axis=0, keepdims=True)
            acc = acc + _dot(vt_ref[:, 0:nk], p_m.astype(BF16))
        accn = acc * (1.0 / l)
        ot = accn[:, :tq] - lam * accn[:, tq:]
        ms = jnp.mean(ot * ot, axis=0, keepdims=True)
        o = (ot * lax.rsqrt(ms + EPS)).T * g_ref[...] * (1.0 - lam_init)
        o_ref[qi * tq:(qi + 1) * tq, :] = o.astype(BF16)

    @pl.when(bound < ATT_BOUND_LIMIT)
    def _():
        for qi in range(nq):
            tile(qi, True)

    @pl.when(jnp.logical_not(bound < ATT_BOUND_LIMIT))
    def _():
        for qi in range(nq):
            tile(qi, False)


def _attention(scalars, qt, k, vt, subln_g, *, seq, lam_init):
    t = k.shape[0]
    bsz = t // seq
    fm = pl.BlockSpec((HEAD_W, seq), lambda b, h: (b * ATT_HEADS + h, 0))
    tm = pl.BlockSpec((seq, HEAD_W), lambda b, h: (b, h))
    return pl.pallas_call(
        functools.partial(_attn_kernel, lam_init=lam_init),
        grid=(bsz, ATT_HEADS),
        in_specs=[pl.BlockSpec(memory_space=pltpu.SMEM), fm, tm, fm, _const_spec((1, HEAD_W))],
        out_specs=tm,
        out_shape=jax.ShapeDtypeStruct((t, ATT_V_WIDTH), BF16),
        compiler_params=_params("parallel", "parallel"),
        name="attn",
    )(scalars, qt, k, vt, subln_g)


def _ssm_kernel(u_ref, bt_ref, cre_ref, cim_ref, are_ref, aim_ref, d_ref, z_ref, st_scr, state_scr):
    rows = u_ref.shape[0]
    bsz = state_scr.shape[0]

    @pl.when(pl.program_id(0) == 0)
    def _():
        state_scr[...] = jnp.zeros(state_scr.shape, F32)

    u = u_ref[...]
    ub = u.astype(BF16)
    n_tiles = 2 * N_STATE // MXU_N
    for j in range(n_tiles):
        k0 = LANES * (((j % (n_tiles // 2)) * MXU_N // SSM_STATE * SSM_GROUP) // LANES)
        st_scr[:, j * MXU_N:(j + 1) * MXU_N] = _dot(ub[:, k0:k0 + LANES], bt_ref[j])

    for cc in range(N_STATE // SCAN_LANES):
        lo = cc * SCAN_LANES
        hi = N_STATE + lo
        a_r = jnp.broadcast_to(are_ref[:, lo:lo + SCAN_LANES], (bsz, SCAN_LANES))
        a_i = jnp.broadcast_to(aim_ref[:, lo:lo + SCAN_LANES], (bsz, SCAN_LANES))

        def step(t, carry, lo=lo, hi=hi, a_r=a_r, a_i=a_i):
            s_r, s_i = carry
            r0 = pl.multiple_of(t * bsz, bsz)
            n_r = a_r * s_r - a_i * s_i + st_scr[pl.ds(r0, bsz), lo:lo + SCAN_LANES]
            n_i = a_r * s_i + a_i * s_r + st_scr[pl.ds(r0, bsz), hi:hi + SCAN_LANES]
            st_scr[pl.ds(r0, bsz), lo:lo + SCAN_LANES] = n_r
            st_scr[pl.ds(r0, bsz), hi:hi + SCAN_LANES] = n_i
            return n_r, n_i

        s_r, s_i = lax.fori_loop(0, rows // bsz, step,
                                 (state_scr[:, lo:lo + SCAN_LANES], state_scr[:, hi:hi + SCAN_LANES]),
                                 unroll=8)
        state_scr[:, lo:lo + SCAN_LANES] = s_r
        state_scr[:, hi:hi + SCAN_LANES] = s_i

    half = N_STATE // 2
    for n in range(2):
        s_re = st_scr[:, n * half:(n + 1) * half].astype(BF16)
        s_im = st_scr[:, N_STATE + n * half:N_STATE + (n + 1) * half].astype(BF16)
        cols = slice(n * MXU_N, (n + 1) * MXU_N)
        y = _dot(s_re, cre_ref[n]) - _dot(s_im, cim_ref[n]) + d_ref[:, cols] * u[:, cols]
        z_ref[:, cols] = jax.nn.gelu(y).astype(BF16)


def _ssm(u_tm, b_tiles, c_re_tiles, c_im_tiles, a_re, a_im, d_skip, *, bsz):
    rows_total = u_tm.shape[0]
    rows = SCAN_T * bsz
    blk = pl.BlockSpec((rows, SSM_WIDTH), lambda i: (i, 0))
    return pl.pallas_call(
        _ssm_kernel,
        grid=(rows_total // rows,),
        in_specs=[blk, _const_spec(b_tiles.shape), _const_spec(c_re_tiles.shape),
                  _const_spec(c_im_tiles.shape), _const_spec((1, N_STATE)), _const_spec((1, N_STATE)),
                  _const_spec((1, SSM_WIDTH))],
        out_specs=blk,
        out_shape=jax.ShapeDtypeStruct((rows_total, SSM_WIDTH), BF16),
        scratch_shapes=[pltpu.VMEM((rows, 2 * N_STATE), F32), pltpu.VMEM((bsz, 2 * N_STATE), F32)],
        compiler_params=_params("arbitrary"),
        name="ssm",
    )(u_tm, b_tiles, c_re_tiles, c_im_tiles, a_re, a_im, d_skip)


def _route(logits, ltri):
    lane = lax.broadcasted_iota(jnp.int32, logits.shape, 1)
    lanef = lane.astype(F32)
    big = float(ROUTE_W)
    neg = -jnp.inf
    gl = jnp.where(lane < N_GROUPS, logits, neg)
    gmax = jnp.max(gl, axis=-1, keepdims=True)
    grp = jnp.min(jnp.where(gl == gmax, lanef, big), axis=-1, keepdims=True)
    p_grp = 1.0 / jnp.sum(jnp.exp(gl - gmax), axis=-1, keepdims=True)
    first = ROUTE_OFF + grp * EXPERTS_PER_GROUP
    in_grp = (lanef >= first) & (lanef < first + EXPERTS_PER_GROUP)
    el = jnp.where(in_grp, logits, neg)
    v1 = jnp.max(el, axis=-1, keepdims=True)
    i1 = jnp.min(jnp.where(el == v1, lanef, big), axis=-1, keepdims=True)
    el2 = jnp.where(lanef == i1, neg, el)
    v2 = jnp.max(el2, axis=-1, keepdims=True)
    i2 = jnp.min(jnp.where(el2 == v2, lanef, big), axis=-1, keepdims=True)
    e21 = jnp.exp(v2 - v1)
    w1 = p_grp / (1.0 + e21)
    w2 = p_grp * (e21 / (1.0 + e21))
    member = lanef == grp
    before = _dot(ltri, jnp.where(member, 1.0, 0.0).astype(BF16))
    rank = jnp.sum(jnp.where(member, before, 0.0), axis=-1, keepdims=True)
    return (jnp.where(lanef == i1, w1, 0.0) + jnp.where(lanef == i2, w2, 0.0)
            + jnp.where(lane == GRP_LANE, grp, 0.0) + jnp.where(lane == RANK_LANE, rank, 0.0))


def _mix_kernel(x_ref, o_ref, z_ref, gate_ref, wau_ref, wglu_ref, bglu_ref, wsu_ref, wout_ref,
                fg_ref, wrh_ref, wrl_ref, br_ref, ltri_ref, x1_ref, hn_ref, route_ref, route_t_ref):
    y_att = _dot(o_ref[...], wau_ref[...])
    z = z_ref[...].astype(F32)
    z = z * jax.nn.sigmoid(_dot(z_ref[...], wglu_ref[...]) + bglu_ref[...])
    y_ssm = _dot(z.astype(BF16), wsu_ref[...])
    mixed = (gate_ref[:, :D_MODEL].astype(F32) * y_att + gate_ref[:, D_MODEL:].astype(F32) * y_ssm)
    x1 = x_ref[...] + _dot(mixed.astype(BF16), wout_ref[...])
    x1_ref[...] = x1
    ms = jnp.mean(x1 * x1, axis=-1, keepdims=True)
    hn = x1 * lax.rsqrt(ms + EPS) * fg_ref[...]
    hn_hi = hn.astype(BF16)
    hn_ref[...] = hn_hi
    hn_lo = (hn - hn_hi.astype(F32)).astype(BF16)
    logits = (_dot(hn_hi, wrh_ref[...]) + _dot(hn_lo, wrh_ref[...]) + _dot(hn_hi, wrl_ref[...])
              + br_ref[...])
    route = _route(logits, ltri_ref[...])
    route_ref[...] = route
    route_t_ref[...] = route.T[:SUBLANES, :]


def _mix(x2d, o, z_sm, gates, wau, wglu, bglu, wsu, wout, ffn_g, wr_hi, wr_lo, br, ltri, *, seq):
    t = x2d.shape[0]
    n_sb = seq // TOK_TILE
    row = lambda w: pl.BlockSpec((TOK_TILE, w), lambda i: (i, 0))
    return pl.pallas_call(
        _mix_kernel,
        grid=(t // TOK_TILE,),
        in_specs=[row(D_MODEL), row(ATT_V_WIDTH),
                  pl.BlockSpec((TOK_TILE, SSM_WIDTH), lambda i: (i % n_sb, i // n_sb)),
                  row(N_BRANCH * D_MODEL),
                  _const_spec(wau.shape), _const_spec(wglu.shape), _const_spec(bglu.shape),
                  _const_spec(wsu.shape), _const_spec(wout.shape), _const_spec(ffn_g.shape),
                  _const_spec(wr_hi.shape), _const_spec(wr_lo.shape), _const_spec(br.shape),
                  _const_spec(ltri.shape)],
        out_specs=[row(D_MODEL), row(D_MODEL), row(ROUTE_W),
                   pl.BlockSpec((SUBLANES, TOK_TILE), lambda i: (0, i))],
        out_shape=[jax.ShapeDtypeStruct((t, D_MODEL), F32),
                   jax.ShapeDtypeStruct((t, D_MODEL), BF16),
                   jax.ShapeDtypeStruct((t, ROUTE_W), F32),
                   jax.ShapeDtypeStruct((SUBLANES, t), F32)],
        compiler_params=_params("parallel"),
        name="mix",
    )(x2d, o, z_sm, gates, wau, wglu, bglu, wsu, wout, ffn_g, wr_hi, wr_lo, br, ltri)


def _moe_kernel(hn_ref, route_ref, route_t_ref, wg_ref, wu_ref, wd_ref, out_ref, acc_scr, wg_scr, wu_scr, wd_scr):
    n = pl.program_id(0)

    @pl.when(pl.program_id(1) == 0)
    def _():
        for j in range(EXPERTS_PER_GROUP):
            cols = slice(j * EXPERT_FF, (j + 1) * EXPERT_FF)
            wg_scr[:, cols] = wg_ref[j].astype(BF16)
            wu_scr[:, cols] = wu_ref[j].astype(BF16)
            wd_scr[cols, :] = wd_ref[j].astype(BF16)

    tm = hn_ref.shape[0]
    nf = n.astype(F32)
    route = route_ref[...]
    lane = lax.broadcasted_iota(jnp.int32, route.shape, 1)
    grp_c = jnp.sum(jnp.where(lane == GRP_LANE, route, 0.0), axis=-1, keepdims=True)
    rank_c = jnp.sum(jnp.where(lane == RANK_LANE, route, 0.0), axis=-1, keepdims=True)
    grp_r = route_t_ref[GRP_LANE:GRP_LANE + 1, :]
    rank_r = route_t_ref[RANK_LANE:RANK_LANE + 1, :]
    r1 = route.astype(BF16)
    r2 = (route - r1.astype(F32)).astype(BF16)
    r3 = (route - r1.astype(F32) - r2.astype(F32)).astype(BF16)
    pieces = jnp.concatenate([r1, r2, r3], axis=1)

    acc_scr[...] = jnp.zeros(acc_scr.shape, F32)
    n_rows = jnp.max(jnp.where(grp_c == nf, rank_c + 1.0, 0.0))
    n_pass = lax.div(n_rows.astype(jnp.int32) + (MOE_CAP - 1), MOE_CAP)

    def one_pass(k, carry):
        base = (k * MOE_CAP).astype(F32)
        slot_r = lax.broadcasted_iota(jnp.int32, (MOE_CAP, tm), 0).astype(F32) + base
        take = jnp.where((grp_r == nf) & (rank_r == slot_r), 1.0, 0.0).astype(BF16)
        slot_c = lax.broadcasted_iota(jnp.int32, (tm, MOE_CAP), 1).astype(F32) + base
        put = jnp.where((grp_c == nf) & (rank_c == slot_c), 1.0, 0.0).astype(BF16)
        xc = _dot(take, hn_ref[...]).astype(BF16)
        rc = _dot(take, pieces)
        rc = rc[:, :ROUTE_W] + rc[:, ROUTE_W:2 * ROUTE_W] + rc[:, 2 * ROUTE_W:]
        lane_c = lax.broadcasted_iota(jnp.int32, rc.shape, 1)
        first = ROUTE_OFF + n * EXPERTS_PER_GROUP
        act = jax.nn.silu(_dot(xc, wg_scr[...])) * _dot(xc, wu_scr[...])
        blocks = []
        for j in range(EXPERTS_PER_GROUP):
            w_j = jnp.sum(jnp.where(lane_c == first + j, rc, 0.0), axis=-1, keepdims=True)
            blocks.append((act[:, j * EXPERT_FF:(j + 1) * EXPERT_FF] * w_j).astype(BF16))
        y = _dot(jnp.concatenate(blocks, axis=1), wd_scr[...])
        acc_scr[...] += _dot(put, y.astype(BF16))
        return carry

    lax.fori_loop(0, n_pass, one_pass, 0)
    out_ref[...] = acc_scr[...].astype(BF16)


def _moe(hn, route, route_t, w_gate, w_up, w_down, *, layer):
    t = hn.shape[0]
    row = lambda w: pl.BlockSpec((TOK_TILE, w), lambda n, i: (i, 0))
    grp_w = lambda shape: pl.BlockSpec((None, None) + shape, lambda n, i: (layer, n, 0, 0, 0),
                                       pipeline_mode=pl.Buffered(1))
    up_shape = (EXPERTS_PER_GROUP, D_MODEL, EXPERT_FF)
    down_shape = (EXPERTS_PER_GROUP, EXPERT_FF, D_MODEL)
    ff = EXPERTS_PER_GROUP * EXPERT_FF
    return pl.pallas_call(
        _moe_kernel,
        grid=(N_GROUPS, t // TOK_TILE),
        in_specs=[row(D_MODEL), row(ROUTE_W), pl.BlockSpec((SUBLANES, TOK_TILE), lambda n, i: (0, i)),
                  grp_w(up_shape), grp_w(up_shape), grp_w(down_shape)],
        out_specs=pl.BlockSpec((None, TOK_TILE, D_MODEL), lambda n, i: (n, i, 0)),
        out_shape=jax.ShapeDtypeStruct((N_GROUPS, t, D_MODEL), BF16),
        scratch_shapes=[pltpu.VMEM((TOK_TILE, D_MODEL), F32), pltpu.VMEM((D_MODEL, ff), BF16),
                        pltpu.VMEM((D_MODEL, ff), BF16), pltpu.VMEM((ff, D_MODEL), BF16)],
        compiler_params=_params("arbitrary", "arbitrary"),
        name="moe",
    )(hn, route, route_t, w_gate, w_up, w_down)


def _ple_kernel(x1_ref, c_ref, p_ref, pg_ref, wpg_ref, wple_ref, out_ref):
    x2 = x1_ref[...]
    for n in range(N_GROUPS):
        x2 = x2 + c_ref[n].astype(F32)
    ms = jnp.mean(x2 * x2, axis=-1, keepdims=True)
    hp = (x2 * lax.rsqrt(ms + EPS) * pg_ref[...]).astype(BF16)
    gate = jax.nn.sigmoid(_dot(hp, wpg_ref[...]))
    out_ref[...] = x2 + gate * _dot(p_ref[...].astype(BF16), wple_ref[...])


def _ple(x1, contrib, p2d, ple_g, wpg, wple):
    t = x1.shape[0]
    row = lambda w: pl.BlockSpec((TOK_TILE, w), lambda i: (i, 0))
    return pl.pallas_call(
        _ple_kernel,
        grid=(t // TOK_TILE,),
        in_specs=[row(D_MODEL), pl.BlockSpec((N_GROUPS, TOK_TILE, D_MODEL), lambda i: (0, i, 0)),
                  row(PLE_DIM), _const_spec(ple_g.shape), _const_spec(wpg.shape), _const_spec(wple.shape)],
        out_specs=row(D_MODEL),
        out_shape=jax.ShapeDtypeStruct((t, D_MODEL), F32),
        compiler_params=_params("parallel"),
        name="ple",
    )(x1, contrib, p2d, ple_g, wpg, wple)


def kernel(x, p, attn_norm_g, w_in, b_gate, q_norm_g, k_norm_g, lam_qk, subln_g, w_attn_up, ssm_lam_re, ssm_lam_im, ssm_log_step, ssm_b_re, ssm_b_im, ssm_c_re, ssm_c_im, ssm_d, w_glu, b_glu, w_ssm_up, w_out, ffn_norm_g, w_router_group, b_router_group, w_router_expert, b_router_expert, w_exp_gate, w_exp_up, w_exp_down, ple_norm_g, w_ple_gate, w_ple):
    bsz, seq, d = x.shape
    depth = w_in.shape[0]
    t = bsz * seq
    assert d == D_MODEL and seq % TOK_TILE == 0 and seq % ATT_TILE == 0
    assert seq % SCAN_T == 0 and bsz == SUBLANES
    tok = jnp.arange(TOK_TILE)
    ltri = (tok[None, :] < tok[:, None]).astype(BF16)

    a_re, a_im, bb_re, bb_im, scalars = _prep(ssm_lam_re, ssm_lam_im, ssm_log_step, ssm_b_re, ssm_b_im, lam_qk,
                                              q_norm_g, k_norm_g)
    head_ids = jnp.arange(COL_CHUNK) // ATT_HEAD_DIM
    gsum = (head_ids[:, None] == head_ids[None, :]).astype(BF16)

    x2d = x.reshape(t, d)
    for i in range(depth):
        lam_init = _lambda_init(i)
        tile_gain = lambda g: jnp.tile(g.reshape(1, HEAD_W), (1, COL_CHUNK // HEAD_W))
        qt, k, vt, u_sm, gates = _inproj(
            x2d, attn_norm_g[i].reshape(1, d), w_in[i].astype(BF16), gsum,
            tile_gain(q_norm_g[i]), tile_gain(k_norm_g[i]), b_gate[i].reshape(1, -1), seq=seq)

        o = _attention(scalars[i], qt, k, vt, subln_g[i].reshape(1, HEAD_W), seq=seq, lam_init=lam_init)

        b_tiles, c_re_tiles, c_im_tiles = _ssm_matrices(bb_re[i], bb_im[i], ssm_c_re[i], ssm_c_im[i])
        z_tm = _ssm(u_sm.reshape(seq * bsz, SSM_WIDTH), b_tiles, c_re_tiles, c_im_tiles,
                    a_re[i, ::SSM_GROUP].reshape(1, N_STATE), a_im[i, ::SSM_GROUP].reshape(1, N_STATE),
                    ssm_d[i].reshape(1, SSM_WIDTH), bsz=bsz)

        w_r = jnp.concatenate(
            [w_router_group[i], jnp.transpose(w_router_expert[i], (1, 0, 2)).reshape(d, N_EXPERTS),
             jnp.zeros((d, ROUTE_W - N_GROUPS - N_EXPERTS), F32)], axis=1)
        w_r_hi = w_r.astype(BF16)
        w_r_lo = (w_r - w_r_hi.astype(F32)).astype(BF16)
        b_r = jnp.concatenate([b_router_group[i], b_router_expert[i].reshape(-1),
                               jnp.zeros((ROUTE_W - N_GROUPS - N_EXPERTS,), F32)]).reshape(1, ROUTE_W)
        x1, hn, route, route_t = _mix(
            x2d, o, z_tm.reshape(seq, bsz * SSM_WIDTH), gates,
            w_attn_up[i].astype(BF16), w_glu[i].astype(BF16), b_glu[i].reshape(1, -1),
            w_ssm_up[i].astype(BF16), w_out[i].astype(BF16), ffn_norm_g[i].reshape(1, d),
            w_r_hi, w_r_lo, b_r, ltri, seq=seq)

        contrib = _moe(hn, route, route_t, w_exp_gate, w_exp_up, w_exp_down, layer=i)
        x2d = _ple(x1, contrib, p[i].reshape(t, PLE_DIM), ple_norm_g[i].reshape(1, d),
                   w_ple_gate[i].astype(BF16), w_ple[i].astype(BF16))
    return x2d.reshape(bsz, seq, d)
```

```python
import functools
import math

import jax
import jax.numpy as jnp
from jax import lax
from jax.experimental import pallas as pl
from jax.experimental.pallas import tpu as pltpu

F32 = jnp.float32
BF16 = jnp.bfloat16

D_MODEL = 1024
ATT_HEADS = 8
ATT_HEAD_DIM = 64
HEAD_W = 2 * ATT_HEAD_DIM
ATT_QK_WIDTH = ATT_HEADS * HEAD_W
ATT_V_WIDTH = ATT_HEADS * HEAD_W
SSM_WIDTH = 512
SSM_GROUP = 16
SSM_GROUPS = SSM_WIDTH // SSM_GROUP
SSM_STATE = 64
N_STATE = SSM_GROUPS * SSM_STATE
N_BRANCH = 2
IN_WIDTH = 2 * ATT_QK_WIDTH + ATT_V_WIDTH + SSM_WIDTH + N_BRANCH * D_MODEL
N_GROUPS = 4
EXPERTS_PER_GROUP = 8
N_EXPERTS = N_GROUPS * EXPERTS_PER_GROUP
EXPERT_FF = 256
PLE_DIM = 256
EPS = 1e-6

LANES = 128
SUBLANES = 8
MXU_N = 256
VMEM_LIMIT = 56 * 1024 * 1024

TOK_TILE = 512
MOE_CAP = 160
COL_CHUNK = 512
ATT_TILE = 256
Q_SCALE = ATT_HEAD_DIM ** -0.5 * math.log2(math.e)
LAM_ROW = 0
BOUND_ROW = 1
ATT_BOUND_SLACK = 1.01
ATT_BOUND_LIMIT = 60.0
SCAN_T = 64
SCAN_LANES = 512
ROUTE_W = LANES
ROUTE_OFF = N_GROUPS
GRP_LANE = 0
RANK_LANE = 1


def _lambda_init(layer_idx):
    return 0.8 - 0.6 * math.exp(-0.3 * layer_idx)


def _dot(a, b):
    return jnp.dot(a, b, preferred_element_type=F32)


def _const_spec(shape):
    nd = len(shape)
    return pl.BlockSpec(shape, lambda *_: (0,) * nd, pipeline_mode=pl.Buffered(1))


def _params(*sem):
    return pltpu.CompilerParams(dimension_semantics=sem, vmem_limit_bytes=VMEM_LIMIT)


def _prep_kernel(lre_ref, lim_ref, lstep_ref, bre_ref, bim_ref, lqk_ref, qg_ref, kg_ref,
                 are_ref, aim_ref, bbre_ref, bbim_ref, lam_ref):
    lr = lre_ref[...]
    li = lim_ref[...]
    dt = jnp.exp(lstep_ref[...])
    mag = jnp.exp(lr * dt)
    ab_re = mag * jnp.cos(li * dt)
    ab_im = mag * jnp.sin(li * dt)
    den = lr * lr + li * li
    nr = ab_re - 1.0
    coef_re = (nr * lr + ab_im * li) / den
    coef_im = (ab_im * lr - nr * li) / den
    br = bre_ref[...]
    bi = bim_ref[...]
    are_ref[...] = ab_re
    aim_ref[...] = ab_im
    bbre_ref[...] = coef_re * br - coef_im * bi
    bbim_ref[...] = coef_re * bi + coef_im * br
    lq = lqk_ref[...]
    s01 = jnp.sum(lq[0:1] * lq[1:2], axis=-1, keepdims=True)
    s23 = jnp.sum(lq[2:3] * lq[3:4], axis=-1, keepdims=True)
    lam_dyn = jnp.exp(s01) - jnp.exp(s23)
    g_max = jnp.max(jnp.abs(qg_ref[...]), keepdims=True) * jnp.max(jnp.abs(kg_ref[...]), keepdims=True)
    bound = g_max * (ATT_HEAD_DIM * Q_SCALE * ATT_BOUND_SLACK)
    row = lax.broadcasted_iota(jnp.int32, lam_ref.shape, 0)
    lam_ref[...] = jnp.where(row == LAM_ROW, lam_dyn, bound)


def _prep(lam_re, lam_im, log_step, b_re, b_im, lam_qk, q_gain, k_gain):
    depth = lam_re.shape[0]
    rows = SSM_WIDTH
    rep = lambda a: jnp.repeat(a, SSM_GROUP, axis=1)
    lstep = jnp.broadcast_to(log_step[:, :, None], lam_re.shape)
    tr = lambda b: jnp.transpose(b, (0, 1, 3, 2)).reshape(depth, rows, SSM_STATE)
    blk = pl.BlockSpec((None, rows, SSM_STATE), lambda i: (i, 0, 0))
    out = jax.ShapeDtypeStruct((depth, rows, SSM_STATE), F32)
    return pl.pallas_call(
        _prep_kernel,
        grid=(depth,),
        in_specs=[blk, blk, blk, blk, blk, pl.BlockSpec((None, 4, ATT_HEAD_DIM), lambda i: (i, 0, 0)),
                  pl.BlockSpec((None, 2, ATT_HEAD_DIM), lambda i: (i, 0, 0)),
                  pl.BlockSpec((None, 2, ATT_HEAD_DIM), lambda i: (i, 0, 0))],
        out_specs=[blk, blk, blk, blk, pl.BlockSpec((None, SUBLANES, LANES), lambda i: (i, 0, 0))],
        out_shape=[out, out, out, out, jax.ShapeDtypeStruct((depth, SUBLANES, LANES), F32)],
        compiler_params=_params("arbitrary"),
        name="prep",
    )(rep(lam_re), rep(lam_im), rep(lstep), tr(b_re), tr(b_im), lam_qk, q_gain, k_gain)


def _block_diag(a):
    g, r, c = a.shape
    eye = jnp.eye(g, dtype=a.dtype)
    return jnp.einsum('grc,gk->grkc', a, eye).reshape(g * r, g * c)


def _ssm_matrices(bb_re, bb_im, c_re, c_im):
    g = SSM_GROUPS
    n_re_tiles = N_STATE // MXU_N
    tiles = []
    for full in (_block_diag(bb_re.reshape(g, SSM_GROUP, SSM_STATE)),
                 _block_diag(bb_im.reshape(g, SSM_GROUP, SSM_STATE))):
        for j in range(n_re_tiles):
            k0 = LANES * ((j * MXU_N // SSM_STATE * SSM_GROUP) // LANES)
            tiles.append(full[k0:k0 + LANES, j * MXU_N:(j + 1) * MXU_N])
    b_tiles = jnp.stack(tiles).astype(BF16)
    c_out = []
    for c in (c_re, c_im):
        full = _block_diag(jnp.transpose(c, (0, 2, 1)))
        half = N_STATE // 2
        c_out.append(jnp.stack([full[n * half:(n + 1) * half, n * MXU_N:(n + 1) * MXU_N]
                                for n in range(2)]).astype(BF16))
    return b_tiles, c_out[0], c_out[1]


def _inproj_kernel(x_ref, g_ref, w_ref, qg_ref, kg_ref, bg_ref,
                   qt_ref, k_ref, vt_ref, u_ref, gate_ref):
    xf = x_ref[...]
    ms = jnp.mean(xf * xf, axis=-1, keepdims=True)
    h = (xf * lax.rsqrt(ms + EPS) * g_ref[...]).astype(BF16)

    def head_norm_t(y, gain_col):
        rows = y.shape[0]
        y3 = y.T.reshape(COL_CHUNK // ATT_HEAD_DIM, ATT_HEAD_DIM, rows)
        ms = jnp.sum(y3 * y3, axis=1, keepdims=True) * (1.0 / ATT_HEAD_DIM)
        return (y3 * lax.rsqrt(ms + EPS)).reshape(COL_CHUNK, rows) * gain_col

    n_qk = ATT_QK_WIDTH // COL_CHUNK
    n_v = ATT_V_WIDTH // COL_CHUNK
    for c in range(IN_WIDTH // COL_CHUNK):
        y = _dot(h, w_ref[:, c * COL_CHUNK:(c + 1) * COL_CHUNK])
        if c < n_qk:
            qt_ref[c * COL_CHUNK:(c + 1) * COL_CHUNK, :] = (head_norm_t(y, qg_ref[...]) * Q_SCALE).astype(BF16)
        elif c < 2 * n_qk:
            cc = c - n_qk
            k_ref[:, cc * COL_CHUNK:(cc + 1) * COL_CHUNK] = head_norm_t(y, kg_ref[...]).T.astype(BF16)
        elif c < 2 * n_qk + n_v:
            cc = c - 2 * n_qk
            vt_ref[cc * COL_CHUNK:(cc + 1) * COL_CHUNK, :] = y.T.astype(BF16)
        elif c == 2 * n_qk + n_v:
            u_ref[...] = y
        else:
            cc = c - (2 * n_qk + n_v + 1)
            b = bg_ref[:, cc * COL_CHUNK:(cc + 1) * COL_CHUNK]
            gate_ref[:, cc * COL_CHUNK:(cc + 1) * COL_CHUNK] = jax.nn.sigmoid(y + b).astype(BF16)


def _inproj(x2d, norm_g, w_in, qg, kg, b_gate, *, seq):
    t = x2d.shape[0]
    n_sb = seq // TOK_TILE
    bsz = t // seq
    row = lambda w: pl.BlockSpec((TOK_TILE, w), lambda i: (i, 0))
    col = lambda w: pl.BlockSpec((w, TOK_TILE), lambda i: (i // n_sb, i % n_sb))
    return pl.pallas_call(
        _inproj_kernel,
        grid=(t // TOK_TILE,),
        in_specs=[row(D_MODEL), _const_spec((1, D_MODEL)), _const_spec((D_MODEL, IN_WIDTH)),
                  _const_spec((COL_CHUNK, 1)), _const_spec((COL_CHUNK, 1)),
                  _const_spec((1, N_BRANCH * D_MODEL))],
        out_specs=[col(ATT_QK_WIDTH), row(ATT_QK_WIDTH), col(ATT_V_WIDTH),
                   pl.BlockSpec((TOK_TILE, SSM_WIDTH), lambda i: (i % n_sb, i // n_sb)),
                   row(N_BRANCH * D_MODEL)],
        out_shape=[jax.ShapeDtypeStruct((bsz * ATT_QK_WIDTH, seq), BF16),
                   jax.ShapeDtypeStruct((t, ATT_QK_WIDTH), BF16),
                   jax.ShapeDtypeStruct((bsz * ATT_V_WIDTH, seq), BF16),
                   jax.ShapeDtypeStruct((seq, bsz * SSM_WIDTH), F32),
                   jax.ShapeDtypeStruct((t, N_BRANCH * D_MODEL), BF16)],
        compiler_params=_params("parallel"),
        name="inproj",
    )(x2d, norm_g, w_in, qg, kg, b_gate)


def _attn_kernel(sc_ref, qt_ref, k_ref, vt_ref, g_ref, o_ref, *, lam_init):
    tq = ATT_TILE
    seq = k_ref.shape[0]
    nq = seq // tq
    lam = sc_ref[LAM_ROW, 0] + lam_init
    bound = sc_ref[BOUND_ROW, 0]
    head_row = lax.broadcasted_iota(jnp.int32, (HEAD_W, tq), 0)
    key_pos = lax.broadcasted_iota(jnp.int32, (tq, 2 * tq), 0)
    qry_pos = lax.broadcasted_iota(jnp.int32, (tq, 2 * tq), 1) & (tq - 1)
    causal = key_pos <= qry_pos
    zero = jnp.zeros((HEAD_W, tq), BF16)

    def tile(qi, shift_by_bound):
        qt = qt_ref[:, qi * tq:(qi + 1) * tq]
        qst = jnp.concatenate([jnp.where(head_row < ATT_HEAD_DIM, qt, zero),
                               jnp.where(head_row >= ATT_HEAD_DIM, qt, zero)], axis=1)
        nk = qi * tq
        s_d = jnp.where(causal, _dot(k_ref[nk:nk + tq, :], qst), -jnp.inf)
        if qi > 0:
            s_m = _dot(k_ref[0:nk, :], qst)
        if shift_by_bound:
            m = bound
        else:
            m = jnp.max(s_d, axis=0, keepdims=True)
            if qi > 0:
                m = jnp.maximum(m, jnp.max(s_m, axis=0, keepdims=True))
        p_d = jnp.exp2(s_d - m)
        l = jnp.sum(p_d, axis=0, keepdims=True)
        acc = _dot(vt_ref[:, nk:nk + tq], p_d.astype(BF16))
        if qi > 0:
            p_m = jnp.exp2(s_m - m)
            l = l + jnp.sum(p_m, axis=0, keepdims=True)
            acc = acc + _dot(vt_ref[:, 0:nk], p_m.astype(BF16))
        accn = acc * (1.0 / l)
        ot = accn[:, :tq] - lam * accn[:, tq:]
        ms = jnp.mean(ot * ot, axis=0, keepdims=True)
        o = (ot * lax.rsqrt(ms + EPS)).T * g_ref[...] * (1.0 - lam_init)
        o_ref[qi * tq:(qi + 1) * tq, :] = o.astype(BF16)

    @pl.when(bound < ATT_BOUND_LIMIT)
    def _():
        for qi in range(nq):
            tile(qi, True)

    @pl.when(jnp.logical_not(bound < ATT_BOUND_LIMIT))
    def _():
        for qi in range(nq):
            tile(qi, False)


def _attention(scalars, qt, k, vt, subln_g, *, seq, lam_init):
    t = k.shape[0]
    bsz = t // seq
    fm = pl.BlockSpec((HEAD_W, seq), lambda b, h: (b * ATT_HEADS + h, 0))
    tm = pl.BlockSpec((seq, HEAD_W), lambda b, h: (b, h))
    return pl.pallas_call(
        functools.partial(_attn_kernel, lam_init=lam_init),
        grid=(bsz, ATT_HEADS),
        in_specs=[pl.BlockSpec(memory_space=pltpu.SMEM), fm, tm, fm, _const_spec((1, HEAD_W))],
        out_specs=tm,
        out_shape=jax.ShapeDtypeStruct((t, ATT_V_WIDTH), BF16),
        compiler_params=_params("parallel", "parallel"),
        name="attn",
    )(scalars, qt, k, vt, subln_g)


def _ssm_kernel(u_ref, bt_ref, cre_ref, cim_ref, are_ref, aim_ref, d_ref, z_ref, st_scr, state_scr):
    rows = u_ref.shape[0]
    bsz = state_scr.shape[0]

    @pl.when(pl.program_id(0) == 0)
    def _():
        state_scr[...] = jnp.zeros(state_scr.shape, F32)

    u = u_ref[...]
    ub = u.astype(BF16)
    n_tiles = 2 * N_STATE // MXU_N
    for j in range(n_tiles):
        k0 = LANES * (((j % (n_tiles // 2)) * MXU_N // SSM_STATE * SSM_GROUP) // LANES)
        st_scr[:, j * MXU_N:(j + 1) * MXU_N] = _dot(ub[:, k0:k0 + LANES], bt_ref[j])

    for cc in range(N_STATE // SCAN_LANES):
        lo = cc * SCAN_LANES
        hi = N_STATE + lo
        a_r = jnp.broadcast_to(are_ref[:, lo:lo + SCAN_LANES], (bsz, SCAN_LANES))
        a_i = jnp.broadcast_to(aim_ref[:, lo:lo + SCAN_LANES], (bsz, SCAN_LANES))

        def step(t, carry, lo=lo, hi=hi, a_r=a_r, a_i=a_i):
            s_r, s_i = carry
            r0 = pl.multiple_of(t * bsz, bsz)
            n_r = a_r * s_r - a_i * s_i + st_scr[pl.ds(r0, bsz), lo:lo + SCAN_LANES]
            n_i = a_r * s_i + a_i * s_r + st_scr[pl.ds(r0, bsz), hi:hi + SCAN_LANES]
            st_scr[pl.ds(r0, bsz), lo:lo + SCAN_LANES] = n_r
            st_scr[pl.ds(r0, bsz), hi:hi + SCAN_LANES] = n_i
            return n_r, n_i

        s_r, s_i = lax.fori_loop(0, rows // bsz, step,
                                 (state_scr[:, lo:lo + SCAN_LANES], state_scr[:, hi:hi + SCAN_LANES]),
                                 unroll=8)
        state_scr[:, lo:lo + SCAN_LANES] = s_r
        state_scr[:, hi:hi + SCAN_LANES] = s_i

    half = N_STATE // 2
    for n in range(2):
        s_re = st_scr[:, n * half:(n + 1) * half].astype(BF16)
        s_im = st_scr[:, N_STATE + n * half:N_STATE + (n + 1) * half].astype(BF16)
        cols = slice(n * MXU_N, (n + 1) * MXU_N)
        y = _dot(s_re, cre_ref[n]) - _dot(s_im, cim_ref[n]) + d_ref[:, cols] * u[:, cols]
        z_ref[:, cols] = jax.nn.gelu(y).astype(BF16)


def _ssm(u_tm, b_tiles, c_re_tiles, c_im_tiles, a_re, a_im, d_skip, *, bsz):
    rows_total = u_tm.shape[0]
    rows = SCAN_T * bsz
    blk = pl.BlockSpec((rows, SSM_WIDTH), lambda i: (i, 0))
    return pl.pallas_call(
        _ssm_kernel,
        grid=(rows_total // rows,),
        in_specs=[blk, _const_spec(b_tiles.shape), _const_spec(c_re_tiles.shape),
                  _const_spec(c_im_tiles.shape), _const_spec((1, N_STATE)), _const_spec((1, N_STATE)),
                  _const_spec((1, SSM_WIDTH))],
        out_specs=blk,
        out_shape=jax.ShapeDtypeStruct((rows_total, SSM_WIDTH), BF16),
        scratch_shapes=[pltpu.VMEM((rows, 2 * N_STATE), F32), pltpu.VMEM((bsz, 2 * N_STATE), F32)],
        compiler_params=_params("arbitrary"),
        name="ssm",
    )(u_tm, b_tiles, c_re_tiles, c_im_tiles, a_re, a_im, d_skip)


def _route(logits, ltri):
    lane = lax.broadcasted_iota(jnp.int32, logits.shape, 1)
    lanef = lane.astype(F32)
    big = float(ROUTE_W)
    neg = -jnp.inf
    gl = jnp.where(lane < N_GROUPS, logits, neg)
    gmax = jnp.max(gl, axis=-1, keepdims=True)
    grp = jnp.min(jnp.where(gl == gmax, lanef, big), axis=-1, keepdims=True)
    p_grp = 1.0 / jnp.sum(jnp.exp(gl - gmax), axis=-1, keepdims=True)
    first = ROUTE_OFF + grp * EXPERTS_PER_GROUP
    in_grp = (lanef >= first) & (lanef < first + EXPERTS_PER_GROUP)
    el = jnp.where(in_grp, logits, neg)
    v1 = jnp.max(el, axis=-1, keepdims=True)
    i1 = jnp.min(jnp.where(el == v1, lanef, big), axis=-1, keepdims=True)
    el2 = jnp.where(lanef == i1, neg, el)
    v2 = jnp.max(el2, axis=-1, keepdims=True)
    i2 = jnp.min(jnp.where(el2 == v2, lanef, big), axis=-1, keepdims=True)
    e21 = jnp.exp(v2 - v1)
    w1 = p_grp / (1.0 + e21)
    w2 = p_grp * (e21 / (1.0 + e21))
    member = lanef == grp
    before = _dot(ltri, jnp.where(member, 1.0, 0.0).astype(BF16))
    rank = jnp.sum(jnp.where(member, before, 0.0), axis=-1, keepdims=True)
    return (jnp.where(lanef == i1, w1, 0.0) + jnp.where(lanef == i2, w2, 0.0)
            + jnp.where(lane == GRP_LANE, grp, 0.0) + jnp.where(lane == RANK_LANE, rank, 0.0))


def _mix_kernel(x_ref, o_ref, z_ref, gate_ref, wau_ref, wglu_ref, bglu_ref, wsu_ref, wout_ref,
                fg_ref, wrh_ref, wrl_ref, br_ref, ltri_ref, x1_ref, hn_ref, route_ref, route_t_ref):
    y_att = _dot(o_ref[...], wau_ref[...])
    z = z_ref[...].astype(F32)
    z = z * jax.nn.sigmoid(_dot(z_ref[...], wglu_ref[...]) + bglu_ref[...])
    y_ssm = _dot(z.astype(BF16), wsu_ref[...])
    mixed = (gate_ref[:, :D_MODEL].astype(F32) * y_att + gate_ref[:, D_MODEL:].astype(F32) * y_ssm)
    x1 = x_ref[...] + _dot(mixed.astype(BF16), wout_ref[...])
    x1_ref[...] = x1
    ms = jnp.mean(x1 * x1, axis=-1, keepdims=True)
    hn = x1 * lax.rsqrt(ms + EPS) * fg_ref[...]
    hn_hi = hn.astype(BF16)
    hn_ref[...] = hn_hi
    hn_lo = (hn - hn_hi.astype(F32)).astype(BF16)
    logits = (_dot(hn_hi, wrh_ref[...]) + _dot(hn_lo, wrh_ref[...]) + _dot(hn_hi, wrl_ref[...])
              + br_ref[...])
    route = _route(logits, ltri_ref[...])
    route_ref[...] = route
    route_t_ref[...] = route.T[:SUBLANES, :]


def _mix(x2d, o, z_sm, gates, wau, wglu, bglu, wsu, wout, ffn_g, wr_hi, wr_lo, br, ltri, *, seq):
    t = x2d.shape[0]
    n_sb = seq // TOK_TILE
    row = lambda w: pl.BlockSpec((TOK_TILE, w), lambda i: (i, 0))
    return pl.pallas_call(
        _mix_kernel,
        grid=(t // TOK_TILE,),
        in_specs=[row(D_MODEL), row(ATT_V_WIDTH),
                  pl.BlockSpec((TOK_TILE, SSM_WIDTH), lambda i: (i % n_sb, i // n_sb)),
                  row(N_BRANCH * D_MODEL),
                  _const_spec(wau.shape), _const_spec(wglu.shape), _const_spec(bglu.shape),
                  _const_spec(wsu.shape), _const_spec(wout.shape), _const_spec(ffn_g.shape),
                  _const_spec(wr_hi.shape), _const_spec(wr_lo.shape), _const_spec(br.shape),
                  _const_spec(ltri.shape)],
        out_specs=[row(D_MODEL), row(D_MODEL), row(ROUTE_W),
                   pl.BlockSpec((SUBLANES, TOK_TILE), lambda i: (0, i))],
        out_shape=[jax.ShapeDtypeStruct((t, D_MODEL), F32),
                   jax.ShapeDtypeStruct((t, D_MODEL), BF16),
                   jax.ShapeDtypeStruct((t, ROUTE_W), F32),
                   jax.ShapeDtypeStruct((SUBLANES, t), F32)],
        compiler_params=_params("parallel"),
        name="mix",
    )(x2d, o, z_sm, gates, wau, wglu, bglu, wsu, wout, ffn_g, wr_hi, wr_lo, br, ltri)


def _moe_kernel(hn_ref, route_ref, route_t_ref, wg_ref, wu_ref, wd_ref, out_ref, acc_scr, wg_scr, wu_scr, wd_scr):
    n = pl.program_id(0)

    @pl.when(pl.program_id(1) == 0)
    def _():
        for j in range(EXPERTS_PER_GROUP):
            cols = slice(j * EXPERT_FF, (j + 1) * EXPERT_FF)
            wg_scr[:, cols] = wg_ref[j].astype(BF16)
            wu_scr[:, cols] = wu_ref[j].astype(BF16)
            wd_scr[cols, :] = wd_ref[j].astype(BF16)

    tm = hn_ref.shape[0]
    nf = n.astype(F32)
    route = route_ref[...]
    lane = lax.broadcasted_iota(jnp.int32, route.shape, 1)
    grp_c = jnp.sum(jnp.where(lane == GRP_LANE, route, 0.0), axis=-1, keepdims=True)
    rank_c = jnp.sum(jnp.where(lane == RANK_LANE, route, 0.0), axis=-1, keepdims=True)
    grp_r = route_t_ref[GRP_LANE:GRP_LANE + 1, :]
    rank_r = route_t_ref[RANK_LANE:RANK_LANE + 1, :]
    r1 = route.astype(BF16)
    r2 = (route - r1.astype(F32)).astype(BF16)
    r3 = (route - r1.astype(F32) - r2.astype(F32)).astype(BF16)
    pieces = jnp.concatenate([r1, r2, r3], axis=1)

    acc_scr[...] = jnp.zeros(acc_scr.shape, F32)
    n_rows = jnp.max(jnp.where(grp_c == nf, rank_c + 1.0, 0.0))
    n_pass = lax.div(n_rows.astype(jnp.int32) + (MOE_CAP - 1), MOE_CAP)

    def one_pass(k, carry):
        base = (k * MOE_CAP).astype(F32)
        slot_r = lax.broadcasted_iota(jnp.int32, (MOE_CAP, tm), 0).astype(F32) + base
        take = jnp.where((grp_r == nf) & (rank_r == slot_r), 1.0, 0.0).astype(BF16)
        slot_c = lax.broadcasted_iota(jnp.int32, (tm, MOE_CAP), 1).astype(F32) + base
        put = jnp.where((grp_c == nf) & (rank_c == slot_c), 1.0, 0.0).astype(BF16)
        xc = _dot(take, hn_ref[...]).astype(BF16)
        rc = _dot(take, pieces)
        rc = rc[:, :ROUTE_W] + rc[:, ROUTE_W:2 * ROUTE_W] + rc[:, 2 * ROUTE_W:]
        lane_c = lax.broadcasted_iota(jnp.int32, rc.shape, 1)
        first = ROUTE_OFF + n * EXPERTS_PER_GROUP
        act = jax.nn.silu(_dot(xc, wg_scr[...])) * _dot(xc, wu_scr[...])
        blocks = []
        for j in range(EXPERTS_PER_GROUP):
            w_j = jnp.sum(jnp.where(lane_c == first + j, rc, 0.0), axis=-1, keepdims=True)
            blocks.append((act[:, j * EXPERT_FF:(j + 1) * EXPERT_FF] * w_j).astype(BF16))
        y = _dot(jnp.concatenate(blocks, axis=1), wd_scr[...])
        acc_scr[...] += _dot(put, y.astype(BF16))
        return carry

    lax.fori_loop(0, n_pass, one_pass, 0)
    out_ref[...] = acc_scr[...].astype(BF16)


def _moe(hn, route, route_t, w_gate, w_up, w_down, *, layer):
    t = hn.shape[0]
    row = lambda w: pl.BlockSpec((TOK_TILE, w), lambda n, i: (i, 0))
    grp_w = lambda shape: pl.BlockSpec((None, None) + shape, lambda n, i: (layer, n, 0, 0, 0),
                                       pipeline_mode=pl.Buffered(1))
    up_shape = (EXPERTS_PER_GROUP, D_MODEL, EXPERT_FF)
    down_shape = (EXPERTS_PER_GROUP, EXPERT_FF, D_MODEL)
    ff = EXPERTS_PER_GROUP * EXPERT_FF
    return pl.pallas_call(
        _moe_kernel,
        grid=(N_GROUPS, t // TOK_TILE),
        in_specs=[row(D_MODEL), row(ROUTE_W), pl.BlockSpec((SUBLANES, TOK_TILE), lambda n, i: (0, i)),
                  grp_w(up_shape), grp_w(up_shape), grp_w(down_shape)],
        out_specs=pl.BlockSpec((None, TOK_TILE, D_MODEL), lambda n, i: (n, i, 0)),
        out_shape=jax.ShapeDtypeStruct((N_GROUPS, t, D_MODEL), BF16),
        scratch_shapes=[pltpu.VMEM((TOK_TILE, D_MODEL), F32), pltpu.VMEM((D_MODEL, ff), BF16),
                        pltpu.VMEM((D_MODEL, ff), BF16), pltpu.VMEM((ff, D_MODEL), BF16)],
        compiler_params=_params("arbitrary", "arbitrary"),
        name="moe",
    )(hn, route, route_t, w_gate, w_up, w_down)


def _ple_kernel(x1_ref, c_ref, p_ref, pg_ref, wpg_ref, wple_ref, out_ref):
    x2 = x1_ref[...]
    for n in range(N_GROUPS):
        x2 = x2 + c_ref[n].astype(F32)
    ms = jnp.mean(x2 * x2, axis=-1, keepdims=True)
    hp = (x2 * lax.rsqrt(ms + EPS) * pg_ref[...]).astype(BF16)
    gate = jax.nn.sigmoid(_dot(hp, wpg_ref[...]))
    out_ref[...] = x2 + gate * _dot(p_ref[...].astype(BF16), wple_ref[...])


def _ple(x1, contrib, p2d, ple_g, wpg, wple):
    t = x1.shape[0]
    row = lambda w: pl.BlockSpec((TOK_TILE, w), lambda i: (i, 0))
    return pl.pallas_call(
        _ple_kernel,
        grid=(t // TOK_TILE,),
        in_specs=[row(D_MODEL), pl.BlockSpec((N_GROUPS, TOK_TILE, D_MODEL), lambda i: (0, i, 0)),
                  row(PLE_DIM), _const_spec(ple_g.shape), _const_spec(wpg.shape), _const_spec(wple.shape)],
        out_specs=row(D_MODEL),
        out_shape=jax.ShapeDtypeStruct((t, D_MODEL), F32),
        compiler_params=_params("parallel"),
        name="ple",
    )(x1, contrib, p2d, ple_g, wpg, wple)


def kernel(x, p, attn_norm_g, w_in, b_gate, q_norm_g, k_norm_g, lam_qk, subln_g, w_attn_up, ssm_lam_re, ssm_lam_im, ssm_log_step, ssm_b_re, ssm_b_im, ssm_c_re, ssm_c_im, ssm_d, w_glu, b_glu, w_ssm_up, w_out, ffn_norm_g, w_router_group, b_router_group, w_router_expert, b_router_expert, w_exp_gate, w_exp_up, w_exp_down, ple_norm_g, w_ple_gate, w_ple):
    bsz, seq, d = x.shape
    depth = w_in.shape[0]
    t = bsz * seq
    assert d == D_MODEL and seq % TOK_TILE == 0 and seq % ATT_TILE == 0
    assert seq % SCAN_T == 0 and bsz == SUBLANES
    tok = jnp.arange(TOK_TILE)
    ltri = (tok[None, :] < tok[:, None]).astype(BF16)

    a_re, a_im, bb_re, bb_im, scalars = _prep(ssm_lam_re, ssm_lam_im, ssm_log_step, ssm_b_re, ssm_b_im, lam_qk,
                                              q_norm_g, k_norm_g)
    x2d = x.reshape(t, d)
    for i in range(depth):
        lam_init = _lambda_init(i)
        tile_gain = lambda g: jnp.tile(g.reshape(HEAD_W, 1), (COL_CHUNK // HEAD_W, 1))
        qt, k, vt, u_sm, gates = _inproj(
            x2d, attn_norm_g[i].reshape(1, d), w_in[i].astype(BF16),
            tile_gain(q_norm_g[i]), tile_gain(k_norm_g[i]), b_gate[i].reshape(1, -1), seq=seq)

        o = _attention(scalars[i], qt, k, vt, subln_g[i].reshape(1, HEAD_W), seq=seq, lam_init=lam_init)

        b_tiles, c_re_tiles, c_im_tiles = _ssm_matrices(bb_re[i], bb_im[i], ssm_c_re[i], ssm_c_im[i])
        z_tm = _ssm(u_sm.reshape(seq * bsz, SSM_WIDTH), b_tiles, c_re_tiles, c_im_tiles,
                    a_re[i, ::SSM_GROUP].reshape(1, N_STATE), a_im[i, ::SSM_GROUP].reshape(1, N_STATE),
                    ssm_d[i].reshape(1, SSM_WIDTH), bsz=bsz)

        w_r = jnp.concatenate(
            [w_router_group[i], jnp.transpose(w_router_expert[i], (1, 0, 2)).reshape(d, N_EXPERTS),
             jnp.zeros((d, ROUTE_W - N_GROUPS - N_EXPERTS), F32)], axis=1)
        w_r_hi = w_r.astype(BF16)
        w_r_lo = (w_r - w_r_hi.astype(F32)).astype(BF16)
        b_r = jnp.concatenate([b_router_group[i], b_router_expert[i].reshape(-1),
                               jnp.zeros((ROUTE_W - N_GROUPS - N_EXPERTS,), F32)]).reshape(1, ROUTE_W)
        x1, hn, route, route_t = _mix(
            x2d, o, z_tm.reshape(seq, bsz * SSM_WIDTH), gates,
            w_attn_up[i].astype(BF16), w_glu[i].astype(BF16), b_glu[i].reshape(1, -1),
            w_ssm_up[i].astype(BF16), w_out[i].astype(BF16), ffn_norm_g[i].reshape(1, d),
            w_r_hi, w_r_lo, b_r, ltri, seq=seq)

        contrib = _moe(hn, route, route_t, w_exp_gate, w_exp_up, w_exp_down, layer=i)
        x2d = _ple(x1, contrib, p[i].reshape(t, PLE_DIM), ple_norm_g[i].reshape(1, d),
                   w_ple_gate[i].astype(BF16), w_ple[i].astype(BF16))
    return x2d.reshape(bsz, seq, d)
```

```python
import functools
import math

import jax
import jax.numpy as jnp
from jax import lax
from jax.experimental import pallas as pl
from jax.experimental.pallas import tpu as pltpu

F32 = jnp.float32
BF16 = jnp.bfloat16

D_MODEL = 1024
ATT_HEADS = 8
ATT_HEAD_DIM = 64
HEAD_W = 2 * ATT_HEAD_DIM
ATT_QK_WIDTH = ATT_HEADS * HEAD_W
ATT_V_WIDTH = ATT_HEADS * HEAD_W
SSM_WIDTH = 512
SSM_GROUP = 16
SSM_GROUPS = SSM_WIDTH // SSM_GROUP
SSM_STATE = 64
N_STATE = SSM_GROUPS * SSM_STATE
N_BRANCH = 2
IN_WIDTH = 2 * ATT_QK_WIDTH + ATT_V_WIDTH + SSM_WIDTH + N_BRANCH * D_MODEL
N_GROUPS = 4
EXPERTS_PER_GROUP = 8
N_EXPERTS = N_GROUPS * EXPERTS_PER_GROUP
EXPERT_FF = 256
PLE_DIM = 256
EPS = 1e-6

LANES = 128
SUBLANES = 8
MXU_N = 256
VMEM_LIMIT = 56 * 1024 * 1024

TOK_TILE = 512
MOE_CAP = 160
COL_CHUNK = 512
ATT_TILE = 256
Q_SCALE = ATT_HEAD_DIM ** -0.5 * math.log2(math.e)
LAM_ROW = 0
BOUND_ROW = 1
ATT_BOUND_SLACK = 1.01
ATT_BOUND_LIMIT = 60.0
SCAN_T = 64
SCAN_LANES = 512
ROUTE_W = LANES
ROUTE_OFF = N_GROUPS
GRP_LANE = 0
RANK_LANE = 1


def _lambda_init(layer_idx):
    return 0.8 - 0.6 * math.exp(-0.3 * layer_idx)


def _dot(a, b):
    return jnp.dot(a, b, preferred_element_type=F32)


def _const_spec(shape):
    nd = len(shape)
    return pl.BlockSpec(shape, lambda *_: (0,) * nd, pipeline_mode=pl.Buffered(1))


def _params(*sem):
    return pltpu.CompilerParams(dimension_semantics=sem, vmem_limit_bytes=VMEM_LIMIT)


def _prep_kernel(lre_ref, lim_ref, lstep_ref, bre_ref, bim_ref, cre_ref, cim_ref, lqk_ref, qg_ref, kg_ref,
                 are_ref, aim_ref, bt_ref, ctre_ref, ctim_ref, lam_ref):
    lr = lre_ref[...]
    li = lim_ref[...]
    dt = jnp.exp(lstep_ref[...])
    mag = jnp.exp(lr * dt)
    ab_re = mag * jnp.cos(li * dt)
    ab_im = mag * jnp.sin(li * dt)
    den = lr * lr + li * li
    nr = ab_re - 1.0
    coef_re = (nr * lr + ab_im * li) / den
    coef_im = (ab_im * lr - nr * li) / den
    br = bre_ref[...]
    bi = bim_ref[...]
    are_ref[...] = ab_re
    aim_ref[...] = ab_im
    bb_re = coef_re * br - coef_im * bi
    bb_im = coef_re * bi + coef_im * br

    def spread(x, reps):
        w = x.shape[1]
        src = lax.broadcasted_iota(jnp.int32, (w, w * reps), 0)
        dst = lax.broadcasted_iota(jnp.int32, (w, w * reps), 1) & (w - 1)
        return _dot(x.astype(BF16), jnp.where(src == dst, 1.0, 0.0).astype(BF16))

    def same_group(shape, row0, row_shift, col0, col_shift):
        rows = (lax.broadcasted_iota(jnp.int32, shape, 0) + row0) >> row_shift
        cols = (lax.broadcasted_iota(jnp.int32, shape, 1) + col0) >> col_shift
        return rows == cols

    n_re = N_STATE // MXU_N
    ch_shift = SSM_GROUP.bit_length() - 1
    st_shift = SSM_STATE.bit_length() - 1
    for part, bb in enumerate((bb_re, bb_im)):
        for j in range(n_re):
            r0 = LANES * ((j * MXU_N // SSM_STATE * SSM_GROUP) // LANES)
            x = spread(bb[r0:r0 + LANES, :], MXU_N // SSM_STATE)
            keep = same_group(x.shape, r0, ch_shift, j * MXU_N, st_shift)
            bt_ref[part * n_re + j] = jnp.where(keep, x, 0.0).astype(BF16)
    half = N_STATE // 2
    for c_ref, ct_ref in ((cre_ref, ctre_ref), (cim_ref, ctim_ref)):
        for n in range(2):
            y = spread(c_ref[n * half:(n + 1) * half, :], MXU_N // SSM_GROUP)
            keep = same_group(y.shape, n * half, st_shift, n * MXU_N, ch_shift)
            ct_ref[n] = jnp.where(keep, y, 0.0).astype(BF16)

    lq = lqk_ref[...]
    s01 = jnp.sum(lq[0:1] * lq[1:2], axis=-1, keepdims=True)
    s23 = jnp.sum(lq[2:3] * lq[3:4], axis=-1, keepdims=True)
    lam_dyn = jnp.exp(s01) - jnp.exp(s23)
    g_max = jnp.max(jnp.abs(qg_ref[...]), keepdims=True) * jnp.max(jnp.abs(kg_ref[...]), keepdims=True)
    bound = g_max * (ATT_HEAD_DIM * Q_SCALE * ATT_BOUND_SLACK)
    row = lax.broadcasted_iota(jnp.int32, lam_ref.shape, 0)
    lam_ref[...] = jnp.where(row == LAM_ROW, lam_dyn, bound)


def _prep(lam_re, lam_im, log_step, b_re, b_im, c_re, c_im, lam_qk, q_gain, k_gain):
    depth = lam_re.shape[0]
    rows = SSM_WIDTH
    rep = lambda a: jnp.repeat(a, SSM_GROUP, axis=1)
    lstep = jnp.broadcast_to(log_step[:, :, None], lam_re.shape)
    tr_b = lambda b: jnp.transpose(b, (0, 1, 3, 2)).reshape(depth, rows, SSM_STATE)
    tr_c = lambda c: jnp.transpose(c, (0, 1, 3, 2)).reshape(depth, N_STATE, SSM_GROUP)
    lay = lambda *shape: pl.BlockSpec((None,) + shape, lambda i: (i,) + (0,) * len(shape))
    blk = lay(rows, SSM_STATE)
    out = jax.ShapeDtypeStruct((depth, rows, SSM_STATE), F32)
    n_bt = 2 * N_STATE // MXU_N
    return pl.pallas_call(
        _prep_kernel,
        grid=(depth,),
        in_specs=[blk, blk, blk, blk, blk, lay(N_STATE, SSM_GROUP), lay(N_STATE, SSM_GROUP),
                  lay(4, ATT_HEAD_DIM), lay(2, ATT_HEAD_DIM), lay(2, ATT_HEAD_DIM)],
        out_specs=[blk, blk, lay(n_bt, LANES, MXU_N), lay(2, N_STATE // 2, MXU_N), lay(2, N_STATE // 2, MXU_N),
                   lay(SUBLANES, LANES)],
        out_shape=[out, out, jax.ShapeDtypeStruct((depth, n_bt, LANES, MXU_N), BF16),
                   jax.ShapeDtypeStruct((depth, 2, N_STATE // 2, MXU_N), BF16),
                   jax.ShapeDtypeStruct((depth, 2, N_STATE // 2, MXU_N), BF16),
                   jax.ShapeDtypeStruct((depth, SUBLANES, LANES), F32)],
        compiler_params=_params("arbitrary"),
        name="prep",
    )(rep(lam_re), rep(lam_im), rep(lstep), tr_b(b_re), tr_b(b_im), tr_c(c_re), tr_c(c_im), lam_qk, q_gain, k_gain)


def _inproj_kernel(x_ref, g_ref, w_ref, qg_ref, kg_ref, bg_ref,
                   qt_ref, k_ref, vt_ref, u_ref, gate_ref):
    xf = x_ref[...]
    ms = jnp.mean(xf * xf, axis=-1, keepdims=True)
    h = (xf * lax.rsqrt(ms + EPS) * g_ref[...]).astype(BF16)

    def head_norm_t(y, gain_col):
        rows = y.shape[0]
        y3 = y.T.reshape(COL_CHUNK // ATT_HEAD_DIM, ATT_HEAD_DIM, rows)
        ms = jnp.sum(y3 * y3, axis=1, keepdims=True) * (1.0 / ATT_HEAD_DIM)
        return (y3 * lax.rsqrt(ms + EPS)).reshape(COL_CHUNK, rows) * gain_col

    n_qk = ATT_QK_WIDTH // COL_CHUNK
    n_v = ATT_V_WIDTH // COL_CHUNK
    for c in range(IN_WIDTH // COL_CHUNK):
        y = _dot(h, w_ref[:, c * COL_CHUNK:(c + 1) * COL_CHUNK])
        if c < n_qk:
            qt_ref[c * COL_CHUNK:(c + 1) * COL_CHUNK, :] = (head_norm_t(y, qg_ref[...]) * Q_SCALE).astype(BF16)
        elif c < 2 * n_qk:
            cc = c - n_qk
            k_ref[:, cc * COL_CHUNK:(cc + 1) * COL_CHUNK] = head_norm_t(y, kg_ref[...]).T.astype(BF16)
        elif c < 2 * n_qk + n_v:
            cc = c - 2 * n_qk
            vt_ref[cc * COL_CHUNK:(cc + 1) * COL_CHUNK, :] = y.T.astype(BF16)
        elif c == 2 * n_qk + n_v:
            u_ref[...] = y
        else:
            cc = c - (2 * n_qk + n_v + 1)
            b = bg_ref[:, cc * COL_CHUNK:(cc + 1) * COL_CHUNK]
            gate_ref[:, cc * COL_CHUNK:(cc + 1) * COL_CHUNK] = jax.nn.sigmoid(y + b).astype(BF16)


def _inproj(x2d, norm_g, w_in, qg, kg, b_gate, *, seq):
    t = x2d.shape[0]
    n_sb = seq // TOK_TILE
    bsz = t // seq
    row = lambda w: pl.BlockSpec((TOK_TILE, w), lambda i: (i, 0))
    col = lambda w: pl.BlockSpec((w, TOK_TILE), lambda i: (i // n_sb, i % n_sb))
    return pl.pallas_call(
        _inproj_kernel,
        grid=(t // TOK_TILE,),
        in_specs=[row(D_MODEL), _const_spec((1, D_MODEL)), _const_spec((D_MODEL, IN_WIDTH)),
                  _const_spec((COL_CHUNK, 1)), _const_spec((COL_CHUNK, 1)),
                  _const_spec((1, N_BRANCH * D_MODEL))],
        out_specs=[col(ATT_QK_WIDTH), row(ATT_QK_WIDTH), col(ATT_V_WIDTH),
                   pl.BlockSpec((TOK_TILE, SSM_WIDTH), lambda i: (i % n_sb, i // n_sb)),
                   row(N_BRANCH * D_MODEL)],
        out_shape=[jax.ShapeDtypeStruct((bsz * ATT_QK_WIDTH, seq), BF16),
                   jax.ShapeDtypeStruct((t, ATT_QK_WIDTH), BF16),
                   jax.ShapeDtypeStruct((bsz * ATT_V_WIDTH, seq), BF16),
                   jax.ShapeDtypeStruct((seq, bsz * SSM_WIDTH), F32),
                   jax.ShapeDtypeStruct((t, N_BRANCH * D_MODEL), BF16)],
        compiler_params=_params("parallel"),
        name="inproj",
    )(x2d, norm_g, w_in, qg, kg, b_gate)


def _attn_kernel(sc_ref, qt_ref, k_ref, vt_ref, g_ref, o_ref, *, lam_init):
    tq = ATT_TILE
    seq = k_ref.shape[0]
    nq = seq // tq
    lam = sc_ref[LAM_ROW, 0] + lam_init
    bound = sc_ref[BOUND_ROW, 0]
    head_row = lax.broadcasted_iota(jnp.int32, (HEAD_W, tq), 0)
    key_pos = lax.broadcasted_iota(jnp.int32, (tq, 2 * tq), 0)
    qry_pos = lax.broadcasted_iota(jnp.int32, (tq, 2 * tq), 1) & (tq - 1)
    causal = key_pos <= qry_pos
    zero = jnp.zeros((HEAD_W, tq), BF16)

    def tile(qi, shift_by_bound):
        qt = qt_ref[:, qi * tq:(qi + 1) * tq]
        qst = jnp.concatenate([jnp.where(head_row < ATT_HEAD_DIM, qt, zero),
                               jnp.where(head_row >= ATT_HEAD_DIM, qt, zero)], axis=1)
        nk = qi * tq
        s_d = jnp.where(causal, _dot(k_ref[nk:nk + tq, :], qst), -jnp.inf)
        if qi > 0:
            s_m = _dot(k_ref[0:nk, :], qst)
        if shift_by_bound:
            m = bound
        else:
            m = jnp.max(s_d, axis=0, keepdims=True)
            if qi > 0:
                m = jnp.maximum(m, jnp.max(s_m, axis=0, keepdims=True))
        p_d = jnp.exp2(s_d - m)
        l = jnp.sum(p_d, axis=0, keepdims=True)
        acc = _dot(vt_ref[:, nk:nk + tq], p_d.astype(BF16))
        if qi > 0:
            p_m = jnp.exp2(s_m - m)
            l = l + jnp.sum(p_m, axis=0, keepdims=True)
            acc = acc + _dot(vt_ref[:, 0:nk], p_m.astype(BF16))
        accn = acc * (1.0 / l)
        ot = accn[:, :tq] - lam * accn[:, tq:]
        ms = jnp.mean(ot * ot, axis=0, keepdims=True)
        o = (ot * lax.rsqrt(ms + EPS)).T * g_ref[...] * (1.0 - lam_init)
        o_ref[qi * tq:(qi + 1) * tq, :] = o.astype(BF16)

    @pl.when(bound < ATT_BOUND_LIMIT)
    def _():
        for qi in range(nq):
            tile(qi, True)

    @pl.when(jnp.logical_not(bound < ATT_BOUND_LIMIT))
    def _():
        for qi in range(nq):
            tile(qi, False)


def _attention(scalars, qt, k, vt, subln_g, *, seq, lam_init):
    t = k.shape[0]
    bsz = t // seq
    fm = pl.BlockSpec((HEAD_W, seq), lambda b, h: (b * ATT_HEADS + h, 0))
    tm = pl.BlockSpec((seq, HEAD_W), lambda b, h: (b, h))
    return pl.pallas_call(
        functools.partial(_attn_kernel, lam_init=lam_init),
        grid=(bsz, ATT_HEADS),
        in_specs=[pl.BlockSpec(memory_space=pltpu.SMEM), fm, tm, fm, _const_spec((1, HEAD_W))],
        out_specs=tm,
        out_shape=jax.ShapeDtypeStruct((t, ATT_V_WIDTH), BF16),
        compiler_params=_params("parallel", "parallel"),
        name="attn",
    )(scalars, qt, k, vt, subln_g)


def _ssm_kernel(u_ref, bt_ref, cre_ref, cim_ref, are_ref, aim_ref, d_ref, z_ref, st_scr, state_scr):
    rows = u_ref.shape[0]
    bsz = state_scr.shape[0]

    @pl.when(pl.program_id(0) == 0)
    def _():
        state_scr[...] = jnp.zeros(state_scr.shape, F32)

    u = u_ref[...]
    ub = u.astype(BF16)
    n_tiles = 2 * N_STATE // MXU_N
    for j in range(n_tiles):
        k0 = LANES * (((j % (n_tiles // 2)) * MXU_N // SSM_STATE * SSM_GROUP) // LANES)
        st_scr[:, j * MXU_N:(j + 1) * MXU_N] = _dot(ub[:, k0:k0 + LANES], bt_ref[j])

    for cc in range(N_STATE // SCAN_LANES):
        lo = cc * SCAN_LANES
        hi = N_STATE + lo
        a_r = jnp.broadcast_to(are_ref[:, lo:lo + SCAN_LANES], (bsz, SCAN_LANES))
        a_i = jnp.broadcast_to(aim_ref[:, lo:lo + SCAN_LANES], (bsz, SCAN_LANES))

        def step(t, carry, lo=lo, hi=hi, a_r=a_r, a_i=a_i):
            s_r, s_i = carry
            r0 = pl.multiple_of(t * bsz, bsz)
            n_r = a_r * s_r - a_i * s_i + st_scr[pl.ds(r0, bsz), lo:lo + SCAN_LANES]
            n_i = a_r * s_i + a_i * s_r + st_scr[pl.ds(r0, bsz), hi:hi + SCAN_LANES]
            st_scr[pl.ds(r0, bsz), lo:lo + SCAN_LANES] = n_r
            st_scr[pl.ds(r0, bsz), hi:hi + SCAN_LANES] = n_i
            return n_r, n_i

        s_r, s_i = lax.fori_loop(0, rows // bsz, step,
                                 (state_scr[:, lo:lo + SCAN_LANES], state_scr[:, hi:hi + SCAN_LANES]),
                                 unroll=8)
        state_scr[:, lo:lo + SCAN_LANES] = s_r
        state_scr[:, hi:hi + SCAN_LANES] = s_i

    half = N_STATE // 2
    for n in range(2):
        s_re = st_scr[:, n * half:(n + 1) * half].astype(BF16)
        s_im = st_scr[:, N_STATE + n * half:N_STATE + (n + 1) * half].astype(BF16)
        cols = slice(n * MXU_N, (n + 1) * MXU_N)
        y = _dot(s_re, cre_ref[n]) - _dot(s_im, cim_ref[n]) + d_ref[:, cols] * u[:, cols]
        z_ref[:, cols] = jax.nn.gelu(y).astype(BF16)


def _ssm(u_tm, b_tiles, c_re_tiles, c_im_tiles, a_re, a_im, d_skip, *, bsz):
    rows_total = u_tm.shape[0]
    rows = SCAN_T * bsz
    blk = pl.BlockSpec((rows, SSM_WIDTH), lambda i: (i, 0))
    return pl.pallas_call(
        _ssm_kernel,
        grid=(rows_total // rows,),
        in_specs=[blk, _const_spec(b_tiles.shape), _const_spec(c_re_tiles.shape),
                  _const_spec(c_im_tiles.shape), _const_spec((1, N_STATE)), _const_spec((1, N_STATE)),
                  _const_spec((1, SSM_WIDTH))],
        out_specs=blk,
        out_shape=jax.ShapeDtypeStruct((rows_total, SSM_WIDTH), BF16),
        scratch_shapes=[pltpu.VMEM((rows, 2 * N_STATE), F32), pltpu.VMEM((bsz, 2 * N_STATE), F32)],
        compiler_params=_params("arbitrary"),
        name="ssm",
    )(u_tm, b_tiles, c_re_tiles, c_im_tiles, a_re, a_im, d_skip)


def _route(logits, ltri):
    lane = lax.broadcasted_iota(jnp.int32, logits.shape, 1)
    lanef = lane.astype(F32)
    big = float(ROUTE_W)
    neg = -jnp.inf
    gl = jnp.where(lane < N_GROUPS, logits, neg)
    gmax = jnp.max(gl, axis=-1, keepdims=True)
    grp = jnp.min(jnp.where(gl == gmax, lanef, big), axis=-1, keepdims=True)
    p_grp = 1.0 / jnp.sum(jnp.exp(gl - gmax), axis=-1, keepdims=True)
    first = ROUTE_OFF + grp * EXPERTS_PER_GROUP
    in_grp = (lanef >= first) & (lanef < first + EXPERTS_PER_GROUP)
    el = jnp.where(in_grp, logits, neg)
    v1 = jnp.max(el, axis=-1, keepdims=True)
    i1 = jnp.min(jnp.where(el == v1, lanef, big), axis=-1, keepdims=True)
    el2 = jnp.where(lanef == i1, neg, el)
    v2 = jnp.max(el2, axis=-1, keepdims=True)
    i2 = jnp.min(jnp.where(el2 == v2, lanef, big), axis=-1, keepdims=True)
    e21 = jnp.exp(v2 - v1)
    w1 = p_grp / (1.0 + e21)
    w2 = p_grp * (e21 / (1.0 + e21))
    member = lanef == grp
    before = _dot(ltri, jnp.where(member, 1.0, 0.0).astype(BF16))
    rank = jnp.sum(jnp.where(member, before, 0.0), axis=-1, keepdims=True)
    return (jnp.where(lanef == i1, w1, 0.0) + jnp.where(lanef == i2, w2, 0.0)
            + jnp.where(lane == GRP_LANE, grp, 0.0) + jnp.where(lane == RANK_LANE, rank, 0.0))


def _mix_kernel(x_ref, o_ref, z_ref, gate_ref, wau_ref, wglu_ref, bglu_ref, wsu_ref, wout_ref,
                fg_ref, wrh_ref, wrl_ref, br_ref, ltri_ref, x1_ref, hn_ref, route_ref, route_t_ref):
    y_att = _dot(o_ref[...], wau_ref[...])
    z = z_ref[...].astype(F32)
    z = z * jax.nn.sigmoid(_dot(z_ref[...], wglu_ref[...]) + bglu_ref[...])
    y_ssm = _dot(z.astype(BF16), wsu_ref[...])
    mixed = (gate_ref[:, :D_MODEL].astype(F32) * y_att + gate_ref[:, D_MODEL:].astype(F32) * y_ssm)
    x1 = x_ref[...] + _dot(mixed.astype(BF16), wout_ref[...])
    x1_ref[...] = x1
    ms = jnp.mean(x1 * x1, axis=-1, keepdims=True)
    hn = x1 * lax.rsqrt(ms + EPS) * fg_ref[...]
    hn_hi = hn.astype(BF16)
    hn_ref[...] = hn_hi
    hn_lo = (hn - hn_hi.astype(F32)).astype(BF16)
    logits = (_dot(hn_hi, wrh_ref[...]) + _dot(hn_lo, wrh_ref[...]) + _dot(hn_hi, wrl_ref[...])
              + br_ref[...])
    route = _route(logits, ltri_ref[...])
    route_ref[...] = route
    route_t_ref[...] = route.T[:SUBLANES, :]


def _mix(x2d, o, z_sm, gates, wau, wglu, bglu, wsu, wout, ffn_g, wr_hi, wr_lo, br, ltri, *, seq):
    t = x2d.shape[0]
    n_sb = seq // TOK_TILE
    row = lambda w: pl.BlockSpec((TOK_TILE, w), lambda i: (i, 0))
    return pl.pallas_call(
        _mix_kernel,
        grid=(t // TOK_TILE,),
        in_specs=[row(D_MODEL), row(ATT_V_WIDTH),
                  pl.BlockSpec((TOK_TILE, SSM_WIDTH), lambda i: (i % n_sb, i // n_sb)),
                  row(N_BRANCH * D_MODEL),
                  _const_spec(wau.shape), _const_spec(wglu.shape), _const_spec(bglu.shape),
                  _const_spec(wsu.shape), _const_spec(wout.shape), _const_spec(ffn_g.shape),
                  _const_spec(wr_hi.shape), _const_spec(wr_lo.shape), _const_spec(br.shape),
                  _const_spec(ltri.shape)],
        out_specs=[row(D_MODEL), row(D_MODEL), row(ROUTE_W),
                   pl.BlockSpec((SUBLANES, TOK_TILE), lambda i: (0, i))],
        out_shape=[jax.ShapeDtypeStruct((t, D_MODEL), F32),
                   jax.ShapeDtypeStruct((t, D_MODEL), BF16),
                   jax.ShapeDtypeStruct((t, ROUTE_W), F32),
                   jax.ShapeDtypeStruct((SUBLANES, t), F32)],
        compiler_params=_params("parallel"),
        name="mix",
    )(x2d, o, z_sm, gates, wau, wglu, bglu, wsu, wout, ffn_g, wr_hi, wr_lo, br, ltri)


def _moe_kernel(hn_ref, route_ref, route_t_ref, wg_ref, wu_ref, wd_ref, out_ref, wg_scr, wu_scr, wd_scr):
    n = pl.program_id(0)

    @pl.when(pl.program_id(1) == 0)
    def _():
        for j in range(EXPERTS_PER_GROUP):
            cols = slice(j * EXPERT_FF, (j + 1) * EXPERT_FF)
            wg_scr[:, cols] = wg_ref[j].astype(BF16)
            wu_scr[:, cols] = wu_ref[j].astype(BF16)
            wd_scr[cols, :] = wd_ref[j].astype(BF16)

    tm = hn_ref.shape[0]
    nf = n.astype(F32)
    route = route_ref[...]
    lane = lax.broadcasted_iota(jnp.int32, route.shape, 1)
    grp_c = jnp.sum(jnp.where(lane == GRP_LANE, route, 0.0), axis=-1, keepdims=True)
    rank_c = jnp.sum(jnp.where(lane == RANK_LANE, route, 0.0), axis=-1, keepdims=True)
    grp_r = route_t_ref[GRP_LANE:GRP_LANE + 1, :]
    rank_r = route_t_ref[RANK_LANE:RANK_LANE + 1, :]
    r1 = route.astype(BF16)
    r2 = (route - r1.astype(F32)).astype(BF16)
    r3 = (route - r1.astype(F32) - r2.astype(F32)).astype(BF16)
    pieces = jnp.concatenate([r1, r2, r3], axis=1)

    def group_rows(base):
        slot_r = lax.broadcasted_iota(jnp.int32, (MOE_CAP, tm), 0).astype(F32) + base
        take = jnp.where((grp_r == nf) & (rank_r == slot_r), 1.0, 0.0).astype(BF16)
        slot_c = lax.broadcasted_iota(jnp.int32, (tm, MOE_CAP), 1).astype(F32) + base
        put = jnp.where((grp_c == nf) & (rank_c == slot_c), 1.0, 0.0).astype(BF16)
        xc = _dot(take, hn_ref[...]).astype(BF16)
        rc = _dot(take, pieces)
        rc = rc[:, :ROUTE_W] + rc[:, ROUTE_W:2 * ROUTE_W] + rc[:, 2 * ROUTE_W:]
        lane_c = lax.broadcasted_iota(jnp.int32, rc.shape, 1)
        first = ROUTE_OFF + n * EXPERTS_PER_GROUP
        act = jax.nn.silu(_dot(xc, wg_scr[...])) * _dot(xc, wu_scr[...])
        blocks = []
        for j in range(EXPERTS_PER_GROUP):
            w_j = jnp.sum(jnp.where(lane_c == first + j, rc, 0.0), axis=-1, keepdims=True)
            blocks.append((act[:, j * EXPERT_FF:(j + 1) * EXPERT_FF] * w_j).astype(BF16))
        y = _dot(jnp.concatenate(blocks, axis=1), wd_scr[...])
        return _dot(put, y.astype(BF16))

    out_ref[...] = group_rows(jnp.float32(0.0)).astype(BF16)

    n_rows = jnp.max(jnp.where(grp_c == nf, rank_c + 1.0, 0.0))
    n_pass = lax.div(n_rows.astype(jnp.int32) + (MOE_CAP - 1), MOE_CAP)

    def extra_pass(k, carry):
        more = group_rows((k * MOE_CAP).astype(F32))
        out_ref[...] = (out_ref[...].astype(F32) + more).astype(BF16)
        return carry

    lax.fori_loop(1, n_pass, extra_pass, 0)


def _moe(hn, route, route_t, w_gate, w_up, w_down, *, layer):
    t = hn.shape[0]
    row = lambda w: pl.BlockSpec((TOK_TILE, w), lambda n, i: (i, 0))
    grp_w = lambda shape: pl.BlockSpec((None, None) + shape, lambda n, i: (layer, n, 0, 0, 0),
                                       pipeline_mode=pl.Buffered(1))
    up_shape = (EXPERTS_PER_GROUP, D_MODEL, EXPERT_FF)
    down_shape = (EXPERTS_PER_GROUP, EXPERT_FF, D_MODEL)
    ff = EXPERTS_PER_GROUP * EXPERT_FF
    return pl.pallas_call(
        _moe_kernel,
        grid=(N_GROUPS, t // TOK_TILE),
        in_specs=[row(D_MODEL), row(ROUTE_W), pl.BlockSpec((SUBLANES, TOK_TILE), lambda n, i: (0, i)),
                  grp_w(up_shape), grp_w(up_shape), grp_w(down_shape)],
        out_specs=pl.BlockSpec((None, TOK_TILE, D_MODEL), lambda n, i: (n, i, 0)),
        out_shape=jax.ShapeDtypeStruct((N_GROUPS, t, D_MODEL), BF16),
        scratch_shapes=[pltpu.VMEM((D_MODEL, ff), BF16), pltpu.VMEM((D_MODEL, ff), BF16),
                        pltpu.VMEM((ff, D_MODEL), BF16)],
        compiler_params=_params("arbitrary", "arbitrary"),
        name="moe",
    )(hn, route, route_t, w_gate, w_up, w_down)


def _ple_kernel(x1_ref, c_ref, p_ref, pg_ref, wpg_ref, wple_ref, out_ref):
    x2 = x1_ref[...]
    for n in range(N_GROUPS):
        x2 = x2 + c_ref[n].astype(F32)
    ms = jnp.mean(x2 * x2, axis=-1, keepdims=True)
    hp = (x2 * lax.rsqrt(ms + EPS) * pg_ref[...]).astype(BF16)
    gate = jax.nn.sigmoid(_dot(hp, wpg_ref[...]))
    out_ref[...] = x2 + gate * _dot(p_ref[...].astype(BF16), wple_ref[...])


def _ple(x1, contrib, p2d, ple_g, wpg, wple):
    t = x1.shape[0]
    row = lambda w: pl.BlockSpec((TOK_TILE, w), lambda i: (i, 0))
    return pl.pallas_call(
        _ple_kernel,
        grid=(t // TOK_TILE,),
        in_specs=[row(D_MODEL), pl.BlockSpec((N_GROUPS, TOK_TILE, D_MODEL), lambda i: (0, i, 0)),
                  row(PLE_DIM), _const_spec(ple_g.shape), _const_spec(wpg.shape), _const_spec(wple.shape)],
        out_specs=row(D_MODEL),
        out_shape=jax.ShapeDtypeStruct((t, D_MODEL), F32),
        compiler_params=_params("parallel"),
        name="ple",
    )(x1, contrib, p2d, ple_g, wpg, wple)


def kernel(x, p, attn_norm_g, w_in, b_gate, q_norm_g, k_norm_g, lam_qk, subln_g, w_attn_up, ssm_lam_re, ssm_lam_im, ssm_log_step, ssm_b_re, ssm_b_im, ssm_c_re, ssm_c_im, ssm_d, w_glu, b_glu, w_ssm_up, w_out, ffn_norm_g, w_router_group, b_router_group, w_router_expert, b_router_expert, w_exp_gate, w_exp_up, w_exp_down, ple_norm_g, w_ple_gate, w_ple):
    bsz, seq, d = x.shape
    depth = w_in.shape[0]
    t = bsz * seq
    assert d == D_MODEL and seq % TOK_TILE == 0 and seq % ATT_TILE == 0
    assert seq % SCAN_T == 0 and bsz == SUBLANES
    tok = jnp.arange(TOK_TILE)
    ltri = (tok[None, :] < tok[:, None]).astype(BF16)

    a_re, a_im, b_tiles, c_re_tiles, c_im_tiles, scalars = _prep(
        ssm_lam_re, ssm_lam_im, ssm_log_step, ssm_b_re, ssm_b_im, ssm_c_re, ssm_c_im, lam_qk, q_norm_g, k_norm_g)
    x2d = x.reshape(t, d)
    for i in range(depth):
        lam_init = _lambda_init(i)
        tile_gain = lambda g: jnp.tile(g.reshape(HEAD_W, 1), (COL_CHUNK // HEAD_W, 1))
        qt, k, vt, u_sm, gates = _inproj(
            x2d, attn_norm_g[i].reshape(1, d), w_in[i].astype(BF16),
            tile_gain(q_norm_g[i]), tile_gain(k_norm_g[i]), b_gate[i].reshape(1, -1), seq=seq)

        o = _attention(scalars[i], qt, k, vt, subln_g[i].reshape(1, HEAD_W), seq=seq, lam_init=lam_init)

        z_tm = _ssm(u_sm.reshape(seq * bsz, SSM_WIDTH), b_tiles[i], c_re_tiles[i], c_im_tiles[i],
                    a_re[i, ::SSM_GROUP].reshape(1, N_STATE), a_im[i, ::SSM_GROUP].reshape(1, N_STATE),
                    ssm_d[i].reshape(1, SSM_WIDTH), bsz=bsz)

        w_r = jnp.concatenate(
            [w_router_group[i], jnp.transpose(w_router_expert[i], (1, 0, 2)).reshape(d, N_EXPERTS),
             jnp.zeros((d, ROUTE_W - N_GROUPS - N_EXPERTS), F32)], axis=1)
        w_r_hi = w_r.astype(BF16)
        w_r_lo = (w_r - w_r_hi.astype(F32)).astype(BF16)
        b_r = jnp.concatenate([b_router_group[i], b_router_expert[i].reshape(-1),
                               jnp.zeros((ROUTE_W - N_GROUPS - N_EXPERTS,), F32)]).reshape(1, ROUTE_W)
        x1, hn, route, route_t = _mix(
            x2d, o, z_tm.reshape(seq, bsz * SSM_WIDTH), gates,
            w_attn_up[i].astype(BF16), w_glu[i].astype(BF16), b_glu[i].reshape(1, -1),
            w_ssm_up[i].astype(BF16), w_out[i].astype(BF16), ffn_norm_g[i].reshape(1, d),
            w_r_hi, w_r_lo, b_r, ltri, seq=seq)

        contrib = _moe(hn, route, route_t, w_exp_gate, w_exp_up, w_exp_down, layer=i)
        x2d = _ple(x1, contrib, p[i].reshape(t, PLE_DIM), ple_norm_g[i].reshape(1, d),
                   w_ple_gate[i].astype(BF16), w_ple[i].astype(BF16))
    return x2d.reshape(bsz, seq, d)
```

```python
import functools
import math

import jax
import jax.numpy as jnp
from jax import lax
from jax.experimental import pallas as pl
from jax.experimental.pallas import tpu as pltpu

F32 = jnp.float32
BF16 = jnp.bfloat16

D_MODEL = 1024
ATT_HEADS = 8
ATT_HEAD_DIM = 64
HEAD_W = 2 * ATT_HEAD_DIM
ATT_QK_WIDTH = ATT_HEADS * HEAD_W
ATT_V_WIDTH = ATT_HEADS * HEAD_W
SSM_WIDTH = 512
SSM_GROUP = 16
SSM_GROUPS = SSM_WIDTH // SSM_GROUP
SSM_STATE = 64
N_STATE = SSM_GROUPS * SSM_STATE
N_BRANCH = 2
IN_WIDTH = 2 * ATT_QK_WIDTH + ATT_V_WIDTH + SSM_WIDTH + N_BRANCH * D_MODEL
N_GROUPS = 4
EXPERTS_PER_GROUP = 8
N_EXPERTS = N_GROUPS * EXPERTS_PER_GROUP
EXPERT_FF = 256
PLE_DIM = 256
EPS = 1e-6

LANES = 128
SUBLANES = 8
MXU_N = 256
VMEM_LIMIT = 56 * 1024 * 1024

TOK_TILE = 512
MOE_CAP = 160
COL_CHUNK = 512
ATT_TILE = 256
Q_SCALE = ATT_HEAD_DIM ** -0.5 * math.log2(math.e)
LAM_ROW = 0
BOUND_ROW = 1
ATT_BOUND_SLACK = 1.01
ATT_BOUND_LIMIT = 60.0
SCAN_T = 64
SCAN_LANES = 512
ROUTE_W = LANES
ROUTE_OFF = N_GROUPS
GRP_LANE = 0
RANK_LANE = 1


def _lambda_init(layer_idx):
    return 0.8 - 0.6 * math.exp(-0.3 * layer_idx)


def _dot(a, b):
    return jnp.dot(a, b, preferred_element_type=F32)


def _const_spec(shape):
    nd = len(shape)
    return pl.BlockSpec(shape, lambda *_: (0,) * nd, pipeline_mode=pl.Buffered(1))


def _layer_weight(stacked, layer):
    shape = stacked.shape[1:]
    return pl.BlockSpec((None,) + shape, lambda *_: (layer,) + (0,) * len(shape), pipeline_mode=pl.Buffered(1))


def _round_once(pairs):
    @pl.when(pl.program_id(0) == 0)
    def _():
        for w_ref, w_scr in pairs:
            w_scr[...] = w_ref[...].astype(BF16)


def _params(*sem):
    return pltpu.CompilerParams(dimension_semantics=sem, vmem_limit_bytes=VMEM_LIMIT)


def _prep_kernel(lre_ref, lim_ref, lstep_ref, bre_ref, bim_ref, cre_ref, cim_ref, lqk_ref, qg_ref, kg_ref,
                 are_ref, aim_ref, bt_ref, ctre_ref, ctim_ref, lam_ref):
    lr = lre_ref[...]
    li = lim_ref[...]
    dt = jnp.exp(lstep_ref[...])
    mag = jnp.exp(lr * dt)
    ab_re = mag * jnp.cos(li * dt)
    ab_im = mag * jnp.sin(li * dt)
    den = lr * lr + li * li
    nr = ab_re - 1.0
    coef_re = (nr * lr + ab_im * li) / den
    coef_im = (ab_im * lr - nr * li) / den
    br = bre_ref[...]
    bi = bim_ref[...]
    are_ref[...] = ab_re
    aim_ref[...] = ab_im
    bb_re = coef_re * br - coef_im * bi
    bb_im = coef_re * bi + coef_im * br

    def spread(x, reps):
        w = x.shape[1]
        src = lax.broadcasted_iota(jnp.int32, (w, w * reps), 0)
        dst = lax.broadcasted_iota(jnp.int32, (w, w * reps), 1) & (w - 1)
        return _dot(x.astype(BF16), jnp.where(src == dst, 1.0, 0.0).astype(BF16))

    def same_group(shape, row0, row_shift, col0, col_shift):
        rows = (lax.broadcasted_iota(jnp.int32, shape, 0) + row0) >> row_shift
        cols = (lax.broadcasted_iota(jnp.int32, shape, 1) + col0) >> col_shift
        return rows == cols

    n_re = N_STATE // MXU_N
    ch_shift = SSM_GROUP.bit_length() - 1
    st_shift = SSM_STATE.bit_length() - 1
    for part, bb in enumerate((bb_re, bb_im)):
        for j in range(n_re):
            r0 = LANES * ((j * MXU_N // SSM_STATE * SSM_GROUP) // LANES)
            x = spread(bb[r0:r0 + LANES, :], MXU_N // SSM_STATE)
            keep = same_group(x.shape, r0, ch_shift, j * MXU_N, st_shift)
            bt_ref[part * n_re + j] = jnp.where(keep, x, 0.0).astype(BF16)
    half = N_STATE // 2
    for c_ref, ct_ref in ((cre_ref, ctre_ref), (cim_ref, ctim_ref)):
        for n in range(2):
            y = spread(c_ref[n * half:(n + 1) * half, :], MXU_N // SSM_GROUP)
            keep = same_group(y.shape, n * half, st_shift, n * MXU_N, ch_shift)
            ct_ref[n] = jnp.where(keep, y, 0.0).astype(BF16)

    lq = lqk_ref[...]
    s01 = jnp.sum(lq[0:1] * lq[1:2], axis=-1, keepdims=True)
    s23 = jnp.sum(lq[2:3] * lq[3:4], axis=-1, keepdims=True)
    lam_dyn = jnp.exp(s01) - jnp.exp(s23)
    g_max = jnp.max(jnp.abs(qg_ref[...]), keepdims=True) * jnp.max(jnp.abs(kg_ref[...]), keepdims=True)
    bound = g_max * (ATT_HEAD_DIM * Q_SCALE * ATT_BOUND_SLACK)
    row = lax.broadcasted_iota(jnp.int32, lam_ref.shape, 0)
    lam_ref[...] = jnp.where(row == LAM_ROW, lam_dyn, bound)


def _prep(lam_re, lam_im, log_step, b_re, b_im, c_re, c_im, lam_qk, q_gain, k_gain):
    depth = lam_re.shape[0]
    rows = SSM_WIDTH
    rep = lambda a: jnp.repeat(a, SSM_GROUP, axis=1)
    lstep = jnp.broadcast_to(log_step[:, :, None], lam_re.shape)
    tr_b = lambda b: jnp.transpose(b, (0, 1, 3, 2)).reshape(depth, rows, SSM_STATE)
    tr_c = lambda c: jnp.transpose(c, (0, 1, 3, 2)).reshape(depth, N_STATE, SSM_GROUP)
    lay = lambda *shape: pl.BlockSpec((None,) + shape, lambda i: (i,) + (0,) * len(shape))
    blk = lay(rows, SSM_STATE)
    out = jax.ShapeDtypeStruct((depth, rows, SSM_STATE), F32)
    n_bt = 2 * N_STATE // MXU_N
    return pl.pallas_call(
        _prep_kernel,
        grid=(depth,),
        in_specs=[blk, blk, blk, blk, blk, lay(N_STATE, SSM_GROUP), lay(N_STATE, SSM_GROUP),
                  lay(4, ATT_HEAD_DIM), lay(2, ATT_HEAD_DIM), lay(2, ATT_HEAD_DIM)],
        out_specs=[blk, blk, lay(n_bt, LANES, MXU_N), lay(2, N_STATE // 2, MXU_N), lay(2, N_STATE // 2, MXU_N),
                   lay(SUBLANES, LANES)],
        out_shape=[out, out, jax.ShapeDtypeStruct((depth, n_bt, LANES, MXU_N), BF16),
                   jax.ShapeDtypeStruct((depth, 2, N_STATE // 2, MXU_N), BF16),
                   jax.ShapeDtypeStruct((depth, 2, N_STATE // 2, MXU_N), BF16),
                   jax.ShapeDtypeStruct((depth, SUBLANES, LANES), F32)],
        compiler_params=_params("arbitrary"),
        name="prep",
    )(rep(lam_re), rep(lam_im), rep(lstep), tr_b(b_re), tr_b(b_im), tr_c(c_re), tr_c(c_im), lam_qk, q_gain, k_gain)


def _inproj_kernel(x_ref, g_ref, w_ref, qg_ref, kg_ref, bg_ref,
                   qt_ref, k_ref, vt_ref, u_ref, gate_ref, w_scr):
    _round_once([(w_ref, w_scr)])
    xf = x_ref[...]
    ms = jnp.mean(xf * xf, axis=-1, keepdims=True)
    h = (xf * lax.rsqrt(ms + EPS) * g_ref[...]).astype(BF16)

    def head_norm_t(y, gain_col):
        rows = y.shape[0]
        y3 = y.T.reshape(COL_CHUNK // ATT_HEAD_DIM, ATT_HEAD_DIM, rows)
        ms = jnp.sum(y3 * y3, axis=1, keepdims=True) * (1.0 / ATT_HEAD_DIM)
        return (y3 * lax.rsqrt(ms + EPS)).reshape(COL_CHUNK, rows) * gain_col

    n_qk = ATT_QK_WIDTH // COL_CHUNK
    n_v = ATT_V_WIDTH // COL_CHUNK
    for c in range(IN_WIDTH // COL_CHUNK):
        y = _dot(h, w_scr[:, c * COL_CHUNK:(c + 1) * COL_CHUNK])
        if c < n_qk:
            qt_ref[c * COL_CHUNK:(c + 1) * COL_CHUNK, :] = (head_norm_t(y, qg_ref[...]) * Q_SCALE).astype(BF16)
        elif c < 2 * n_qk:
            cc = c - n_qk
            k_ref[:, cc * COL_CHUNK:(cc + 1) * COL_CHUNK] = head_norm_t(y, kg_ref[...]).T.astype(BF16)
        elif c < 2 * n_qk + n_v:
            cc = c - 2 * n_qk
            vt_ref[cc * COL_CHUNK:(cc + 1) * COL_CHUNK, :] = y.T.astype(BF16)
        elif c == 2 * n_qk + n_v:
            u_ref[...] = y
        else:
            cc = c - (2 * n_qk + n_v + 1)
            b = bg_ref[:, cc * COL_CHUNK:(cc + 1) * COL_CHUNK]
            gate_ref[:, cc * COL_CHUNK:(cc + 1) * COL_CHUNK] = jax.nn.sigmoid(y + b).astype(BF16)


def _inproj(x2d, norm_g, w_in, qg, kg, b_gate, *, seq, layer):
    t = x2d.shape[0]
    n_sb = seq // TOK_TILE
    bsz = t // seq
    row = lambda w: pl.BlockSpec((TOK_TILE, w), lambda i: (i, 0))
    col = lambda w: pl.BlockSpec((w, TOK_TILE), lambda i: (i // n_sb, i % n_sb))
    return pl.pallas_call(
        _inproj_kernel,
        grid=(t // TOK_TILE,),
        in_specs=[row(D_MODEL), _const_spec((1, D_MODEL)), _layer_weight(w_in, layer),
                  _const_spec((COL_CHUNK, 1)), _const_spec((COL_CHUNK, 1)),
                  _const_spec((1, N_BRANCH * D_MODEL))],
        out_specs=[col(ATT_QK_WIDTH), row(ATT_QK_WIDTH), col(ATT_V_WIDTH),
                   pl.BlockSpec((TOK_TILE, SSM_WIDTH), lambda i: (i % n_sb, i // n_sb)),
                   row(N_BRANCH * D_MODEL)],
        out_shape=[jax.ShapeDtypeStruct((bsz * ATT_QK_WIDTH, seq), BF16),
                   jax.ShapeDtypeStruct((t, ATT_QK_WIDTH), BF16),
                   jax.ShapeDtypeStruct((bsz * ATT_V_WIDTH, seq), BF16),
                   jax.ShapeDtypeStruct((seq, bsz * SSM_WIDTH), F32),
                   jax.ShapeDtypeStruct((t, N_BRANCH * D_MODEL), BF16)],
        scratch_shapes=[pltpu.VMEM((D_MODEL, IN_WIDTH), BF16)],
        compiler_params=_params("arbitrary"),
        name="inproj",
    )(x2d, norm_g, w_in, qg, kg, b_gate)


def _attn_kernel(sc_ref, qt_ref, k_ref, vt_ref, g_ref, o_ref, *, lam_init):
    tq = ATT_TILE
    seq = k_ref.shape[0]
    nq = seq // tq
    lam = sc_ref[LAM_ROW, 0] + lam_init
    bound = sc_ref[BOUND_ROW, 0]
    head_row = lax.broadcasted_iota(jnp.int32, (HEAD_W, tq), 0)
    key_pos = lax.broadcasted_iota(jnp.int32, (tq, 2 * tq), 0)
    qry_pos = lax.broadcasted_iota(jnp.int32, (tq, 2 * tq), 1) & (tq - 1)
    causal = key_pos <= qry_pos
    zero = jnp.zeros((HEAD_W, tq), BF16)

    def tile(qi, shift_by_bound):
        qt = qt_ref[:, qi * tq:(qi + 1) * tq]
        qst = jnp.concatenate([jnp.where(head_row < ATT_HEAD_DIM, qt, zero),
                               jnp.where(head_row >= ATT_HEAD_DIM, qt, zero)], axis=1)
        nk = qi * tq
        s_d = jnp.where(causal, _dot(k_ref[nk:nk + tq, :], qst), -jnp.inf)
        if qi > 0:
            s_m = _dot(k_ref[0:nk, :], qst)
        if shift_by_bound:
            m = bound
        else:
            m = jnp.max(s_d, axis=0, keepdims=True)
            if qi > 0:
                m = jnp.maximum(m, jnp.max(s_m, axis=0, keepdims=True))
        p_d = jnp.exp2(s_d - m)
        l = jnp.sum(p_d, axis=0, keepdims=True)
        acc = _dot(vt_ref[:, nk:nk + tq], p_d.astype(BF16))
        if qi > 0:
            p_m = jnp.exp2(s_m - m)
            l = l + jnp.sum(p_m, axis=0, keepdims=True)
            acc = acc + _dot(vt_ref[:, 0:nk], p_m.astype(BF16))
        accn = acc * (1.0 / l)
        ot = accn[:, :tq] - lam * accn[:, tq:]
        ms = jnp.mean(ot * ot, axis=0, keepdims=True)
        o = (ot * lax.rsqrt(ms + EPS)).T * g_ref[...] * (1.0 - lam_init)
        o_ref[qi * tq:(qi + 1) * tq, :] = o.astype(BF16)

    @pl.when(bound < ATT_BOUND_LIMIT)
    def _():
        for qi in range(nq):
            tile(qi, True)

    @pl.when(jnp.logical_not(bound < ATT_BOUND_LIMIT))
    def _():
        for qi in range(nq):
            tile(qi, False)


def _attention(scalars, qt, k, vt, subln_g, *, seq, lam_init):
    t = k.shape[0]
    bsz = t // seq
    fm = pl.BlockSpec((HEAD_W, seq), lambda b, h: (b * ATT_HEADS + h, 0))
    tm = pl.BlockSpec((seq, HEAD_W), lambda b, h: (b, h))
    return pl.pallas_call(
        functools.partial(_attn_kernel, lam_init=lam_init),
        grid=(bsz, ATT_HEADS),
        in_specs=[pl.BlockSpec(memory_space=pltpu.SMEM), fm, tm, fm, _const_spec((1, HEAD_W))],
        out_specs=tm,
        out_shape=jax.ShapeDtypeStruct((t, ATT_V_WIDTH), BF16),
        compiler_params=_params("parallel", "parallel"),
        name="attn",
    )(scalars, qt, k, vt, subln_g)


def _ssm_kernel(u_ref, bt_ref, cre_ref, cim_ref, are_ref, aim_ref, d_ref, z_ref, st_scr, state_scr):
    rows = u_ref.shape[0]
    bsz = state_scr.shape[0]

    @pl.when(pl.program_id(0) == 0)
    def _():
        state_scr[...] = jnp.zeros(state_scr.shape, F32)

    u = u_ref[...]
    ub = u.astype(BF16)
    n_tiles = 2 * N_STATE // MXU_N
    for j in range(n_tiles):
        k0 = LANES * (((j % (n_tiles // 2)) * MXU_N // SSM_STATE * SSM_GROUP) // LANES)
        st_scr[:, j * MXU_N:(j + 1) * MXU_N] = _dot(ub[:, k0:k0 + LANES], bt_ref[j])

    for cc in range(N_STATE // SCAN_LANES):
        lo = cc * SCAN_LANES
        hi = N_STATE + lo
        a_r = jnp.broadcast_to(are_ref[:, lo:lo + SCAN_LANES], (bsz, SCAN_LANES))
        a_i = jnp.broadcast_to(aim_ref[:, lo:lo + SCAN_LANES], (bsz, SCAN_LANES))

        def step(t, carry, lo=lo, hi=hi, a_r=a_r, a_i=a_i):
            s_r, s_i = carry
            r0 = pl.multiple_of(t * bsz, bsz)
            n_r = a_r * s_r - a_i * s_i + st_scr[pl.ds(r0, bsz), lo:lo + SCAN_LANES]
            n_i = a_r * s_i + a_i * s_r + st_scr[pl.ds(r0, bsz), hi:hi + SCAN_LANES]
            st_scr[pl.ds(r0, bsz), lo:lo + SCAN_LANES] = n_r
            st_scr[pl.ds(r0, bsz), hi:hi + SCAN_LANES] = n_i
            return n_r, n_i

        s_r, s_i = lax.fori_loop(0, rows // bsz, step,
                                 (state_scr[:, lo:lo + SCAN_LANES], state_scr[:, hi:hi + SCAN_LANES]),
                                 unroll=8)
        state_scr[:, lo:lo + SCAN_LANES] = s_r
        state_scr[:, hi:hi + SCAN_LANES] = s_i

    half = N_STATE // 2
    for n in range(2):
        s_re = st_scr[:, n * half:(n + 1) * half].astype(BF16)
        s_im = st_scr[:, N_STATE + n * half:N_STATE + (n + 1) * half].astype(BF16)
        cols = slice(n * MXU_N, (n + 1) * MXU_N)
        y = _dot(s_re, cre_ref[n]) - _dot(s_im, cim_ref[n]) + d_ref[:, cols] * u[:, cols]
        z_ref[:, cols] = jax.nn.gelu(y).astype(BF16)


def _ssm(u_tm, b_tiles, c_re_tiles, c_im_tiles, a_re, a_im, d_skip, *, bsz):
    rows_total = u_tm.shape[0]
    rows = SCAN_T * bsz
    blk = pl.BlockSpec((rows, SSM_WIDTH), lambda i: (i, 0))
    return pl.pallas_call(
        _ssm_kernel,
        grid=(rows_total // rows,),
        in_specs=[blk, _const_spec(b_tiles.shape), _const_spec(c_re_tiles.shape),
                  _const_spec(c_im_tiles.shape), _const_spec((1, N_STATE)), _const_spec((1, N_STATE)),
                  _const_spec((1, SSM_WIDTH))],
        out_specs=blk,
        out_shape=jax.ShapeDtypeStruct((rows_total, SSM_WIDTH), BF16),
        scratch_shapes=[pltpu.VMEM((rows, 2 * N_STATE), F32), pltpu.VMEM((bsz, 2 * N_STATE), F32)],
        compiler_params=_params("arbitrary"),
        name="ssm",
    )(u_tm, b_tiles, c_re_tiles, c_im_tiles, a_re, a_im, d_skip)


def _route(logits, ltri):
    lane = lax.broadcasted_iota(jnp.int32, logits.shape, 1)
    lanef = lane.astype(F32)
    big = float(ROUTE_W)
    neg = -jnp.inf
    gl = jnp.where(lane < N_GROUPS, logits, neg)
    gmax = jnp.max(gl, axis=-1, keepdims=True)
    grp = jnp.min(jnp.where(gl == gmax, lanef, big), axis=-1, keepdims=True)
    p_grp = 1.0 / jnp.sum(jnp.exp(gl - gmax), axis=-1, keepdims=True)
    first = ROUTE_OFF + grp * EXPERTS_PER_GROUP
    in_grp = (lanef >= first) & (lanef < first + EXPERTS_PER_GROUP)
    el = jnp.where(in_grp, logits, neg)
    v1 = jnp.max(el, axis=-1, keepdims=True)
    i1 = jnp.min(jnp.where(el == v1, lanef, big), axis=-1, keepdims=True)
    el2 = jnp.where(lanef == i1, neg, el)
    v2 = jnp.max(el2, axis=-1, keepdims=True)
    i2 = jnp.min(jnp.where(el2 == v2, lanef, big), axis=-1, keepdims=True)
    e21 = jnp.exp(v2 - v1)
    w1 = p_grp / (1.0 + e21)
    w2 = p_grp * (e21 / (1.0 + e21))
    member = lanef == grp
    before = _dot(ltri, jnp.where(member, 1.0, 0.0).astype(BF16))
    rank = jnp.sum(jnp.where(member, before, 0.0), axis=-1, keepdims=True)
    return (jnp.where(lanef == i1, w1, 0.0) + jnp.where(lanef == i2, w2, 0.0)
            + jnp.where(lane == GRP_LANE, grp, 0.0) + jnp.where(lane == RANK_LANE, rank, 0.0))


def _mix_kernel(x_ref, o_ref, z_ref, gate_ref, wau_ref, wglu_ref, bglu_ref, wsu_ref, wout_ref,
                fg_ref, wrh_ref, wrl_ref, br_ref, ltri_ref, x1_ref, hn_ref, route_ref, route_t_ref,
                wau_scr, wglu_scr, wsu_scr, wout_scr):
    _round_once([(wau_ref, wau_scr), (wglu_ref, wglu_scr), (wsu_ref, wsu_scr), (wout_ref, wout_scr)])
    y_att = _dot(o_ref[...], wau_scr[...])
    z = z_ref[...].astype(F32)
    z = z * jax.nn.sigmoid(_dot(z_ref[...], wglu_scr[...]) + bglu_ref[...])
    y_ssm = _dot(z.astype(BF16), wsu_scr[...])
    mixed = (gate_ref[:, :D_MODEL].astype(F32) * y_att + gate_ref[:, D_MODEL:].astype(F32) * y_ssm)
    x1 = x_ref[...] + _dot(mixed.astype(BF16), wout_scr[...])
    x1_ref[...] = x1
    ms = jnp.mean(x1 * x1, axis=-1, keepdims=True)
    hn = x1 * lax.rsqrt(ms + EPS) * fg_ref[...]
    hn_hi = hn.astype(BF16)
    hn_ref[...] = hn_hi
    hn_lo = (hn - hn_hi.astype(F32)).astype(BF16)
    logits = (_dot(hn_hi, wrh_ref[...]) + _dot(hn_lo, wrh_ref[...]) + _dot(hn_hi, wrl_ref[...])
              + br_ref[...])
    route = _route(logits, ltri_ref[...])
    route_ref[...] = route
    route_t_ref[...] = route.T[:SUBLANES, :]


def _mix(x2d, o, z_sm, gates, wau, wglu, bglu, wsu, wout, ffn_g, wr_hi, wr_lo, br, ltri, *, seq, layer):
    t = x2d.shape[0]
    n_sb = seq // TOK_TILE
    row = lambda w: pl.BlockSpec((TOK_TILE, w), lambda i: (i, 0))
    return pl.pallas_call(
        _mix_kernel,
        grid=(t // TOK_TILE,),
        in_specs=[row(D_MODEL), row(ATT_V_WIDTH),
                  pl.BlockSpec((TOK_TILE, SSM_WIDTH), lambda i: (i % n_sb, i // n_sb)),
                  row(N_BRANCH * D_MODEL),
                  _layer_weight(wau, layer), _layer_weight(wglu, layer), _const_spec(bglu.shape),
                  _layer_weight(wsu, layer), _layer_weight(wout, layer), _const_spec(ffn_g.shape),
                  _const_spec(wr_hi.shape), _const_spec(wr_lo.shape), _const_spec(br.shape),
                  _const_spec(ltri.shape)],
        out_specs=[row(D_MODEL), row(D_MODEL), row(ROUTE_W),
                   pl.BlockSpec((SUBLANES, TOK_TILE), lambda i: (0, i))],
        out_shape=[jax.ShapeDtypeStruct((t, D_MODEL), F32),
                   jax.ShapeDtypeStruct((t, D_MODEL), BF16),
                   jax.ShapeDtypeStruct((t, ROUTE_W), F32),
                   jax.ShapeDtypeStruct((SUBLANES, t), F32)],
        scratch_shapes=[pltpu.VMEM(w.shape[1:], BF16) for w in (wau, wglu, wsu, wout)],
        compiler_params=_params("arbitrary"),
        name="mix",
    )(x2d, o, z_sm, gates, wau, wglu, bglu, wsu, wout, ffn_g, wr_hi, wr_lo, br, ltri)


def _moe_kernel(hn_ref, route_ref, route_t_ref, wg_ref, wu_ref, wd_ref, out_ref, wg_scr, wu_scr, wd_scr):
    n = pl.program_id(0)

    @pl.when(pl.program_id(1) == 0)
    def _():
        for j in range(EXPERTS_PER_GROUP):
            cols = slice(j * EXPERT_FF, (j + 1) * EXPERT_FF)
            wg_scr[:, cols] = wg_ref[j].astype(BF16)
            wu_scr[:, cols] = wu_ref[j].astype(BF16)
            wd_scr[cols, :] = wd_ref[j].astype(BF16)

    tm = hn_ref.shape[0]
    nf = n.astype(F32)
    route = route_ref[...]
    lane = lax.broadcasted_iota(jnp.int32, route.shape, 1)
    grp_c = jnp.sum(jnp.where(lane == GRP_LANE, route, 0.0), axis=-1, keepdims=True)
    rank_c = jnp.sum(jnp.where(lane == RANK_LANE, route, 0.0), axis=-1, keepdims=True)
    grp_r = route_t_ref[GRP_LANE:GRP_LANE + 1, :]
    rank_r = route_t_ref[RANK_LANE:RANK_LANE + 1, :]
    r1 = route.astype(BF16)
    r2 = (route - r1.astype(F32)).astype(BF16)
    r3 = (route - r1.astype(F32) - r2.astype(F32)).astype(BF16)
    pieces = jnp.concatenate([r1, r2, r3], axis=1)

    def group_rows(base):
        slot_r = lax.broadcasted_iota(jnp.int32, (MOE_CAP, tm), 0).astype(F32) + base
        take = jnp.where((grp_r == nf) & (rank_r == slot_r), 1.0, 0.0).astype(BF16)
        slot_c = lax.broadcasted_iota(jnp.int32, (tm, MOE_CAP), 1).astype(F32) + base
        put = jnp.where((grp_c == nf) & (rank_c == slot_c), 1.0, 0.0).astype(BF16)
        xc = _dot(take, hn_ref[...]).astype(BF16)
        rc = _dot(take, pieces)
        rc = rc[:, :ROUTE_W] + rc[:, ROUTE_W:2 * ROUTE_W] + rc[:, 2 * ROUTE_W:]
        lane_c = lax.broadcasted_iota(jnp.int32, rc.shape, 1)
        first = ROUTE_OFF + n * EXPERTS_PER_GROUP
        act = jax.nn.silu(_dot(xc, wg_scr[...])) * _dot(xc, wu_scr[...])
        blocks = []
        for j in range(EXPERTS_PER_GROUP):
            w_j = jnp.sum(jnp.where(lane_c == first + j, rc, 0.0), axis=-1, keepdims=True)
            blocks.append((act[:, j * EXPERT_FF:(j + 1) * EXPERT_FF] * w_j).astype(BF16))
        y = _dot(jnp.concatenate(blocks, axis=1), wd_scr[...])
        return _dot(put, y.astype(BF16))

    out_ref[...] = group_rows(jnp.float32(0.0)).astype(BF16)

    n_rows = jnp.max(jnp.where(grp_c == nf, rank_c + 1.0, 0.0))
    n_pass = lax.div(n_rows.astype(jnp.int32) + (MOE_CAP - 1), MOE_CAP)

    def extra_pass(k, carry):
        more = group_rows((k * MOE_CAP).astype(F32))
        out_ref[...] = (out_ref[...].astype(F32) + more).astype(BF16)
        return carry

    lax.fori_loop(1, n_pass, extra_pass, 0)


def _moe(hn, route, route_t, w_gate, w_up, w_down, *, layer):
    t = hn.shape[0]
    row = lambda w: pl.BlockSpec((TOK_TILE, w), lambda n, i: (i, 0))
    grp_w = lambda shape: pl.BlockSpec((None, None) + shape, lambda n, i: (layer, n, 0, 0, 0),
                                       pipeline_mode=pl.Buffered(1))
    up_shape = (EXPERTS_PER_GROUP, D_MODEL, EXPERT_FF)
    down_shape = (EXPERTS_PER_GROUP, EXPERT_FF, D_MODEL)
    ff = EXPERTS_PER_GROUP * EXPERT_FF
    return pl.pallas_call(
        _moe_kernel,
        grid=(N_GROUPS, t // TOK_TILE),
        in_specs=[row(D_MODEL), row(ROUTE_W), pl.BlockSpec((SUBLANES, TOK_TILE), lambda n, i: (0, i)),
                  grp_w(up_shape), grp_w(up_shape), grp_w(down_shape)],
        out_specs=pl.BlockSpec((None, TOK_TILE, D_MODEL), lambda n, i: (n, i, 0)),
        out_shape=jax.ShapeDtypeStruct((N_GROUPS, t, D_MODEL), BF16),
        scratch_shapes=[pltpu.VMEM((D_MODEL, ff), BF16), pltpu.VMEM((D_MODEL, ff), BF16),
                        pltpu.VMEM((ff, D_MODEL), BF16)],
        compiler_params=_params("arbitrary", "arbitrary"),
        name="moe",
    )(hn, route, route_t, w_gate, w_up, w_down)


def _ple_kernel(x1_ref, c_ref, p_ref, pg_ref, wpg_ref, wple_ref, out_ref, wpg_scr, wple_scr):
    _round_once([(wpg_ref, wpg_scr), (wple_ref, wple_scr)])
    x2 = x1_ref[...]
    for n in range(N_GROUPS):
        x2 = x2 + c_ref[n].astype(F32)
    ms = jnp.mean(x2 * x2, axis=-1, keepdims=True)
    hp = (x2 * lax.rsqrt(ms + EPS) * pg_ref[...]).astype(BF16)
    gate = jax.nn.sigmoid(_dot(hp, wpg_scr[...]))
    out_ref[...] = x2 + gate * _dot(p_ref[...].astype(BF16), wple_scr[...])


def _ple(x1, contrib, p2d, ple_g, wpg, wple, *, layer):
    t = x1.shape[0]
    row = lambda w: pl.BlockSpec((TOK_TILE, w), lambda i: (i, 0))
    return pl.pallas_call(
        _ple_kernel,
        grid=(t // TOK_TILE,),
        in_specs=[row(D_MODEL), pl.BlockSpec((N_GROUPS, TOK_TILE, D_MODEL), lambda i: (0, i, 0)),
                  row(PLE_DIM), _const_spec(ple_g.shape), _layer_weight(wpg, layer), _layer_weight(wple, layer)],
        out_specs=row(D_MODEL),
        out_shape=jax.ShapeDtypeStruct((t, D_MODEL), F32),
        scratch_shapes=[pltpu.VMEM(wpg.shape[1:], BF16), pltpu.VMEM(wple.shape[1:], BF16)],
        compiler_params=_params("arbitrary"),
        name="ple",
    )(x1, contrib, p2d, ple_g, wpg, wple)


def kernel(x, p, attn_norm_g, w_in, b_gate, q_norm_g, k_norm_g, lam_qk, subln_g, w_attn_up, ssm_lam_re, ssm_lam_im, ssm_log_step, ssm_b_re, ssm_b_im, ssm_c_re, ssm_c_im, ssm_d, w_glu, b_glu, w_ssm_up, w_out, ffn_norm_g, w_router_group, b_router_group, w_router_expert, b_router_expert, w_exp_gate, w_exp_up, w_exp_down, ple_norm_g, w_ple_gate, w_ple):
    bsz, seq, d = x.shape
    depth = w_in.shape[0]
    t = bsz * seq
    assert d == D_MODEL and seq % TOK_TILE == 0 and seq % ATT_TILE == 0
    assert seq % SCAN_T == 0 and bsz == SUBLANES
    tok = jnp.arange(TOK_TILE)
    ltri = (tok[None, :] < tok[:, None]).astype(BF16)

    a_re, a_im, b_tiles, c_re_tiles, c_im_tiles, scalars = _prep(
        ssm_lam_re, ssm_lam_im, ssm_log_step, ssm_b_re, ssm_b_im, ssm_c_re, ssm_c_im, lam_qk, q_norm_g, k_norm_g)
    x2d = x.reshape(t, d)
    for i in range(depth):
        lam_init = _lambda_init(i)
        tile_gain = lambda g: jnp.tile(g.reshape(HEAD_W, 1), (COL_CHUNK // HEAD_W, 1))
        qt, k, vt, u_sm, gates = _inproj(
            x2d, attn_norm_g[i].reshape(1, d), w_in,
            tile_gain(q_norm_g[i]), tile_gain(k_norm_g[i]), b_gate[i].reshape(1, -1), seq=seq, layer=i)

        o = _attention(scalars[i], qt, k, vt, subln_g[i].reshape(1, HEAD_W), seq=seq, lam_init=lam_init)

        z_tm = _ssm(u_sm.reshape(seq * bsz, SSM_WIDTH), b_tiles[i], c_re_tiles[i], c_im_tiles[i],
                    a_re[i, ::SSM_GROUP].reshape(1, N_STATE), a_im[i, ::SSM_GROUP].reshape(1, N_STATE),
                    ssm_d[i].reshape(1, SSM_WIDTH), bsz=bsz)

        w_r = jnp.concatenate(
            [w_router_group[i], jnp.transpose(w_router_expert[i], (1, 0, 2)).reshape(d, N_EXPERTS),
             jnp.zeros((d, ROUTE_W - N_GROUPS - N_EXPERTS), F32)], axis=1)
        w_r_hi = w_r.astype(BF16)
        w_r_lo = (w_r - w_r_hi.astype(F32)).astype(BF16)
        b_r = jnp.concatenate([b_router_group[i], b_router_expert[i].reshape(-1),
                               jnp.zeros((ROUTE_W - N_GROUPS - N_EXPERTS,), F32)]).reshape(1, ROUTE_W)
        x1, hn, route, route_t = _mix(
            x2d, o, z_tm.reshape(seq, bsz * SSM_WIDTH), gates,
            w_attn_up, w_glu, b_glu[i].reshape(1, -1), w_ssm_up, w_out, ffn_norm_g[i].reshape(1, d),
            w_r_hi, w_r_lo, b_r, ltri, seq=seq, layer=i)

        contrib = _moe(hn, route, route_t, w_exp_gate, w_exp_up, w_exp_down, layer=i)
        x2d = _ple(x1, contrib, p[i].reshape(t, PLE_DIM), ple_norm_g[i].reshape(1, d), w_ple_gate, w_ple, layer=i)
    return x2d.reshape(bsz, seq, d)
```

```python
import functools
import math

import jax
import jax.numpy as jnp
from jax import lax
from jax.experimental import pallas as pl
from jax.experimental.pallas import tpu as pltpu

F32 = jnp.float32
BF16 = jnp.bfloat16

D_MODEL = 1024
ATT_HEADS = 8
ATT_HEAD_DIM = 64
HEAD_W = 2 * ATT_HEAD_DIM
ATT_QK_WIDTH = ATT_HEADS * HEAD_W
ATT_V_WIDTH = ATT_HEADS * HEAD_W
SSM_WIDTH = 512
SSM_GROUP = 16
SSM_GROUPS = SSM_WIDTH // SSM_GROUP
SSM_STATE = 64
N_STATE = SSM_GROUPS * SSM_STATE
N_BRANCH = 2
IN_WIDTH = 2 * ATT_QK_WIDTH + ATT_V_WIDTH + SSM_WIDTH + N_BRANCH * D_MODEL
N_GROUPS = 4
EXPERTS_PER_GROUP = 8
N_EXPERTS = N_GROUPS * EXPERTS_PER_GROUP
EXPERT_FF = 256
PLE_DIM = 256
EPS = 1e-6

LANES = 128
SUBLANES = 8
MXU_N = 256
VMEM_LIMIT = 56 * 1024 * 1024

TOK_TILE = 512
MOE_CAP = 160
COL_CHUNK = 512
ATT_TILE = 256
Q_SCALE = ATT_HEAD_DIM ** -0.5 * math.log2(math.e)
LAM_ROW = 0
BOUND_ROW = 1
ATT_BOUND_SLACK = 1.01
ATT_BOUND_LIMIT = 60.0
SCAN_T = 64
SCAN_LANES = 512
ROUTE_W = LANES
ROUTE_OFF = N_GROUPS
GRP_LANE = 0
RANK_LANE = 1


def _lambda_init(layer_idx):
    return 0.8 - 0.6 * math.exp(-0.3 * layer_idx)


def _dot(a, b):
    return jnp.dot(a, b, preferred_element_type=F32)


def _const_spec(shape):
    nd = len(shape)
    return pl.BlockSpec(shape, lambda *_: (0,) * nd, pipeline_mode=pl.Buffered(1))


def _layer_weight(stacked, layer):
    shape = stacked.shape[1:]
    return pl.BlockSpec((None,) + shape, lambda *_: (layer,) + (0,) * len(shape), pipeline_mode=pl.Buffered(1))


def _round_once(pairs):
    @pl.when(pl.program_id(0) == 0)
    def _():
        for w_ref, w_scr in pairs:
            w_scr[...] = w_ref[...].astype(BF16)


def _params(*sem):
    return pltpu.CompilerParams(dimension_semantics=sem, vmem_limit_bytes=VMEM_LIMIT)


def _prep_kernel(lre_ref, lim_ref, lstep_ref, bre_ref, bim_ref, cre_ref, cim_ref, lqk_ref, qg_ref, kg_ref,
                 are_ref, aim_ref, bt_ref, ctre_ref, ctim_ref, lam_ref):
    lr = lre_ref[...]
    li = lim_ref[...]
    dt = jnp.exp(lstep_ref[...])
    mag = jnp.exp(lr * dt)
    ab_re = mag * jnp.cos(li * dt)
    ab_im = mag * jnp.sin(li * dt)
    den = lr * lr + li * li
    nr = ab_re - 1.0
    coef_re = (nr * lr + ab_im * li) / den
    coef_im = (ab_im * lr - nr * li) / den
    br = bre_ref[...]
    bi = bim_ref[...]
    are_ref[...] = ab_re
    aim_ref[...] = ab_im
    bb_re = coef_re * br - coef_im * bi
    bb_im = coef_re * bi + coef_im * br

    def spread(x, reps):
        w = x.shape[1]
        src = lax.broadcasted_iota(jnp.int32, (w, w * reps), 0)
        dst = lax.broadcasted_iota(jnp.int32, (w, w * reps), 1) & (w - 1)
        return _dot(x.astype(BF16), jnp.where(src == dst, 1.0, 0.0).astype(BF16))

    def same_group(shape, row0, row_shift, col0, col_shift):
        rows = (lax.broadcasted_iota(jnp.int32, shape, 0) + row0) >> row_shift
        cols = (lax.broadcasted_iota(jnp.int32, shape, 1) + col0) >> col_shift
        return rows == cols

    n_re = N_STATE // MXU_N
    ch_shift = SSM_GROUP.bit_length() - 1
    st_shift = SSM_STATE.bit_length() - 1
    for part, bb in enumerate((bb_re, bb_im)):
        for j in range(n_re):
            r0 = LANES * ((j * MXU_N // SSM_STATE * SSM_GROUP) // LANES)
            x = spread(bb[r0:r0 + LANES, :], MXU_N // SSM_STATE)
            keep = same_group(x.shape, r0, ch_shift, j * MXU_N, st_shift)
            bt_ref[part * n_re + j] = jnp.where(keep, x, 0.0).astype(BF16)
    half = N_STATE // 2
    for c_ref, ct_ref in ((cre_ref, ctre_ref), (cim_ref, ctim_ref)):
        for n in range(2):
            y = spread(c_ref[n * half:(n + 1) * half, :], MXU_N // SSM_GROUP)
            keep = same_group(y.shape, n * half, st_shift, n * MXU_N, ch_shift)
            ct_ref[n] = jnp.where(keep, y, 0.0).astype(BF16)

    lq = lqk_ref[...]
    s01 = jnp.sum(lq[0:1] * lq[1:2], axis=-1, keepdims=True)
    s23 = jnp.sum(lq[2:3] * lq[3:4], axis=-1, keepdims=True)
    lam_dyn = jnp.exp(s01) - jnp.exp(s23)
    g_max = jnp.max(jnp.abs(qg_ref[...]), keepdims=True) * jnp.max(jnp.abs(kg_ref[...]), keepdims=True)
    bound = g_max * (ATT_HEAD_DIM * Q_SCALE * ATT_BOUND_SLACK)
    row = lax.broadcasted_iota(jnp.int32, lam_ref.shape, 0)
    lam_ref[...] = jnp.where(row == LAM_ROW, lam_dyn, bound)


def _prep(lam_re, lam_im, log_step, b_re, b_im, c_re, c_im, lam_qk, q_gain, k_gain):
    depth = lam_re.shape[0]
    rows = SSM_WIDTH
    rep = lambda a: jnp.repeat(a, SSM_GROUP, axis=1)
    lstep = jnp.broadcast_to(log_step[:, :, None], lam_re.shape)
    tr_b = lambda b: jnp.transpose(b, (0, 1, 3, 2)).reshape(depth, rows, SSM_STATE)
    tr_c = lambda c: jnp.transpose(c, (0, 1, 3, 2)).reshape(depth, N_STATE, SSM_GROUP)
    lay = lambda *shape: pl.BlockSpec((None,) + shape, lambda i: (i,) + (0,) * len(shape))
    blk = lay(rows, SSM_STATE)
    out = jax.ShapeDtypeStruct((depth, rows, SSM_STATE), F32)
    n_bt = 2 * N_STATE // MXU_N
    return pl.pallas_call(
        _prep_kernel,
        grid=(depth,),
        in_specs=[blk, blk, blk, blk, blk, lay(N_STATE, SSM_GROUP), lay(N_STATE, SSM_GROUP),
                  lay(4, ATT_HEAD_DIM), lay(2, ATT_HEAD_DIM), lay(2, ATT_HEAD_DIM)],
        out_specs=[blk, blk, lay(n_bt, LANES, MXU_N), lay(2, N_STATE // 2, MXU_N), lay(2, N_STATE // 2, MXU_N),
                   lay(SUBLANES, LANES)],
        out_shape=[out, out, jax.ShapeDtypeStruct((depth, n_bt, LANES, MXU_N), BF16),
                   jax.ShapeDtypeStruct((depth, 2, N_STATE // 2, MXU_N), BF16),
                   jax.ShapeDtypeStruct((depth, 2, N_STATE // 2, MXU_N), BF16),
                   jax.ShapeDtypeStruct((depth, SUBLANES, LANES), F32)],
        compiler_params=_params("arbitrary"),
        name="prep",
    )(rep(lam_re), rep(lam_im), rep(lstep), tr_b(b_re), tr_b(b_im), tr_c(c_re), tr_c(c_im), lam_qk, q_gain, k_gain)


def _inproj_kernel(x_ref, g_ref, w_ref, qg_ref, kg_ref, bg_ref,
                   qt_ref, k_ref, vt_ref, u_hbm, gate_ref, w_scr, u_buf, u_sem, *, n_sb):
    _round_once([(w_ref, w_scr)])
    step = pl.program_id(0)

    def u_copy(i):
        s0 = pl.multiple_of((i % n_sb) * TOK_TILE, TOK_TILE)
        return pltpu.make_async_copy(u_buf, u_hbm.at[pl.ds(s0, TOK_TILE), i // n_sb], u_sem)

    xf = x_ref[...]
    ms = jnp.mean(xf * xf, axis=-1, keepdims=True)
    h = (xf * lax.rsqrt(ms + EPS) * g_ref[...]).astype(BF16)

    def head_norm_t(y, gain_col):
        rows = y.shape[0]
        y3 = y.T.reshape(COL_CHUNK // ATT_HEAD_DIM, ATT_HEAD_DIM, rows)
        ms = jnp.sum(y3 * y3, axis=1, keepdims=True) * (1.0 / ATT_HEAD_DIM)
        return (y3 * lax.rsqrt(ms + EPS)).reshape(COL_CHUNK, rows) * gain_col

    n_qk = ATT_QK_WIDTH // COL_CHUNK
    n_v = ATT_V_WIDTH // COL_CHUNK
    for c in range(IN_WIDTH // COL_CHUNK):
        y = _dot(h, w_scr[:, c * COL_CHUNK:(c + 1) * COL_CHUNK])
        if c < n_qk:
            qt_ref[c * COL_CHUNK:(c + 1) * COL_CHUNK, :] = (head_norm_t(y, qg_ref[...]) * Q_SCALE).astype(BF16)
        elif c < 2 * n_qk:
            cc = c - n_qk
            k_ref[:, cc * COL_CHUNK:(cc + 1) * COL_CHUNK] = head_norm_t(y, kg_ref[...]).T.astype(BF16)
        elif c < 2 * n_qk + n_v:
            cc = c - 2 * n_qk
            vt_ref[cc * COL_CHUNK:(cc + 1) * COL_CHUNK, :] = y.T.astype(BF16)
        elif c == 2 * n_qk + n_v:
            @pl.when(step > 0)
            def _():
                u_copy(step - 1).wait()
            u_buf[...] = y
            u_copy(step).start()
        else:
            cc = c - (2 * n_qk + n_v + 1)
            b = bg_ref[:, cc * COL_CHUNK:(cc + 1) * COL_CHUNK]
            gate_ref[:, cc * COL_CHUNK:(cc + 1) * COL_CHUNK] = jax.nn.sigmoid(y + b).astype(BF16)

    @pl.when(step == pl.num_programs(0) - 1)
    def _():
        u_copy(step).wait()


def _inproj(x2d, norm_g, w_in, qg, kg, b_gate, *, seq, layer):
    t = x2d.shape[0]
    n_sb = seq // TOK_TILE
    bsz = t // seq
    row = lambda w: pl.BlockSpec((TOK_TILE, w), lambda i: (i, 0))
    col = lambda w: pl.BlockSpec((w, TOK_TILE), lambda i: (i // n_sb, i % n_sb))
    return pl.pallas_call(
        functools.partial(_inproj_kernel, n_sb=n_sb),
        grid=(t // TOK_TILE,),
        in_specs=[row(D_MODEL), _const_spec((1, D_MODEL)), _layer_weight(w_in, layer),
                  _const_spec((COL_CHUNK, 1)), _const_spec((COL_CHUNK, 1)),
                  _const_spec((1, N_BRANCH * D_MODEL))],
        out_specs=[col(ATT_QK_WIDTH), row(ATT_QK_WIDTH), col(ATT_V_WIDTH),
                   pl.BlockSpec(memory_space=pl.ANY),
                   row(N_BRANCH * D_MODEL)],
        out_shape=[jax.ShapeDtypeStruct((bsz * ATT_QK_WIDTH, seq), BF16),
                   jax.ShapeDtypeStruct((t, ATT_QK_WIDTH), BF16),
                   jax.ShapeDtypeStruct((bsz * ATT_V_WIDTH, seq), BF16),
                   jax.ShapeDtypeStruct((seq, bsz, SSM_WIDTH), F32),
                   jax.ShapeDtypeStruct((t, N_BRANCH * D_MODEL), BF16)],
        scratch_shapes=[pltpu.VMEM((D_MODEL, IN_WIDTH), BF16), pltpu.VMEM((TOK_TILE, SSM_WIDTH), F32),
                        pltpu.SemaphoreType.DMA(())],
        compiler_params=_params("arbitrary"),
        name="inproj",
    )(x2d, norm_g, w_in, qg, kg, b_gate)


def _attn_kernel(sc_ref, qt_ref, k_ref, vt_ref, g_ref, o_ref, *, lam_init):
    tq = ATT_TILE
    seq = k_ref.shape[0]
    nq = seq // tq
    lam = sc_ref[LAM_ROW, 0] + lam_init
    bound = sc_ref[BOUND_ROW, 0]
    head_row = lax.broadcasted_iota(jnp.int32, (HEAD_W, tq), 0)
    key_pos = lax.broadcasted_iota(jnp.int32, (tq, 2 * tq), 0)
    qry_pos = lax.broadcasted_iota(jnp.int32, (tq, 2 * tq), 1) & (tq - 1)
    causal = key_pos <= qry_pos
    zero = jnp.zeros((HEAD_W, tq), BF16)

    def tile(qi, shift_by_bound):
        qt = qt_ref[:, qi * tq:(qi + 1) * tq]
        qst = jnp.concatenate([jnp.where(head_row < ATT_HEAD_DIM, qt, zero),
                               jnp.where(head_row >= ATT_HEAD_DIM, qt, zero)], axis=1)
        nk = qi * tq
        s_d = jnp.where(causal, _dot(k_ref[nk:nk + tq, :], qst), -jnp.inf)
        if qi > 0:
            s_m = _dot(k_ref[0:nk, :], qst)
        if shift_by_bound:
            m = bound
        else:
            m = jnp.max(s_d, axis=0, keepdims=True)
            if qi > 0:
                m = jnp.maximum(m, jnp.max(s_m, axis=0, keepdims=True))
        p_d = jnp.exp2(s_d - m)
        l = jnp.sum(p_d, axis=0, keepdims=True)
        acc = _dot(vt_ref[:, nk:nk + tq], p_d.astype(BF16))
        if qi > 0:
            p_m = jnp.exp2(s_m - m)
            l = l + jnp.sum(p_m, axis=0, keepdims=True)
            acc = acc + _dot(vt_ref[:, 0:nk], p_m.astype(BF16))
        accn = acc * (1.0 / l)
        ot = accn[:, :tq] - lam * accn[:, tq:]
        ms = jnp.mean(ot * ot, axis=0, keepdims=True)
        o = (ot * lax.rsqrt(ms + EPS)).T * g_ref[...] * (1.0 - lam_init)
        o_ref[qi * tq:(qi + 1) * tq, :] = o.astype(BF16)

    @pl.when(bound < ATT_BOUND_LIMIT)
    def _():
        for qi in range(nq):
            tile(qi, True)

    @pl.when(jnp.logical_not(bound < ATT_BOUND_LIMIT))
    def _():
        for qi in range(nq):
            tile(qi, False)


def _attention(scalars, qt, k, vt, subln_g, *, seq, lam_init):
    t = k.shape[0]
    bsz = t // seq
    fm = pl.BlockSpec((HEAD_W, seq), lambda b, h: (b * ATT_HEADS + h, 0))
    tm = pl.BlockSpec((seq, HEAD_W), lambda b, h: (b, h))
    return pl.pallas_call(
        functools.partial(_attn_kernel, lam_init=lam_init),
        grid=(bsz, ATT_HEADS),
        in_specs=[pl.BlockSpec(memory_space=pltpu.SMEM), fm, tm, fm, _const_spec((1, HEAD_W))],
        out_specs=tm,
        out_shape=jax.ShapeDtypeStruct((t, ATT_V_WIDTH), BF16),
        compiler_params=_params("parallel", "parallel"),
        name="attn",
    )(scalars, qt, k, vt, subln_g)


def _ssm_kernel(u_ref, bt_ref, cre_ref, cim_ref, are_ref, aim_ref, d_ref, z_hbm, st_scr, state_scr, z_buf, z_sem):
    n_t, bsz, _ = u_ref.shape
    rows = n_t * bsz
    chunk = pl.program_id(0)

    def z_copies(i):
        t0 = pl.multiple_of(i * n_t, n_t)
        return [pltpu.make_async_copy(z_buf.at[:, b], z_hbm.at[b, pl.ds(t0, n_t)], z_sem) for b in range(bsz)]

    @pl.when(chunk == 0)
    def _():
        state_scr[...] = jnp.zeros(state_scr.shape, F32)

    u = u_ref[...].reshape(rows, SSM_WIDTH)
    ub = u.astype(BF16)
    n_tiles = 2 * N_STATE // MXU_N
    for j in range(n_tiles):
        k0 = LANES * (((j % (n_tiles // 2)) * MXU_N // SSM_STATE * SSM_GROUP) // LANES)
        st_scr[:, j * MXU_N:(j + 1) * MXU_N] = _dot(ub[:, k0:k0 + LANES], bt_ref[j])

    for cc in range(N_STATE // SCAN_LANES):
        lo = cc * SCAN_LANES
        hi = N_STATE + lo
        a_r = jnp.broadcast_to(are_ref[:, lo:lo + SCAN_LANES], (bsz, SCAN_LANES))
        a_i = jnp.broadcast_to(aim_ref[:, lo:lo + SCAN_LANES], (bsz, SCAN_LANES))

        def step(t, carry, lo=lo, hi=hi, a_r=a_r, a_i=a_i):
            s_r, s_i = carry
            r0 = pl.multiple_of(t * bsz, bsz)
            n_r = a_r * s_r - a_i * s_i + st_scr[pl.ds(r0, bsz), lo:lo + SCAN_LANES]
            n_i = a_r * s_i + a_i * s_r + st_scr[pl.ds(r0, bsz), hi:hi + SCAN_LANES]
            st_scr[pl.ds(r0, bsz), lo:lo + SCAN_LANES] = n_r
            st_scr[pl.ds(r0, bsz), hi:hi + SCAN_LANES] = n_i
            return n_r, n_i

        s_r, s_i = lax.fori_loop(0, rows // bsz, step,
                                 (state_scr[:, lo:lo + SCAN_LANES], state_scr[:, hi:hi + SCAN_LANES]),
                                 unroll=8)
        state_scr[:, lo:lo + SCAN_LANES] = s_r
        state_scr[:, hi:hi + SCAN_LANES] = s_i

    @pl.when(chunk > 0)
    def _():
        for cp in z_copies(chunk - 1):
            cp.wait()
    half = N_STATE // 2
    for n in range(2):
        s_re = st_scr[:, n * half:(n + 1) * half].astype(BF16)
        s_im = st_scr[:, N_STATE + n * half:N_STATE + (n + 1) * half].astype(BF16)
        cols = slice(n * MXU_N, (n + 1) * MXU_N)
        y = _dot(s_re, cre_ref[n]) - _dot(s_im, cim_ref[n]) + d_ref[:, cols] * u[:, cols]
        z_buf[:, :, cols] = jax.nn.gelu(y).reshape(n_t, bsz, MXU_N)
    for cp in z_copies(chunk):
        cp.start()

    @pl.when(chunk == pl.num_programs(0) - 1)
    def _():
        for cp in z_copies(chunk):
            cp.wait()


def _ssm(u_tm, b_tiles, c_re_tiles, c_im_tiles, a_re, a_im, d_skip):
    seq, bsz, _ = u_tm.shape
    return pl.pallas_call(
        _ssm_kernel,
        grid=(seq // SCAN_T,),
        in_specs=[pl.BlockSpec((SCAN_T, bsz, SSM_WIDTH), lambda i: (i, 0, 0)),
                  _const_spec(b_tiles.shape), _const_spec(c_re_tiles.shape),
                  _const_spec(c_im_tiles.shape), _const_spec((1, N_STATE)), _const_spec((1, N_STATE)),
                  _const_spec((1, SSM_WIDTH))],
        out_specs=pl.BlockSpec(memory_space=pl.ANY),
        out_shape=jax.ShapeDtypeStruct((bsz, seq, SSM_WIDTH), F32),
        scratch_shapes=[pltpu.VMEM((SCAN_T * bsz, 2 * N_STATE), F32), pltpu.VMEM((bsz, 2 * N_STATE), F32),
                        pltpu.VMEM((SCAN_T, bsz, SSM_WIDTH), F32), pltpu.SemaphoreType.DMA(())],
        compiler_params=_params("arbitrary"),
        name="ssm",
    )(u_tm, b_tiles, c_re_tiles, c_im_tiles, a_re, a_im, d_skip)


def _route(logits, ltri):
    lane = lax.broadcasted_iota(jnp.int32, logits.shape, 1)
    lanef = lane.astype(F32)
    big = float(ROUTE_W)
    neg = -jnp.inf
    gl = jnp.where(lane < N_GROUPS, logits, neg)
    gmax = jnp.max(gl, axis=-1, keepdims=True)
    grp = jnp.min(jnp.where(gl == gmax, lanef, big), axis=-1, keepdims=True)
    p_grp = 1.0 / jnp.sum(jnp.exp(gl - gmax), axis=-1, keepdims=True)
    first = ROUTE_OFF + grp * EXPERTS_PER_GROUP
    in_grp = (lanef >= first) & (lanef < first + EXPERTS_PER_GROUP)
    el = jnp.where(in_grp, logits, neg)
    v1 = jnp.max(el, axis=-1, keepdims=True)
    i1 = jnp.min(jnp.where(el == v1, lanef, big), axis=-1, keepdims=True)
    el2 = jnp.where(lanef == i1, neg, el)
    v2 = jnp.max(el2, axis=-1, keepdims=True)
    i2 = jnp.min(jnp.where(el2 == v2, lanef, big), axis=-1, keepdims=True)
    e21 = jnp.exp(v2 - v1)
    w1 = p_grp / (1.0 + e21)
    w2 = p_grp * (e21 / (1.0 + e21))
    member = lanef == grp
    before = _dot(ltri, jnp.where(member, 1.0, 0.0).astype(BF16))
    rank = jnp.sum(jnp.where(member, before, 0.0), axis=-1, keepdims=True)
    return (jnp.where(lanef == i1, w1, 0.0) + jnp.where(lanef == i2, w2, 0.0)
            + jnp.where(lane == GRP_LANE, grp, 0.0) + jnp.where(lane == RANK_LANE, rank, 0.0))


def _mix_kernel(x_ref, o_ref, z_ref, gate_ref, wau_ref, wglu_ref, bglu_ref, wsu_ref, wout_ref,
                fg_ref, wrh_ref, wrl_ref, br_ref, ltri_ref, x1_ref, hn_ref, route_ref, route_t_ref,
                wau_scr, wglu_scr, wsu_scr, wout_scr):
    _round_once([(wau_ref, wau_scr), (wglu_ref, wglu_scr), (wsu_ref, wsu_scr), (wout_ref, wout_scr)])
    y_att = _dot(o_ref[...], wau_scr[...])
    z = z_ref[...]
    z = z * jax.nn.sigmoid(_dot(z.astype(BF16), wglu_scr[...]) + bglu_ref[...])
    y_ssm = _dot(z.astype(BF16), wsu_scr[...])
    mixed = (gate_ref[:, :D_MODEL].astype(F32) * y_att + gate_ref[:, D_MODEL:].astype(F32) * y_ssm)
    x1 = x_ref[...] + _dot(mixed.astype(BF16), wout_scr[...])
    x1_ref[...] = x1
    ms = jnp.mean(x1 * x1, axis=-1, keepdims=True)
    hn = x1 * lax.rsqrt(ms + EPS) * fg_ref[...]
    hn_hi = hn.astype(BF16)
    hn_ref[...] = hn_hi
    hn_lo = (hn - hn_hi.astype(F32)).astype(BF16)
    logits = (_dot(hn_hi, wrh_ref[...]) + _dot(hn_lo, wrh_ref[...]) + _dot(hn_hi, wrl_ref[...])
              + br_ref[...])
    route = _route(logits, ltri_ref[...])
    route_ref[...] = route
    route_t_ref[...] = route.T[:SUBLANES, :]


def _mix(x2d, o, z, gates, wau, wglu, bglu, wsu, wout, ffn_g, wr_hi, wr_lo, br, ltri, *, layer):
    t = x2d.shape[0]
    row = lambda w: pl.BlockSpec((TOK_TILE, w), lambda i: (i, 0))
    return pl.pallas_call(
        _mix_kernel,
        grid=(t // TOK_TILE,),
        in_specs=[row(D_MODEL), row(ATT_V_WIDTH), row(SSM_WIDTH), row(N_BRANCH * D_MODEL),
                  _layer_weight(wau, layer), _layer_weight(wglu, layer), _const_spec(bglu.shape),
                  _layer_weight(wsu, layer), _layer_weight(wout, layer), _const_spec(ffn_g.shape),
                  _const_spec(wr_hi.shape), _const_spec(wr_lo.shape), _const_spec(br.shape),
                  _const_spec(ltri.shape)],
        out_specs=[row(D_MODEL), row(D_MODEL), row(ROUTE_W),
                   pl.BlockSpec((SUBLANES, TOK_TILE), lambda i: (0, i))],
        out_shape=[jax.ShapeDtypeStruct((t, D_MODEL), F32),
                   jax.ShapeDtypeStruct((t, D_MODEL), BF16),
                   jax.ShapeDtypeStruct((t, ROUTE_W), F32),
                   jax.ShapeDtypeStruct((SUBLANES, t), F32)],
        scratch_shapes=[pltpu.VMEM(w.shape[1:], BF16) for w in (wau, wglu, wsu, wout)],
        compiler_params=_params("arbitrary"),
        name="mix",
    )(x2d, o, z, gates, wau, wglu, bglu, wsu, wout, ffn_g, wr_hi, wr_lo, br, ltri)


def _moe_kernel(hn_ref, route_ref, route_t_ref, wg_ref, wu_ref, wd_ref, out_ref, wg_scr, wu_scr, wd_scr):
    n = pl.program_id(0)

    @pl.when(pl.program_id(1) == 0)
    def _():
        for j in range(EXPERTS_PER_GROUP):
            cols = slice(j * EXPERT_FF, (j + 1) * EXPERT_FF)
            wg_scr[:, cols] = wg_ref[j].astype(BF16)
            wu_scr[:, cols] = wu_ref[j].astype(BF16)
            wd_scr[cols, :] = wd_ref[j].astype(BF16)

    tm = hn_ref.shape[0]
    nf = n.astype(F32)
    route = route_ref[...]
    lane = lax.broadcasted_iota(jnp.int32, route.shape, 1)
    grp_c = jnp.sum(jnp.where(lane == GRP_LANE, route, 0.0), axis=-1, keepdims=True)
    rank_c = jnp.sum(jnp.where(lane == RANK_LANE, route, 0.0), axis=-1, keepdims=True)
    grp_r = route_t_ref[GRP_LANE:GRP_LANE + 1, :]
    rank_r = route_t_ref[RANK_LANE:RANK_LANE + 1, :]
    r1 = route.astype(BF16)
    r2 = (route - r1.astype(F32)).astype(BF16)
    r3 = (route - r1.astype(F32) - r2.astype(F32)).astype(BF16)
    pieces = jnp.concatenate([r1, r2, r3], axis=1)

    def group_rows(base):
        slot_r = lax.broadcasted_iota(jnp.int32, (MOE_CAP, tm), 0).astype(F32) + base
        take = jnp.where((grp_r == nf) & (rank_r == slot_r), 1.0, 0.0).astype(BF16)
        slot_c = lax.broadcasted_iota(jnp.int32, (tm, MOE_CAP), 1).astype(F32) + base
        put = jnp.where((grp_c == nf) & (rank_c == slot_c), 1.0, 0.0).astype(BF16)
        xc = _dot(take, hn_ref[...]).astype(BF16)
        rc = _dot(take, pieces)
        rc = rc[:, :ROUTE_W] + rc[:, ROUTE_W:2 * ROUTE_W] + rc[:, 2 * ROUTE_W:]
        lane_c = lax.broadcasted_iota(jnp.int32, rc.shape, 1)
        first = ROUTE_OFF + n * EXPERTS_PER_GROUP
        act = jax.nn.silu(_dot(xc, wg_scr[...])) * _dot(xc, wu_scr[...])
        blocks = []
        for j in range(EXPERTS_PER_GROUP):
            w_j = jnp.sum(jnp.where(lane_c == first + j, rc, 0.0), axis=-1, keepdims=True)
            blocks.append((act[:, j * EXPERT_FF:(j + 1) * EXPERT_FF] * w_j).astype(BF16))
        y = _dot(jnp.concatenate(blocks, axis=1), wd_scr[...])
        return _dot(put, y.astype(BF16))

    out_ref[...] = group_rows(jnp.float32(0.0)).astype(BF16)

    n_rows = jnp.max(jnp.where(grp_c == nf, rank_c + 1.0, 0.0))
    n_pass = lax.div(n_rows.astype(jnp.int32) + (MOE_CAP - 1), MOE_CAP)

    def extra_pass(k, carry):
        more = group_rows((k * MOE_CAP).astype(F32))
        out_ref[...] = (out_ref[...].astype(F32) + more).astype(BF16)
        return carry

    lax.fori_loop(1, n_pass, extra_pass, 0)


def _moe(hn, route, route_t, w_gate, w_up, w_down, *, layer):
    t = hn.shape[0]
    row = lambda w: pl.BlockSpec((TOK_TILE, w), lambda n, i: (i, 0))
    grp_w = lambda shape: pl.BlockSpec((None, None) + shape, lambda n, i: (layer, n, 0, 0, 0),
                                       pipeline_mode=pl.Buffered(1))
    up_shape = (EXPERTS_PER_GROUP, D_MODEL, EXPERT_FF)
    down_shape = (EXPERTS_PER_GROUP, EXPERT_FF, D_MODEL)
    ff = EXPERTS_PER_GROUP * EXPERT_FF
    return pl.pallas_call(
        _moe_kernel,
        grid=(N_GROUPS, t // TOK_TILE),
        in_specs=[row(D_MODEL), row(ROUTE_W), pl.BlockSpec((SUBLANES, TOK_TILE), lambda n, i: (0, i)),
                  grp_w(up_shape), grp_w(up_shape), grp_w(down_shape)],
        out_specs=pl.BlockSpec((None, TOK_TILE, D_MODEL), lambda n, i: (n, i, 0)),
        out_shape=jax.ShapeDtypeStruct((N_GROUPS, t, D_MODEL), BF16),
        scratch_shapes=[pltpu.VMEM((D_MODEL, ff), BF16), pltpu.VMEM((D_MODEL, ff), BF16),
                        pltpu.VMEM((ff, D_MODEL), BF16)],
        compiler_params=_params("arbitrary", "arbitrary"),
        name="moe",
    )(hn, route, route_t, w_gate, w_up, w_down)


def _ple_kernel(x1_ref, c_ref, p_ref, pg_ref, wpg_ref, wple_ref, out_ref, wpg_scr, wple_scr):
    _round_once([(wpg_ref, wpg_scr), (wple_ref, wple_scr)])
    x2 = x1_ref[...]
    for n in range(N_GROUPS):
        x2 = x2 + c_ref[n].astype(F32)
    ms = jnp.mean(x2 * x2, axis=-1, keepdims=True)
    hp = (x2 * lax.rsqrt(ms + EPS) * pg_ref[...]).astype(BF16)
    gate = jax.nn.sigmoid(_dot(hp, wpg_scr[...]))
    out_ref[...] = x2 + gate * _dot(p_ref[...].astype(BF16), wple_scr[...])


def _ple(x1, contrib, p2d, ple_g, wpg, wple, *, layer):
    t = x1.shape[0]
    row = lambda w: pl.BlockSpec((TOK_TILE, w), lambda i: (i, 0))
    return pl.pallas_call(
        _ple_kernel,
        grid=(t // TOK_TILE,),
        in_specs=[row(D_MODEL), pl.BlockSpec((N_GROUPS, TOK_TILE, D_MODEL), lambda i: (0, i, 0)),
                  row(PLE_DIM), _const_spec(ple_g.shape), _layer_weight(wpg, layer), _layer_weight(wple, layer)],
        out_specs=row(D_MODEL),
        out_shape=jax.ShapeDtypeStruct((t, D_MODEL), F32),
        scratch_shapes=[pltpu.VMEM(wpg.shape[1:], BF16), pltpu.VMEM(wple.shape[1:], BF16)],
        compiler_params=_params("arbitrary"),
        name="ple",
    )(x1, contrib, p2d, ple_g, wpg, wple)


def kernel(x, p, attn_norm_g, w_in, b_gate, q_norm_g, k_norm_g, lam_qk, subln_g, w_attn_up, ssm_lam_re, ssm_lam_im, ssm_log_step, ssm_b_re, ssm_b_im, ssm_c_re, ssm_c_im, ssm_d, w_glu, b_glu, w_ssm_up, w_out, ffn_norm_g, w_router_group, b_router_group, w_router_expert, b_router_expert, w_exp_gate, w_exp_up, w_exp_down, ple_norm_g, w_ple_gate, w_ple):
    bsz, seq, d = x.shape
    depth = w_in.shape[0]
    t = bsz * seq
    assert d == D_MODEL and seq % TOK_TILE == 0 and seq % ATT_TILE == 0
    assert seq % SCAN_T == 0 and bsz == SUBLANES
    tok = jnp.arange(TOK_TILE)
    ltri = (tok[None, :] < tok[:, None]).astype(BF16)

    a_re, a_im, b_tiles, c_re_tiles, c_im_tiles, scalars = _prep(
        ssm_lam_re, ssm_lam_im, ssm_log_step, ssm_b_re, ssm_b_im, ssm_c_re, ssm_c_im, lam_qk, q_norm_g, k_norm_g)
    x2d = x.reshape(t, d)
    for i in range(depth):
        lam_init = _lambda_init(i)
        tile_gain = lambda g: jnp.tile(g.reshape(HEAD_W, 1), (COL_CHUNK // HEAD_W, 1))
        qt, k, vt, u_tm, gates = _inproj(
            x2d, attn_norm_g[i].reshape(1, d), w_in,
            tile_gain(q_norm_g[i]), tile_gain(k_norm_g[i]), b_gate[i].reshape(1, -1), seq=seq, layer=i)

        o = _attention(scalars[i], qt, k, vt, subln_g[i].reshape(1, HEAD_W), seq=seq, lam_init=lam_init)

        z = _ssm(u_tm, b_tiles[i], c_re_tiles[i], c_im_tiles[i],
                 a_re[i, ::SSM_GROUP].reshape(1, N_STATE), a_im[i, ::SSM_GROUP].reshape(1, N_STATE),
                 ssm_d[i].reshape(1, SSM_WIDTH))

        w_r = jnp.concatenate(
            [w_router_group[i], jnp.transpose(w_router_expert[i], (1, 0, 2)).reshape(d, N_EXPERTS),
             jnp.zeros((d, ROUTE_W - N_GROUPS - N_EXPERTS), F32)], axis=1)
        w_r_hi = w_r.astype(BF16)
        w_r_lo = (w_r - w_r_hi.astype(F32)).astype(BF16)
        b_r = jnp.concatenate([b_router_group[i], b_router_expert[i].reshape(-1),
                               jnp.zeros((ROUTE_W - N_GROUPS - N_EXPERTS,), F32)]).reshape(1, ROUTE_W)
        x1, hn, route, route_t = _mix(
            x2d, o, z.reshape(t, SSM_WIDTH), gates,
            w_attn_up, w_glu, b_glu[i].reshape(1, -1), w_ssm_up, w_out, ffn_norm_g[i].reshape(1, d),
            w_r_hi, w_r_lo, b_r, ltri, layer=i)

        contrib = _moe(hn, route, route_t, w_exp_gate, w_exp_up, w_exp_down, layer=i)
        x2d = _ple(x1, contrib, p[i].reshape(t, PLE_DIM), ple_norm_g[i].reshape(1, d), w_ple_gate, w_ple, layer=i)
    return x2d.reshape(bsz, seq, d)
```

```python
import functools
import math

import jax
import jax.numpy as jnp
from jax import lax
from jax.experimental import pallas as pl
from jax.experimental.pallas import tpu as pltpu

F32 = jnp.float32
BF16 = jnp.bfloat16

D_MODEL = 1024
ATT_HEADS = 8
ATT_HEAD_DIM = 64
HEAD_W = 2 * ATT_HEAD_DIM
ATT_QK_WIDTH = ATT_HEADS * HEAD_W
ATT_V_WIDTH = ATT_HEADS * HEAD_W
SSM_WIDTH = 512
SSM_GROUP = 16
SSM_GROUPS = SSM_WIDTH // SSM_GROUP
SSM_STATE = 64
N_STATE = SSM_GROUPS * SSM_STATE
N_BRANCH = 2
IN_WIDTH = 2 * ATT_QK_WIDTH + ATT_V_WIDTH + SSM_WIDTH + N_BRANCH * D_MODEL
N_GROUPS = 4
EXPERTS_PER_GROUP = 8
N_EXPERTS = N_GROUPS * EXPERTS_PER_GROUP
EXPERT_FF = 256
PLE_DIM = 256
EPS = 1e-6

LANES = 128
SUBLANES = 8
MXU_N = 256
VMEM_LIMIT = 56 * 1024 * 1024

TOK_TILE = 512
MOE_CAP = 144
COL_CHUNK = 512
ATT_TILE = 256
Q_SCALE = ATT_HEAD_DIM ** -0.5 * math.log2(math.e)
LAM_ROW = 0
BOUND_ROW = 1
ATT_BOUND_SLACK = 1.01
ATT_BOUND_LIMIT = 60.0
SCAN_T = 64
SCAN_LANES = 1024
ROUTE_W = LANES
ROUTE_OFF = N_GROUPS
GRP_LANE = 0
RANK_LANE = 1


def _lambda_init(layer_idx):
    return 0.8 - 0.6 * math.exp(-0.3 * layer_idx)


def _dot(a, b):
    return jnp.dot(a, b, preferred_element_type=F32)


def _const_spec(shape):
    nd = len(shape)
    return pl.BlockSpec(shape, lambda *_: (0,) * nd, pipeline_mode=pl.Buffered(1))


def _layer_weight(stacked, layer):
    shape = stacked.shape[1:]
    return pl.BlockSpec((None,) + shape, lambda *_: (layer,) + (0,) * len(shape), pipeline_mode=pl.Buffered(1))


def _round_once(pairs):
    @pl.when(pl.program_id(0) == 0)
    def _():
        for w_ref, w_scr in pairs:
            w_scr[...] = w_ref[...].astype(BF16)


def _params(*sem):
    return pltpu.CompilerParams(dimension_semantics=sem, vmem_limit_bytes=VMEM_LIMIT)


def _prep_kernel(lre_ref, lim_ref, lstep_ref, bre_ref, bim_ref, cre_ref, cim_ref, lqk_ref, qg_ref, kg_ref,
                 are_ref, aim_ref, bt_ref, ctre_ref, ctim_ref, lam_ref):
    lr = lre_ref[...]
    li = lim_ref[...]
    dt = jnp.exp(lstep_ref[...])
    mag = jnp.exp(lr * dt)
    ab_re = mag * jnp.cos(li * dt)
    ab_im = mag * jnp.sin(li * dt)
    den = lr * lr + li * li
    nr = ab_re - 1.0
    coef_re = (nr * lr + ab_im * li) / den
    coef_im = (ab_im * lr - nr * li) / den
    br = bre_ref[...]
    bi = bim_ref[...]
    are_ref[...] = ab_re
    aim_ref[...] = ab_im
    bb_re = coef_re * br - coef_im * bi
    bb_im = coef_re * bi + coef_im * br

    def spread(x, reps):
        w = x.shape[1]
        src = lax.broadcasted_iota(jnp.int32, (w, w * reps), 0)
        dst = lax.broadcasted_iota(jnp.int32, (w, w * reps), 1) & (w - 1)
        return _dot(x.astype(BF16), jnp.where(src == dst, 1.0, 0.0).astype(BF16))

    def same_group(shape, row0, row_shift, col0, col_shift):
        rows = (lax.broadcasted_iota(jnp.int32, shape, 0) + row0) >> row_shift
        cols = (lax.broadcasted_iota(jnp.int32, shape, 1) + col0) >> col_shift
        return rows == cols

    n_re = N_STATE // MXU_N
    ch_shift = SSM_GROUP.bit_length() - 1
    st_shift = SSM_STATE.bit_length() - 1
    for part, bb in enumerate((bb_re, bb_im)):
        for j in range(n_re):
            r0 = LANES * ((j * MXU_N // SSM_STATE * SSM_GROUP) // LANES)
            x = spread(bb[r0:r0 + LANES, :], MXU_N // SSM_STATE)
            keep = same_group(x.shape, r0, ch_shift, j * MXU_N, st_shift)
            bt_ref[part * n_re + j] = jnp.where(keep, x, 0.0).astype(BF16)
    half = N_STATE // 2
    for c_ref, ct_ref in ((cre_ref, ctre_ref), (cim_ref, ctim_ref)):
        for n in range(2):
            y = spread(c_ref[n * half:(n + 1) * half, :], MXU_N // SSM_GROUP)
            keep = same_group(y.shape, n * half, st_shift, n * MXU_N, ch_shift)
            ct_ref[n] = jnp.where(keep, y, 0.0).astype(BF16)

    lq = lqk_ref[...]
    s01 = jnp.sum(lq[0:1] * lq[1:2], axis=-1, keepdims=True)
    s23 = jnp.sum(lq[2:3] * lq[3:4], axis=-1, keepdims=True)
    lam_dyn = jnp.exp(s01) - jnp.exp(s23)
    g_max = jnp.max(jnp.abs(qg_ref[...]), keepdims=True) * jnp.max(jnp.abs(kg_ref[...]), keepdims=True)
    bound = g_max * (ATT_HEAD_DIM * Q_SCALE * ATT_BOUND_SLACK)
    row = lax.broadcasted_iota(jnp.int32, lam_ref.shape, 0)
    lam_ref[...] = jnp.where(row == LAM_ROW, lam_dyn, bound)


def _prep(lam_re, lam_im, log_step, b_re, b_im, c_re, c_im, lam_qk, q_gain, k_gain):
    depth = lam_re.shape[0]
    rows = SSM_WIDTH
    rep = lambda a: jnp.repeat(a, SSM_GROUP, axis=1)
    lstep = jnp.broadcast_to(log_step[:, :, None], lam_re.shape)
    tr_b = lambda b: jnp.transpose(b, (0, 1, 3, 2)).reshape(depth, rows, SSM_STATE)
    tr_c = lambda c: jnp.transpose(c, (0, 1, 3, 2)).reshape(depth, N_STATE, SSM_GROUP)
    lay = lambda *shape: pl.BlockSpec((None,) + shape, lambda i: (i,) + (0,) * len(shape))
    blk = lay(rows, SSM_STATE)
    out = jax.ShapeDtypeStruct((depth, rows, SSM_STATE), F32)
    n_bt = 2 * N_STATE // MXU_N
    return pl.pallas_call(
        _prep_kernel,
        grid=(depth,),
        in_specs=[blk, blk, blk, blk, blk, lay(N_STATE, SSM_GROUP), lay(N_STATE, SSM_GROUP),
                  lay(4, ATT_HEAD_DIM), lay(2, ATT_HEAD_DIM), lay(2, ATT_HEAD_DIM)],
        out_specs=[blk, blk, lay(n_bt, LANES, MXU_N), lay(2, N_STATE // 2, MXU_N), lay(2, N_STATE // 2, MXU_N),
                   lay(SUBLANES, LANES)],
        out_shape=[out, out, jax.ShapeDtypeStruct((depth, n_bt, LANES, MXU_N), BF16),
                   jax.ShapeDtypeStruct((depth, 2, N_STATE // 2, MXU_N), BF16),
                   jax.ShapeDtypeStruct((depth, 2, N_STATE // 2, MXU_N), BF16),
                   jax.ShapeDtypeStruct((depth, SUBLANES, LANES), F32)],
        compiler_params=_params("arbitrary"),
        name="prep",
    )(rep(lam_re), rep(lam_im), rep(lstep), tr_b(b_re), tr_b(b_im), tr_c(c_re), tr_c(c_im), lam_qk, q_gain, k_gain)


def _inproj_kernel(x_ref, g_ref, w_ref, qg_ref, kg_ref, bg_ref,
                   qt_ref, k_ref, vt_ref, u_hbm, gate_ref, w_scr, u_buf, u_sem, *, n_sb):
    _round_once([(w_ref, w_scr)])
    step = pl.program_id(0)

    def u_copy(i):
        s0 = pl.multiple_of((i % n_sb) * TOK_TILE, TOK_TILE)
        return pltpu.make_async_copy(u_buf, u_hbm.at[pl.ds(s0, TOK_TILE), i // n_sb], u_sem)

    xf = x_ref[...]
    ms = jnp.mean(xf * xf, axis=-1, keepdims=True)
    h = (xf * lax.rsqrt(ms + EPS) * g_ref[...]).astype(BF16)

    def head_norm_t(y, gain_col):
        rows = y.shape[0]
        y3 = y.T.reshape(COL_CHUNK // ATT_HEAD_DIM, ATT_HEAD_DIM, rows)
        ms = jnp.sum(y3 * y3, axis=1, keepdims=True) * (1.0 / ATT_HEAD_DIM)
        return (y3 * lax.rsqrt(ms + EPS)).reshape(COL_CHUNK, rows) * gain_col

    n_qk = ATT_QK_WIDTH // COL_CHUNK
    n_v = ATT_V_WIDTH // COL_CHUNK
    for c in range(IN_WIDTH // COL_CHUNK):
        y = _dot(h, w_scr[:, c * COL_CHUNK:(c + 1) * COL_CHUNK])
        if c < n_qk:
            qt_ref[c * COL_CHUNK:(c + 1) * COL_CHUNK, :] = (head_norm_t(y, qg_ref[...]) * Q_SCALE).astype(BF16)
        elif c < 2 * n_qk:
            cc = c - n_qk
            k_ref[:, cc * COL_CHUNK:(cc + 1) * COL_CHUNK] = head_norm_t(y, kg_ref[...]).T.astype(BF16)
        elif c < 2 * n_qk + n_v:
            cc = c - 2 * n_qk
            vt_ref[cc * COL_CHUNK:(cc + 1) * COL_CHUNK, :] = y.T.astype(BF16)
        elif c == 2 * n_qk + n_v:
            @pl.when(step > 0)
            def _():
                u_copy(step - 1).wait()
            u_buf[...] = y
            u_copy(step).start()
        else:
            cc = c - (2 * n_qk + n_v + 1)
            b = bg_ref[:, cc * COL_CHUNK:(cc + 1) * COL_CHUNK]
            gate_ref[:, cc * COL_CHUNK:(cc + 1) * COL_CHUNK] = jax.nn.sigmoid(y + b).astype(BF16)

    @pl.when(step == pl.num_programs(0) - 1)
    def _():
        u_copy(step).wait()


def _inproj(x2d, norm_g, w_in, qg, kg, b_gate, *, seq, layer):
    t = x2d.shape[0]
    n_sb = seq // TOK_TILE
    bsz = t // seq
    row = lambda w: pl.BlockSpec((TOK_TILE, w), lambda i: (i, 0))
    col = lambda w: pl.BlockSpec((w, TOK_TILE), lambda i: (i // n_sb, i % n_sb))
    return pl.pallas_call(
        functools.partial(_inproj_kernel, n_sb=n_sb),
        grid=(t // TOK_TILE,),
        in_specs=[row(D_MODEL), _const_spec((1, D_MODEL)), _layer_weight(w_in, layer),
                  _const_spec((COL_CHUNK, 1)), _const_spec((COL_CHUNK, 1)),
                  _const_spec((1, N_BRANCH * D_MODEL))],
        out_specs=[col(ATT_QK_WIDTH), row(ATT_QK_WIDTH), col(ATT_V_WIDTH),
                   pl.BlockSpec(memory_space=pl.ANY),
                   row(N_BRANCH * D_MODEL)],
        out_shape=[jax.ShapeDtypeStruct((bsz * ATT_QK_WIDTH, seq), BF16),
                   jax.ShapeDtypeStruct((t, ATT_QK_WIDTH), BF16),
                   jax.ShapeDtypeStruct((bsz * ATT_V_WIDTH, seq), BF16),
                   jax.ShapeDtypeStruct((seq, bsz, SSM_WIDTH), F32),
                   jax.ShapeDtypeStruct((t, N_BRANCH * D_MODEL), BF16)],
        scratch_shapes=[pltpu.VMEM((D_MODEL, IN_WIDTH), BF16), pltpu.VMEM((TOK_TILE, SSM_WIDTH), F32),
                        pltpu.SemaphoreType.DMA(())],
        compiler_params=_params("arbitrary"),
        name="inproj",
    )(x2d, norm_g, w_in, qg, kg, b_gate)


def _attn_kernel(sc_ref, qt_ref, k_ref, vt_ref, g_ref, o_ref, *, lam_init):
    tq = ATT_TILE
    seq = k_ref.shape[0]
    nq = seq // tq
    lam = sc_ref[LAM_ROW, 0] + lam_init
    bound = sc_ref[BOUND_ROW, 0]
    head_row = lax.broadcasted_iota(jnp.int32, (HEAD_W, tq), 0)
    key_pos = lax.broadcasted_iota(jnp.int32, (tq, 2 * tq), 0)
    qry_pos = lax.broadcasted_iota(jnp.int32, (tq, 2 * tq), 1) & (tq - 1)
    causal = key_pos <= qry_pos
    zero = jnp.zeros((HEAD_W, tq), BF16)

    def tile(qi, shift_by_bound):
        qt = qt_ref[:, qi * tq:(qi + 1) * tq]
        qst = jnp.concatenate([jnp.where(head_row < ATT_HEAD_DIM, qt, zero),
                               jnp.where(head_row >= ATT_HEAD_DIM, qt, zero)], axis=1)
        nk = qi * tq
        s_d = jnp.where(causal, _dot(k_ref[nk:nk + tq, :], qst), -jnp.inf)
        if qi > 0:
            s_m = _dot(k_ref[0:nk, :], qst)
        if shift_by_bound:
            m = bound
        else:
            m = jnp.max(s_d, axis=0, keepdims=True)
            if qi > 0:
                m = jnp.maximum(m, jnp.max(s_m, axis=0, keepdims=True))
        p_d = jnp.exp2(s_d - m)
        l = jnp.sum(p_d, axis=0, keepdims=True)
        acc = _dot(vt_ref[:, nk:nk + tq], p_d.astype(BF16))
        if qi > 0:
            p_m = jnp.exp2(s_m - m)
            l = l + jnp.sum(p_m, axis=0, keepdims=True)
            acc = acc + _dot(vt_ref[:, 0:nk], p_m.astype(BF16))
        accn = acc * (1.0 / l)
        ot = accn[:, :tq] - lam * accn[:, tq:]
        ms = jnp.mean(ot * ot, axis=0, keepdims=True)
        o = (ot * lax.rsqrt(ms + EPS)).T * g_ref[...] * (1.0 - lam_init)
        o_ref[qi * tq:(qi + 1) * tq, :] = o.astype(BF16)

    @pl.when(bound < ATT_BOUND_LIMIT)
    def _():
        for qi in range(nq):
            tile(qi, True)

    @pl.when(jnp.logical_not(bound < ATT_BOUND_LIMIT))
    def _():
        for qi in range(nq):
            tile(qi, False)


def _attention(scalars, qt, k, vt, subln_g, *, seq, lam_init):
    t = k.shape[0]
    bsz = t // seq
    fm = pl.BlockSpec((HEAD_W, seq), lambda b, h: (b * ATT_HEADS + h, 0))
    tm = pl.BlockSpec((seq, HEAD_W), lambda b, h: (b, h))
    return pl.pallas_call(
        functools.partial(_attn_kernel, lam_init=lam_init),
        grid=(bsz, ATT_HEADS),
        in_specs=[pl.BlockSpec(memory_space=pltpu.SMEM), fm, tm, fm, _const_spec((1, HEAD_W))],
        out_specs=tm,
        out_shape=jax.ShapeDtypeStruct((t, ATT_V_WIDTH), BF16),
        compiler_params=_params("parallel", "parallel"),
        name="attn",
    )(scalars, qt, k, vt, subln_g)


def _ssm_kernel(u_ref, bt_ref, cre_ref, cim_ref, are_ref, aim_ref, d_ref, z_hbm, st_scr, state_scr, z_buf, z_sem):
    n_t, bsz, _ = u_ref.shape
    rows = n_t * bsz
    chunk = pl.program_id(0)

    def z_copies(i):
        t0 = pl.multiple_of(i * n_t, n_t)
        return [pltpu.make_async_copy(z_buf.at[:, b], z_hbm.at[b, pl.ds(t0, n_t)], z_sem) for b in range(bsz)]

    @pl.when(chunk == 0)
    def _():
        state_scr[...] = jnp.zeros(state_scr.shape, F32)

    u = u_ref[...].reshape(rows, SSM_WIDTH)
    ub = u.astype(BF16)
    n_tiles = 2 * N_STATE // MXU_N
    for j in range(n_tiles):
        k0 = LANES * (((j % (n_tiles // 2)) * MXU_N // SSM_STATE * SSM_GROUP) // LANES)
        st_scr[:, j * MXU_N:(j + 1) * MXU_N] = _dot(ub[:, k0:k0 + LANES], bt_ref[j])

    for cc in range(N_STATE // SCAN_LANES):
        lo = cc * SCAN_LANES
        hi = N_STATE + lo
        a_r = jnp.broadcast_to(are_ref[:, lo:lo + SCAN_LANES], (bsz, SCAN_LANES))
        a_i = jnp.broadcast_to(aim_ref[:, lo:lo + SCAN_LANES], (bsz, SCAN_LANES))

        def step(t, carry, lo=lo, hi=hi, a_r=a_r, a_i=a_i):
            s_r, s_i = carry
            r0 = pl.multiple_of(t * bsz, bsz)
            n_r = a_r * s_r - a_i * s_i + st_scr[pl.ds(r0, bsz), lo:lo + SCAN_LANES]
            n_i = a_r * s_i + a_i * s_r + st_scr[pl.ds(r0, bsz), hi:hi + SCAN_LANES]
            st_scr[pl.ds(r0, bsz), lo:lo + SCAN_LANES] = n_r
            st_scr[pl.ds(r0, bsz), hi:hi + SCAN_LANES] = n_i
            return n_r, n_i

        s_r, s_i = lax.fori_loop(0, rows // bsz, step,
                                 (state_scr[:, lo:lo + SCAN_LANES], state_scr[:, hi:hi + SCAN_LANES]),
                                 unroll=8)
        state_scr[:, lo:lo + SCAN_LANES] = s_r
        state_scr[:, hi:hi + SCAN_LANES] = s_i

    @pl.when(chunk > 0)
    def _():
        for cp in z_copies(chunk - 1):
            cp.wait()
    half = N_STATE // 2
    for n in range(2):
        s_re = st_scr[:, n * half:(n + 1) * half].astype(BF16)
        s_im = st_scr[:, N_STATE + n * half:N_STATE + (n + 1) * half].astype(BF16)
        cols = slice(n * MXU_N, (n + 1) * MXU_N)
        y = _dot(s_re, cre_ref[n]) - _dot(s_im, cim_ref[n]) + d_ref[:, cols] * u[:, cols]
        z_buf[:, :, cols] = jax.nn.gelu(y).reshape(n_t, bsz, MXU_N)
    for cp in z_copies(chunk):
        cp.start()

    @pl.when(chunk == pl.num_programs(0) - 1)
    def _():
        for cp in z_copies(chunk):
            cp.wait()


def _ssm(u_tm, b_tiles, c_re_tiles, c_im_tiles, a_re, a_im, d_skip):
    seq, bsz, _ = u_tm.shape
    return pl.pallas_call(
        _ssm_kernel,
        grid=(seq // SCAN_T,),
        in_specs=[pl.BlockSpec((SCAN_T, bsz, SSM_WIDTH), lambda i: (i, 0, 0)),
                  _const_spec(b_tiles.shape), _const_spec(c_re_tiles.shape),
                  _const_spec(c_im_tiles.shape), _const_spec((1, N_STATE)), _const_spec((1, N_STATE)),
                  _const_spec((1, SSM_WIDTH))],
        out_specs=pl.BlockSpec(memory_space=pl.ANY),
        out_shape=jax.ShapeDtypeStruct((bsz, seq, SSM_WIDTH), F32),
        scratch_shapes=[pltpu.VMEM((SCAN_T * bsz, 2 * N_STATE), F32), pltpu.VMEM((bsz, 2 * N_STATE), F32),
                        pltpu.VMEM((SCAN_T, bsz, SSM_WIDTH), F32), pltpu.SemaphoreType.DMA(())],
        compiler_params=_params("arbitrary"),
        name="ssm",
    )(u_tm, b_tiles, c_re_tiles, c_im_tiles, a_re, a_im, d_skip)


def _route(logits, ltri):
    lane = lax.broadcasted_iota(jnp.int32, logits.shape, 1)
    lanef = lane.astype(F32)
    big = float(ROUTE_W)
    neg = -jnp.inf
    gl = jnp.where(lane < N_GROUPS, logits, neg)
    gmax = jnp.max(gl, axis=-1, keepdims=True)
    grp = jnp.min(jnp.where(gl == gmax, lanef, big), axis=-1, keepdims=True)
    p_grp = 1.0 / jnp.sum(jnp.exp(gl - gmax), axis=-1, keepdims=True)
    first = ROUTE_OFF + grp * EXPERTS_PER_GROUP
    in_grp = (lanef >= first) & (lanef < first + EXPERTS_PER_GROUP)
    el = jnp.where(in_grp, logits, neg)
    v1 = jnp.max(el, axis=-1, keepdims=True)
    i1 = jnp.min(jnp.where(el == v1, lanef, big), axis=-1, keepdims=True)
    el2 = jnp.where(lanef == i1, neg, el)
    v2 = jnp.max(el2, axis=-1, keepdims=True)
    i2 = jnp.min(jnp.where(el2 == v2, lanef, big), axis=-1, keepdims=True)
    e21 = jnp.exp(v2 - v1)
    w1 = p_grp / (1.0 + e21)
    w2 = p_grp * (e21 / (1.0 + e21))
    member = lanef == grp
    before = _dot(ltri, jnp.where(member, 1.0, 0.0).astype(BF16))
    rank = jnp.sum(jnp.where(member, before, 0.0), axis=-1, keepdims=True)
    return (jnp.where(lanef == i1, w1, 0.0) + jnp.where(lanef == i2, w2, 0.0)
            + jnp.where(lane == GRP_LANE, grp, 0.0) + jnp.where(lane == RANK_LANE, rank, 0.0))


def _mix_kernel(x_ref, o_ref, z_ref, gate_ref, wau_ref, wglu_ref, bglu_ref, wsu_ref, wout_ref,
                fg_ref, wr_ref, br_ref, ltri_ref, x1_ref, hn_ref, route_ref, route_t_ref,
                wau_scr, wglu_scr, wsu_scr, wout_scr):
    _round_once([(wau_ref, wau_scr), (wglu_ref, wglu_scr), (wsu_ref, wsu_scr), (wout_ref, wout_scr)])
    y_att = _dot(o_ref[...], wau_scr[...])
    z = z_ref[...]
    z = z * jax.nn.sigmoid(_dot(z.astype(BF16), wglu_scr[...]) + bglu_ref[...])
    y_ssm = _dot(z.astype(BF16), wsu_scr[...])
    mixed = (gate_ref[:, :D_MODEL].astype(F32) * y_att + gate_ref[:, D_MODEL:].astype(F32) * y_ssm)
    x1 = x_ref[...] + _dot(mixed.astype(BF16), wout_scr[...])
    x1_ref[...] = x1
    ms = jnp.mean(x1 * x1, axis=-1, keepdims=True)
    hn = x1 * lax.rsqrt(ms + EPS) * fg_ref[...]
    hn_hi = hn.astype(BF16)
    hn_ref[...] = hn_hi
    hn_lo = (hn - hn_hi.astype(F32)).astype(BF16)
    parts = _dot(jnp.concatenate([hn_hi, hn_lo], axis=1), wr_ref[...])
    logits = parts[:, :ROUTE_W] + parts[:, ROUTE_W:] + br_ref[...]
    route = _route(logits, ltri_ref[...])
    route_ref[...] = route
    route_t_ref[...] = route.T[:SUBLANES, :]


def _mix(x2d, o, z, gates, wau, wglu, bglu, wsu, wout, ffn_g, wr, br, ltri, *, layer):
    t = x2d.shape[0]
    row = lambda w: pl.BlockSpec((TOK_TILE, w), lambda i: (i, 0))
    return pl.pallas_call(
        _mix_kernel,
        grid=(t // TOK_TILE,),
        in_specs=[row(D_MODEL), row(ATT_V_WIDTH), row(SSM_WIDTH), row(N_BRANCH * D_MODEL),
                  _layer_weight(wau, layer), _layer_weight(wglu, layer), _const_spec(bglu.shape),
                  _layer_weight(wsu, layer), _layer_weight(wout, layer), _const_spec(ffn_g.shape),
                  _const_spec(wr.shape), _const_spec(br.shape),
                  _const_spec(ltri.shape)],
        out_specs=[row(D_MODEL), row(D_MODEL), row(ROUTE_W),
                   pl.BlockSpec((SUBLANES, TOK_TILE), lambda i: (0, i))],
        out_shape=[jax.ShapeDtypeStruct((t, D_MODEL), F32),
                   jax.ShapeDtypeStruct((t, D_MODEL), BF16),
                   jax.ShapeDtypeStruct((t, ROUTE_W), F32),
                   jax.ShapeDtypeStruct((SUBLANES, t), F32)],
        scratch_shapes=[pltpu.VMEM(w.shape[1:], BF16) for w in (wau, wglu, wsu, wout)],
        compiler_params=_params("arbitrary"),
        name="mix",
    )(x2d, o, z, gates, wau, wglu, bglu, wsu, wout, ffn_g, wr, br, ltri)


def _moe_kernel(hn_ref, route_ref, route_t_ref, wg_ref, wu_ref, wd_ref, out_ref, wg_scr, wu_scr, wd_scr):
    n = pl.program_id(0)

    @pl.when(pl.program_id(1) == 0)
    def _():
        for j in range(EXPERTS_PER_GROUP):
            cols = slice(j * EXPERT_FF, (j + 1) * EXPERT_FF)
            wg_scr[:, cols] = wg_ref[j].astype(BF16)
            wu_scr[:, cols] = wu_ref[j].astype(BF16)
            wd_scr[cols, :] = wd_ref[j].astype(BF16)

    tm = hn_ref.shape[0]
    nf = n.astype(F32)
    route = route_ref[...]
    lane = lax.broadcasted_iota(jnp.int32, route.shape, 1)
    grp_c = jnp.sum(jnp.where(lane == GRP_LANE, route, 0.0), axis=-1, keepdims=True)
    rank_c = jnp.sum(jnp.where(lane == RANK_LANE, route, 0.0), axis=-1, keepdims=True)
    grp_r = route_t_ref[GRP_LANE:GRP_LANE + 1, :]
    rank_r = route_t_ref[RANK_LANE:RANK_LANE + 1, :]
    r1 = route.astype(BF16)
    r2 = (route - r1.astype(F32)).astype(BF16)
    r3 = (route - r1.astype(F32) - r2.astype(F32)).astype(BF16)
    pieces = jnp.concatenate([r1, r2, r3], axis=1)

    def group_rows(base):
        slot_r = lax.broadcasted_iota(jnp.int32, (MOE_CAP, tm), 0).astype(F32) + base
        take = jnp.where((grp_r == nf) & (rank_r == slot_r), 1.0, 0.0).astype(BF16)
        slot_c = lax.broadcasted_iota(jnp.int32, (tm, MOE_CAP), 1).astype(F32) + base
        put = jnp.where((grp_c == nf) & (rank_c == slot_c), 1.0, 0.0).astype(BF16)
        xc = _dot(take, hn_ref[...]).astype(BF16)
        rc = _dot(take, pieces)
        rc = rc[:, :ROUTE_W] + rc[:, ROUTE_W:2 * ROUTE_W] + rc[:, 2 * ROUTE_W:]
        lane_c = lax.broadcasted_iota(jnp.int32, rc.shape, 1)
        first = ROUTE_OFF + n * EXPERTS_PER_GROUP
        act = jax.nn.silu(_dot(xc, wg_scr[...])) * _dot(xc, wu_scr[...])
        blocks = []
        for j in range(EXPERTS_PER_GROUP):
            w_j = jnp.sum(jnp.where(lane_c == first + j, rc, 0.0), axis=-1, keepdims=True)
            blocks.append((act[:, j * EXPERT_FF:(j + 1) * EXPERT_FF] * w_j).astype(BF16))
        y = _dot(jnp.concatenate(blocks, axis=1), wd_scr[...])
        return _dot(put, y.astype(BF16))

    out_ref[...] = group_rows(jnp.float32(0.0)).astype(BF16)

    n_rows = jnp.max(jnp.where(grp_c == nf, rank_c + 1.0, 0.0))
    n_pass = lax.div(n_rows.astype(jnp.int32) + (MOE_CAP - 1), MOE_CAP)

    def extra_pass(k, carry):
        more = group_rows((k * MOE_CAP).astype(F32))
        out_ref[...] = (out_ref[...].astype(F32) + more).astype(BF16)
        return carry

    lax.fori_loop(1, n_pass, extra_pass, 0)


def _moe(hn, route, route_t, w_gate, w_up, w_down, *, layer):
    t = hn.shape[0]
    row = lambda w: pl.BlockSpec((TOK_TILE, w), lambda n, i: (i, 0))
    grp_w = lambda shape: pl.BlockSpec((None, None) + shape, lambda n, i: (layer, n, 0, 0, 0),
                                       pipeline_mode=pl.Buffered(1))
    up_shape = (EXPERTS_PER_GROUP, D_MODEL, EXPERT_FF)
    down_shape = (EXPERTS_PER_GROUP, EXPERT_FF, D_MODEL)
    ff = EXPERTS_PER_GROUP * EXPERT_FF
    return pl.pallas_call(
        _moe_kernel,
        grid=(N_GROUPS, t // TOK_TILE),
        in_specs=[row(D_MODEL), row(ROUTE_W), pl.BlockSpec((SUBLANES, TOK_TILE), lambda n, i: (0, i)),
                  grp_w(up_shape), grp_w(up_shape), grp_w(down_shape)],
        out_specs=pl.BlockSpec((None, TOK_TILE, D_MODEL), lambda n, i: (n, i, 0)),
        out_shape=jax.ShapeDtypeStruct((N_GROUPS, t, D_MODEL), BF16),
        scratch_shapes=[pltpu.VMEM((D_MODEL, ff), BF16), pltpu.VMEM((D_MODEL, ff), BF16),
                        pltpu.VMEM((ff, D_MODEL), BF16)],
        compiler_params=_params("arbitrary", "arbitrary"),
        name="moe",
    )(hn, route, route_t, w_gate, w_up, w_down)


def _ple_kernel(x1_ref, c_ref, p_ref, pg_ref, wpg_ref, wple_ref, out_ref, wpg_scr, wple_scr):
    _round_once([(wpg_ref, wpg_scr), (wple_ref, wple_scr)])
    x2 = x1_ref[...]
    for n in range(N_GROUPS):
        x2 = x2 + c_ref[n].astype(F32)
    ms = jnp.mean(x2 * x2, axis=-1, keepdims=True)
    hp = (x2 * lax.rsqrt(ms + EPS) * pg_ref[...]).astype(BF16)
    gate = jax.nn.sigmoid(_dot(hp, wpg_scr[...]))
    out_ref[...] = x2 + gate * _dot(p_ref[...].astype(BF16), wple_scr[...])


def _ple(x1, contrib, p, ple_g, wpg, wple, *, layer):
    t = x1.shape[0]
    n_sb = p.shape[2] // TOK_TILE
    row = lambda w: pl.BlockSpec((TOK_TILE, w), lambda i: (i, 0))
    return pl.pallas_call(
        _ple_kernel,
        grid=(t // TOK_TILE,),
        in_specs=[row(D_MODEL), pl.BlockSpec((N_GROUPS, TOK_TILE, D_MODEL), lambda i: (0, i, 0)),
                  pl.BlockSpec((None, None, TOK_TILE, PLE_DIM), lambda i: (layer, i // n_sb, i % n_sb, 0)),
                  _const_spec(ple_g.shape), _layer_weight(wpg, layer), _layer_weight(wple, layer)],
        out_specs=row(D_MODEL),
        out_shape=jax.ShapeDtypeStruct((t, D_MODEL), F32),
        scratch_shapes=[pltpu.VMEM(wpg.shape[1:], BF16), pltpu.VMEM(wple.shape[1:], BF16)],
        compiler_params=_params("arbitrary"),
        name="ple",
    )(x1, contrib, p, ple_g, wpg, wple)


def kernel(x, p, attn_norm_g, w_in, b_gate, q_norm_g, k_norm_g, lam_qk, subln_g, w_attn_up, ssm_lam_re, ssm_lam_im, ssm_log_step, ssm_b_re, ssm_b_im, ssm_c_re, ssm_c_im, ssm_d, w_glu, b_glu, w_ssm_up, w_out, ffn_norm_g, w_router_group, b_router_group, w_router_expert, b_router_expert, w_exp_gate, w_exp_up, w_exp_down, ple_norm_g, w_ple_gate, w_ple):
    bsz, seq, d = x.shape
    depth = w_in.shape[0]
    t = bsz * seq
    assert d == D_MODEL and seq % TOK_TILE == 0 and seq % ATT_TILE == 0
    assert seq % SCAN_T == 0 and bsz == SUBLANES
    tok = jnp.arange(TOK_TILE)
    ltri = (tok[None, :] < tok[:, None]).astype(BF16)

    a_re, a_im, b_tiles, c_re_tiles, c_im_tiles, scalars = _prep(
        ssm_lam_re, ssm_lam_im, ssm_log_step, ssm_b_re, ssm_b_im, ssm_c_re, ssm_c_im, lam_qk, q_norm_g, k_norm_g)
    x2d = x.reshape(t, d)
    for i in range(depth):
        lam_init = _lambda_init(i)
        tile_gain = lambda g: jnp.tile(g.reshape(HEAD_W, 1), (COL_CHUNK // HEAD_W, 1))
        qt, k, vt, u_tm, gates = _inproj(
            x2d, attn_norm_g[i].reshape(1, d), w_in,
            tile_gain(q_norm_g[i]), tile_gain(k_norm_g[i]), b_gate[i].reshape(1, -1), seq=seq, layer=i)

        o = _attention(scalars[i], qt, k, vt, subln_g[i].reshape(1, HEAD_W), seq=seq, lam_init=lam_init)

        z = _ssm(u_tm, b_tiles[i], c_re_tiles[i], c_im_tiles[i],
                 a_re[i, ::SSM_GROUP].reshape(1, N_STATE), a_im[i, ::SSM_GROUP].reshape(1, N_STATE),
                 ssm_d[i].reshape(1, SSM_WIDTH))

        w_r = jnp.concatenate(
            [w_router_group[i], jnp.transpose(w_router_expert[i], (1, 0, 2)).reshape(d, N_EXPERTS),
             jnp.zeros((d, ROUTE_W - N_GROUPS - N_EXPERTS), F32)], axis=1)
        w_r_hi = w_r.astype(BF16)
        w_r_lo = (w_r - w_r_hi.astype(F32)).astype(BF16)
        w_r2 = jnp.concatenate([jnp.concatenate([w_r_hi, w_r_lo], axis=1),
                                jnp.concatenate([w_r_hi, jnp.zeros_like(w_r_lo)], axis=1)], axis=0)
        b_r = jnp.concatenate([b_router_group[i], b_router_expert[i].reshape(-1),
                               jnp.zeros((ROUTE_W - N_GROUPS - N_EXPERTS,), F32)]).reshape(1, ROUTE_W)
        x1, hn, route, route_t = _mix(
            x2d, o, z.reshape(t, SSM_WIDTH), gates,
            w_attn_up, w_glu, b_glu[i].reshape(1, -1), w_ssm_up, w_out, ffn_norm_g[i].reshape(1, d),
            w_r2, b_r, ltri, layer=i)

        contrib = _moe(hn, route, route_t, w_exp_gate, w_exp_up, w_exp_down, layer=i)
        x2d = _ple(x1, contrib, p, ple_norm_g[i].reshape(1, d), w_ple_gate, w_ple, layer=i)
    return x2d.reshape(bsz, seq, d)
```

```python
import functools
import math

import jax
import jax.numpy as jnp
from jax import lax
from jax.experimental import pallas as pl
from jax.experimental.pallas import tpu as pltpu

F32 = jnp.float32
BF16 = jnp.bfloat16

D_MODEL = 1024
ATT_HEADS = 8
ATT_HEAD_DIM = 64
HEAD_W = 2 * ATT_HEAD_DIM
ATT_QK_WIDTH = ATT_HEADS * HEAD_W
ATT_V_WIDTH = ATT_HEADS * HEAD_W
SSM_WIDTH = 512
SSM_GROUP = 16
SSM_GROUPS = SSM_WIDTH // SSM_GROUP
SSM_STATE = 64
N_STATE = SSM_GROUPS * SSM_STATE
N_BRANCH = 2
IN_WIDTH = 2 * ATT_QK_WIDTH + ATT_V_WIDTH + SSM_WIDTH + N_BRANCH * D_MODEL
N_GROUPS = 4
EXPERTS_PER_GROUP = 8
N_EXPERTS = N_GROUPS * EXPERTS_PER_GROUP
EXPERT_FF = 256
PLE_DIM = 256
EPS = 1e-6

LANES = 128
SUBLANES = 8
MXU_N = 256
VMEM_LIMIT = 56 * 1024 * 1024

TOK_TILE = 512
MOE_CAP = 160
COL_CHUNK = 512
ATT_TILE = 256
Q_SCALE = ATT_HEAD_DIM ** -0.5 * math.log2(math.e)
LAM_ROW = 0
BOUND_ROW = 1
ATT_BOUND_SLACK = 1.01
ATT_BOUND_LIMIT = 60.0
SCAN_T = 64
SCAN_LANES = 1024
ROUTE_W = LANES
ROUTE_OFF = N_GROUPS
GRP_LANE = 0
RANK_LANE = 1


def _lambda_init(layer_idx):
    return 0.8 - 0.6 * math.exp(-0.3 * layer_idx)


def _dot(a, b):
    return jnp.dot(a, b, preferred_element_type=F32)


def _const_spec(shape):
    nd = len(shape)
    return pl.BlockSpec(shape, lambda *_: (0,) * nd, pipeline_mode=pl.Buffered(1))


def _layer_weight(stacked, layer):
    shape = stacked.shape[1:]
    return pl.BlockSpec((None,) + shape, lambda *_: (layer,) + (0,) * len(shape), pipeline_mode=pl.Buffered(1))


def _round_once(pairs):
    @pl.when(pl.program_id(0) == 0)
    def _():
        for w_ref, w_scr in pairs:
            w_scr[...] = w_ref[...].astype(BF16)


def _params(*sem):
    return pltpu.CompilerParams(dimension_semantics=sem, vmem_limit_bytes=VMEM_LIMIT)


def _prep_kernel(lre_ref, lim_ref, lstep_ref, bre_ref, bim_ref, cre_ref, cim_ref, lqk_ref, qg_ref, kg_ref,
                 are_ref, aim_ref, bt_ref, ctre_ref, ctim_ref, lam_ref):
    lr = lre_ref[...]
    li = lim_ref[...]
    dt = jnp.exp(lstep_ref[...])
    mag = jnp.exp(lr * dt)
    ab_re = mag * jnp.cos(li * dt)
    ab_im = mag * jnp.sin(li * dt)
    den = lr * lr + li * li
    nr = ab_re - 1.0
    coef_re = (nr * lr + ab_im * li) / den
    coef_im = (ab_im * lr - nr * li) / den
    br = bre_ref[...]
    bi = bim_ref[...]
    are_ref[...] = ab_re
    aim_ref[...] = ab_im
    bb_re = coef_re * br - coef_im * bi
    bb_im = coef_re * bi + coef_im * br

    def spread(x, reps):
        w = x.shape[1]
        src = lax.broadcasted_iota(jnp.int32, (w, w * reps), 0)
        dst = lax.broadcasted_iota(jnp.int32, (w, w * reps), 1) & (w - 1)
        return _dot(x.astype(BF16), jnp.where(src == dst, 1.0, 0.0).astype(BF16))

    def same_group(shape, row0, row_shift, col0, col_shift):
        rows = (lax.broadcasted_iota(jnp.int32, shape, 0) + row0) >> row_shift
        cols = (lax.broadcasted_iota(jnp.int32, shape, 1) + col0) >> col_shift
        return rows == cols

    n_re = N_STATE // MXU_N
    ch_shift = SSM_GROUP.bit_length() - 1
    st_shift = SSM_STATE.bit_length() - 1
    for part, bb in enumerate((bb_re, bb_im)):
        for j in range(n_re):
            r0 = LANES * ((j * MXU_N // SSM_STATE * SSM_GROUP) // LANES)
            x = spread(bb[r0:r0 + LANES, :], MXU_N // SSM_STATE)
            keep = same_group(x.shape, r0, ch_shift, j * MXU_N, st_shift)
            bt_ref[part * n_re + j] = jnp.where(keep, x, 0.0).astype(BF16)
    half = N_STATE // 2
    for c_ref, ct_ref in ((cre_ref, ctre_ref), (cim_ref, ctim_ref)):
        for n in range(2):
            y = spread(c_ref[n * half:(n + 1) * half, :], MXU_N // SSM_GROUP)
            keep = same_group(y.shape, n * half, st_shift, n * MXU_N, ch_shift)
            ct_ref[n] = jnp.where(keep, y, 0.0).astype(BF16)

    lq = lqk_ref[...]
    s01 = jnp.sum(lq[0:1] * lq[1:2], axis=-1, keepdims=True)
    s23 = jnp.sum(lq[2:3] * lq[3:4], axis=-1, keepdims=True)
    lam_dyn = jnp.exp(s01) - jnp.exp(s23)
    g_max = jnp.max(jnp.abs(qg_ref[...]), keepdims=True) * jnp.max(jnp.abs(kg_ref[...]), keepdims=True)
    bound = g_max * (ATT_HEAD_DIM * Q_SCALE * ATT_BOUND_SLACK)
    row = lax.broadcasted_iota(jnp.int32, lam_ref.shape, 0)
    lam_ref[...] = jnp.where(row == LAM_ROW, lam_dyn, bound)


def _prep(lam_re, lam_im, log_step, b_re, b_im, c_re, c_im, lam_qk, q_gain, k_gain):
    depth = lam_re.shape[0]
    rows = SSM_WIDTH
    rep = lambda a: jnp.repeat(a, SSM_GROUP, axis=1)
    lstep = jnp.broadcast_to(log_step[:, :, None], lam_re.shape)
    tr_b = lambda b: jnp.transpose(b, (0, 1, 3, 2)).reshape(depth, rows, SSM_STATE)
    tr_c = lambda c: jnp.transpose(c, (0, 1, 3, 2)).reshape(depth, N_STATE, SSM_GROUP)
    lay = lambda *shape: pl.BlockSpec((None,) + shape, lambda i: (i,) + (0,) * len(shape))
    blk = lay(rows, SSM_STATE)
    out = jax.ShapeDtypeStruct((depth, rows, SSM_STATE), F32)
    n_bt = 2 * N_STATE // MXU_N
    return pl.pallas_call(
        _prep_kernel,
        grid=(depth,),
        in_specs=[blk, blk, blk, blk, blk, lay(N_STATE, SSM_GROUP), lay(N_STATE, SSM_GROUP),
                  lay(4, ATT_HEAD_DIM), lay(2, ATT_HEAD_DIM), lay(2, ATT_HEAD_DIM)],
        out_specs=[blk, blk, lay(n_bt, LANES, MXU_N), lay(2, N_STATE // 2, MXU_N), lay(2, N_STATE // 2, MXU_N),
                   lay(SUBLANES, LANES)],
        out_shape=[out, out, jax.ShapeDtypeStruct((depth, n_bt, LANES, MXU_N), BF16),
                   jax.ShapeDtypeStruct((depth, 2, N_STATE // 2, MXU_N), BF16),
                   jax.ShapeDtypeStruct((depth, 2, N_STATE // 2, MXU_N), BF16),
                   jax.ShapeDtypeStruct((depth, SUBLANES, LANES), F32)],
        compiler_params=_params("arbitrary"),
        name="prep",
    )(rep(lam_re), rep(lam_im), rep(lstep), tr_b(b_re), tr_b(b_im), tr_c(c_re), tr_c(c_im), lam_qk, q_gain, k_gain)


def _inproj_kernel(x_ref, g_ref, w_ref, qg_ref, kg_ref, bg_ref,
                   qt_ref, k_ref, vt_ref, u_hbm, gate_ref, w_scr, u_buf, u_sem, *, n_sb):
    _round_once([(w_ref, w_scr)])
    step = pl.program_id(0)

    def u_copy(i):
        s0 = pl.multiple_of((i % n_sb) * TOK_TILE, TOK_TILE)
        return pltpu.make_async_copy(u_buf, u_hbm.at[pl.ds(s0, TOK_TILE), i // n_sb], u_sem)

    xf = x_ref[...]
    ms = jnp.mean(xf * xf, axis=-1, keepdims=True)
    h = (xf * lax.rsqrt(ms + EPS) * g_ref[...]).astype(BF16)

    def head_norm_t(y, gain_col):
        rows = y.shape[0]
        y3 = y.T.reshape(COL_CHUNK // ATT_HEAD_DIM, ATT_HEAD_DIM, rows)
        ms = jnp.sum(y3 * y3, axis=1, keepdims=True) * (1.0 / ATT_HEAD_DIM)
        return (y3 * lax.rsqrt(ms + EPS)).reshape(COL_CHUNK, rows) * gain_col

    n_qk = ATT_QK_WIDTH // COL_CHUNK
    n_v = ATT_V_WIDTH // COL_CHUNK
    for c in range(IN_WIDTH // COL_CHUNK):
        y = _dot(h, w_scr[:, c * COL_CHUNK:(c + 1) * COL_CHUNK])
        if c < n_qk:
            qt_ref[c * COL_CHUNK:(c + 1) * COL_CHUNK, :] = (head_norm_t(y, qg_ref[...]) * Q_SCALE).astype(BF16)
        elif c < 2 * n_qk:
            cc = c - n_qk
            k_ref[:, cc * COL_CHUNK:(cc + 1) * COL_CHUNK] = head_norm_t(y, kg_ref[...]).T.astype(BF16)
        elif c < 2 * n_qk + n_v:
            cc = c - 2 * n_qk
            vt_ref[cc * COL_CHUNK:(cc + 1) * COL_CHUNK, :] = y.T.astype(BF16)
        elif c == 2 * n_qk + n_v:
            @pl.when(step > 0)
            def _():
                u_copy(step - 1).wait()
            u_buf[...] = y
            u_copy(step).start()
        else:
            cc = c - (2 * n_qk + n_v + 1)
            b = bg_ref[:, cc * COL_CHUNK:(cc + 1) * COL_CHUNK]
            gate_ref[:, cc * COL_CHUNK:(cc + 1) * COL_CHUNK] = jax.nn.sigmoid(y + b).astype(BF16)

    @pl.when(step == pl.num_programs(0) - 1)
    def _():
        u_copy(step).wait()


def _inproj(x2d, norm_g, w_in, qg, kg, b_gate, *, seq, layer):
    t = x2d.shape[0]
    n_sb = seq // TOK_TILE
    bsz = t // seq
    row = lambda w: pl.BlockSpec((TOK_TILE, w), lambda i: (i, 0))
    col = lambda w: pl.BlockSpec((w, TOK_TILE), lambda i: (i // n_sb, i % n_sb))
    return pl.pallas_call(
        functools.partial(_inproj_kernel, n_sb=n_sb),
        grid=(t // TOK_TILE,),
        in_specs=[row(D_MODEL), _const_spec((1, D_MODEL)), _layer_weight(w_in, layer),
                  _const_spec((COL_CHUNK, 1)), _const_spec((COL_CHUNK, 1)),
                  _const_spec((1, N_BRANCH * D_MODEL))],
        out_specs=[col(ATT_QK_WIDTH), row(ATT_QK_WIDTH), col(ATT_V_WIDTH),
                   pl.BlockSpec(memory_space=pl.ANY),
                   row(N_BRANCH * D_MODEL)],
        out_shape=[jax.ShapeDtypeStruct((bsz * ATT_QK_WIDTH, seq), BF16),
                   jax.ShapeDtypeStruct((t, ATT_QK_WIDTH), BF16),
                   jax.ShapeDtypeStruct((bsz * ATT_V_WIDTH, seq), BF16),
                   jax.ShapeDtypeStruct((seq, bsz, SSM_WIDTH), F32),
                   jax.ShapeDtypeStruct((t, N_BRANCH * D_MODEL), BF16)],
        scratch_shapes=[pltpu.VMEM((D_MODEL, IN_WIDTH), BF16), pltpu.VMEM((TOK_TILE, SSM_WIDTH), F32),
                        pltpu.SemaphoreType.DMA(())],
        compiler_params=_params("arbitrary"),
        name="inproj",
    )(x2d, norm_g, w_in, qg, kg, b_gate)


def _attn_kernel(sc_ref, qt_ref, k_ref, vt_ref, g_ref, o_ref, *, lam_init):
    tq = ATT_TILE
    seq = k_ref.shape[0]
    nq = seq // tq
    lam = sc_ref[LAM_ROW, 0] + lam_init
    bound = sc_ref[BOUND_ROW, 0]
    head_row = lax.broadcasted_iota(jnp.int32, (HEAD_W, tq), 0)
    key_pos = lax.broadcasted_iota(jnp.int32, (tq, 2 * tq), 0)
    qry_pos = lax.broadcasted_iota(jnp.int32, (tq, 2 * tq), 1) & (tq - 1)
    causal = key_pos <= qry_pos
    zero = jnp.zeros((HEAD_W, tq), BF16)

    def tile(qi, shift_by_bound):
        qt = qt_ref[:, qi * tq:(qi + 1) * tq]
        qst = jnp.concatenate([jnp.where(head_row < ATT_HEAD_DIM, qt, zero),
                               jnp.where(head_row >= ATT_HEAD_DIM, qt, zero)], axis=1)
        nk = qi * tq
        s_d = jnp.where(causal, _dot(k_ref[nk:nk + tq, :], qst), -jnp.inf)
        if qi > 0:
            s_m = _dot(k_ref[0:nk, :], qst)
        if shift_by_bound:
            m = bound
        else:
            m = jnp.max(s_d, axis=0, keepdims=True)
            if qi > 0:
                m = jnp.maximum(m, jnp.max(s_m, axis=0, keepdims=True))
        p_d = jnp.exp2(s_d - m)
        l = jnp.sum(p_d, axis=0, keepdims=True)
        acc = _dot(vt_ref[:, nk:nk + tq], p_d.astype(BF16))
        if qi > 0:
            p_m = jnp.exp2(s_m - m)
            l = l + jnp.sum(p_m, axis=0, keepdims=True)
            acc = acc + _dot(vt_ref[:, 0:nk], p_m.astype(BF16))
        accn = acc * (1.0 / l)
        ot = accn[:, :tq] - lam * accn[:, tq:]
        ms = jnp.mean(ot * ot, axis=0, keepdims=True)
        o = (ot * lax.rsqrt(ms + EPS)).T * g_ref[...] * (1.0 - lam_init)
        o_ref[qi * tq:(qi + 1) * tq, :] = o.astype(BF16)

    @pl.when(bound < ATT_BOUND_LIMIT)
    def _():
        for qi in range(nq):
            tile(qi, True)

    @pl.when(jnp.logical_not(bound < ATT_BOUND_LIMIT))
    def _():
        for qi in range(nq):
            tile(qi, False)


def _attention(scalars, qt, k, vt, subln_g, *, seq, lam_init):
    t = k.shape[0]
    bsz = t // seq
    fm = pl.BlockSpec((HEAD_W, seq), lambda b, h: (b * ATT_HEADS + h, 0))
    tm = pl.BlockSpec((seq, HEAD_W), lambda b, h: (b, h))
    return pl.pallas_call(
        functools.partial(_attn_kernel, lam_init=lam_init),
        grid=(bsz, ATT_HEADS),
        in_specs=[pl.BlockSpec(memory_space=pltpu.SMEM), fm, tm, fm, _const_spec((1, HEAD_W))],
        out_specs=tm,
        out_shape=jax.ShapeDtypeStruct((t, ATT_V_WIDTH), BF16),
        compiler_params=_params("parallel", "parallel"),
        name="attn",
    )(scalars, qt, k, vt, subln_g)


def _ssm_kernel(u_ref, bt_ref, cre_ref, cim_ref, are_ref, aim_ref, d_ref, z_hbm, st_scr, state_scr, z_buf, z_sem):
    n_t, bsz, _ = u_ref.shape
    rows = n_t * bsz
    chunk = pl.program_id(0)

    def z_copies(i):
        t0 = pl.multiple_of(i * n_t, n_t)
        return [pltpu.make_async_copy(z_buf.at[:, b], z_hbm.at[b, pl.ds(t0, n_t)], z_sem) for b in range(bsz)]

    @pl.when(chunk == 0)
    def _():
        state_scr[...] = jnp.zeros(state_scr.shape, F32)

    u = u_ref[...].reshape(rows, SSM_WIDTH)
    ub = u.astype(BF16)
    n_tiles = 2 * N_STATE // MXU_N
    for j in range(n_tiles):
        k0 = LANES * (((j % (n_tiles // 2)) * MXU_N // SSM_STATE * SSM_GROUP) // LANES)
        st_scr[:, j * MXU_N:(j + 1) * MXU_N] = _dot(ub[:, k0:k0 + LANES], bt_ref[j])

    for cc in range(N_STATE // SCAN_LANES):
        lo = cc * SCAN_LANES
        hi = N_STATE + lo
        a_r = jnp.broadcast_to(are_ref[:, lo:lo + SCAN_LANES], (bsz, SCAN_LANES))
        a_i = jnp.broadcast_to(aim_ref[:, lo:lo + SCAN_LANES], (bsz, SCAN_LANES))

        def step(t, carry, lo=lo, hi=hi, a_r=a_r, a_i=a_i):
            s_r, s_i = carry
            r0 = pl.multiple_of(t * bsz, bsz)
            n_r = a_r * s_r - a_i * s_i + st_scr[pl.ds(r0, bsz), lo:lo + SCAN_LANES]
            n_i = a_r * s_i + a_i * s_r + st_scr[pl.ds(r0, bsz), hi:hi + SCAN_LANES]
            st_scr[pl.ds(r0, bsz), lo:lo + SCAN_LANES] = n_r
            st_scr[pl.ds(r0, bsz), hi:hi + SCAN_LANES] = n_i
            return n_r, n_i

        s_r, s_i = lax.fori_loop(0, rows // bsz, step,
                                 (state_scr[:, lo:lo + SCAN_LANES], state_scr[:, hi:hi + SCAN_LANES]),
                                 unroll=8)
        state_scr[:, lo:lo + SCAN_LANES] = s_r
        state_scr[:, hi:hi + SCAN_LANES] = s_i

    @pl.when(chunk > 0)
    def _():
        for cp in z_copies(chunk - 1):
            cp.wait()
    half = N_STATE // 2
    for n in range(2):
        s_re = st_scr[:, n * half:(n + 1) * half].astype(BF16)
        s_im = st_scr[:, N_STATE + n * half:N_STATE + (n + 1) * half].astype(BF16)
        cols = slice(n * MXU_N, (n + 1) * MXU_N)
        y = _dot(s_re, cre_ref[n]) - _dot(s_im, cim_ref[n]) + d_ref[:, cols] * u[:, cols]
        z_buf[:, :, cols] = jax.nn.gelu(y).reshape(n_t, bsz, MXU_N)
    for cp in z_copies(chunk):
        cp.start()

    @pl.when(chunk == pl.num_programs(0) - 1)
    def _():
        for cp in z_copies(chunk):
            cp.wait()


def _ssm(u_tm, b_tiles, c_re_tiles, c_im_tiles, a_re, a_im, d_skip):
    seq, bsz, _ = u_tm.shape
    return pl.pallas_call(
        _ssm_kernel,
        grid=(seq // SCAN_T,),
        in_specs=[pl.BlockSpec((SCAN_T, bsz, SSM_WIDTH), lambda i: (i, 0, 0)),
                  _const_spec(b_tiles.shape), _const_spec(c_re_tiles.shape),
                  _const_spec(c_im_tiles.shape), _const_spec((1, N_STATE)), _const_spec((1, N_STATE)),
                  _const_spec((1, SSM_WIDTH))],
        out_specs=pl.BlockSpec(memory_space=pl.ANY),
        out_shape=jax.ShapeDtypeStruct((bsz, seq, SSM_WIDTH), F32),
        scratch_shapes=[pltpu.VMEM((SCAN_T * bsz, 2 * N_STATE), F32), pltpu.VMEM((bsz, 2 * N_STATE), F32),
                        pltpu.VMEM((SCAN_T, bsz, SSM_WIDTH), F32), pltpu.SemaphoreType.DMA(())],
        compiler_params=_params("arbitrary"),
        name="ssm",
    )(u_tm, b_tiles, c_re_tiles, c_im_tiles, a_re, a_im, d_skip)


def _route(logits, ltri):
    lane = lax.broadcasted_iota(jnp.int32, logits.shape, 1)
    lanef = lane.astype(F32)
    big = float(ROUTE_W)
    neg = -jnp.inf
    gl = jnp.where(lane < N_GROUPS, logits, neg)
    gmax = jnp.max(gl, axis=-1, keepdims=True)
    grp = jnp.min(jnp.where(gl == gmax, lanef, big), axis=-1, keepdims=True)
    p_grp = 1.0 / jnp.sum(jnp.exp(gl - gmax), axis=-1, keepdims=True)
    first = ROUTE_OFF + grp * EXPERTS_PER_GROUP
    in_grp = (lanef >= first) & (lanef < first + EXPERTS_PER_GROUP)
    el = jnp.where(in_grp, logits, neg)
    v1 = jnp.max(el, axis=-1, keepdims=True)
    i1 = jnp.min(jnp.where(el == v1, lanef, big), axis=-1, keepdims=True)
    el2 = jnp.where(lanef == i1, neg, el)
    v2 = jnp.max(el2, axis=-1, keepdims=True)
    i2 = jnp.min(jnp.where(el2 == v2, lanef, big), axis=-1, keepdims=True)
    e21 = jnp.exp(v2 - v1)
    w1 = p_grp / (1.0 + e21)
    w2 = p_grp * (e21 / (1.0 + e21))
    member = lanef == grp
    before = _dot(ltri, jnp.where(member, 1.0, 0.0).astype(BF16))
    rank = jnp.sum(jnp.where(member, before, 0.0), axis=-1, keepdims=True)
    return (jnp.where(lanef == i1, w1, 0.0) + jnp.where(lanef == i2, w2, 0.0)
            + jnp.where(lane == GRP_LANE, grp, 0.0) + jnp.where(lane == RANK_LANE, rank, 0.0))


def _mix_kernel(x_ref, o_ref, z_ref, gate_ref, wau_ref, wglu_ref, bglu_ref, wsu_ref, wout_ref,
                fg_ref, wrh_ref, wrl_ref, br_ref, ltri_ref, x1_ref, hn_ref, route_ref, route_t_ref,
                wau_scr, wglu_scr, wsu_scr, wout_scr):
    _round_once([(wau_ref, wau_scr), (wglu_ref, wglu_scr), (wsu_ref, wsu_scr), (wout_ref, wout_scr)])
    y_att = _dot(o_ref[...], wau_scr[...])
    z = z_ref[...]
    z = z * jax.nn.sigmoid(_dot(z.astype(BF16), wglu_scr[...]) + bglu_ref[...])
    y_ssm = _dot(z.astype(BF16), wsu_scr[...])
    mixed = (gate_ref[:, :D_MODEL].astype(F32) * y_att + gate_ref[:, D_MODEL:].astype(F32) * y_ssm)
    x1 = x_ref[...] + _dot(mixed.astype(BF16), wout_scr[...])
    x1_ref[...] = x1
    ms = jnp.mean(x1 * x1, axis=-1, keepdims=True)
    hn = x1 * lax.rsqrt(ms + EPS) * fg_ref[...]
    hn_hi = hn.astype(BF16)
    hn_ref[...] = hn_hi
    hn_lo = (hn - hn_hi.astype(F32)).astype(BF16)
    logits = (_dot(hn_hi, wrh_ref[...]) + _dot(hn_lo, wrh_ref[...]) + _dot(hn_hi, wrl_ref[...])
              + br_ref[...])
    route = _route(logits, ltri_ref[...])
    route_ref[...] = route
    route_t_ref[...] = route.T[:SUBLANES, :]


def _mix(x2d, o, z, gates, wau, wglu, bglu, wsu, wout, ffn_g, wr_hi, wr_lo, br, ltri, *, layer):
    t = x2d.shape[0]
    row = lambda w: pl.BlockSpec((TOK_TILE, w), lambda i: (i, 0))
    return pl.pallas_call(
        _mix_kernel,
        grid=(t // TOK_TILE,),
        in_specs=[row(D_MODEL), row(ATT_V_WIDTH), row(SSM_WIDTH), row(N_BRANCH * D_MODEL),
                  _layer_weight(wau, layer), _layer_weight(wglu, layer), _const_spec(bglu.shape),
                  _layer_weight(wsu, layer), _layer_weight(wout, layer), _const_spec(ffn_g.shape),
                  _const_spec(wr_hi.shape), _const_spec(wr_lo.shape), _const_spec(br.shape),
                  _const_spec(ltri.shape)],
        out_specs=[row(D_MODEL), row(D_MODEL), row(ROUTE_W),
                   pl.BlockSpec((SUBLANES, TOK_TILE), lambda i: (0, i))],
        out_shape=[jax.ShapeDtypeStruct((t, D_MODEL), F32),
                   jax.ShapeDtypeStruct((t, D_MODEL), BF16),
                   jax.ShapeDtypeStruct((t, ROUTE_W), F32),
                   jax.ShapeDtypeStruct((SUBLANES, t), F32)],
        scratch_shapes=[pltpu.VMEM(w.shape[1:], BF16) for w in (wau, wglu, wsu, wout)],
        compiler_params=_params("arbitrary"),
        name="mix",
    )(x2d, o, z, gates, wau, wglu, bglu, wsu, wout, ffn_g, wr_hi, wr_lo, br, ltri)


def _moe_kernel(hn_ref, route_ref, route_t_ref, wg_ref, wu_ref, wd_ref, out_ref, wg_scr, wu_scr, wd_scr):
    n = pl.program_id(0)

    @pl.when(pl.program_id(1) == 0)
    def _():
        for j in range(EXPERTS_PER_GROUP):
            cols = slice(j * EXPERT_FF, (j + 1) * EXPERT_FF)
            wg_scr[:, cols] = wg_ref[j].astype(BF16)
            wu_scr[:, cols] = wu_ref[j].astype(BF16)
            wd_scr[cols, :] = wd_ref[j].astype(BF16)

    tm = hn_ref.shape[0]
    nf = n.astype(F32)
    route = route_ref[...]
    lane = lax.broadcasted_iota(jnp.int32, route.shape, 1)
    grp_c = jnp.sum(jnp.where(lane == GRP_LANE, route, 0.0), axis=-1, keepdims=True)
    rank_c = jnp.sum(jnp.where(lane == RANK_LANE, route, 0.0), axis=-1, keepdims=True)
    grp_r = route_t_ref[GRP_LANE:GRP_LANE + 1, :]
    rank_r = route_t_ref[RANK_LANE:RANK_LANE + 1, :]
    r1 = route.astype(BF16)
    r2 = (route - r1.astype(F32)).astype(BF16)
    r3 = (route - r1.astype(F32) - r2.astype(F32)).astype(BF16)
    pieces = jnp.concatenate([r1, r2, r3], axis=1)

    def group_rows(base):
        slot_r = lax.broadcasted_iota(jnp.int32, (MOE_CAP, tm), 0).astype(F32) + base
        take = jnp.where((grp_r == nf) & (rank_r == slot_r), 1.0, 0.0).astype(BF16)
        slot_c = lax.broadcasted_iota(jnp.int32, (tm, MOE_CAP), 1).astype(F32) + base
        put = jnp.where((grp_c == nf) & (rank_c == slot_c), 1.0, 0.0).astype(BF16)
        xc = _dot(take, hn_ref[...]).astype(BF16)
        rc = _dot(take, pieces)
        rc = rc[:, :ROUTE_W] + rc[:, ROUTE_W:2 * ROUTE_W] + rc[:, 2 * ROUTE_W:]
        lane_c = lax.broadcasted_iota(jnp.int32, rc.shape, 1)
        first = ROUTE_OFF + n * EXPERTS_PER_GROUP
        act = jax.nn.silu(_dot(xc, wg_scr[...])) * _dot(xc, wu_scr[...])
        blocks = []
        for j in range(EXPERTS_PER_GROUP):
            w_j = jnp.sum(jnp.where(lane_c == first + j, rc, 0.0), axis=-1, keepdims=True)
            blocks.append((act[:, j * EXPERT_FF:(j + 1) * EXPERT_FF] * w_j).astype(BF16))
        y = _dot(jnp.concatenate(blocks, axis=1), wd_scr[...])
        return _dot(put, y.astype(BF16))

    out_ref[...] = group_rows(jnp.float32(0.0)).astype(BF16)

    n_rows = jnp.max(jnp.where(grp_c == nf, rank_c + 1.0, 0.0))
    n_pass = lax.div(n_rows.astype(jnp.int32) + (MOE_CAP - 1), MOE_CAP)

    def extra_pass(k, carry):
        more = group_rows((k * MOE_CAP).astype(F32))
        out_ref[...] = (out_ref[...].astype(F32) + more).astype(BF16)
        return carry

    lax.fori_loop(1, n_pass, extra_pass, 0)


def _moe(hn, route, route_t, w_gate, w_up, w_down, *, layer):
    t = hn.shape[0]
    row = lambda w: pl.BlockSpec((TOK_TILE, w), lambda n, i: (i, 0))
    grp_w = lambda shape: pl.BlockSpec((None, None) + shape, lambda n, i: (layer, n, 0, 0, 0),
                                       pipeline_mode=pl.Buffered(1))
    up_shape = (EXPERTS_PER_GROUP, D_MODEL, EXPERT_FF)
    down_shape = (EXPERTS_PER_GROUP, EXPERT_FF, D_MODEL)
    ff = EXPERTS_PER_GROUP * EXPERT_FF
    return pl.pallas_call(
        _moe_kernel,
        grid=(N_GROUPS, t // TOK_TILE),
        in_specs=[row(D_MODEL), row(ROUTE_W), pl.BlockSpec((SUBLANES, TOK_TILE), lambda n, i: (0, i)),
                  grp_w(up_shape), grp_w(up_shape), grp_w(down_shape)],
        out_specs=pl.BlockSpec((None, TOK_TILE, D_MODEL), lambda n, i: (n, i, 0)),
        out_shape=jax.ShapeDtypeStruct((N_GROUPS, t, D_MODEL), BF16),
        scratch_shapes=[pltpu.VMEM((D_MODEL, ff), BF16), pltpu.VMEM((D_MODEL, ff), BF16),
                        pltpu.VMEM((ff, D_MODEL), BF16)],
        compiler_params=_params("arbitrary", "arbitrary"),
        name="moe",
    )(hn, route, route_t, w_gate, w_up, w_down)


def _ple_kernel(x1_ref, c_ref, p_ref, pg_ref, wpg_ref, wple_ref, out_ref, wpg_scr, wple_scr):
    _round_once([(wpg_ref, wpg_scr), (wple_ref, wple_scr)])
    x2 = x1_ref[...]
    for n in range(N_GROUPS):
        x2 = x2 + c_ref[n].astype(F32)
    ms = jnp.mean(x2 * x2, axis=-1, keepdims=True)
    hp = (x2 * lax.rsqrt(ms + EPS) * pg_ref[...]).astype(BF16)
    gate = jax.nn.sigmoid(_dot(hp, wpg_scr[...]))
    out_ref[...] = x2 + gate * _dot(p_ref[...].astype(BF16), wple_scr[...])


def _ple(x1, contrib, p, ple_g, wpg, wple, *, layer):
    t = x1.shape[0]
    n_sb = p.shape[2] // TOK_TILE
    row = lambda w: pl.BlockSpec((TOK_TILE, w), lambda i: (i, 0))
    return pl.pallas_call(
        _ple_kernel,
        grid=(t // TOK_TILE,),
        in_specs=[row(D_MODEL), pl.BlockSpec((N_GROUPS, TOK_TILE, D_MODEL), lambda i: (0, i, 0)),
                  pl.BlockSpec((None, None, TOK_TILE, PLE_DIM), lambda i: (layer, i // n_sb, i % n_sb, 0)),
                  _const_spec(ple_g.shape), _layer_weight(wpg, layer), _layer_weight(wple, layer)],
        out_specs=row(D_MODEL),
        out_shape=jax.ShapeDtypeStruct((t, D_MODEL), F32),
        scratch_shapes=[pltpu.VMEM(wpg.shape[1:], BF16), pltpu.VMEM(wple.shape[1:], BF16)],
        compiler_params=_params("arbitrary"),
        name="ple",
    )(x1, contrib, p, ple_g, wpg, wple)


def kernel(x, p, attn_norm_g, w_in, b_gate, q_norm_g, k_norm_g, lam_qk, subln_g, w_attn_up, ssm_lam_re, ssm_lam_im, ssm_log_step, ssm_b_re, ssm_b_im, ssm_c_re, ssm_c_im, ssm_d, w_glu, b_glu, w_ssm_up, w_out, ffn_norm_g, w_router_group, b_router_group, w_router_expert, b_router_expert, w_exp_gate, w_exp_up, w_exp_down, ple_norm_g, w_ple_gate, w_ple):
    bsz, seq, d = x.shape
    depth = w_in.shape[0]
    t = bsz * seq
    assert d == D_MODEL and seq % TOK_TILE == 0 and seq % ATT_TILE == 0
    assert seq % SCAN_T == 0 and bsz == SUBLANES
    tok = jnp.arange(TOK_TILE)
    ltri = (tok[None, :] < tok[:, None]).astype(BF16)

    a_re, a_im, b_tiles, c_re_tiles, c_im_tiles, scalars = _prep(
        ssm_lam_re, ssm_lam_im, ssm_log_step, ssm_b_re, ssm_b_im, ssm_c_re, ssm_c_im, lam_qk, q_norm_g, k_norm_g)
    x2d = x.reshape(t, d)
    for i in range(depth):
        lam_init = _lambda_init(i)
        tile_gain = lambda g: jnp.tile(g.reshape(HEAD_W, 1), (COL_CHUNK // HEAD_W, 1))
        qt, k, vt, u_tm, gates = _inproj(
            x2d, attn_norm_g[i].reshape(1, d), w_in,
            tile_gain(q_norm_g[i]), tile_gain(k_norm_g[i]), b_gate[i].reshape(1, -1), seq=seq, layer=i)

        o = _attention(scalars[i], qt, k, vt, subln_g[i].reshape(1, HEAD_W), seq=seq, lam_init=lam_init)

        z = _ssm(u_tm, b_tiles[i], c_re_tiles[i], c_im_tiles[i],
                 a_re[i, ::SSM_GROUP].reshape(1, N_STATE), a_im[i, ::SSM_GROUP].reshape(1, N_STATE),
                 ssm_d[i].reshape(1, SSM_WIDTH))

        w_r = jnp.concatenate(
            [w_router_group[i], jnp.transpose(w_router_expert[i], (1, 0, 2)).reshape(d, N_EXPERTS),
             jnp.zeros((d, ROUTE_W - N_GROUPS - N_EXPERTS), F32)], axis=1)
        w_r_hi = w_r.astype(BF16)
        w_r_lo = (w_r - w_r_hi.astype(F32)).astype(BF16)
        b_r = jnp.concatenate([b_router_group[i], b_router_expert[i].reshape(-1),
                               jnp.zeros((ROUTE_W - N_GROUPS - N_EXPERTS,), F32)]).reshape(1, ROUTE_W)
        x1, hn, route, route_t = _mix(
            x2d, o, z.reshape(t, SSM_WIDTH), gates,
            w_attn_up, w_glu, b_glu[i].reshape(1, -1), w_ssm_up, w_out, ffn_norm_g[i].reshape(1, d),
            w_r_hi, w_r_lo, b_r, ltri, layer=i)

        contrib = _moe(hn, route, route_t, w_exp_gate, w_exp_up, w_exp_down, layer=i)
        x2d = _ple(x1, contrib, p, ple_norm_g[i].reshape(1, d), w_ple_gate, w_ple, layer=i)
    return x2d.reshape(bsz, seq, d)
```

```python
import functools
import math

import jax
import jax.numpy as jnp
from jax import lax
from jax.experimental import pallas as pl
from jax.experimental.pallas import tpu as pltpu

F32 = jnp.float32
BF16 = jnp.bfloat16

D_MODEL = 1024
ATT_HEADS = 8
ATT_HEAD_DIM = 64
HEAD_W = 2 * ATT_HEAD_DIM
ATT_QK_WIDTH = ATT_HEADS * HEAD_W
ATT_V_WIDTH = ATT_HEADS * HEAD_W
SSM_WIDTH = 512
SSM_GROUP = 16
SSM_GROUPS = SSM_WIDTH // SSM_GROUP
SSM_STATE = 64
N_STATE = SSM_GROUPS * SSM_STATE
N_BRANCH = 2
IN_WIDTH = 2 * ATT_QK_WIDTH + ATT_V_WIDTH + SSM_WIDTH + N_BRANCH * D_MODEL
N_GROUPS = 4
EXPERTS_PER_GROUP = 8
N_EXPERTS = N_GROUPS * EXPERTS_PER_GROUP
EXPERT_FF = 256
PLE_DIM = 256
EPS = 1e-6

LANES = 128
SUBLANES = 8
MXU_N = 256
VMEM_LIMIT = 56 * 1024 * 1024

TOK_TILE = 512
MOE_CAP = 160
COL_CHUNK = 512
ATT_TILE = 256
Q_SCALE = ATT_HEAD_DIM ** -0.5 * math.log2(math.e)
LAM_ROW = 0
BOUND_ROW = 1
ATT_BOUND_SLACK = 1.01
ATT_BOUND_LIMIT = 60.0
SCAN_T = 64
SCAN_LANES = 1024
ROUTE_W = LANES
ROUTE_OFF = N_GROUPS
GRP_LANE = 0
RANK_LANE = 1


def _lambda_init(layer_idx):
    return 0.8 - 0.6 * math.exp(-0.3 * layer_idx)


def _dot(a, b):
    return jnp.dot(a, b, preferred_element_type=F32)


def _const_spec(shape):
    nd = len(shape)
    return pl.BlockSpec(shape, lambda *_: (0,) * nd, pipeline_mode=pl.Buffered(1))


def _layer_weight(stacked, layer):
    shape = stacked.shape[1:]
    return pl.BlockSpec((None,) + shape, lambda *_: (layer,) + (0,) * len(shape), pipeline_mode=pl.Buffered(1))


def _round_once(pairs):
    @pl.when(pl.program_id(0) == 0)
    def _():
        for w_ref, w_scr in pairs:
            w_scr[...] = w_ref[...].astype(BF16)


def _params(*sem):
    return pltpu.CompilerParams(dimension_semantics=sem, vmem_limit_bytes=VMEM_LIMIT)


def _prep_kernel(lre_ref, lim_ref, lstep_ref, bre_ref, bim_ref, cre_ref, cim_ref, lqk_ref, qg_ref, kg_ref,
                 are_ref, aim_ref, bt_ref, ctre_ref, ctim_ref, lam_ref):
    lr = lre_ref[...]
    li = lim_ref[...]
    dt = jnp.exp(lstep_ref[...])
    mag = jnp.exp(lr * dt)
    ab_re = mag * jnp.cos(li * dt)
    ab_im = mag * jnp.sin(li * dt)
    den = lr * lr + li * li
    nr = ab_re - 1.0
    coef_re = (nr * lr + ab_im * li) / den
    coef_im = (ab_im * lr - nr * li) / den
    br = bre_ref[...]
    bi = bim_ref[...]
    are_ref[...] = ab_re
    aim_ref[...] = ab_im
    bb_re = coef_re * br - coef_im * bi
    bb_im = coef_re * bi + coef_im * br

    def spread(x, reps):
        w = x.shape[1]
        src = lax.broadcasted_iota(jnp.int32, (w, w * reps), 0)
        dst = lax.broadcasted_iota(jnp.int32, (w, w * reps), 1) & (w - 1)
        return _dot(x.astype(BF16), jnp.where(src == dst, 1.0, 0.0).astype(BF16))

    def same_group(shape, row0, row_shift, col0, col_shift):
        rows = (lax.broadcasted_iota(jnp.int32, shape, 0) + row0) >> row_shift
        cols = (lax.broadcasted_iota(jnp.int32, shape, 1) + col0) >> col_shift
        return rows == cols

    n_re = N_STATE // MXU_N
    ch_shift = SSM_GROUP.bit_length() - 1
    st_shift = SSM_STATE.bit_length() - 1
    for part, bb in enumerate((bb_re, bb_im)):
        for j in range(n_re):
            r0 = LANES * ((j * MXU_N // SSM_STATE * SSM_GROUP) // LANES)
            x = spread(bb[r0:r0 + LANES, :], MXU_N // SSM_STATE)
            keep = same_group(x.shape, r0, ch_shift, j * MXU_N, st_shift)
            bt_ref[part * n_re + j] = jnp.where(keep, x, 0.0).astype(BF16)
    half = N_STATE // 2
    for c_ref, ct_ref in ((cre_ref, ctre_ref), (cim_ref, ctim_ref)):
        for n in range(2):
            y = spread(c_ref[n * half:(n + 1) * half, :], MXU_N // SSM_GROUP)
            keep = same_group(y.shape, n * half, st_shift, n * MXU_N, ch_shift)
            ct_ref[n] = jnp.where(keep, y, 0.0).astype(BF16)

    lq = lqk_ref[...]
    s01 = jnp.sum(lq[0:1] * lq[1:2], axis=-1, keepdims=True)
    s23 = jnp.sum(lq[2:3] * lq[3:4], axis=-1, keepdims=True)
    lam_dyn = jnp.exp(s01) - jnp.exp(s23)
    g_max = jnp.max(jnp.abs(qg_ref[...]), keepdims=True) * jnp.max(jnp.abs(kg_ref[...]), keepdims=True)
    bound = g_max * (ATT_HEAD_DIM * Q_SCALE * ATT_BOUND_SLACK)
    row = lax.broadcasted_iota(jnp.int32, lam_ref.shape, 0)
    lam_ref[...] = jnp.where(row == LAM_ROW, lam_dyn, bound)


def _prep(lam_re, lam_im, log_step, b_re, b_im, c_re, c_im, lam_qk, q_gain, k_gain):
    depth = lam_re.shape[0]
    rows = SSM_WIDTH
    rep = lambda a: jnp.repeat(a, SSM_GROUP, axis=1)
    lstep = jnp.broadcast_to(log_step[:, :, None], lam_re.shape)
    tr_b = lambda b: jnp.transpose(b, (0, 1, 3, 2)).reshape(depth, rows, SSM_STATE)
    tr_c = lambda c: jnp.transpose(c, (0, 1, 3, 2)).reshape(depth, N_STATE, SSM_GROUP)
    lay = lambda *shape: pl.BlockSpec((None,) + shape, lambda i: (i,) + (0,) * len(shape))
    blk = lay(rows, SSM_STATE)
    out = jax.ShapeDtypeStruct((depth, rows, SSM_STATE), F32)
    n_bt = 2 * N_STATE // MXU_N
    return pl.pallas_call(
        _prep_kernel,
        grid=(depth,),
        in_specs=[blk, blk, blk, blk, blk, lay(N_STATE, SSM_GROUP), lay(N_STATE, SSM_GROUP),
                  lay(4, ATT_HEAD_DIM), lay(2, ATT_HEAD_DIM), lay(2, ATT_HEAD_DIM)],
        out_specs=[blk, blk, lay(n_bt, LANES, MXU_N), lay(2, N_STATE // 2, MXU_N), lay(2, N_STATE // 2, MXU_N),
                   lay(SUBLANES, LANES)],
        out_shape=[out, out, jax.ShapeDtypeStruct((depth, n_bt, LANES, MXU_N), BF16),
                   jax.ShapeDtypeStruct((depth, 2, N_STATE // 2, MXU_N), BF16),
                   jax.ShapeDtypeStruct((depth, 2, N_STATE // 2, MXU_N), BF16),
                   jax.ShapeDtypeStruct((depth, SUBLANES, LANES), F32)],
        compiler_params=_params("arbitrary"),
        name="prep",
    )(rep(lam_re), rep(lam_im), rep(lstep), tr_b(b_re), tr_b(b_im), tr_c(c_re), tr_c(c_im), lam_qk, q_gain, k_gain)


def _inproj_kernel(x_ref, g_ref, w_ref, qg_ref, kg_ref, bg_ref,
                   qt_ref, k_ref, vt_ref, u_hbm, gate_ref, w_scr, u_buf, u_sem, *, n_sb):
    _round_once([(w_ref, w_scr)])
    step = pl.program_id(0)

    def u_copy(i):
        s0 = pl.multiple_of((i % n_sb) * TOK_TILE, TOK_TILE)
        return pltpu.make_async_copy(u_buf, u_hbm.at[pl.ds(s0, TOK_TILE), i // n_sb], u_sem)

    xf = x_ref[...]
    ms = jnp.mean(xf * xf, axis=-1, keepdims=True)
    h = (xf * lax.rsqrt(ms + EPS) * g_ref[...]).astype(BF16)

    def head_norm_t(y, gain_col):
        rows = y.shape[0]
        y3 = y.T.reshape(COL_CHUNK // ATT_HEAD_DIM, ATT_HEAD_DIM, rows)
        ms = jnp.sum(y3 * y3, axis=1, keepdims=True) * (1.0 / ATT_HEAD_DIM)
        return (y3 * lax.rsqrt(ms + EPS)).reshape(COL_CHUNK, rows) * gain_col

    n_qk = ATT_QK_WIDTH // COL_CHUNK
    n_v = ATT_V_WIDTH // COL_CHUNK
    for c in range(IN_WIDTH // COL_CHUNK):
        y = _dot(h, w_scr[:, c * COL_CHUNK:(c + 1) * COL_CHUNK])
        if c < n_qk:
            qt_ref[c * COL_CHUNK:(c + 1) * COL_CHUNK, :] = (head_norm_t(y, qg_ref[...]) * Q_SCALE).astype(BF16)
        elif c < 2 * n_qk:
            cc = c - n_qk
            k_ref[:, cc * COL_CHUNK:(cc + 1) * COL_CHUNK] = head_norm_t(y, kg_ref[...]).T.astype(BF16)
        elif c < 2 * n_qk + n_v:
            cc = c - 2 * n_qk
            vt_ref[cc * COL_CHUNK:(cc + 1) * COL_CHUNK, :] = y.T.astype(BF16)
        elif c == 2 * n_qk + n_v:
            @pl.when(step > 0)
            def _():
                u_copy(step - 1).wait()
            u_buf[...] = y
        else:
            cc = c - (2 * n_qk + n_v + 1)
            b = bg_ref[:, cc * COL_CHUNK:(cc + 1) * COL_CHUNK]
            gate_ref[:, cc * COL_CHUNK:(cc + 1) * COL_CHUNK] = jax.nn.sigmoid(y + b).astype(BF16)

    u_copy(step).start()

    @pl.when(step == pl.num_programs(0) - 1)
    def _():
        u_copy(step).wait()


def _inproj(x2d, norm_g, w_in, qg, kg, b_gate, *, seq, layer):
    t = x2d.shape[0]
    n_sb = seq // TOK_TILE
    bsz = t // seq
    row = lambda w: pl.BlockSpec((TOK_TILE, w), lambda i: (i, 0))
    col = lambda w: pl.BlockSpec((w, TOK_TILE), lambda i: (i // n_sb, i % n_sb))
    return pl.pallas_call(
        functools.partial(_inproj_kernel, n_sb=n_sb),
        grid=(t // TOK_TILE,),
        in_specs=[row(D_MODEL), _const_spec((1, D_MODEL)), _layer_weight(w_in, layer),
                  _const_spec((COL_CHUNK, 1)), _const_spec((COL_CHUNK, 1)),
                  _const_spec((1, N_BRANCH * D_MODEL))],
        out_specs=[col(ATT_QK_WIDTH), row(ATT_QK_WIDTH), col(ATT_V_WIDTH),
                   pl.BlockSpec(memory_space=pl.ANY),
                   row(N_BRANCH * D_MODEL)],
        out_shape=[jax.ShapeDtypeStruct((bsz * ATT_QK_WIDTH, seq), BF16),
                   jax.ShapeDtypeStruct((t, ATT_QK_WIDTH), BF16),
                   jax.ShapeDtypeStruct((bsz * ATT_V_WIDTH, seq), BF16),
                   jax.ShapeDtypeStruct((seq, bsz, SSM_WIDTH), F32),
                   jax.ShapeDtypeStruct((t, N_BRANCH * D_MODEL), BF16)],
        scratch_shapes=[pltpu.VMEM((D_MODEL, IN_WIDTH), BF16), pltpu.VMEM((TOK_TILE, SSM_WIDTH), F32),
                        pltpu.SemaphoreType.DMA(())],
        compiler_params=_params("arbitrary"),
        name="inproj",
    )(x2d, norm_g, w_in, qg, kg, b_gate)


def _attn_kernel(sc_ref, qt_ref, k_ref, vt_ref, g_ref, o_ref, *, lam_init):
    tq = ATT_TILE
    seq = k_ref.shape[0]
    nq = seq // tq
    lam = sc_ref[LAM_ROW, 0] + lam_init
    bound = sc_ref[BOUND_ROW, 0]
    head_row = lax.broadcasted_iota(jnp.int32, (HEAD_W, tq), 0)
    key_pos = lax.broadcasted_iota(jnp.int32, (tq, 2 * tq), 0)
    qry_pos = lax.broadcasted_iota(jnp.int32, (tq, 2 * tq), 1) & (tq - 1)
    causal = key_pos <= qry_pos
    zero = jnp.zeros((HEAD_W, tq), BF16)

    def tile(qi, shift_by_bound):
        qt = qt_ref[:, qi * tq:(qi + 1) * tq]
        qst = jnp.concatenate([jnp.where(head_row < ATT_HEAD_DIM, qt, zero),
                               jnp.where(head_row >= ATT_HEAD_DIM, qt, zero)], axis=1)
        nk = qi * tq
        s_d = jnp.where(causal, _dot(k_ref[nk:nk + tq, :], qst), -jnp.inf)
        if qi > 0:
            s_m = _dot(k_ref[0:nk, :], qst)
        if shift_by_bound:
            m = bound
        else:
            m = jnp.max(s_d, axis=0, keepdims=True)
            if qi > 0:
                m = jnp.maximum(m, jnp.max(s_m, axis=0, keepdims=True))
        p_d = jnp.exp2(s_d - m)
        l = jnp.sum(p_d, axis=0, keepdims=True)
        acc = _dot(vt_ref[:, nk:nk + tq], p_d.astype(BF16))
        if qi > 0:
            p_m = jnp.exp2(s_m - m)
            l = l + jnp.sum(p_m, axis=0, keepdims=True)
            acc = acc + _dot(vt_ref[:, 0:nk], p_m.astype(BF16))
        accn = acc * (1.0 / l)
        ot = accn[:, :tq] - lam * accn[:, tq:]
        ms = jnp.mean(ot * ot, axis=0, keepdims=True)
        o = (ot * lax.rsqrt(ms + EPS)).T * g_ref[...] * (1.0 - lam_init)
        o_ref[qi * tq:(qi + 1) * tq, :] = o.astype(BF16)

    @pl.when(bound < ATT_BOUND_LIMIT)
    def _():
        for qi in range(nq):
            tile(qi, True)

    @pl.when(jnp.logical_not(bound < ATT_BOUND_LIMIT))
    def _():
        for qi in range(nq):
            tile(qi, False)


def _attention(scalars, qt, k, vt, subln_g, *, seq, lam_init):
    t = k.shape[0]
    bsz = t // seq
    fm = pl.BlockSpec((HEAD_W, seq), lambda b, h: (b * ATT_HEADS + h, 0))
    tm = pl.BlockSpec((seq, HEAD_W), lambda b, h: (b, h))
    return pl.pallas_call(
        functools.partial(_attn_kernel, lam_init=lam_init),
        grid=(bsz, ATT_HEADS),
        in_specs=[pl.BlockSpec(memory_space=pltpu.SMEM), fm, tm, fm, _const_spec((1, HEAD_W))],
        out_specs=tm,
        out_shape=jax.ShapeDtypeStruct((t, ATT_V_WIDTH), BF16),
        compiler_params=_params("parallel", "parallel"),
        name="attn",
    )(scalars, qt, k, vt, subln_g)


def _ssm_kernel(u_ref, bt_ref, cre_ref, cim_ref, are_ref, aim_ref, d_ref, z_hbm, st_scr, state_scr, z_buf, z_sem):
    n_t, bsz, _ = u_ref.shape
    rows = n_t * bsz
    chunk = pl.program_id(0)

    def z_copies(i):
        t0 = pl.multiple_of(i * n_t, n_t)
        return [pltpu.make_async_copy(z_buf.at[:, b], z_hbm.at[b, pl.ds(t0, n_t)], z_sem) for b in range(bsz)]

    @pl.when(chunk == 0)
    def _():
        state_scr[...] = jnp.zeros(state_scr.shape, F32)

    u = u_ref[...].reshape(rows, SSM_WIDTH)
    ub = u.astype(BF16)
    n_tiles = 2 * N_STATE // MXU_N
    for j in range(n_tiles):
        k0 = LANES * (((j % (n_tiles // 2)) * MXU_N // SSM_STATE * SSM_GROUP) // LANES)
        st_scr[:, j * MXU_N:(j + 1) * MXU_N] = _dot(ub[:, k0:k0 + LANES], bt_ref[j])

    for cc in range(N_STATE // SCAN_LANES):
        lo = cc * SCAN_LANES
        hi = N_STATE + lo
        a_r = jnp.broadcast_to(are_ref[:, lo:lo + SCAN_LANES], (bsz, SCAN_LANES))
        a_i = jnp.broadcast_to(aim_ref[:, lo:lo + SCAN_LANES], (bsz, SCAN_LANES))

        def step(t, carry, lo=lo, hi=hi, a_r=a_r, a_i=a_i):
            s_r, s_i = carry
            r0 = pl.multiple_of(t * bsz, bsz)
            n_r = a_r * s_r - a_i * s_i + st_scr[pl.ds(r0, bsz), lo:lo + SCAN_LANES]
            n_i = a_r * s_i + a_i * s_r + st_scr[pl.ds(r0, bsz), hi:hi + SCAN_LANES]
            st_scr[pl.ds(r0, bsz), lo:lo + SCAN_LANES] = n_r
            st_scr[pl.ds(r0, bsz), hi:hi + SCAN_LANES] = n_i
            return n_r, n_i

        s_r, s_i = lax.fori_loop(0, rows // bsz, step,
                                 (state_scr[:, lo:lo + SCAN_LANES], state_scr[:, hi:hi + SCAN_LANES]),
                                 unroll=8)
        state_scr[:, lo:lo + SCAN_LANES] = s_r
        state_scr[:, hi:hi + SCAN_LANES] = s_i

    @pl.when(chunk > 0)
    def _():
        for cp in z_copies(chunk - 1):
            cp.wait()
    half = N_STATE // 2
    for n in range(2):
        s_re = st_scr[:, n * half:(n + 1) * half].astype(BF16)
        s_im = st_scr[:, N_STATE + n * half:N_STATE + (n + 1) * half].astype(BF16)
        cols = slice(n * MXU_N, (n + 1) * MXU_N)
        y = _dot(s_re, cre_ref[n]) - _dot(s_im, cim_ref[n]) + d_ref[:, cols] * u[:, cols]
        z_buf[:, :, cols] = jax.nn.gelu(y).reshape(n_t, bsz, MXU_N)
    for cp in z_copies(chunk):
        cp.start()

    @pl.when(chunk == pl.num_programs(0) - 1)
    def _():
        for cp in z_copies(chunk):
            cp.wait()


def _ssm(u_tm, b_tiles, c_re_tiles, c_im_tiles, a_re, a_im, d_skip):
    seq, bsz, _ = u_tm.shape
    return pl.pallas_call(
        _ssm_kernel,
        grid=(seq // SCAN_T,),
        in_specs=[pl.BlockSpec((SCAN_T, bsz, SSM_WIDTH), lambda i: (i, 0, 0)),
                  _const_spec(b_tiles.shape), _const_spec(c_re_tiles.shape),
                  _const_spec(c_im_tiles.shape), _const_spec((1, N_STATE)), _const_spec((1, N_STATE)),
                  _const_spec((1, SSM_WIDTH))],
        out_specs=pl.BlockSpec(memory_space=pl.ANY),
        out_shape=jax.ShapeDtypeStruct((bsz, seq, SSM_WIDTH), F32),
        scratch_shapes=[pltpu.VMEM((SCAN_T * bsz, 2 * N_STATE), F32), pltpu.VMEM((bsz, 2 * N_STATE), F32),
                        pltpu.VMEM((SCAN_T, bsz, SSM_WIDTH), F32), pltpu.SemaphoreType.DMA(())],
        compiler_params=_params("arbitrary"),
        name="ssm",
    )(u_tm, b_tiles, c_re_tiles, c_im_tiles, a_re, a_im, d_skip)


def _route(logits, ltri):
    lane = lax.broadcasted_iota(jnp.int32, logits.shape, 1)
    lanef = lane.astype(F32)
    big = float(ROUTE_W)
    neg = -jnp.inf
    gl = jnp.where(lane < N_GROUPS, logits, neg)
    gmax = jnp.max(gl, axis=-1, keepdims=True)
    grp = jnp.min(jnp.where(gl == gmax, lanef, big), axis=-1, keepdims=True)
    p_grp = 1.0 / jnp.sum(jnp.exp(gl - gmax), axis=-1, keepdims=True)
    first = ROUTE_OFF + grp * EXPERTS_PER_GROUP
    in_grp = (lanef >= first) & (lanef < first + EXPERTS_PER_GROUP)
    el = jnp.where(in_grp, logits, neg)
    v1 = jnp.max(el, axis=-1, keepdims=True)
    i1 = jnp.min(jnp.where(el == v1, lanef, big), axis=-1, keepdims=True)
    el2 = jnp.where(lanef == i1, neg, el)
    v2 = jnp.max(el2, axis=-1, keepdims=True)
    i2 = jnp.min(jnp.where(el2 == v2, lanef, big), axis=-1, keepdims=True)
    e21 = jnp.exp(v2 - v1)
    w1 = p_grp / (1.0 + e21)
    w2 = p_grp * (e21 / (1.0 + e21))
    member = lanef == grp
    before = _dot(ltri, jnp.where(member, 1.0, 0.0).astype(BF16))
    rank = jnp.sum(jnp.where(member, before, 0.0), axis=-1, keepdims=True)
    return (jnp.where(lanef == i1, w1, 0.0) + jnp.where(lanef == i2, w2, 0.0)
            + jnp.where(lane == GRP_LANE, grp, 0.0) + jnp.where(lane == RANK_LANE, rank, 0.0))


def _mix_kernel(x_ref, o_ref, z_ref, gate_ref, wau_ref, wglu_ref, bglu_ref, wsu_ref, wout_ref,
                fg_ref, wrh_ref, wrl_ref, br_ref, ltri_ref, x1_ref, hn_ref, route_ref, route_t_ref,
                wau_scr, wglu_scr, wsu_scr, wout_scr):
    _round_once([(wau_ref, wau_scr), (wglu_ref, wglu_scr), (wsu_ref, wsu_scr), (wout_ref, wout_scr)])
    y_att = _dot(o_ref[...], wau_scr[...])
    z = z_ref[...]
    z = z * jax.nn.sigmoid(_dot(z.astype(BF16), wglu_scr[...]) + bglu_ref[...])
    y_ssm = _dot(z.astype(BF16), wsu_scr[...])
    mixed = (gate_ref[:, :D_MODEL].astype(F32) * y_att + gate_ref[:, D_MODEL:].astype(F32) * y_ssm)
    x1 = x_ref[...] + _dot(mixed.astype(BF16), wout_scr[...])
    x1_ref[...] = x1
    ms = jnp.mean(x1 * x1, axis=-1, keepdims=True)
    hn = x1 * lax.rsqrt(ms + EPS) * fg_ref[...]
    hn_hi = hn.astype(BF16)
    hn_ref[...] = hn_hi
    hn_lo = (hn - hn_hi.astype(F32)).astype(BF16)
    logits = (_dot(hn_hi, wrh_ref[...]) + _dot(hn_lo, wrh_ref[...]) + _dot(hn_hi, wrl_ref[...])
              + br_ref[...])
    route = _route(logits, ltri_ref[...])
    route_ref[...] = route
    route_t_ref[...] = route.T[:SUBLANES, :]


def _mix(x2d, o, z, gates, wau, wglu, bglu, wsu, wout, ffn_g, wr_hi, wr_lo, br, ltri, *, layer):
    t = x2d.shape[0]
    row = lambda w: pl.BlockSpec((TOK_TILE, w), lambda i: (i, 0))
    return pl.pallas_call(
        _mix_kernel,
        grid=(t // TOK_TILE,),
        in_specs=[row(D_MODEL), row(ATT_V_WIDTH), row(SSM_WIDTH), row(N_BRANCH * D_MODEL),
                  _layer_weight(wau, layer), _layer_weight(wglu, layer), _const_spec(bglu.shape),
                  _layer_weight(wsu, layer), _layer_weight(wout, layer), _const_spec(ffn_g.shape),
                  _const_spec(wr_hi.shape), _const_spec(wr_lo.shape), _const_spec(br.shape),
                  _const_spec(ltri.shape)],
        out_specs=[row(D_MODEL), row(D_MODEL), row(ROUTE_W),
                   pl.BlockSpec((SUBLANES, TOK_TILE), lambda i: (0, i))],
        out_shape=[jax.ShapeDtypeStruct((t, D_MODEL), F32),
                   jax.ShapeDtypeStruct((t, D_MODEL), BF16),
                   jax.ShapeDtypeStruct((t, ROUTE_W), F32),
                   jax.ShapeDtypeStruct((SUBLANES, t), F32)],
        scratch_shapes=[pltpu.VMEM(w.shape[1:], BF16) for w in (wau, wglu, wsu, wout)],
        compiler_params=_params("arbitrary"),
        name="mix",
    )(x2d, o, z, gates, wau, wglu, bglu, wsu, wout, ffn_g, wr_hi, wr_lo, br, ltri)


def _moe_kernel(hn_ref, route_ref, route_t_ref, wg_ref, wu_ref, wd_ref, out_ref, wg_scr, wu_scr, wd_scr):
    n = pl.program_id(0)

    @pl.when(pl.program_id(1) == 0)
    def _():
        for j in range(EXPERTS_PER_GROUP):
            cols = slice(j * EXPERT_FF, (j + 1) * EXPERT_FF)
            wg_scr[:, cols] = wg_ref[j].astype(BF16)
            wu_scr[:, cols] = wu_ref[j].astype(BF16)
            wd_scr[cols, :] = wd_ref[j].astype(BF16)

    tm = hn_ref.shape[0]
    nf = n.astype(F32)
    route = route_ref[...]
    lane = lax.broadcasted_iota(jnp.int32, route.shape, 1)
    grp_c = jnp.sum(jnp.where(lane == GRP_LANE, route, 0.0), axis=-1, keepdims=True)
    rank_c = jnp.sum(jnp.where(lane == RANK_LANE, route, 0.0), axis=-1, keepdims=True)
    grp_r = route_t_ref[GRP_LANE:GRP_LANE + 1, :]
    rank_r = route_t_ref[RANK_LANE:RANK_LANE + 1, :]
    r1 = route.astype(BF16)
    r2 = (route - r1.astype(F32)).astype(BF16)
    r3 = (route - r1.astype(F32) - r2.astype(F32)).astype(BF16)
    pieces = jnp.concatenate([r1, r2, r3], axis=1)

    def group_rows(base):
        slot_r = lax.broadcasted_iota(jnp.int32, (MOE_CAP, tm), 0).astype(F32) + base
        take = jnp.where((grp_r == nf) & (rank_r == slot_r), 1.0, 0.0).astype(BF16)
        slot_c = lax.broadcasted_iota(jnp.int32, (tm, MOE_CAP), 1).astype(F32) + base
        put = jnp.where((grp_c == nf) & (rank_c == slot_c), 1.0, 0.0).astype(BF16)
        xc = _dot(take, hn_ref[...]).astype(BF16)
        rc = _dot(take, pieces)
        rc = rc[:, :ROUTE_W] + rc[:, ROUTE_W:2 * ROUTE_W] + rc[:, 2 * ROUTE_W:]
        lane_c = lax.broadcasted_iota(jnp.int32, rc.shape, 1)
        first = ROUTE_OFF + n * EXPERTS_PER_GROUP
        act = jax.nn.silu(_dot(xc, wg_scr[...])) * _dot(xc, wu_scr[...])
        blocks = []
        for j in range(EXPERTS_PER_GROUP):
            w_j = jnp.sum(jnp.where(lane_c == first + j, rc, 0.0), axis=-1, keepdims=True)
            blocks.append((act[:, j * EXPERT_FF:(j + 1) * EXPERT_FF] * w_j).astype(BF16))
        y = _dot(jnp.concatenate(blocks, axis=1), wd_scr[...])
        return _dot(put, y.astype(BF16))

    out_ref[...] = group_rows(jnp.float32(0.0)).astype(BF16)

    n_rows = jnp.max(jnp.where(grp_c == nf, rank_c + 1.0, 0.0))
    n_pass = lax.div(n_rows.astype(jnp.int32) + (MOE_CAP - 1), MOE_CAP)

    def extra_pass(k, carry):
        more = group_rows((k * MOE_CAP).astype(F32))
        out_ref[...] = (out_ref[...].astype(F32) + more).astype(BF16)
        return carry

    lax.fori_loop(1, n_pass, extra_pass, 0)


def _moe(hn, route, route_t, w_gate, w_up, w_down, *, layer):
    t = hn.shape[0]
    row = lambda w: pl.BlockSpec((TOK_TILE, w), lambda n, i: (i, 0))
    grp_w = lambda shape: pl.BlockSpec((None, None) + shape, lambda n, i: (layer, n, 0, 0, 0),
                                       pipeline_mode=pl.Buffered(1))
    up_shape = (EXPERTS_PER_GROUP, D_MODEL, EXPERT_FF)
    down_shape = (EXPERTS_PER_GROUP, EXPERT_FF, D_MODEL)
    ff = EXPERTS_PER_GROUP * EXPERT_FF
    return pl.pallas_call(
        _moe_kernel,
        grid=(N_GROUPS, t // TOK_TILE),
        in_specs=[row(D_MODEL), row(ROUTE_W), pl.BlockSpec((SUBLANES, TOK_TILE), lambda n, i: (0, i)),
                  grp_w(up_shape), grp_w(up_shape), grp_w(down_shape)],
        out_specs=pl.BlockSpec((None, TOK_TILE, D_MODEL), lambda n, i: (n, i, 0)),
        out_shape=jax.ShapeDtypeStruct((N_GROUPS, t, D_MODEL), BF16),
        scratch_shapes=[pltpu.VMEM((D_MODEL, ff), BF16), pltpu.VMEM((D_MODEL, ff), BF16),
                        pltpu.VMEM((ff, D_MODEL), BF16)],
        compiler_params=_params("arbitrary", "arbitrary"),
        name="moe",
    )(hn, route, route_t, w_gate, w_up, w_down)


def _ple_kernel(x1_ref, c_ref, p_ref, pg_ref, wpg_ref, wple_ref, out_ref, wpg_scr, wple_scr):
    _round_once([(wpg_ref, wpg_scr), (wple_ref, wple_scr)])
    x2 = x1_ref[...]
    for n in range(N_GROUPS):
        x2 = x2 + c_ref[n].astype(F32)
    ms = jnp.mean(x2 * x2, axis=-1, keepdims=True)
    hp = (x2 * lax.rsqrt(ms + EPS) * pg_ref[...]).astype(BF16)
    gate = jax.nn.sigmoid(_dot(hp, wpg_scr[...]))
    out_ref[...] = x2 + gate * _dot(p_ref[...].astype(BF16), wple_scr[...])


def _ple(x1, contrib, p, ple_g, wpg, wple, *, layer):
    t = x1.shape[0]
    n_sb = p.shape[2] // TOK_TILE
    row = lambda w: pl.BlockSpec((TOK_TILE, w), lambda i: (i, 0))
    return pl.pallas_call(
        _ple_kernel,
        grid=(t // TOK_TILE,),
        in_specs=[row(D_MODEL), pl.BlockSpec((N_GROUPS, TOK_TILE, D_MODEL), lambda i: (0, i, 0)),
                  pl.BlockSpec((None, None, TOK_TILE, PLE_DIM), lambda i: (layer, i // n_sb, i % n_sb, 0)),
                  _const_spec(ple_g.shape), _layer_weight(wpg, layer), _layer_weight(wple, layer)],
        out_specs=row(D_MODEL),
        out_shape=jax.ShapeDtypeStruct((t, D_MODEL), F32),
        scratch_shapes=[pltpu.VMEM(wpg.shape[1:], BF16), pltpu.VMEM(wple.shape[1:], BF16)],
        compiler_params=_params("arbitrary"),
        name="ple",
    )(x1, contrib, p, ple_g, wpg, wple)


def kernel(x, p, attn_norm_g, w_in, b_gate, q_norm_g, k_norm_g, lam_qk, subln_g, w_attn_up, ssm_lam_re, ssm_lam_im, ssm_log_step, ssm_b_re, ssm_b_im, ssm_c_re, ssm_c_im, ssm_d, w_glu, b_glu, w_ssm_up, w_out, ffn_norm_g, w_router_group, b_router_group, w_router_expert, b_router_expert, w_exp_gate, w_exp_up, w_exp_down, ple_norm_g, w_ple_gate, w_ple):
    bsz, seq, d = x.shape
    depth = w_in.shape[0]
    t = bsz * seq
    assert d == D_MODEL and seq % TOK_TILE == 0 and seq % ATT_TILE == 0
    assert seq % SCAN_T == 0 and bsz == SUBLANES
    tok = jnp.arange(TOK_TILE)
    ltri = (tok[None, :] < tok[:, None]).astype(BF16)

    a_re, a_im, b_tiles, c_re_tiles, c_im_tiles, scalars = _prep(
        ssm_lam_re, ssm_lam_im, ssm_log_step, ssm_b_re, ssm_b_im, ssm_c_re, ssm_c_im, lam_qk, q_norm_g, k_norm_g)
    x2d = x.reshape(t, d)
    for i in range(depth):
        lam_init = _lambda_init(i)
        tile_gain = lambda g: jnp.tile(g.reshape(HEAD_W, 1), (COL_CHUNK // HEAD_W, 1))
        qt, k, vt, u_tm, gates = _inproj(
            x2d, attn_norm_g[i].reshape(1, d), w_in,
            tile_gain(q_norm_g[i]), tile_gain(k_norm_g[i]), b_gate[i].reshape(1, -1), seq=seq, layer=i)

        o = _attention(scalars[i], qt, k, vt, subln_g[i].reshape(1, HEAD_W), seq=seq, lam_init=lam_init)

        z = _ssm(u_tm, b_tiles[i], c_re_tiles[i], c_im_tiles[i],
                 a_re[i, ::SSM_GROUP].reshape(1, N_STATE), a_im[i, ::SSM_GROUP].reshape(1, N_STATE),
                 ssm_d[i].reshape(1, SSM_WIDTH))

        w_r = jnp.concatenate(
            [w_router_group[i], jnp.transpose(w_router_expert[i], (1, 0, 2)).reshape(d, N_EXPERTS),
             jnp.zeros((d, ROUTE_W - N_GROUPS - N_EXPERTS), F32)], axis=1)
        w_r_hi = w_r.astype(BF16)
        w_r_lo = (w_r - w_r_hi.astype(F32)).astype(BF16)
        b_r = jnp.concatenate([b_router_group[i], b_router_expert[i].reshape(-1),
                               jnp.zeros((ROUTE_W - N_GROUPS - N_EXPERTS,), F32)]).reshape(1, ROUTE_W)
        x1, hn, route, route_t = _mix(
            x2d, o, z.reshape(t, SSM_WIDTH), gates,
            w_attn_up, w_glu, b_glu[i].reshape(1, -1), w_ssm_up, w_out, ffn_norm_g[i].reshape(1, d),
            w_r_hi, w_r_lo, b_r, ltri, layer=i)

        contrib = _moe(hn, route, route_t, w_exp_gate, w_exp_up, w_exp_down, layer=i)
        x2d = _ple(x1, contrib, p, ple_norm_g[i].reshape(1, d), w_ple_gate, w_ple, layer=i)
    return x2d.reshape(bsz, seq, d)
```

```python
import functools
import math

import jax
import jax.numpy as jnp
from jax import lax
from jax.experimental import pallas as pl
from jax.experimental.pallas import tpu as pltpu

F32 = jnp.float32
BF16 = jnp.bfloat16

D_MODEL = 1024
ATT_HEADS = 8
ATT_HEAD_DIM = 64
HEAD_W = 2 * ATT_HEAD_DIM
ATT_QK_WIDTH = ATT_HEADS * HEAD_W
ATT_V_WIDTH = ATT_HEADS * HEAD_W
SSM_WIDTH = 512
SSM_GROUP = 16
SSM_GROUPS = SSM_WIDTH // SSM_GROUP
SSM_STATE = 64
N_STATE = SSM_GROUPS * SSM_STATE
N_BRANCH = 2
IN_WIDTH = 2 * ATT_QK_WIDTH + ATT_V_WIDTH + SSM_WIDTH + N_BRANCH * D_MODEL
N_GROUPS = 4
EXPERTS_PER_GROUP = 8
N_EXPERTS = N_GROUPS * EXPERTS_PER_GROUP
EXPERT_FF = 256
PLE_DIM = 256
EPS = 1e-6

LANES = 128
SUBLANES = 8
MXU_N = 256
VMEM_LIMIT = 56 * 1024 * 1024

TOK_TILE = 512
MOE_CAP = 160
MOE_SUB = 2
COL_CHUNK = 512
ATT_TILE = 256
Q_SCALE = ATT_HEAD_DIM ** -0.5 * math.log2(math.e)
LAM_ROW = 0
BOUND_ROW = 1
ATT_BOUND_SLACK = 1.01
ATT_BOUND_LIMIT = 60.0
SCAN_T = 64
SCAN_LANES = 1024
ROUTE_W = LANES
ROUTE_OFF = N_GROUPS
GRP_LANE = 0
RANK_LANE = 1


def _lambda_init(layer_idx):
    return 0.8 - 0.6 * math.exp(-0.3 * layer_idx)


def _dot(a, b):
    return jnp.dot(a, b, preferred_element_type=F32)


def _const_spec(shape):
    nd = len(shape)
    return pl.BlockSpec(shape, lambda *_: (0,) * nd, pipeline_mode=pl.Buffered(1))


def _layer_weight(stacked, layer):
    shape = stacked.shape[1:]
    return pl.BlockSpec((None,) + shape, lambda *_: (layer,) + (0,) * len(shape), pipeline_mode=pl.Buffered(1))


def _round_once(pairs):
    @pl.when(pl.program_id(0) == 0)
    def _():
        for w_ref, w_scr in pairs:
            w_scr[...] = w_ref[...].astype(BF16)


def _params(*sem):
    return pltpu.CompilerParams(dimension_semantics=sem, vmem_limit_bytes=VMEM_LIMIT)


def _prep_kernel(lre_ref, lim_ref, lstep_ref, bre_ref, bim_ref, cre_ref, cim_ref, lqk_ref, qg_ref, kg_ref,
                 are_ref, aim_ref, bt_ref, ctre_ref, ctim_ref, lam_ref):
    lr = lre_ref[...]
    li = lim_ref[...]
    dt = jnp.exp(lstep_ref[...])
    mag = jnp.exp(lr * dt)
    ab_re = mag * jnp.cos(li * dt)
    ab_im = mag * jnp.sin(li * dt)
    den = lr * lr + li * li
    nr = ab_re - 1.0
    coef_re = (nr * lr + ab_im * li) / den
    coef_im = (ab_im * lr - nr * li) / den
    br = bre_ref[...]
    bi = bim_ref[...]
    are_ref[...] = ab_re
    aim_ref[...] = ab_im
    bb_re = coef_re * br - coef_im * bi
    bb_im = coef_re * bi + coef_im * br

    def spread(x, reps):
        w = x.shape[1]
        src = lax.broadcasted_iota(jnp.int32, (w, w * reps), 0)
        dst = lax.broadcasted_iota(jnp.int32, (w, w * reps), 1) & (w - 1)
        return _dot(x.astype(BF16), jnp.where(src == dst, 1.0, 0.0).astype(BF16))

    def same_group(shape, row0, row_shift, col0, col_shift):
        rows = (lax.broadcasted_iota(jnp.int32, shape, 0) + row0) >> row_shift
        cols = (lax.broadcasted_iota(jnp.int32, shape, 1) + col0) >> col_shift
        return rows == cols

    n_re = N_STATE // MXU_N
    ch_shift = SSM_GROUP.bit_length() - 1
    st_shift = SSM_STATE.bit_length() - 1
    for part, bb in enumerate((bb_re, bb_im)):
        for j in range(n_re):
            r0 = LANES * ((j * MXU_N // SSM_STATE * SSM_GROUP) // LANES)
            x = spread(bb[r0:r0 + LANES, :], MXU_N // SSM_STATE)
            keep = same_group(x.shape, r0, ch_shift, j * MXU_N, st_shift)
            bt_ref[part * n_re + j] = jnp.where(keep, x, 0.0).astype(BF16)
    half = N_STATE // 2
    for c_ref, ct_ref in ((cre_ref, ctre_ref), (cim_ref, ctim_ref)):
        for n in range(2):
            y = spread(c_ref[n * half:(n + 1) * half, :], MXU_N // SSM_GROUP)
            keep = same_group(y.shape, n * half, st_shift, n * MXU_N, ch_shift)
            ct_ref[n] = jnp.where(keep, y, 0.0).astype(BF16)

    lq = lqk_ref[...]
    s01 = jnp.sum(lq[0:1] * lq[1:2], axis=-1, keepdims=True)
    s23 = jnp.sum(lq[2:3] * lq[3:4], axis=-1, keepdims=True)
    lam_dyn = jnp.exp(s01) - jnp.exp(s23)
    g_max = jnp.max(jnp.abs(qg_ref[...]), keepdims=True) * jnp.max(jnp.abs(kg_ref[...]), keepdims=True)
    bound = g_max * (ATT_HEAD_DIM * Q_SCALE * ATT_BOUND_SLACK)
    row = lax.broadcasted_iota(jnp.int32, lam_ref.shape, 0)
    lam_ref[...] = jnp.where(row == LAM_ROW, lam_dyn, bound)


def _prep(lam_re, lam_im, log_step, b_re, b_im, c_re, c_im, lam_qk, q_gain, k_gain):
    depth = lam_re.shape[0]
    rows = SSM_WIDTH
    rep = lambda a: jnp.repeat(a, SSM_GROUP, axis=1)
    lstep = jnp.broadcast_to(log_step[:, :, None], lam_re.shape)
    tr_b = lambda b: jnp.transpose(b, (0, 1, 3, 2)).reshape(depth, rows, SSM_STATE)
    tr_c = lambda c: jnp.transpose(c, (0, 1, 3, 2)).reshape(depth, N_STATE, SSM_GROUP)
    lay = lambda *shape: pl.BlockSpec((None,) + shape, lambda i: (i,) + (0,) * len(shape))
    blk = lay(rows, SSM_STATE)
    out = jax.ShapeDtypeStruct((depth, rows, SSM_STATE), F32)
    n_bt = 2 * N_STATE // MXU_N
    return pl.pallas_call(
        _prep_kernel,
        grid=(depth,),
        in_specs=[blk, blk, blk, blk, blk, lay(N_STATE, SSM_GROUP), lay(N_STATE, SSM_GROUP),
                  lay(4, ATT_HEAD_DIM), lay(2, ATT_HEAD_DIM), lay(2, ATT_HEAD_DIM)],
        out_specs=[blk, blk, lay(n_bt, LANES, MXU_N), lay(2, N_STATE // 2, MXU_N), lay(2, N_STATE // 2, MXU_N),
                   lay(SUBLANES, LANES)],
        out_shape=[out, out, jax.ShapeDtypeStruct((depth, n_bt, LANES, MXU_N), BF16),
                   jax.ShapeDtypeStruct((depth, 2, N_STATE // 2, MXU_N), BF16),
                   jax.ShapeDtypeStruct((depth, 2, N_STATE // 2, MXU_N), BF16),
                   jax.ShapeDtypeStruct((depth, SUBLANES, LANES), F32)],
        compiler_params=_params("arbitrary"),
        name="prep",
    )(rep(lam_re), rep(lam_im), rep(lstep), tr_b(b_re), tr_b(b_im), tr_c(c_re), tr_c(c_im), lam_qk, q_gain, k_gain)


def _inproj_kernel(x_ref, g_ref, w_ref, qg_ref, kg_ref, bg_ref,
                   qt_ref, k_ref, vt_ref, u_hbm, gate_ref, w_scr, u_buf, u_sem, *, n_sb):
    _round_once([(w_ref, w_scr)])
    step = pl.program_id(0)

    def u_copy(i):
        s0 = pl.multiple_of((i % n_sb) * TOK_TILE, TOK_TILE)
        return pltpu.make_async_copy(u_buf, u_hbm.at[pl.ds(s0, TOK_TILE), i // n_sb], u_sem)

    xf = x_ref[...]
    ms = jnp.mean(xf * xf, axis=-1, keepdims=True)
    h = (xf * lax.rsqrt(ms + EPS) * g_ref[...]).astype(BF16)

    def head_norm_t(y, gain_col):
        rows = y.shape[0]
        y3 = y.T.reshape(COL_CHUNK // ATT_HEAD_DIM, ATT_HEAD_DIM, rows)
        ms = jnp.sum(y3 * y3, axis=1, keepdims=True) * (1.0 / ATT_HEAD_DIM)
        return (y3 * lax.rsqrt(ms + EPS)).reshape(COL_CHUNK, rows) * gain_col

    n_qk = ATT_QK_WIDTH // COL_CHUNK
    n_v = ATT_V_WIDTH // COL_CHUNK
    for c in range(IN_WIDTH // COL_CHUNK):
        y = _dot(h, w_scr[:, c * COL_CHUNK:(c + 1) * COL_CHUNK])
        if c < n_qk:
            qt_ref[c * COL_CHUNK:(c + 1) * COL_CHUNK, :] = (head_norm_t(y, qg_ref[...]) * Q_SCALE).astype(BF16)
        elif c < 2 * n_qk:
            cc = c - n_qk
            k_ref[:, cc * COL_CHUNK:(cc + 1) * COL_CHUNK] = head_norm_t(y, kg_ref[...]).T.astype(BF16)
        elif c < 2 * n_qk + n_v:
            cc = c - 2 * n_qk
            vt_ref[cc * COL_CHUNK:(cc + 1) * COL_CHUNK, :] = y.T.astype(BF16)
        elif c == 2 * n_qk + n_v:
            @pl.when(step > 0)
            def _():
                u_copy(step - 1).wait()
            u_buf[...] = y
        else:
            cc = c - (2 * n_qk + n_v + 1)
            b = bg_ref[:, cc * COL_CHUNK:(cc + 1) * COL_CHUNK]
            gate_ref[:, cc * COL_CHUNK:(cc + 1) * COL_CHUNK] = jax.nn.sigmoid(y + b).astype(BF16)

    u_copy(step).start()

    @pl.when(step == pl.num_programs(0) - 1)
    def _():
        u_copy(step).wait()


def _inproj(x2d, norm_g, w_in, qg, kg, b_gate, *, seq, layer):
    t = x2d.shape[0]
    n_sb = seq // TOK_TILE
    bsz = t // seq
    row = lambda w: pl.BlockSpec((TOK_TILE, w), lambda i: (i, 0))
    col = lambda w: pl.BlockSpec((w, TOK_TILE), lambda i: (i // n_sb, i % n_sb))
    return pl.pallas_call(
        functools.partial(_inproj_kernel, n_sb=n_sb),
        grid=(t // TOK_TILE,),
        in_specs=[row(D_MODEL), _const_spec((1, D_MODEL)), _layer_weight(w_in, layer),
                  _const_spec((COL_CHUNK, 1)), _const_spec((COL_CHUNK, 1)),
                  _const_spec((1, N_BRANCH * D_MODEL))],
        out_specs=[col(ATT_QK_WIDTH), row(ATT_QK_WIDTH), col(ATT_V_WIDTH),
                   pl.BlockSpec(memory_space=pl.ANY),
                   row(N_BRANCH * D_MODEL)],
        out_shape=[jax.ShapeDtypeStruct((bsz * ATT_QK_WIDTH, seq), BF16),
                   jax.ShapeDtypeStruct((t, ATT_QK_WIDTH), BF16),
                   jax.ShapeDtypeStruct((bsz * ATT_V_WIDTH, seq), BF16),
                   jax.ShapeDtypeStruct((seq, bsz, SSM_WIDTH), F32),
                   jax.ShapeDtypeStruct((t, N_BRANCH * D_MODEL), BF16)],
        scratch_shapes=[pltpu.VMEM((D_MODEL, IN_WIDTH), BF16), pltpu.VMEM((TOK_TILE, SSM_WIDTH), F32),
                        pltpu.SemaphoreType.DMA(())],
        compiler_params=_params("arbitrary"),
        name="inproj",
    )(x2d, norm_g, w_in, qg, kg, b_gate)


def _attn_kernel(sc_ref, qt_ref, k_ref, vt_ref, g_ref, o_ref, *, lam_init):
    tq = ATT_TILE
    seq = k_ref.shape[0]
    nq = seq // tq
    lam = sc_ref[LAM_ROW, 0] + lam_init
    bound = sc_ref[BOUND_ROW, 0]
    head_row = lax.broadcasted_iota(jnp.int32, (HEAD_W, tq), 0)
    key_pos = lax.broadcasted_iota(jnp.int32, (tq, 2 * tq), 0)
    qry_pos = lax.broadcasted_iota(jnp.int32, (tq, 2 * tq), 1) & (tq - 1)
    causal = key_pos <= qry_pos
    zero = jnp.zeros((HEAD_W, tq), BF16)

    def tile(qi, shift_by_bound):
        qt = qt_ref[:, qi * tq:(qi + 1) * tq]
        qst = jnp.concatenate([jnp.where(head_row < ATT_HEAD_DIM, qt, zero),
                               jnp.where(head_row >= ATT_HEAD_DIM, qt, zero)], axis=1)
        nk = qi * tq
        s_d = jnp.where(causal, _dot(k_ref[nk:nk + tq, :], qst), -jnp.inf)
        if qi > 0:
            s_m = _dot(k_ref[0:nk, :], qst)
        if shift_by_bound:
            m = bound
        else:
            m = jnp.max(s_d, axis=0, keepdims=True)
            if qi > 0:
                m = jnp.maximum(m, jnp.max(s_m, axis=0, keepdims=True))
        p_d = jnp.exp2(s_d - m)
        l = jnp.sum(p_d, axis=0, keepdims=True)
        acc = _dot(vt_ref[:, nk:nk + tq], p_d.astype(BF16))
        if qi > 0:
            p_m = jnp.exp2(s_m - m)
            l = l + jnp.sum(p_m, axis=0, keepdims=True)
            acc = acc + _dot(vt_ref[:, 0:nk], p_m.astype(BF16))
        accn = acc * (1.0 / l)
        ot = accn[:, :tq] - lam * accn[:, tq:]
        ms = jnp.mean(ot * ot, axis=0, keepdims=True)
        o = (ot * lax.rsqrt(ms + EPS)).T * g_ref[...] * (1.0 - lam_init)
        o_ref[qi * tq:(qi + 1) * tq, :] = o.astype(BF16)

    @pl.when(bound < ATT_BOUND_LIMIT)
    def _():
        for qi in range(nq):
            tile(qi, True)

    @pl.when(jnp.logical_not(bound < ATT_BOUND_LIMIT))
    def _():
        for qi in range(nq):
            tile(qi, False)


def _attention(scalars, qt, k, vt, subln_g, *, seq, lam_init):
    t = k.shape[0]
    bsz = t // seq
    fm = pl.BlockSpec((HEAD_W, seq), lambda b, h: (b * ATT_HEADS + h, 0))
    tm = pl.BlockSpec((seq, HEAD_W), lambda b, h: (b, h))
    return pl.pallas_call(
        functools.partial(_attn_kernel, lam_init=lam_init),
        grid=(bsz, ATT_HEADS),
        in_specs=[pl.BlockSpec(memory_space=pltpu.SMEM), fm, tm, fm, _const_spec((1, HEAD_W))],
        out_specs=tm,
        out_shape=jax.ShapeDtypeStruct((t, ATT_V_WIDTH), BF16),
        compiler_params=_params("parallel", "parallel"),
        name="attn",
    )(scalars, qt, k, vt, subln_g)


def _ssm_kernel(u_ref, bt_ref, cre_ref, cim_ref, are_ref, aim_ref, d_ref, z_hbm, st_scr, state_scr, z_buf, z_sem):
    n_t, bsz, _ = u_ref.shape
    rows = n_t * bsz
    chunk = pl.program_id(0)

    def z_copies(i):
        t0 = pl.multiple_of(i * n_t, n_t)
        return [pltpu.make_async_copy(z_buf.at[:, b], z_hbm.at[b, pl.ds(t0, n_t)], z_sem) for b in range(bsz)]

    @pl.when(chunk == 0)
    def _():
        state_scr[...] = jnp.zeros(state_scr.shape, F32)

    u = u_ref[...].reshape(rows, SSM_WIDTH)
    ub = u.astype(BF16)
    n_tiles = 2 * N_STATE // MXU_N
    for j in range(n_tiles):
        k0 = LANES * (((j % (n_tiles // 2)) * MXU_N // SSM_STATE * SSM_GROUP) // LANES)
        st_scr[:, j * MXU_N:(j + 1) * MXU_N] = _dot(ub[:, k0:k0 + LANES], bt_ref[j])

    for cc in range(N_STATE // SCAN_LANES):
        lo = cc * SCAN_LANES
        hi = N_STATE + lo
        a_r = jnp.broadcast_to(are_ref[:, lo:lo + SCAN_LANES], (bsz, SCAN_LANES))
        a_i = jnp.broadcast_to(aim_ref[:, lo:lo + SCAN_LANES], (bsz, SCAN_LANES))

        def step(t, carry, lo=lo, hi=hi, a_r=a_r, a_i=a_i):
            s_r, s_i = carry
            r0 = pl.multiple_of(t * bsz, bsz)
            n_r = a_r * s_r - a_i * s_i + st_scr[pl.ds(r0, bsz), lo:lo + SCAN_LANES]
            n_i = a_r * s_i + a_i * s_r + st_scr[pl.ds(r0, bsz), hi:hi + SCAN_LANES]
            st_scr[pl.ds(r0, bsz), lo:lo + SCAN_LANES] = n_r
            st_scr[pl.ds(r0, bsz), hi:hi + SCAN_LANES] = n_i
            return n_r, n_i

        s_r, s_i = lax.fori_loop(0, rows // bsz, step,
                                 (state_scr[:, lo:lo + SCAN_LANES], state_scr[:, hi:hi + SCAN_LANES]),
                                 unroll=8)
        state_scr[:, lo:lo + SCAN_LANES] = s_r
        state_scr[:, hi:hi + SCAN_LANES] = s_i

    @pl.when(chunk > 0)
    def _():
        for cp in z_copies(chunk - 1):
            cp.wait()
    half = N_STATE // 2
    for n in range(2):
        s_re = st_scr[:, n * half:(n + 1) * half].astype(BF16)
        s_im = st_scr[:, N_STATE + n * half:N_STATE + (n + 1) * half].astype(BF16)
        cols = slice(n * MXU_N, (n + 1) * MXU_N)
        y = _dot(s_re, cre_ref[n]) - _dot(s_im, cim_ref[n]) + d_ref[:, cols] * u[:, cols]
        z_buf[:, :, cols] = jax.nn.gelu(y).reshape(n_t, bsz, MXU_N)
    for cp in z_copies(chunk):
        cp.start()

    @pl.when(chunk == pl.num_programs(0) - 1)
    def _():
        for cp in z_copies(chunk):
            cp.wait()


def _ssm(u_tm, b_tiles, c_re_tiles, c_im_tiles, a_re, a_im, d_skip):
    seq, bsz, _ = u_tm.shape
    return pl.pallas_call(
        _ssm_kernel,
        grid=(seq // SCAN_T,),
        in_specs=[pl.BlockSpec((SCAN_T, bsz, SSM_WIDTH), lambda i: (i, 0, 0)),
                  _const_spec(b_tiles.shape), _const_spec(c_re_tiles.shape),
                  _const_spec(c_im_tiles.shape), _const_spec((1, N_STATE)), _const_spec((1, N_STATE)),
                  _const_spec((1, SSM_WIDTH))],
        out_specs=pl.BlockSpec(memory_space=pl.ANY),
        out_shape=jax.ShapeDtypeStruct((bsz, seq, SSM_WIDTH), F32),
        scratch_shapes=[pltpu.VMEM((SCAN_T * bsz, 2 * N_STATE), F32), pltpu.VMEM((bsz, 2 * N_STATE), F32),
                        pltpu.VMEM((SCAN_T, bsz, SSM_WIDTH), F32), pltpu.SemaphoreType.DMA(())],
        compiler_params=_params("arbitrary"),
        name="ssm",
    )(u_tm, b_tiles, c_re_tiles, c_im_tiles, a_re, a_im, d_skip)


def _route(logits, ltri):
    lane = lax.broadcasted_iota(jnp.int32, logits.shape, 1)
    lanef = lane.astype(F32)
    big = float(ROUTE_W)
    neg = -jnp.inf
    gl = jnp.where(lane < N_GROUPS, logits, neg)
    gmax = jnp.max(gl, axis=-1, keepdims=True)
    grp = jnp.min(jnp.where(gl == gmax, lanef, big), axis=-1, keepdims=True)
    p_grp = 1.0 / jnp.sum(jnp.exp(gl - gmax), axis=-1, keepdims=True)
    first = ROUTE_OFF + grp * EXPERTS_PER_GROUP
    in_grp = (lanef >= first) & (lanef < first + EXPERTS_PER_GROUP)
    el = jnp.where(in_grp, logits, neg)
    v1 = jnp.max(el, axis=-1, keepdims=True)
    i1 = jnp.min(jnp.where(el == v1, lanef, big), axis=-1, keepdims=True)
    el2 = jnp.where(lanef == i1, neg, el)
    v2 = jnp.max(el2, axis=-1, keepdims=True)
    i2 = jnp.min(jnp.where(el2 == v2, lanef, big), axis=-1, keepdims=True)
    e21 = jnp.exp(v2 - v1)
    w1 = p_grp / (1.0 + e21)
    w2 = p_grp * (e21 / (1.0 + e21))
    member = lanef == grp
    before = _dot(ltri, jnp.where(member, 1.0, 0.0).astype(BF16))
    rank = jnp.sum(jnp.where(member, before, 0.0), axis=-1, keepdims=True)
    return (jnp.where(lanef == i1, w1, 0.0) + jnp.where(lanef == i2, w2, 0.0)
            + jnp.where(lane == GRP_LANE, grp, 0.0) + jnp.where(lane == RANK_LANE, rank, 0.0))


def _mix_kernel(x_ref, o_ref, z_ref, gate_ref, wau_ref, wglu_ref, bglu_ref, wsu_ref, wout_ref,
                fg_ref, wrh_ref, wrl_ref, br_ref, ltri_ref, x1_ref, hn_ref, route_ref, route_t_ref,
                wau_scr, wglu_scr, wsu_scr, wout_scr):
    _round_once([(wau_ref, wau_scr), (wglu_ref, wglu_scr), (wsu_ref, wsu_scr), (wout_ref, wout_scr)])
    y_att = _dot(o_ref[...], wau_scr[...])
    z = z_ref[...]
    z = z * jax.nn.sigmoid(_dot(z.astype(BF16), wglu_scr[...]) + bglu_ref[...])
    y_ssm = _dot(z.astype(BF16), wsu_scr[...])
    mixed = (gate_ref[:, :D_MODEL].astype(F32) * y_att + gate_ref[:, D_MODEL:].astype(F32) * y_ssm)
    x1 = x_ref[...] + _dot(mixed.astype(BF16), wout_scr[...])
    x1_ref[...] = x1
    ms = jnp.mean(x1 * x1, axis=-1, keepdims=True)
    hn = x1 * lax.rsqrt(ms + EPS) * fg_ref[...]
    hn_hi = hn.astype(BF16)
    hn_ref[...] = hn_hi
    hn_lo = (hn - hn_hi.astype(F32)).astype(BF16)
    logits = (_dot(hn_hi, wrh_ref[...]) + _dot(hn_lo, wrh_ref[...]) + _dot(hn_hi, wrl_ref[...])
              + br_ref[...])
    route = _route(logits, ltri_ref[...])
    route_ref[...] = route
    route_t_ref[...] = route.T[:SUBLANES, :]


def _mix(x2d, o, z, gates, wau, wglu, bglu, wsu, wout, ffn_g, wr_hi, wr_lo, br, ltri, *, layer):
    t = x2d.shape[0]
    row = lambda w: pl.BlockSpec((TOK_TILE, w), lambda i: (i, 0))
    return pl.pallas_call(
        _mix_kernel,
        grid=(t // TOK_TILE,),
        in_specs=[row(D_MODEL), row(ATT_V_WIDTH), row(SSM_WIDTH), row(N_BRANCH * D_MODEL),
                  _layer_weight(wau, layer), _layer_weight(wglu, layer), _const_spec(bglu.shape),
                  _layer_weight(wsu, layer), _layer_weight(wout, layer), _const_spec(ffn_g.shape),
                  _const_spec(wr_hi.shape), _const_spec(wr_lo.shape), _const_spec(br.shape),
                  _const_spec(ltri.shape)],
        out_specs=[row(D_MODEL), row(D_MODEL), row(ROUTE_W),
                   pl.BlockSpec((SUBLANES, TOK_TILE), lambda i: (0, i))],
        out_shape=[jax.ShapeDtypeStruct((t, D_MODEL), F32),
                   jax.ShapeDtypeStruct((t, D_MODEL), BF16),
                   jax.ShapeDtypeStruct((t, ROUTE_W), F32),
                   jax.ShapeDtypeStruct((SUBLANES, t), F32)],
        scratch_shapes=[pltpu.VMEM(w.shape[1:], BF16) for w in (wau, wglu, wsu, wout)],
        compiler_params=_params("arbitrary"),
        name="mix",
    )(x2d, o, z, gates, wau, wglu, bglu, wsu, wout, ffn_g, wr_hi, wr_lo, br, ltri)


def _moe_kernel(hn_ref, route_ref, route_t_ref, wg_ref, wu_ref, wd_ref, out_ref, wg_scr, wu_scr, wd_scr):
    n = pl.program_id(0)

    @pl.when(pl.program_id(1) == 0)
    def _():
        for j in range(EXPERTS_PER_GROUP):
            cols = slice(j * EXPERT_FF, (j + 1) * EXPERT_FF)
            wg_scr[:, cols] = wg_ref[j].astype(BF16)
            wu_scr[:, cols] = wu_ref[j].astype(BF16)
            wd_scr[cols, :] = wd_ref[j].astype(BF16)

    tm = TOK_TILE
    nf = n.astype(F32)

    def routing_tile(rows):
        route = route_ref[rows, :]
        lane = lax.broadcasted_iota(jnp.int32, route.shape, 1)
        grp_c = jnp.sum(jnp.where(lane == GRP_LANE, route, 0.0), axis=-1, keepdims=True)
        rank_c = jnp.sum(jnp.where(lane == RANK_LANE, route, 0.0), axis=-1, keepdims=True)
        grp_r = route_t_ref[GRP_LANE:GRP_LANE + 1, rows]
        rank_r = route_t_ref[RANK_LANE:RANK_LANE + 1, rows]
        r1 = route.astype(BF16)
        r2 = (route - r1.astype(F32)).astype(BF16)
        r3 = (route - r1.astype(F32) - r2.astype(F32)).astype(BF16)
        pieces = jnp.concatenate([r1, r2, r3], axis=1)

        def group_rows(base):
            slot_r = lax.broadcasted_iota(jnp.int32, (MOE_CAP, tm), 0).astype(F32) + base
            take = jnp.where((grp_r == nf) & (rank_r == slot_r), 1.0, 0.0).astype(BF16)
            slot_c = lax.broadcasted_iota(jnp.int32, (tm, MOE_CAP), 1).astype(F32) + base
            put = jnp.where((grp_c == nf) & (rank_c == slot_c), 1.0, 0.0).astype(BF16)
            xc = _dot(take, hn_ref[rows, :]).astype(BF16)
            rc = _dot(take, pieces)
            rc = rc[:, :ROUTE_W] + rc[:, ROUTE_W:2 * ROUTE_W] + rc[:, 2 * ROUTE_W:]
            lane_c = lax.broadcasted_iota(jnp.int32, rc.shape, 1)
            first = ROUTE_OFF + n * EXPERTS_PER_GROUP
            act = jax.nn.silu(_dot(xc, wg_scr[...])) * _dot(xc, wu_scr[...])
            blocks = []
            for j in range(EXPERTS_PER_GROUP):
                w_j = jnp.sum(jnp.where(lane_c == first + j, rc, 0.0), axis=-1, keepdims=True)
                blocks.append((act[:, j * EXPERT_FF:(j + 1) * EXPERT_FF] * w_j).astype(BF16))
            y = _dot(jnp.concatenate(blocks, axis=1), wd_scr[...])
            return _dot(put, y.astype(BF16))

        out_ref[rows, :] = group_rows(jnp.float32(0.0)).astype(BF16)

        n_rows = jnp.max(jnp.where(grp_c == nf, rank_c + 1.0, 0.0))
        n_pass = lax.div(n_rows.astype(jnp.int32) + (MOE_CAP - 1), MOE_CAP)

        def extra_pass(k, carry):
            more = group_rows((k * MOE_CAP).astype(F32))
            out_ref[rows, :] = (out_ref[rows, :].astype(F32) + more).astype(BF16)
            return carry

        lax.fori_loop(1, n_pass, extra_pass, 0)

    for s in range(MOE_SUB):
        routing_tile(slice(s * TOK_TILE, (s + 1) * TOK_TILE))


def _moe(hn, route, route_t, w_gate, w_up, w_down, *, layer):
    t = hn.shape[0]
    rows = MOE_SUB * TOK_TILE
    row = lambda w: pl.BlockSpec((rows, w), lambda n, i: (i, 0))
    grp_w = lambda shape: pl.BlockSpec((None, None) + shape, lambda n, i: (layer, n, 0, 0, 0),
                                       pipeline_mode=pl.Buffered(1))
    up_shape = (EXPERTS_PER_GROUP, D_MODEL, EXPERT_FF)
    down_shape = (EXPERTS_PER_GROUP, EXPERT_FF, D_MODEL)
    ff = EXPERTS_PER_GROUP * EXPERT_FF
    return pl.pallas_call(
        _moe_kernel,
        grid=(N_GROUPS, t // rows),
        in_specs=[row(D_MODEL), row(ROUTE_W), pl.BlockSpec((SUBLANES, rows), lambda n, i: (0, i)),
                  grp_w(up_shape), grp_w(up_shape), grp_w(down_shape)],
        out_specs=pl.BlockSpec((None, rows, D_MODEL), lambda n, i: (n, i, 0)),
        out_shape=jax.ShapeDtypeStruct((N_GROUPS, t, D_MODEL), BF16),
        scratch_shapes=[pltpu.VMEM((D_MODEL, ff), BF16), pltpu.VMEM((D_MODEL, ff), BF16),
                        pltpu.VMEM((ff, D_MODEL), BF16)],
        compiler_params=_params("arbitrary", "arbitrary"),
        name="moe",
    )(hn, route, route_t, w_gate, w_up, w_down)


def _ple_kernel(x1_ref, c_ref, p_ref, pg_ref, wpg_ref, wple_ref, out_ref, wpg_scr, wple_scr):
    _round_once([(wpg_ref, wpg_scr), (wple_ref, wple_scr)])
    x2 = x1_ref[...]
    for n in range(N_GROUPS):
        x2 = x2 + c_ref[n].astype(F32)
    ms = jnp.mean(x2 * x2, axis=-1, keepdims=True)
    hp = (x2 * lax.rsqrt(ms + EPS) * pg_ref[...]).astype(BF16)
    gate = jax.nn.sigmoid(_dot(hp, wpg_scr[...]))
    out_ref[...] = x2 + gate * _dot(p_ref[...].astype(BF16), wple_scr[...])


def _ple(x1, contrib, p, ple_g, wpg, wple, *, layer):
    t = x1.shape[0]
    n_sb = p.shape[2] // TOK_TILE
    row = lambda w: pl.BlockSpec((TOK_TILE, w), lambda i: (i, 0))
    return pl.pallas_call(
        _ple_kernel,
        grid=(t // TOK_TILE,),
        in_specs=[row(D_MODEL), pl.BlockSpec((N_GROUPS, TOK_TILE, D_MODEL), lambda i: (0, i, 0)),
                  pl.BlockSpec((None, None, TOK_TILE, PLE_DIM), lambda i: (layer, i // n_sb, i % n_sb, 0)),
                  _const_spec(ple_g.shape), _layer_weight(wpg, layer), _layer_weight(wple, layer)],
        out_specs=row(D_MODEL),
        out_shape=jax.ShapeDtypeStruct((t, D_MODEL), F32),
        scratch_shapes=[pltpu.VMEM(wpg.shape[1:], BF16), pltpu.VMEM(wple.shape[1:], BF16)],
        compiler_params=_params("arbitrary"),
        name="ple",
    )(x1, contrib, p, ple_g, wpg, wple)


def kernel(x, p, attn_norm_g, w_in, b_gate, q_norm_g, k_norm_g, lam_qk, subln_g, w_attn_up, ssm_lam_re, ssm_lam_im, ssm_log_step, ssm_b_re, ssm_b_im, ssm_c_re, ssm_c_im, ssm_d, w_glu, b_glu, w_ssm_up, w_out, ffn_norm_g, w_router_group, b_router_group, w_router_expert, b_router_expert, w_exp_gate, w_exp_up, w_exp_down, ple_norm_g, w_ple_gate, w_ple):
    bsz, seq, d = x.shape
    depth = w_in.shape[0]
    t = bsz * seq
    assert d == D_MODEL and seq % TOK_TILE == 0 and seq % ATT_TILE == 0
    assert seq % SCAN_T == 0 and bsz == SUBLANES and t % (MOE_SUB * TOK_TILE) == 0
    tok = jnp.arange(TOK_TILE)
    ltri = (tok[None, :] < tok[:, None]).astype(BF16)

    a_re, a_im, b_tiles, c_re_tiles, c_im_tiles, scalars = _prep(
        ssm_lam_re, ssm_lam_im, ssm_log_step, ssm_b_re, ssm_b_im, ssm_c_re, ssm_c_im, lam_qk, q_norm_g, k_norm_g)
    x2d = x.reshape(t, d)
    for i in range(depth):
        lam_init = _lambda_init(i)
        tile_gain = lambda g: jnp.tile(g.reshape(HEAD_W, 1), (COL_CHUNK // HEAD_W, 1))
        qt, k, vt, u_tm, gates = _inproj(
            x2d, attn_norm_g[i].reshape(1, d), w_in,
            tile_gain(q_norm_g[i]), tile_gain(k_norm_g[i]), b_gate[i].reshape(1, -1), seq=seq, layer=i)

        o = _attention(scalars[i], qt, k, vt, subln_g[i].reshape(1, HEAD_W), seq=seq, lam_init=lam_init)

        z = _ssm(u_tm, b_tiles[i], c_re_tiles[i], c_im_tiles[i],
                 a_re[i, ::SSM_GROUP].reshape(1, N_STATE), a_im[i, ::SSM_GROUP].reshape(1, N_STATE),
                 ssm_d[i].reshape(1, SSM_WIDTH))

        w_r = jnp.concatenate(
            [w_router_group[i], jnp.transpose(w_router_expert[i], (1, 0, 2)).reshape(d, N_EXPERTS),
             jnp.zeros((d, ROUTE_W - N_GROUPS - N_EXPERTS), F32)], axis=1)
        w_r_hi = w_r.astype(BF16)
        w_r_lo = (w_r - w_r_hi.astype(F32)).astype(BF16)
        b_r = jnp.concatenate([b_router_group[i], b_router_expert[i].reshape(-1),
                               jnp.zeros((ROUTE_W - N_GROUPS - N_EXPERTS,), F32)]).reshape(1, ROUTE_W)
        x1, hn, route, route_t = _mix(
            x2d, o, z.reshape(t, SSM_WIDTH), gates,
            w_attn_up, w_glu, b_glu[i].reshape(1, -1), w_ssm_up, w_out, ffn_norm_g[i].reshape(1, d),
            w_r_hi, w_r_lo, b_r, ltri, layer=i)

        contrib = _moe(hn, route, route_t, w_exp_gate, w_exp_up, w_exp_down, layer=i)
        x2d = _ple(x1, contrib, p, ple_norm_g[i].reshape(1, d), w_ple_gate, w_ple, layer=i)
    return x2d.reshape(bsz, seq, d)
```

```python
import functools
import math

import jax
import jax.numpy as jnp
from jax import lax
from jax.experimental import pallas as pl
from jax.experimental.pallas import tpu as pltpu

F32 = jnp.float32
BF16 = jnp.bfloat16

D_MODEL = 1024
ATT_HEADS = 8
ATT_HEAD_DIM = 64
HEAD_W = 2 * ATT_HEAD_DIM
ATT_QK_WIDTH = ATT_HEADS * HEAD_W
ATT_V_WIDTH = ATT_HEADS * HEAD_W
SSM_WIDTH = 512
SSM_GROUP = 16
SSM_GROUPS = SSM_WIDTH // SSM_GROUP
SSM_STATE = 64
N_STATE = SSM_GROUPS * SSM_STATE
N_BRANCH = 2
IN_WIDTH = 2 * ATT_QK_WIDTH + ATT_V_WIDTH + SSM_WIDTH + N_BRANCH * D_MODEL
N_GROUPS = 4
EXPERTS_PER_GROUP = 8
N_EXPERTS = N_GROUPS * EXPERTS_PER_GROUP
EXPERT_FF = 256
PLE_DIM = 256
EPS = 1e-6

LANES = 128
SUBLANES = 8
MXU_N = 256
VMEM_LIMIT = 56 * 1024 * 1024

TOK_TILE = 512
MOE_CAP = 160
MOE_SUB = 2
COL_CHUNK = 512
ATT_TILE = 256
Q_SCALE = ATT_HEAD_DIM ** -0.5 * math.log2(math.e)
LAM_ROW = 0
BOUND_ROW = 1
ATT_BOUND_SLACK = 1.01
ATT_BOUND_LIMIT = 60.0
SCAN_T = 64
SCAN_LANES = 1024
ROUTE_W = LANES
ROUTE_OFF = N_GROUPS
GRP_LANE = 0
RANK_LANE = 1


def _lambda_init(layer_idx):
    return 0.8 - 0.6 * math.exp(-0.3 * layer_idx)


def _dot(a, b):
    return jnp.dot(a, b, preferred_element_type=F32)


def _const_spec(shape):
    nd = len(shape)
    return pl.BlockSpec(shape, lambda *_: (0,) * nd, pipeline_mode=pl.Buffered(1))


def _layer_weight(stacked, layer):
    shape = stacked.shape[1:]
    return pl.BlockSpec((None,) + shape, lambda *_: (layer,) + (0,) * len(shape), pipeline_mode=pl.Buffered(1))


def _round_once(pairs):
    @pl.when(pl.program_id(0) == 0)
    def _():
        for w_ref, w_scr in pairs:
            w_scr[...] = w_ref[...].astype(BF16)


def _params(*sem):
    return pltpu.CompilerParams(dimension_semantics=sem, vmem_limit_bytes=VMEM_LIMIT)


def _prep_kernel(lre_ref, lim_ref, lstep_ref, bre_ref, bim_ref, cre_ref, cim_ref, lqk_ref, qg_ref, kg_ref,
                 are_ref, aim_ref, bt_ref, ctre_ref, ctim_ref, lam_ref):
    lr = lre_ref[...]
    li = lim_ref[...]
    dt = jnp.exp(lstep_ref[...])
    mag = jnp.exp(lr * dt)
    ab_re = mag * jnp.cos(li * dt)
    ab_im = mag * jnp.sin(li * dt)
    den = lr * lr + li * li
    nr = ab_re - 1.0
    coef_re = (nr * lr + ab_im * li) / den
    coef_im = (ab_im * lr - nr * li) / den
    br = bre_ref[...]
    bi = bim_ref[...]
    are_ref[...] = ab_re
    aim_ref[...] = ab_im
    bb_re = coef_re * br - coef_im * bi
    bb_im = coef_re * bi + coef_im * br

    def spread(x, reps):
        w = x.shape[1]
        src = lax.broadcasted_iota(jnp.int32, (w, w * reps), 0)
        dst = lax.broadcasted_iota(jnp.int32, (w, w * reps), 1) & (w - 1)
        return _dot(x.astype(BF16), jnp.where(src == dst, 1.0, 0.0).astype(BF16))

    def same_group(shape, row0, row_shift, col0, col_shift):
        rows = (lax.broadcasted_iota(jnp.int32, shape, 0) + row0) >> row_shift
        cols = (lax.broadcasted_iota(jnp.int32, shape, 1) + col0) >> col_shift
        return rows == cols

    n_re = N_STATE // MXU_N
    ch_shift = SSM_GROUP.bit_length() - 1
    st_shift = SSM_STATE.bit_length() - 1
    for part, bb in enumerate((bb_re, bb_im)):
        for j in range(n_re):
            r0 = LANES * ((j * MXU_N // SSM_STATE * SSM_GROUP) // LANES)
            x = spread(bb[r0:r0 + LANES, :], MXU_N // SSM_STATE)
            keep = same_group(x.shape, r0, ch_shift, j * MXU_N, st_shift)
            bt_ref[part * n_re + j] = jnp.where(keep, x, 0.0).astype(BF16)
    half = N_STATE // 2
    for c_ref, ct_ref in ((cre_ref, ctre_ref), (cim_ref, ctim_ref)):
        for n in range(2):
            y = spread(c_ref[n * half:(n + 1) * half, :], MXU_N // SSM_GROUP)
            keep = same_group(y.shape, n * half, st_shift, n * MXU_N, ch_shift)
            ct_ref[n] = jnp.where(keep, y, 0.0).astype(BF16)

    lq = lqk_ref[...]
    s01 = jnp.sum(lq[0:1] * lq[1:2], axis=-1, keepdims=True)
    s23 = jnp.sum(lq[2:3] * lq[3:4], axis=-1, keepdims=True)
    lam_dyn = jnp.exp(s01) - jnp.exp(s23)
    g_max = jnp.max(jnp.abs(qg_ref[...]), keepdims=True) * jnp.max(jnp.abs(kg_ref[...]), keepdims=True)
    bound = g_max * (ATT_HEAD_DIM * Q_SCALE * ATT_BOUND_SLACK)
    row = lax.broadcasted_iota(jnp.int32, lam_ref.shape, 0)
    lam_ref[...] = jnp.where(row == LAM_ROW, lam_dyn, bound)


def _prep(lam_re, lam_im, log_step, b_re, b_im, c_re, c_im, lam_qk, q_gain, k_gain):
    depth = lam_re.shape[0]
    rows = SSM_WIDTH
    rep = lambda a: jnp.repeat(a, SSM_GROUP, axis=1)
    lstep = jnp.broadcast_to(log_step[:, :, None], lam_re.shape)
    tr_b = lambda b: jnp.transpose(b, (0, 1, 3, 2)).reshape(depth, rows, SSM_STATE)
    tr_c = lambda c: jnp.transpose(c, (0, 1, 3, 2)).reshape(depth, N_STATE, SSM_GROUP)
    lay = lambda *shape: pl.BlockSpec((None,) + shape, lambda i: (i,) + (0,) * len(shape))
    blk = lay(rows, SSM_STATE)
    out = jax.ShapeDtypeStruct((depth, rows, SSM_STATE), F32)
    n_bt = 2 * N_STATE // MXU_N
    return pl.pallas_call(
        _prep_kernel,
        grid=(depth,),
        in_specs=[blk, blk, blk, blk, blk, lay(N_STATE, SSM_GROUP), lay(N_STATE, SSM_GROUP),
                  lay(4, ATT_HEAD_DIM), lay(2, ATT_HEAD_DIM), lay(2, ATT_HEAD_DIM)],
        out_specs=[blk, blk, lay(n_bt, LANES, MXU_N), lay(2, N_STATE // 2, MXU_N), lay(2, N_STATE // 2, MXU_N),
                   lay(SUBLANES, LANES)],
        out_shape=[out, out, jax.ShapeDtypeStruct((depth, n_bt, LANES, MXU_N), BF16),
                   jax.ShapeDtypeStruct((depth, 2, N_STATE // 2, MXU_N), BF16),
                   jax.ShapeDtypeStruct((depth, 2, N_STATE // 2, MXU_N), BF16),
                   jax.ShapeDtypeStruct((depth, SUBLANES, LANES), F32)],
        compiler_params=_params("arbitrary"),
        name="prep",
    )(rep(lam_re), rep(lam_im), rep(lstep), tr_b(b_re), tr_b(b_im), tr_c(c_re), tr_c(c_im), lam_qk, q_gain, k_gain)


def _inproj_kernel(x_ref, g_ref, w_ref, qg_ref, kg_ref, bg_ref,
                   qt_ref, k_ref, vt_ref, u_hbm, gate_ref, w_scr, u_buf, u_sem, *, n_sb):
    _round_once([(w_ref, w_scr)])
    step = pl.program_id(0)

    def u_copy(i):
        s0 = pl.multiple_of((i % n_sb) * TOK_TILE, TOK_TILE)
        return pltpu.make_async_copy(u_buf, u_hbm.at[pl.ds(s0, TOK_TILE), i // n_sb], u_sem)

    xf = x_ref[...]
    ms = jnp.mean(xf * xf, axis=-1, keepdims=True)
    h = (xf * lax.rsqrt(ms + EPS) * g_ref[...]).astype(BF16)

    def head_norm_t(y, gain_col):
        rows = y.shape[0]
        y3 = y.T.reshape(COL_CHUNK // ATT_HEAD_DIM, ATT_HEAD_DIM, rows)
        ms = jnp.sum(y3 * y3, axis=1, keepdims=True) * (1.0 / ATT_HEAD_DIM)
        return (y3 * lax.rsqrt(ms + EPS)).reshape(COL_CHUNK, rows) * gain_col

    n_qk = ATT_QK_WIDTH // COL_CHUNK
    n_v = ATT_V_WIDTH // COL_CHUNK
    for c in range(IN_WIDTH // COL_CHUNK):
        y = _dot(h, w_scr[:, c * COL_CHUNK:(c + 1) * COL_CHUNK])
        if c < n_qk:
            qt_ref[c * COL_CHUNK:(c + 1) * COL_CHUNK, :] = (head_norm_t(y, qg_ref[...]) * Q_SCALE).astype(BF16)
        elif c < 2 * n_qk:
            cc = c - n_qk
            k_ref[:, cc * COL_CHUNK:(cc + 1) * COL_CHUNK] = head_norm_t(y, kg_ref[...]).T.astype(BF16)
        elif c < 2 * n_qk + n_v:
            cc = c - 2 * n_qk
            vt_ref[cc * COL_CHUNK:(cc + 1) * COL_CHUNK, :] = y.T.astype(BF16)
        elif c == 2 * n_qk + n_v:
            @pl.when(step > 0)
            def _():
                u_copy(step - 1).wait()
            u_buf[...] = y
        else:
            cc = c - (2 * n_qk + n_v + 1)
            b = bg_ref[:, cc * COL_CHUNK:(cc + 1) * COL_CHUNK]
            gate_ref[:, cc * COL_CHUNK:(cc + 1) * COL_CHUNK] = jax.nn.sigmoid(y + b).astype(BF16)

    u_copy(step).start()

    @pl.when(step == pl.num_programs(0) - 1)
    def _():
        u_copy(step).wait()


def _inproj(x2d, norm_g, w_in, qg, kg, b_gate, *, seq, layer):
    t = x2d.shape[0]
    n_sb = seq // TOK_TILE
    bsz = t // seq
    row = lambda w: pl.BlockSpec((TOK_TILE, w), lambda i: (i, 0))
    col = lambda w: pl.BlockSpec((w, TOK_TILE), lambda i: (i // n_sb, i % n_sb))
    return pl.pallas_call(
        functools.partial(_inproj_kernel, n_sb=n_sb),
        grid=(t // TOK_TILE,),
        in_specs=[row(D_MODEL), _const_spec((1, D_MODEL)), _layer_weight(w_in, layer),
                  _const_spec((COL_CHUNK, 1)), _const_spec((COL_CHUNK, 1)),
                  _const_spec((1, N_BRANCH * D_MODEL))],
        out_specs=[col(ATT_QK_WIDTH), row(ATT_QK_WIDTH), col(ATT_V_WIDTH),
                   pl.BlockSpec(memory_space=pl.ANY),
                   row(N_BRANCH * D_MODEL)],
        out_shape=[jax.ShapeDtypeStruct((bsz * ATT_QK_WIDTH, seq), BF16),
                   jax.ShapeDtypeStruct((t, ATT_QK_WIDTH), BF16),
                   jax.ShapeDtypeStruct((bsz * ATT_V_WIDTH, seq), BF16),
                   jax.ShapeDtypeStruct((seq, bsz, SSM_WIDTH), F32),
                   jax.ShapeDtypeStruct((t, N_BRANCH * D_MODEL), BF16)],
        scratch_shapes=[pltpu.VMEM((D_MODEL, IN_WIDTH), BF16), pltpu.VMEM((TOK_TILE, SSM_WIDTH), F32),
                        pltpu.SemaphoreType.DMA(())],
        compiler_params=_params("arbitrary"),
        name="inproj",
    )(x2d, norm_g, w_in, qg, kg, b_gate)


def _attn_kernel(sc_ref, qt_ref, k_ref, vt_ref, g_ref, o_ref, *, lam_init):
    tq = ATT_TILE
    seq = k_ref.shape[0]
    nq = seq // tq
    lam = sc_ref[LAM_ROW, 0] + lam_init
    bound = sc_ref[BOUND_ROW, 0]
    head_row = lax.broadcasted_iota(jnp.int32, (HEAD_W, tq), 0)
    key_pos = lax.broadcasted_iota(jnp.int32, (tq, 2 * tq), 0)
    qry_pos = lax.broadcasted_iota(jnp.int32, (tq, 2 * tq), 1) & (tq - 1)
    causal = key_pos <= qry_pos
    zero = jnp.zeros((HEAD_W, tq), BF16)

    def tile(qi, shift_by_bound):
        qt = qt_ref[:, qi * tq:(qi + 1) * tq]
        qst = jnp.concatenate([jnp.where(head_row < ATT_HEAD_DIM, qt, zero),
                               jnp.where(head_row >= ATT_HEAD_DIM, qt, zero)], axis=1)
        nk = qi * tq
        s_d = jnp.where(causal, _dot(k_ref[nk:nk + tq, :], qst), -jnp.inf)
        if qi > 0:
            s_m = _dot(k_ref[0:nk, :], qst)
        if shift_by_bound:
            m = bound
        else:
            m = jnp.max(s_d, axis=0, keepdims=True)
            if qi > 0:
                m = jnp.maximum(m, jnp.max(s_m, axis=0, keepdims=True))
        p_d = jnp.exp2(s_d - m)
        l = jnp.sum(p_d, axis=0, keepdims=True)
        acc = _dot(vt_ref[:, nk:nk + tq], p_d.astype(BF16))
        if qi > 0:
            p_m = jnp.exp2(s_m - m)
            l = l + jnp.sum(p_m, axis=0, keepdims=True)
            acc = acc + _dot(vt_ref[:, 0:nk], p_m.astype(BF16))
        accn = acc * (1.0 / l)
        ot = accn[:, :tq] - lam * accn[:, tq:]
        ms = jnp.mean(ot * ot, axis=0, keepdims=True)
        o = (ot * lax.rsqrt(ms + EPS)).T * g_ref[...] * (1.0 - lam_init)
        o_ref[qi * tq:(qi + 1) * tq, :] = o.astype(BF16)

    @pl.when(bound < ATT_BOUND_LIMIT)
    def _():
        for qi in range(nq):
            tile(qi, True)

    @pl.when(jnp.logical_not(bound < ATT_BOUND_LIMIT))
    def _():
        for qi in range(nq):
            tile(qi, False)


def _attention(scalars, qt, k, vt, subln_g, *, seq, lam_init):
    t = k.shape[0]
    bsz = t // seq
    fm = pl.BlockSpec((HEAD_W, seq), lambda b, h: (b * ATT_HEADS + h, 0))
    tm = pl.BlockSpec((seq, HEAD_W), lambda b, h: (b, h))
    return pl.pallas_call(
        functools.partial(_attn_kernel, lam_init=lam_init),
        grid=(bsz, ATT_HEADS),
        in_specs=[pl.BlockSpec(memory_space=pltpu.SMEM), fm, tm, fm, _const_spec((1, HEAD_W))],
        out_specs=tm,
        out_shape=jax.ShapeDtypeStruct((t, ATT_V_WIDTH), BF16),
        compiler_params=_params("parallel", "parallel"),
        name="attn",
    )(scalars, qt, k, vt, subln_g)


def _ssm_kernel(u_ref, bt_ref, cre_ref, cim_ref, are_ref, aim_ref, d_ref, z_hbm, st_scr, state_scr, z_buf, z_sem):
    n_t, bsz, _ = u_ref.shape
    rows = n_t * bsz
    chunk = pl.program_id(0)

    def z_copies(i):
        t0 = pl.multiple_of(i * n_t, n_t)
        return [pltpu.make_async_copy(z_buf.at[:, b], z_hbm.at[b, pl.ds(t0, n_t)], z_sem) for b in range(bsz)]

    @pl.when(chunk == 0)
    def _():
        state_scr[...] = jnp.zeros(state_scr.shape, F32)

    u = u_ref[...].reshape(rows, SSM_WIDTH)
    ub = u.astype(BF16)
    n_tiles = 2 * N_STATE // MXU_N
    for j in range(n_tiles):
        k0 = LANES * (((j % (n_tiles // 2)) * MXU_N // SSM_STATE * SSM_GROUP) // LANES)
        st_scr[:, j * MXU_N:(j + 1) * MXU_N] = _dot(ub[:, k0:k0 + LANES], bt_ref[j])

    for cc in range(N_STATE // SCAN_LANES):
        lo = cc * SCAN_LANES
        hi = N_STATE + lo
        a_r = jnp.broadcast_to(are_ref[:, lo:lo + SCAN_LANES], (bsz, SCAN_LANES))
        a_i = jnp.broadcast_to(aim_ref[:, lo:lo + SCAN_LANES], (bsz, SCAN_LANES))

        def step(t, carry, lo=lo, hi=hi, a_r=a_r, a_i=a_i):
            s_r, s_i = carry
            r0 = pl.multiple_of(t * bsz, bsz)
            n_r = a_r * s_r - a_i * s_i + st_scr[pl.ds(r0, bsz), lo:lo + SCAN_LANES]
            n_i = a_r * s_i + a_i * s_r + st_scr[pl.ds(r0, bsz), hi:hi + SCAN_LANES]
            st_scr[pl.ds(r0, bsz), lo:lo + SCAN_LANES] = n_r
            st_scr[pl.ds(r0, bsz), hi:hi + SCAN_LANES] = n_i
            return n_r, n_i

        s_r, s_i = lax.fori_loop(0, rows // bsz, step,
                                 (state_scr[:, lo:lo + SCAN_LANES], state_scr[:, hi:hi + SCAN_LANES]),
                                 unroll=8)
        state_scr[:, lo:lo + SCAN_LANES] = s_r
        state_scr[:, hi:hi + SCAN_LANES] = s_i

    @pl.when(chunk > 0)
    def _():
        for cp in z_copies(chunk - 1):
            cp.wait()
    half = N_STATE // 2
    for n in range(2):
        s_re = st_scr[:, n * half:(n + 1) * half].astype(BF16)
        s_im = st_scr[:, N_STATE + n * half:N_STATE + (n + 1) * half].astype(BF16)
        cols = slice(n * MXU_N, (n + 1) * MXU_N)
        y = _dot(s_re, cre_ref[n]) - _dot(s_im, cim_ref[n]) + d_ref[:, cols] * u[:, cols]
        z_buf[:, :, cols] = jax.nn.gelu(y).reshape(n_t, bsz, MXU_N)
    for cp in z_copies(chunk):
        cp.start()

    @pl.when(chunk == pl.num_programs(0) - 1)
    def _():
        for cp in z_copies(chunk):
            cp.wait()


def _ssm(u_tm, b_tiles, c_re_tiles, c_im_tiles, a_re, a_im, d_skip):
    seq, bsz, _ = u_tm.shape
    return pl.pallas_call(
        _ssm_kernel,
        grid=(seq // SCAN_T,),
        in_specs=[pl.BlockSpec((SCAN_T, bsz, SSM_WIDTH), lambda i: (i, 0, 0)),
                  _const_spec(b_tiles.shape), _const_spec(c_re_tiles.shape),
                  _const_spec(c_im_tiles.shape), _const_spec((1, N_STATE)), _const_spec((1, N_STATE)),
                  _const_spec((1, SSM_WIDTH))],
        out_specs=pl.BlockSpec(memory_space=pl.ANY),
        out_shape=jax.ShapeDtypeStruct((bsz, seq, SSM_WIDTH), F32),
        scratch_shapes=[pltpu.VMEM((SCAN_T * bsz, 2 * N_STATE), F32), pltpu.VMEM((bsz, 2 * N_STATE), F32),
                        pltpu.VMEM((SCAN_T, bsz, SSM_WIDTH), F32), pltpu.SemaphoreType.DMA(())],
        compiler_params=_params("arbitrary"),
        name="ssm",
    )(u_tm, b_tiles, c_re_tiles, c_im_tiles, a_re, a_im, d_skip)


def _route(logits, ltri):
    lane = lax.broadcasted_iota(jnp.int32, logits.shape, 1)
    lanef = lane.astype(F32)
    big = float(ROUTE_W)
    neg = -jnp.inf
    gl = jnp.where(lane < N_GROUPS, logits, neg)
    gmax = jnp.max(gl, axis=-1, keepdims=True)
    grp = jnp.min(jnp.where(gl == gmax, lanef, big), axis=-1, keepdims=True)
    p_grp = 1.0 / jnp.sum(jnp.exp(gl - gmax), axis=-1, keepdims=True)
    first = ROUTE_OFF + grp * EXPERTS_PER_GROUP
    in_grp = (lanef >= first) & (lanef < first + EXPERTS_PER_GROUP)
    el = jnp.where(in_grp, logits, neg)
    v1 = jnp.max(el, axis=-1, keepdims=True)
    i1 = jnp.min(jnp.where(el == v1, lanef, big), axis=-1, keepdims=True)
    el2 = jnp.where(lanef == i1, neg, el)
    v2 = jnp.max(el2, axis=-1, keepdims=True)
    i2 = jnp.min(jnp.where(el2 == v2, lanef, big), axis=-1, keepdims=True)
    e21 = jnp.exp(v2 - v1)
    w1 = p_grp / (1.0 + e21)
    w2 = p_grp * (e21 / (1.0 + e21))
    member = lanef == grp
    before = _dot(ltri, jnp.where(member, 1.0, 0.0).astype(BF16))
    rank = jnp.sum(jnp.where(member, before, 0.0), axis=-1, keepdims=True)
    return (jnp.where(lanef == i1, w1, 0.0) + jnp.where(lanef == i2, w2, 0.0)
            + jnp.where(lane == GRP_LANE, grp, 0.0) + jnp.where(lane == RANK_LANE, rank, 0.0))


def _mix_kernel(x_ref, o_ref, z_ref, gate_ref, wau_ref, wglu_ref, bglu_ref, wsu_ref, wout_ref,
                fg_ref, wrh_ref, wrl_ref, br_ref, ltri_ref, x1_ref, hn_ref, route_ref, route_t_ref,
                wau_scr, wglu_scr, wsu_scr, wout_scr):
    _round_once([(wau_ref, wau_scr), (wglu_ref, wglu_scr), (wsu_ref, wsu_scr), (wout_ref, wout_scr)])
    y_att = _dot(o_ref[...], wau_scr[...])
    z = z_ref[...]
    z = z * jax.nn.sigmoid(_dot(z.astype(BF16), wglu_scr[...]) + bglu_ref[...])
    y_ssm = _dot(z.astype(BF16), wsu_scr[...])
    mixed = (gate_ref[:, :D_MODEL].astype(F32) * y_att + gate_ref[:, D_MODEL:].astype(F32) * y_ssm)
    x1 = x_ref[...] + _dot(mixed.astype(BF16), wout_scr[...])
    x1_ref[...] = x1
    ms = jnp.mean(x1 * x1, axis=-1, keepdims=True)
    hn = x1 * lax.rsqrt(ms + EPS) * fg_ref[...]
    hn_hi = hn.astype(BF16)
    hn_ref[...] = hn_hi
    hn_lo = (hn - hn_hi.astype(F32)).astype(BF16)
    logits = (_dot(hn_hi, wrh_ref[...]) + _dot(hn_lo, wrh_ref[...]) + _dot(hn_hi, wrl_ref[...])
              + br_ref[...])
    route = _route(logits, ltri_ref[...])
    route_ref[...] = route
    route_t_ref[...] = route.T[:SUBLANES, :]


def _mix(x2d, o, z, gates, wau, wglu, bglu, wsu, wout, ffn_g, wr_hi, wr_lo, br, ltri, *, layer):
    t = x2d.shape[0]
    row = lambda w: pl.BlockSpec((TOK_TILE, w), lambda i: (i, 0))
    return pl.pallas_call(
        _mix_kernel,
        grid=(t // TOK_TILE,),
        in_specs=[row(D_MODEL), row(ATT_V_WIDTH), row(SSM_WIDTH), row(N_BRANCH * D_MODEL),
                  _layer_weight(wau, layer), _layer_weight(wglu, layer), _const_spec(bglu.shape),
                  _layer_weight(wsu, layer), _layer_weight(wout, layer), _const_spec(ffn_g.shape),
                  _const_spec(wr_hi.shape), _const_spec(wr_lo.shape), _const_spec(br.shape),
                  _const_spec(ltri.shape)],
        out_specs=[row(D_MODEL), row(D_MODEL), row(ROUTE_W),
                   pl.BlockSpec((SUBLANES, TOK_TILE), lambda i: (0, i))],
        out_shape=[jax.ShapeDtypeStruct((t, D_MODEL), F32),
                   jax.ShapeDtypeStruct((t, D_MODEL), BF16),
                   jax.ShapeDtypeStruct((t, ROUTE_W), F32),
                   jax.ShapeDtypeStruct((SUBLANES, t), F32)],
        scratch_shapes=[pltpu.VMEM(w.shape[1:], BF16) for w in (wau, wglu, wsu, wout)],
        compiler_params=_params("arbitrary"),
        name="mix",
    )(x2d, o, z, gates, wau, wglu, bglu, wsu, wout, ffn_g, wr_hi, wr_lo, br, ltri)


def _moe_kernel(hn_ref, route_ref, route_t_ref, wg_ref, wu_ref, wd_ref, out_ref, wg_scr, wu_scr, wd_scr):
    n = pl.program_id(0)

    @pl.when(pl.program_id(1) == 0)
    def _():
        for j in range(EXPERTS_PER_GROUP):
            cols = slice(j * EXPERT_FF, (j + 1) * EXPERT_FF)
            wg_scr[:, cols] = wg_ref[j].astype(BF16)
            wu_scr[:, cols] = wu_ref[j].astype(BF16)
            wd_scr[cols, :] = wd_ref[j].astype(BF16)

    tm = TOK_TILE
    nf = n.astype(F32)

    def routing(rows):
        route = route_ref[rows, :]
        lane = lax.broadcasted_iota(jnp.int32, route.shape, 1)
        grp_c = jnp.sum(jnp.where(lane == GRP_LANE, route, 0.0), axis=-1, keepdims=True)
        rank_c = jnp.sum(jnp.where(lane == RANK_LANE, route, 0.0), axis=-1, keepdims=True)
        grp_r = route_t_ref[GRP_LANE:GRP_LANE + 1, rows]
        rank_r = route_t_ref[RANK_LANE:RANK_LANE + 1, rows]
        r1 = route.astype(BF16)
        r2 = (route - r1.astype(F32)).astype(BF16)
        r3 = (route - r1.astype(F32) - r2.astype(F32)).astype(BF16)
        return grp_c, rank_c, grp_r, rank_r, jnp.concatenate([r1, r2, r3], axis=1)

    def compact(rows, info, base):
        grp_c, rank_c, grp_r, rank_r, pieces = info
        slot_r = lax.broadcasted_iota(jnp.int32, (MOE_CAP, tm), 0).astype(F32) + base
        take = jnp.where((grp_r == nf) & (rank_r == slot_r), 1.0, 0.0).astype(BF16)
        slot_c = lax.broadcasted_iota(jnp.int32, (tm, MOE_CAP), 1).astype(F32) + base
        put = jnp.where((grp_c == nf) & (rank_c == slot_c), 1.0, 0.0).astype(BF16)
        xc = _dot(take, hn_ref[rows, :]).astype(BF16)
        rc = _dot(take, pieces)
        rc = rc[:, :ROUTE_W] + rc[:, ROUTE_W:2 * ROUTE_W] + rc[:, 2 * ROUTE_W:]
        return xc, rc, put

    def experts(xc, rc):
        lane_c = lax.broadcasted_iota(jnp.int32, rc.shape, 1)
        first = ROUTE_OFF + n * EXPERTS_PER_GROUP
        act = jax.nn.silu(_dot(xc, wg_scr[...])) * _dot(xc, wu_scr[...])
        blocks = []
        for j in range(EXPERTS_PER_GROUP):
            w_j = jnp.sum(jnp.where(lane_c == first + j, rc, 0.0), axis=-1, keepdims=True)
            blocks.append((act[:, j * EXPERT_FF:(j + 1) * EXPERT_FF] * w_j).astype(BF16))
        return _dot(jnp.concatenate(blocks, axis=1), wd_scr[...]).astype(BF16)

    tiles = [slice(s * TOK_TILE, (s + 1) * TOK_TILE) for s in range(MOE_SUB)]
    infos = [routing(rows) for rows in tiles]
    firsts = [compact(rows, info, jnp.float32(0.0)) for rows, info in zip(tiles, infos)]
    y = experts(jnp.concatenate([f[0] for f in firsts], axis=0), jnp.concatenate([f[1] for f in firsts], axis=0))
    for s, rows in enumerate(tiles):
        out_ref[rows, :] = _dot(firsts[s][2], y[s * MOE_CAP:(s + 1) * MOE_CAP]).astype(BF16)

    for rows, info in zip(tiles, infos):
        n_rows = jnp.max(jnp.where(info[0] == nf, info[1] + 1.0, 0.0))
        n_pass = lax.div(n_rows.astype(jnp.int32) + (MOE_CAP - 1), MOE_CAP)

        def extra_pass(k, carry, rows=rows, info=info):
            xc, rc, put = compact(rows, info, (k * MOE_CAP).astype(F32))
            more = _dot(put, experts(xc, rc))
            out_ref[rows, :] = (out_ref[rows, :].astype(F32) + more).astype(BF16)
            return carry

        lax.fori_loop(1, n_pass, extra_pass, 0)


def _moe(hn, route, route_t, w_gate, w_up, w_down, *, layer):
    t = hn.shape[0]
    rows = MOE_SUB * TOK_TILE
    row = lambda w: pl.BlockSpec((rows, w), lambda n, i: (i, 0))
    grp_w = lambda shape: pl.BlockSpec((None, None) + shape, lambda n, i: (layer, n, 0, 0, 0),
                                       pipeline_mode=pl.Buffered(1))
    up_shape = (EXPERTS_PER_GROUP, D_MODEL, EXPERT_FF)
    down_shape = (EXPERTS_PER_GROUP, EXPERT_FF, D_MODEL)
    ff = EXPERTS_PER_GROUP * EXPERT_FF
    return pl.pallas_call(
        _moe_kernel,
        grid=(N_GROUPS, t // rows),
        in_specs=[row(D_MODEL), row(ROUTE_W), pl.BlockSpec((SUBLANES, rows), lambda n, i: (0, i)),
                  grp_w(up_shape), grp_w(up_shape), grp_w(down_shape)],
        out_specs=pl.BlockSpec((None, rows, D_MODEL), lambda n, i: (n, i, 0)),
        out_shape=jax.ShapeDtypeStruct((N_GROUPS, t, D_MODEL), BF16),
        scratch_shapes=[pltpu.VMEM((D_MODEL, ff), BF16), pltpu.VMEM((D_MODEL, ff), BF16),
                        pltpu.VMEM((ff, D_MODEL), BF16)],
        compiler_params=_params("arbitrary", "arbitrary"),
        name="moe",
    )(hn, route, route_t, w_gate, w_up, w_down)


def _ple_kernel(x1_ref, c_ref, p_ref, pg_ref, wpg_ref, wple_ref, out_ref, wpg_scr, wple_scr):
    _round_once([(wpg_ref, wpg_scr), (wple_ref, wple_scr)])
    x2 = x1_ref[...]
    for n in range(N_GROUPS):
        x2 = x2 + c_ref[n].astype(F32)
    ms = jnp.mean(x2 * x2, axis=-1, keepdims=True)
    hp = (x2 * lax.rsqrt(ms + EPS) * pg_ref[...]).astype(BF16)
    gate = jax.nn.sigmoid(_dot(hp, wpg_scr[...]))
    out_ref[...] = x2 + gate * _dot(p_ref[...].astype(BF16), wple_scr[...])


def _ple(x1, contrib, p, ple_g, wpg, wple, *, layer):
    t = x1.shape[0]
    n_sb = p.shape[2] // TOK_TILE
    row = lambda w: pl.BlockSpec((TOK_TILE, w), lambda i: (i, 0))
    return pl.pallas_call(
        _ple_kernel,
        grid=(t // TOK_TILE,),
        in_specs=[row(D_MODEL), pl.BlockSpec((N_GROUPS, TOK_TILE, D_MODEL), lambda i: (0, i, 0)),
                  pl.BlockSpec((None, None, TOK_TILE, PLE_DIM), lambda i: (layer, i // n_sb, i % n_sb, 0)),
                  _const_spec(ple_g.shape), _layer_weight(wpg, layer), _layer_weight(wple, layer)],
        out_specs=row(D_MODEL),
        out_shape=jax.ShapeDtypeStruct((t, D_MODEL), F32),
        scratch_shapes=[pltpu.VMEM(wpg.shape[1:], BF16), pltpu.VMEM(wple.shape[1:], BF16)],
        compiler_params=_params("arbitrary"),
        name="ple",
    )(x1, contrib, p, ple_g, wpg, wple)


def kernel(x, p, attn_norm_g, w_in, b_gate, q_norm_g, k_norm_g, lam_qk, subln_g, w_attn_up, ssm_lam_re, ssm_lam_im, ssm_log_step, ssm_b_re, ssm_b_im, ssm_c_re, ssm_c_im, ssm_d, w_glu, b_glu, w_ssm_up, w_out, ffn_norm_g, w_router_group, b_router_group, w_router_expert, b_router_expert, w_exp_gate, w_exp_up, w_exp_down, ple_norm_g, w_ple_gate, w_ple):
    bsz, seq, d = x.shape
    depth = w_in.shape[0]
    t = bsz * seq
    assert d == D_MODEL and seq % TOK_TILE == 0 and seq % ATT_TILE == 0
    assert seq % SCAN_T == 0 and bsz == SUBLANES and t % (MOE_SUB * TOK_TILE) == 0
    tok = jnp.arange(TOK_TILE)
    ltri = (tok[None, :] < tok[:, None]).astype(BF16)

    a_re, a_im, b_tiles, c_re_tiles, c_im_tiles, scalars = _prep(
        ssm_lam_re, ssm_lam_im, ssm_log_step, ssm_b_re, ssm_b_im, ssm_c_re, ssm_c_im, lam_qk, q_norm_g, k_norm_g)
    x2d = x.reshape(t, d)
    for i in range(depth):
        lam_init = _lambda_init(i)
        tile_gain = lambda g: jnp.tile(g.reshape(HEAD_W, 1), (COL_CHUNK // HEAD_W, 1))
        qt, k, vt, u_tm, gates = _inproj(
            x2d, attn_norm_g[i].reshape(1, d), w_in,
            tile_gain(q_norm_g[i]), tile_gain(k_norm_g[i]), b_gate[i].reshape(1, -1), seq=seq, layer=i)

        o = _attention(scalars[i], qt, k, vt, subln_g[i].reshape(1, HEAD_W), seq=seq, lam_init=lam_init)

        z = _ssm(u_tm, b_tiles[i], c_re_tiles[i], c_im_tiles[i],
                 a_re[i, ::SSM_GROUP].reshape(1, N_STATE), a_im[i, ::SSM_GROUP].reshape(1, N_STATE),
                 ssm_d[i].reshape(1, SSM_WIDTH))

        w_r = jnp.concatenate(
            [w_router_group[i], jnp.transpose(w_router_expert[i], (1, 0, 2)).reshape(d, N_EXPERTS),
             jnp.zeros((d, ROUTE_W - N_GROUPS - N_EXPERTS), F32)], axis=1)
        w_r_hi = w_r.astype(BF16)
        w_r_lo = (w_r - w_r_hi.astype(F32)).astype(BF16)
        b_r = jnp.concatenate([b_router_group[i], b_router_expert[i].reshape(-1),
                               jnp.zeros((ROUTE_W - N_GROUPS - N_EXPERTS,), F32)]).reshape(1, ROUTE_W)
        x1, hn, route, route_t = _mix(
            x2d, o, z.reshape(t, SSM_WIDTH), gates,
            w_attn_up, w_glu, b_glu[i].reshape(1, -1), w_ssm_up, w_out, ffn_norm_g[i].reshape(1, d),
            w_r_hi, w_r_lo, b_r, ltri, layer=i)

        contrib = _moe(hn, route, route_t, w_exp_gate, w_exp_up, w_exp_down, layer=i)
        x2d = _ple(x1, contrib, p, ple_norm_g[i].reshape(1, d), w_ple_gate, w_ple, layer=i)
    return x2d.reshape(bsz, seq, d)
```

```python
import functools
import math

import jax
import jax.numpy as jnp
from jax import lax
from jax.experimental import pallas as pl
from jax.experimental.pallas import tpu as pltpu

F32 = jnp.float32
BF16 = jnp.bfloat16

D_MODEL = 1024
ATT_HEADS = 8
ATT_HEAD_DIM = 64
HEAD_W = 2 * ATT_HEAD_DIM
ATT_QK_WIDTH = ATT_HEADS * HEAD_W
ATT_V_WIDTH = ATT_HEADS * HEAD_W
SSM_WIDTH = 512
SSM_GROUP = 16
SSM_GROUPS = SSM_WIDTH // SSM_GROUP
SSM_STATE = 64
N_STATE = SSM_GROUPS * SSM_STATE
N_BRANCH = 2
IN_WIDTH = 2 * ATT_QK_WIDTH + ATT_V_WIDTH + SSM_WIDTH + N_BRANCH * D_MODEL
N_GROUPS = 4
EXPERTS_PER_GROUP = 8
N_EXPERTS = N_GROUPS * EXPERTS_PER_GROUP
EXPERT_FF = 256
PLE_DIM = 256
EPS = 1e-6

LANES = 128
SUBLANES = 8
MXU_N = 256
VMEM_LIMIT = 56 * 1024 * 1024

TOK_TILE = 512
MOE_CAP = 160
COL_CHUNK = 512
ATT_TILE = 256
Q_SCALE = ATT_HEAD_DIM ** -0.5 * math.log2(math.e)
LAM_ROW = 0
BOUND_ROW = 1
ATT_BOUND_SLACK = 1.01
ATT_BOUND_LIMIT = 60.0
SCAN_T = 64
SCAN_LANES = 1024
ROUTE_W = LANES
ROUTE_OFF = N_GROUPS
GRP_LANE = 0
RANK_LANE = 1


def _lambda_init(layer_idx):
    return 0.8 - 0.6 * math.exp(-0.3 * layer_idx)


def _dot(a, b):
    return jnp.dot(a, b, preferred_element_type=F32)


def _const_spec(shape):
    nd = len(shape)
    return pl.BlockSpec(shape, lambda *_: (0,) * nd, pipeline_mode=pl.Buffered(1))


def _layer_weight(stacked, layer):
    shape = stacked.shape[1:]
    return pl.BlockSpec((None,) + shape, lambda *_: (layer,) + (0,) * len(shape), pipeline_mode=pl.Buffered(1))


def _round_once(pairs):
    @pl.when(pl.program_id(0) == 0)
    def _():
        for w_ref, w_scr in pairs:
            w_scr[...] = w_ref[...].astype(BF16)


def _params(*sem):
    return pltpu.CompilerParams(dimension_semantics=sem, vmem_limit_bytes=VMEM_LIMIT)


def _prep_kernel(lre_ref, lim_ref, lstep_ref, bre_ref, bim_ref, cre_ref, cim_ref, lqk_ref, qg_ref, kg_ref,
                 are_ref, aim_ref, bt_ref, ctre_ref, ctim_ref, lam_ref):
    lr = lre_ref[...]
    li = lim_ref[...]
    dt = jnp.exp(lstep_ref[...])
    mag = jnp.exp(lr * dt)
    ab_re = mag * jnp.cos(li * dt)
    ab_im = mag * jnp.sin(li * dt)
    den = lr * lr + li * li
    nr = ab_re - 1.0
    coef_re = (nr * lr + ab_im * li) / den
    coef_im = (ab_im * lr - nr * li) / den
    br = bre_ref[...]
    bi = bim_ref[...]
    are_ref[...] = ab_re
    aim_ref[...] = ab_im
    bb_re = coef_re * br - coef_im * bi
    bb_im = coef_re * bi + coef_im * br

    def spread(x, reps):
        w = x.shape[1]
        src = lax.broadcasted_iota(jnp.int32, (w, w * reps), 0)
        dst = lax.broadcasted_iota(jnp.int32, (w, w * reps), 1) & (w - 1)
        return _dot(x.astype(BF16), jnp.where(src == dst, 1.0, 0.0).astype(BF16))

    def same_group(shape, row0, row_shift, col0, col_shift):
        rows = (lax.broadcasted_iota(jnp.int32, shape, 0) + row0) >> row_shift
        cols = (lax.broadcasted_iota(jnp.int32, shape, 1) + col0) >> col_shift
        return rows == cols

    n_re = N_STATE // MXU_N
    ch_shift = SSM_GROUP.bit_length() - 1
    st_shift = SSM_STATE.bit_length() - 1
    for part, bb in enumerate((bb_re, bb_im)):
        for j in range(n_re):
            r0 = LANES * ((j * MXU_N // SSM_STATE * SSM_GROUP) // LANES)
            x = spread(bb[r0:r0 + LANES, :], MXU_N // SSM_STATE)
            keep = same_group(x.shape, r0, ch_shift, j * MXU_N, st_shift)
            bt_ref[part * n_re + j] = jnp.where(keep, x, 0.0).astype(BF16)
    half = N_STATE // 2
    for c_ref, ct_ref in ((cre_ref, ctre_ref), (cim_ref, ctim_ref)):
        for n in range(2):
            y = spread(c_ref[n * half:(n + 1) * half, :], MXU_N // SSM_GROUP)
            keep = same_group(y.shape, n * half, st_shift, n * MXU_N, ch_shift)
            ct_ref[n] = jnp.where(keep, y, 0.0).astype(BF16)

    lq = lqk_ref[...]
    s01 = jnp.sum(lq[0:1] * lq[1:2], axis=-1, keepdims=True)
    s23 = jnp.sum(lq[2:3] * lq[3:4], axis=-1, keepdims=True)
    lam_dyn = jnp.exp(s01) - jnp.exp(s23)
    g_max = jnp.max(jnp.abs(qg_ref[...]), keepdims=True) * jnp.max(jnp.abs(kg_ref[...]), keepdims=True)
    bound = g_max * (ATT_HEAD_DIM * Q_SCALE * ATT_BOUND_SLACK)
    row = lax.broadcasted_iota(jnp.int32, lam_ref.shape, 0)
    lam_ref[...] = jnp.where(row == LAM_ROW, lam_dyn, bound)


def _prep(lam_re, lam_im, log_step, b_re, b_im, c_re, c_im, lam_qk, q_gain, k_gain):
    depth = lam_re.shape[0]
    rows = SSM_WIDTH
    rep = lambda a: jnp.repeat(a, SSM_GROUP, axis=1)
    lstep = jnp.broadcast_to(log_step[:, :, None], lam_re.shape)
    tr_b = lambda b: jnp.transpose(b, (0, 1, 3, 2)).reshape(depth, rows, SSM_STATE)
    tr_c = lambda c: jnp.transpose(c, (0, 1, 3, 2)).reshape(depth, N_STATE, SSM_GROUP)
    lay = lambda *shape: pl.BlockSpec((None,) + shape, lambda i: (i,) + (0,) * len(shape))
    blk = lay(rows, SSM_STATE)
    out = jax.ShapeDtypeStruct((depth, rows, SSM_STATE), F32)
    n_bt = 2 * N_STATE // MXU_N
    return pl.pallas_call(
        _prep_kernel,
        grid=(depth,),
        in_specs=[blk, blk, blk, blk, blk, lay(N_STATE, SSM_GROUP), lay(N_STATE, SSM_GROUP),
                  lay(4, ATT_HEAD_DIM), lay(2, ATT_HEAD_DIM), lay(2, ATT_HEAD_DIM)],
        out_specs=[blk, blk, lay(n_bt, LANES, MXU_N), lay(2, N_STATE // 2, MXU_N), lay(2, N_STATE // 2, MXU_N),
                   lay(SUBLANES, LANES)],
        out_shape=[out, out, jax.ShapeDtypeStruct((depth, n_bt, LANES, MXU_N), BF16),
                   jax.ShapeDtypeStruct((depth, 2, N_STATE // 2, MXU_N), BF16),
                   jax.ShapeDtypeStruct((depth, 2, N_STATE // 2, MXU_N), BF16),
                   jax.ShapeDtypeStruct((depth, SUBLANES, LANES), F32)],
        compiler_params=_params("arbitrary"),
        name="prep",
    )(rep(lam_re), rep(lam_im), rep(lstep), tr_b(b_re), tr_b(b_im), tr_c(c_re), tr_c(c_im), lam_qk, q_gain, k_gain)


def _inproj_kernel(x_ref, g_ref, w_ref, qg_ref, kg_ref, bg_ref,
                   qt_ref, k_ref, vt_ref, u_hbm, gate_ref, w_scr, u_buf, u_sem, *, n_sb):
    _round_once([(w_ref, w_scr)])
    step = pl.program_id(0)

    def u_copy(i):
        s0 = pl.multiple_of((i % n_sb) * TOK_TILE, TOK_TILE)
        return pltpu.make_async_copy(u_buf, u_hbm.at[pl.ds(s0, TOK_TILE), i // n_sb], u_sem)

    xf = x_ref[...]
    ms = jnp.mean(xf * xf, axis=-1, keepdims=True)
    h = (xf * lax.rsqrt(ms + EPS) * g_ref[...]).astype(BF16)

    def head_norm_t(y, gain_col):
        rows = y.shape[0]
        y3 = y.T.reshape(COL_CHUNK // ATT_HEAD_DIM, ATT_HEAD_DIM, rows)
        ms = jnp.sum(y3 * y3, axis=1, keepdims=True) * (1.0 / ATT_HEAD_DIM)
        return (y3 * lax.rsqrt(ms + EPS)).reshape(COL_CHUNK, rows) * gain_col

    n_qk = ATT_QK_WIDTH // COL_CHUNK
    n_v = ATT_V_WIDTH // COL_CHUNK
    for c in range(IN_WIDTH // COL_CHUNK):
        y = _dot(h, w_scr[:, c * COL_CHUNK:(c + 1) * COL_CHUNK])
        if c < n_qk:
            qt_ref[c * COL_CHUNK:(c + 1) * COL_CHUNK, :] = (head_norm_t(y, qg_ref[...]) * Q_SCALE).astype(BF16)
        elif c < 2 * n_qk:
            cc = c - n_qk
            k_ref[:, cc * COL_CHUNK:(cc + 1) * COL_CHUNK] = head_norm_t(y, kg_ref[...]).T.astype(BF16)
        elif c < 2 * n_qk + n_v:
            cc = c - 2 * n_qk
            vt_ref[cc * COL_CHUNK:(cc + 1) * COL_CHUNK, :] = y.T.astype(BF16)
        elif c == 2 * n_qk + n_v:
            @pl.when(step > 0)
            def _():
                u_copy(step - 1).wait()
            u_buf[...] = y
        else:
            cc = c - (2 * n_qk + n_v + 1)
            b = bg_ref[:, cc * COL_CHUNK:(cc + 1) * COL_CHUNK]
            gate_ref[:, cc * COL_CHUNK:(cc + 1) * COL_CHUNK] = jax.nn.sigmoid(y + b).astype(BF16)

    u_copy(step).start()

    @pl.when(step == pl.num_programs(0) - 1)
    def _():
        u_copy(step).wait()


def _inproj(x2d, norm_g, w_in, qg, kg, b_gate, *, seq, layer):
    t = x2d.shape[0]
    n_sb = seq // TOK_TILE
    bsz = t // seq
    row = lambda w: pl.BlockSpec((TOK_TILE, w), lambda i: (i, 0))
    col = lambda w: pl.BlockSpec((w, TOK_TILE), lambda i: (i // n_sb, i % n_sb))
    return pl.pallas_call(
        functools.partial(_inproj_kernel, n_sb=n_sb),
        grid=(t // TOK_TILE,),
        in_specs=[row(D_MODEL), _const_spec((1, D_MODEL)), _layer_weight(w_in, layer),
                  _const_spec((COL_CHUNK, 1)), _const_spec((COL_CHUNK, 1)),
                  _const_spec((1, N_BRANCH * D_MODEL))],
        out_specs=[col(ATT_QK_WIDTH), row(ATT_QK_WIDTH), col(ATT_V_WIDTH),
                   pl.BlockSpec(memory_space=pl.ANY),
                   row(N_BRANCH * D_MODEL)],
        out_shape=[jax.ShapeDtypeStruct((bsz * ATT_QK_WIDTH, seq), BF16),
                   jax.ShapeDtypeStruct((t, ATT_QK_WIDTH), BF16),
                   jax.ShapeDtypeStruct((bsz * ATT_V_WIDTH, seq), BF16),
                   jax.ShapeDtypeStruct((seq, bsz, SSM_WIDTH), F32),
                   jax.ShapeDtypeStruct((t, N_BRANCH * D_MODEL), BF16)],
        scratch_shapes=[pltpu.VMEM((D_MODEL, IN_WIDTH), BF16), pltpu.VMEM((TOK_TILE, SSM_WIDTH), F32),
                        pltpu.SemaphoreType.DMA(())],
        compiler_params=_params("arbitrary"),
        name="inproj",
    )(x2d, norm_g, w_in, qg, kg, b_gate)


def _attn_kernel(sc_ref, qt_ref, k_ref, vt_ref, g_ref, o_ref, *, lam_init):
    tq = ATT_TILE
    seq = k_ref.shape[0]
    nq = seq // tq
    lam = sc_ref[LAM_ROW, 0] + lam_init
    bound = sc_ref[BOUND_ROW, 0]
    head_row = lax.broadcasted_iota(jnp.int32, (HEAD_W, tq), 0)
    key_pos = lax.broadcasted_iota(jnp.int32, (tq, 2 * tq), 0)
    qry_pos = lax.broadcasted_iota(jnp.int32, (tq, 2 * tq), 1) & (tq - 1)
    causal = key_pos <= qry_pos
    zero = jnp.zeros((HEAD_W, tq), BF16)

    def tile(qi, shift_by_bound):
        qt = qt_ref[:, qi * tq:(qi + 1) * tq]
        qst = jnp.concatenate([jnp.where(head_row < ATT_HEAD_DIM, qt, zero),
                               jnp.where(head_row >= ATT_HEAD_DIM, qt, zero)], axis=1)
        nk = qi * tq
        s_d = jnp.where(causal, _dot(k_ref[nk:nk + tq, :], qst), -jnp.inf)
        if qi > 0:
            s_m = _dot(k_ref[0:nk, :], qst)
        if shift_by_bound:
            m = bound
        else:
            m = jnp.max(s_d, axis=0, keepdims=True)
            if qi > 0:
                m = jnp.maximum(m, jnp.max(s_m, axis=0, keepdims=True))
        p_d = jnp.exp2(s_d - m)
        l = jnp.sum(p_d, axis=0, keepdims=True)
        acc = _dot(vt_ref[:, nk:nk + tq], p_d.astype(BF16))
        if qi > 0:
            p_m = jnp.exp2(s_m - m)
            l = l + jnp.sum(p_m, axis=0, keepdims=True)
            acc = acc + _dot(vt_ref[:, 0:nk], p_m.astype(BF16))
        accn = acc * (1.0 / l)
        ot = accn[:, :tq] - lam * accn[:, tq:]
        ms = jnp.mean(ot * ot, axis=0, keepdims=True)
        o = (ot * lax.rsqrt(ms + EPS)).T * g_ref[...] * (1.0 - lam_init)
        o_ref[qi * tq:(qi + 1) * tq, :] = o.astype(BF16)

    @pl.when(bound < ATT_BOUND_LIMIT)
    def _():
        for qi in range(nq):
            tile(qi, True)

    @pl.when(jnp.logical_not(bound < ATT_BOUND_LIMIT))
    def _():
        for qi in range(nq):
            tile(qi, False)


def _attention(scalars, qt, k, vt, subln_g, *, seq, lam_init):
    t = k.shape[0]
    bsz = t // seq
    fm = pl.BlockSpec((HEAD_W, seq), lambda b, h: (b * ATT_HEADS + h, 0))
    tm = pl.BlockSpec((seq, HEAD_W), lambda b, h: (b, h))
    return pl.pallas_call(
        functools.partial(_attn_kernel, lam_init=lam_init),
        grid=(bsz, ATT_HEADS),
        in_specs=[pl.BlockSpec(memory_space=pltpu.SMEM), fm, tm, fm, _const_spec((1, HEAD_W))],
        out_specs=tm,
        out_shape=jax.ShapeDtypeStruct((t, ATT_V_WIDTH), BF16),
        compiler_params=_params("parallel", "parallel"),
        name="attn",
    )(scalars, qt, k, vt, subln_g)


def _ssm_kernel(u_ref, bt_ref, cre_ref, cim_ref, are_ref, aim_ref, d_ref, z_hbm, st_scr, state_scr, z_buf, z_sem):
    n_t, bsz, _ = u_ref.shape
    rows = n_t * bsz
    chunk = pl.program_id(0)

    def z_copies(i):
        t0 = pl.multiple_of(i * n_t, n_t)
        return [pltpu.make_async_copy(z_buf.at[:, b], z_hbm.at[b, pl.ds(t0, n_t)], z_sem) for b in range(bsz)]

    @pl.when(chunk == 0)
    def _():
        state_scr[...] = jnp.zeros(state_scr.shape, F32)

    u = u_ref[...].reshape(rows, SSM_WIDTH)
    ub = u.astype(BF16)
    n_tiles = 2 * N_STATE // MXU_N
    for j in range(n_tiles):
        k0 = LANES * (((j % (n_tiles // 2)) * MXU_N // SSM_STATE * SSM_GROUP) // LANES)
        st_scr[:, j * MXU_N:(j + 1) * MXU_N] = _dot(ub[:, k0:k0 + LANES], bt_ref[j])

    for cc in range(N_STATE // SCAN_LANES):
        lo = cc * SCAN_LANES
        hi = N_STATE + lo
        a_r = jnp.broadcast_to(are_ref[:, lo:lo + SCAN_LANES], (bsz, SCAN_LANES))
        a_i = jnp.broadcast_to(aim_ref[:, lo:lo + SCAN_LANES], (bsz, SCAN_LANES))

        def step(t, carry, lo=lo, hi=hi, a_r=a_r, a_i=a_i):
            s_r, s_i = carry
            r0 = pl.multiple_of(t * bsz, bsz)
            n_r = a_r * s_r - a_i * s_i + st_scr[pl.ds(r0, bsz), lo:lo + SCAN_LANES]
            n_i = a_r * s_i + a_i * s_r + st_scr[pl.ds(r0, bsz), hi:hi + SCAN_LANES]
            st_scr[pl.ds(r0, bsz), lo:lo + SCAN_LANES] = n_r
            st_scr[pl.ds(r0, bsz), hi:hi + SCAN_LANES] = n_i
            return n_r, n_i

        s_r, s_i = lax.fori_loop(0, rows // bsz, step,
                                 (state_scr[:, lo:lo + SCAN_LANES], state_scr[:, hi:hi + SCAN_LANES]),
                                 unroll=8)
        state_scr[:, lo:lo + SCAN_LANES] = s_r
        state_scr[:, hi:hi + SCAN_LANES] = s_i

    @pl.when(chunk > 0)
    def _():
        for cp in z_copies(chunk - 1):
            cp.wait()
    half = N_STATE // 2
    for n in range(2):
        s_re = st_scr[:, n * half:(n + 1) * half].astype(BF16)
        s_im = st_scr[:, N_STATE + n * half:N_STATE + (n + 1) * half].astype(BF16)
        cols = slice(n * MXU_N, (n + 1) * MXU_N)
        y = _dot(s_re, cre_ref[n]) - _dot(s_im, cim_ref[n]) + d_ref[:, cols] * u[:, cols]
        z_buf[:, :, cols] = jax.nn.gelu(y).reshape(n_t, bsz, MXU_N)
    for cp in z_copies(chunk):
        cp.start()

    @pl.when(chunk == pl.num_programs(0) - 1)
    def _():
        for cp in z_copies(chunk):
            cp.wait()


def _ssm(u_tm, b_tiles, c_re_tiles, c_im_tiles, a_re, a_im, d_skip):
    seq, bsz, _ = u_tm.shape
    return pl.pallas_call(
        _ssm_kernel,
        grid=(seq // SCAN_T,),
        in_specs=[pl.BlockSpec((SCAN_T, bsz, SSM_WIDTH), lambda i: (i, 0, 0)),
                  _const_spec(b_tiles.shape), _const_spec(c_re_tiles.shape),
                  _const_spec(c_im_tiles.shape), _const_spec((1, N_STATE)), _const_spec((1, N_STATE)),
                  _const_spec((1, SSM_WIDTH))],
        out_specs=pl.BlockSpec(memory_space=pl.ANY),
        out_shape=jax.ShapeDtypeStruct((bsz, seq, SSM_WIDTH), F32),
        scratch_shapes=[pltpu.VMEM((SCAN_T * bsz, 2 * N_STATE), F32), pltpu.VMEM((bsz, 2 * N_STATE), F32),
                        pltpu.VMEM((SCAN_T, bsz, SSM_WIDTH), F32), pltpu.SemaphoreType.DMA(())],
        compiler_params=_params("arbitrary"),
        name="ssm",
    )(u_tm, b_tiles, c_re_tiles, c_im_tiles, a_re, a_im, d_skip)


def _route(logits, ltri):
    lane = lax.broadcasted_iota(jnp.int32, logits.shape, 1)
    lanef = lane.astype(F32)
    big = float(ROUTE_W)
    neg = -jnp.inf
    gl = jnp.where(lane < N_GROUPS, logits, neg)
    gmax = jnp.max(gl, axis=-1, keepdims=True)
    grp = jnp.min(jnp.where(gl == gmax, lanef, big), axis=-1, keepdims=True)
    p_grp = 1.0 / jnp.sum(jnp.exp(gl - gmax), axis=-1, keepdims=True)
    first = ROUTE_OFF + grp * EXPERTS_PER_GROUP
    in_grp = (lanef >= first) & (lanef < first + EXPERTS_PER_GROUP)
    el = jnp.where(in_grp, logits, neg)
    v1 = jnp.max(el, axis=-1, keepdims=True)
    i1 = jnp.min(jnp.where(el == v1, lanef, big), axis=-1, keepdims=True)
    el2 = jnp.where(lanef == i1, neg, el)
    v2 = jnp.max(el2, axis=-1, keepdims=True)
    i2 = jnp.min(jnp.where(el2 == v2, lanef, big), axis=-1, keepdims=True)
    e21 = jnp.exp(v2 - v1)
    w1 = p_grp / (1.0 + e21)
    w2 = p_grp * (e21 / (1.0 + e21))
    member = lanef == grp
    before = _dot(ltri, jnp.where(member, 1.0, 0.0).astype(BF16))
    rank = jnp.sum(jnp.where(member, before, 0.0), axis=-1, keepdims=True)
    return (jnp.where(lanef == i1, w1, 0.0) + jnp.where(lanef == i2, w2, 0.0)
            + jnp.where(lane == GRP_LANE, grp, 0.0) + jnp.where(lane == RANK_LANE, rank, 0.0))


def _mix_kernel(x_ref, o_ref, z_ref, gate_ref, wau_ref, wglu_ref, bglu_ref, wsu_ref, wout_ref,
                fg_ref, wrh_ref, wrl_ref, br_ref, ltri_ref, x1_ref, hn_ref, route_ref, route_t_ref,
                wau_scr, wglu_scr, wsu_scr, wout_scr):
    _round_once([(wau_ref, wau_scr), (wglu_ref, wglu_scr), (wsu_ref, wsu_scr), (wout_ref, wout_scr)])
    y_att = _dot(o_ref[...], wau_scr[...])
    z = z_ref[...]
    z = z * jax.nn.sigmoid(_dot(z.astype(BF16), wglu_scr[...]) + bglu_ref[...])
    y_ssm = _dot(z.astype(BF16), wsu_scr[...])
    mixed = (gate_ref[:, :D_MODEL].astype(F32) * y_att + gate_ref[:, D_MODEL:].astype(F32) * y_ssm)
    x1 = x_ref[...] + _dot(mixed.astype(BF16), wout_scr[...])
    x1_ref[...] = x1
    ms = jnp.mean(x1 * x1, axis=-1, keepdims=True)
    hn = x1 * lax.rsqrt(ms + EPS) * fg_ref[...]
    hn_hi = hn.astype(BF16)
    hn_ref[...] = hn_hi
    hn_lo = (hn - hn_hi.astype(F32)).astype(BF16)
    logits = (_dot(hn_hi, wrh_ref[...]) + _dot(hn_lo, wrh_ref[...]) + _dot(hn_hi, wrl_ref[...])
              + br_ref[...])
    route = _route(logits, ltri_ref[...])
    route_ref[...] = route
    route_t_ref[...] = route.T[:SUBLANES, :]


def _mix(x2d, o, z, gates, wau, wglu, bglu, wsu, wout, ffn_g, wr_hi, wr_lo, br, ltri, *, layer):
    t = x2d.shape[0]
    row = lambda w: pl.BlockSpec((TOK_TILE, w), lambda i: (i, 0))
    return pl.pallas_call(
        _mix_kernel,
        grid=(t // TOK_TILE,),
        in_specs=[row(D_MODEL), row(ATT_V_WIDTH), row(SSM_WIDTH), row(N_BRANCH * D_MODEL),
                  _layer_weight(wau, layer), _layer_weight(wglu, layer), _const_spec(bglu.shape),
                  _layer_weight(wsu, layer), _layer_weight(wout, layer), _const_spec(ffn_g.shape),
                  _const_spec(wr_hi.shape), _const_spec(wr_lo.shape), _const_spec(br.shape),
                  _const_spec(ltri.shape)],
        out_specs=[row(D_MODEL), row(D_MODEL), row(ROUTE_W),
                   pl.BlockSpec((SUBLANES, TOK_TILE), lambda i: (0, i))],
        out_shape=[jax.ShapeDtypeStruct((t, D_MODEL), F32),
                   jax.ShapeDtypeStruct((t, D_MODEL), BF16),
                   jax.ShapeDtypeStruct((t, ROUTE_W), F32),
                   jax.ShapeDtypeStruct((SUBLANES, t), F32)],
        scratch_shapes=[pltpu.VMEM(w.shape[1:], BF16) for w in (wau, wglu, wsu, wout)],
        compiler_params=_params("arbitrary"),
        name="mix",
    )(x2d, o, z, gates, wau, wglu, bglu, wsu, wout, ffn_g, wr_hi, wr_lo, br, ltri)


def _moe_kernel(hn_ref, route_ref, route_t_ref, wg_hbm, wu_hbm, wd_hbm, out_ref,
                wg_f32, wu_f32, wd_f32, w_sem, wg_scr, wu_scr, wd_scr, *, layer):
    n = pl.program_id(0)

    def fetch(g):
        pairs = ((wg_hbm, wg_f32), (wu_hbm, wu_f32), (wd_hbm, wd_f32))
        return [pltpu.make_async_copy(src.at[layer, g], dst, w_sem.at[k]) for k, (src, dst) in enumerate(pairs)]

    @pl.when(pl.program_id(1) == 0)
    def _():
        @pl.when(n == 0)
        def _():
            for cp in fetch(n):
                cp.start()
        for cp in fetch(n):
            cp.wait()
        for j in range(EXPERTS_PER_GROUP):
            cols = slice(j * EXPERT_FF, (j + 1) * EXPERT_FF)
            wg_scr[:, cols] = wg_f32[j].astype(BF16)
            wu_scr[:, cols] = wu_f32[j].astype(BF16)
            wd_scr[cols, :] = wd_f32[j].astype(BF16)
        @pl.when(n + 1 < N_GROUPS)
        def _():
            for cp in fetch(n + 1):
                cp.start()

    tm = hn_ref.shape[0]
    nf = n.astype(F32)
    route = route_ref[...]
    lane = lax.broadcasted_iota(jnp.int32, route.shape, 1)
    grp_c = jnp.sum(jnp.where(lane == GRP_LANE, route, 0.0), axis=-1, keepdims=True)
    rank_c = jnp.sum(jnp.where(lane == RANK_LANE, route, 0.0), axis=-1, keepdims=True)
    grp_r = route_t_ref[GRP_LANE:GRP_LANE + 1, :]
    rank_r = route_t_ref[RANK_LANE:RANK_LANE + 1, :]
    r1 = route.astype(BF16)
    r2 = (route - r1.astype(F32)).astype(BF16)
    r3 = (route - r1.astype(F32) - r2.astype(F32)).astype(BF16)
    pieces = jnp.concatenate([r1, r2, r3], axis=1)

    def group_rows(base):
        slot_r = lax.broadcasted_iota(jnp.int32, (MOE_CAP, tm), 0).astype(F32) + base
        take = jnp.where((grp_r == nf) & (rank_r == slot_r), 1.0, 0.0).astype(BF16)
        slot_c = lax.broadcasted_iota(jnp.int32, (tm, MOE_CAP), 1).astype(F32) + base
        put = jnp.where((grp_c == nf) & (rank_c == slot_c), 1.0, 0.0).astype(BF16)
        xc = _dot(take, hn_ref[...]).astype(BF16)
        rc = _dot(take, pieces)
        rc = rc[:, :ROUTE_W] + rc[:, ROUTE_W:2 * ROUTE_W] + rc[:, 2 * ROUTE_W:]
        lane_c = lax.broadcasted_iota(jnp.int32, rc.shape, 1)
        first = ROUTE_OFF + n * EXPERTS_PER_GROUP
        act = jax.nn.silu(_dot(xc, wg_scr[...])) * _dot(xc, wu_scr[...])
        blocks = []
        for j in range(EXPERTS_PER_GROUP):
            w_j = jnp.sum(jnp.where(lane_c == first + j, rc, 0.0), axis=-1, keepdims=True)
            blocks.append((act[:, j * EXPERT_FF:(j + 1) * EXPERT_FF] * w_j).astype(BF16))
        y = _dot(jnp.concatenate(blocks, axis=1), wd_scr[...])
        return _dot(put, y.astype(BF16))

    out_ref[...] = group_rows(jnp.float32(0.0)).astype(BF16)

    n_rows = jnp.max(jnp.where(grp_c == nf, rank_c + 1.0, 0.0))
    n_pass = lax.div(n_rows.astype(jnp.int32) + (MOE_CAP - 1), MOE_CAP)

    def extra_pass(k, carry):
        more = group_rows((k * MOE_CAP).astype(F32))
        out_ref[...] = (out_ref[...].astype(F32) + more).astype(BF16)
        return carry

    lax.fori_loop(1, n_pass, extra_pass, 0)


def _moe(hn, route, route_t, w_gate, w_up, w_down, *, layer):
    t = hn.shape[0]
    row = lambda w: pl.BlockSpec((TOK_TILE, w), lambda n, i: (i, 0))
    up_shape = (EXPERTS_PER_GROUP, D_MODEL, EXPERT_FF)
    down_shape = (EXPERTS_PER_GROUP, EXPERT_FF, D_MODEL)
    ff = EXPERTS_PER_GROUP * EXPERT_FF
    return pl.pallas_call(
        functools.partial(_moe_kernel, layer=layer),
        grid=(N_GROUPS, t // TOK_TILE),
        in_specs=[row(D_MODEL), row(ROUTE_W), pl.BlockSpec((SUBLANES, TOK_TILE), lambda n, i: (0, i)),
                  pl.BlockSpec(memory_space=pl.ANY), pl.BlockSpec(memory_space=pl.ANY),
                  pl.BlockSpec(memory_space=pl.ANY)],
        out_specs=pl.BlockSpec((None, TOK_TILE, D_MODEL), lambda n, i: (n, i, 0)),
        out_shape=jax.ShapeDtypeStruct((N_GROUPS, t, D_MODEL), BF16),
        scratch_shapes=[pltpu.VMEM(up_shape, F32), pltpu.VMEM(up_shape, F32), pltpu.VMEM(down_shape, F32),
                        pltpu.SemaphoreType.DMA((3,)),
                        pltpu.VMEM((D_MODEL, ff), BF16), pltpu.VMEM((D_MODEL, ff), BF16),
                        pltpu.VMEM((ff, D_MODEL), BF16)],
        compiler_params=_params("arbitrary", "arbitrary"),
        name="moe",
    )(hn, route, route_t, w_gate, w_up, w_down)


def _ple_kernel(x1_ref, c_ref, p_ref, pg_ref, wpg_ref, wple_ref, out_ref, wpg_scr, wple_scr):
    _round_once([(wpg_ref, wpg_scr), (wple_ref, wple_scr)])
    x2 = x1_ref[...]
    for n in range(N_GROUPS):
        x2 = x2 + c_ref[n].astype(F32)
    ms = jnp.mean(x2 * x2, axis=-1, keepdims=True)
    hp = (x2 * lax.rsqrt(ms + EPS) * pg_ref[...]).astype(BF16)
    gate = jax.nn.sigmoid(_dot(hp, wpg_scr[...]))
    out_ref[...] = x2 + gate * _dot(p_ref[...].astype(BF16), wple_scr[...])


def _ple(x1, contrib, p, ple_g, wpg, wple, *, layer):
    t = x1.shape[0]
    n_sb = p.shape[2] // TOK_TILE
    row = lambda w: pl.BlockSpec((TOK_TILE, w), lambda i: (i, 0))
    return pl.pallas_call(
        _ple_kernel,
        grid=(t // TOK_TILE,),
        in_specs=[row(D_MODEL), pl.BlockSpec((N_GROUPS, TOK_TILE, D_MODEL), lambda i: (0, i, 0)),
                  pl.BlockSpec((None, None, TOK_TILE, PLE_DIM), lambda i: (layer, i // n_sb, i % n_sb, 0)),
                  _const_spec(ple_g.shape), _layer_weight(wpg, layer), _layer_weight(wple, layer)],
        out_specs=row(D_MODEL),
        out_shape=jax.ShapeDtypeStruct((t, D_MODEL), F32),
        scratch_shapes=[pltpu.VMEM(wpg.shape[1:], BF16), pltpu.VMEM(wple.shape[1:], BF16)],
        compiler_params=_params("arbitrary"),
        name="ple",
    )(x1, contrib, p, ple_g, wpg, wple)


def kernel(x, p, attn_norm_g, w_in, b_gate, q_norm_g, k_norm_g, lam_qk, subln_g, w_attn_up, ssm_lam_re, ssm_lam_im, ssm_log_step, ssm_b_re, ssm_b_im, ssm_c_re, ssm_c_im, ssm_d, w_glu, b_glu, w_ssm_up, w_out, ffn_norm_g, w_router_group, b_router_group, w_router_expert, b_router_expert, w_exp_gate, w_exp_up, w_exp_down, ple_norm_g, w_ple_gate, w_ple):
    bsz, seq, d = x.shape
    depth = w_in.shape[0]
    t = bsz * seq
    assert d == D_MODEL and seq % TOK_TILE == 0 and seq % ATT_TILE == 0
    assert seq % SCAN_T == 0 and bsz == SUBLANES
    tok = jnp.arange(TOK_TILE)
    ltri = (tok[None, :] < tok[:, None]).astype(BF16)

    a_re, a_im, b_tiles, c_re_tiles, c_im_tiles, scalars = _prep(
        ssm_lam_re, ssm_lam_im, ssm_log_step, ssm_b_re, ssm_b_im, ssm_c_re, ssm_c_im, lam_qk, q_norm_g, k_norm_g)
    x2d = x.reshape(t, d)
    for i in range(depth):
        lam_init = _lambda_init(i)
        tile_gain = lambda g: jnp.tile(g.reshape(HEAD_W, 1), (COL_CHUNK // HEAD_W, 1))
        qt, k, vt, u_tm, gates = _inproj(
            x2d, attn_norm_g[i].reshape(1, d), w_in,
            tile_gain(q_norm_g[i]), tile_gain(k_norm_g[i]), b_gate[i].reshape(1, -1), seq=seq, layer=i)

        o = _attention(scalars[i], qt, k, vt, subln_g[i].reshape(1, HEAD_W), seq=seq, lam_init=lam_init)

        z = _ssm(u_tm, b_tiles[i], c_re_tiles[i], c_im_tiles[i],
                 a_re[i, ::SSM_GROUP].reshape(1, N_STATE), a_im[i, ::SSM_GROUP].reshape(1, N_STATE),
                 ssm_d[i].reshape(1, SSM_WIDTH))

        w_r = jnp.concatenate(
            [w_router_group[i], jnp.transpose(w_router_expert[i], (1, 0, 2)).reshape(d, N_EXPERTS),
             jnp.zeros((d, ROUTE_W - N_GROUPS - N_EXPERTS), F32)], axis=1)
        w_r_hi = w_r.astype(BF16)
        w_r_lo = (w_r - w_r_hi.astype(F32)).astype(BF16)
        b_r = jnp.concatenate([b_router_group[i], b_router_expert[i].reshape(-1),
                               jnp.zeros((ROUTE_W - N_GROUPS - N_EXPERTS,), F32)]).reshape(1, ROUTE_W)
        x1, hn, route, route_t = _mix(
            x2d, o, z.reshape(t, SSM_WIDTH), gates,
            w_attn_up, w_glu, b_glu[i].reshape(1, -1), w_ssm_up, w_out, ffn_norm_g[i].reshape(1, d),
            w_r_hi, w_r_lo, b_r, ltri, layer=i)

        contrib = _moe(hn, route, route_t, w_exp_gate, w_exp_up, w_exp_down, layer=i)
        x2d = _ple(x1, contrib, p, ple_norm_g[i].reshape(1, d), w_ple_gate, w_ple, layer=i)
    return x2d.reshape(bsz, seq, d)
```

```python
import functools
import math

import jax
import jax.numpy as jnp
from jax import lax
from jax.experimental import pallas as pl
from jax.experimental.pallas import tpu as pltpu

F32 = jnp.float32
BF16 = jnp.bfloat16

D_MODEL = 1024
ATT_HEADS = 8
ATT_HEAD_DIM = 64
HEAD_W = 2 * ATT_HEAD_DIM
ATT_QK_WIDTH = ATT_HEADS * HEAD_W
ATT_V_WIDTH = ATT_HEADS * HEAD_W
SSM_WIDTH = 512
SSM_GROUP = 16
SSM_GROUPS = SSM_WIDTH // SSM_GROUP
SSM_STATE = 64
N_STATE = SSM_GROUPS * SSM_STATE
N_BRANCH = 2
IN_WIDTH = 2 * ATT_QK_WIDTH + ATT_V_WIDTH + SSM_WIDTH + N_BRANCH * D_MODEL
N_GROUPS = 4
EXPERTS_PER_GROUP = 8
N_EXPERTS = N_GROUPS * EXPERTS_PER_GROUP
EXPERT_FF = 256
PLE_DIM = 256
EPS = 1e-6

LANES = 128
SUBLANES = 8
MXU_N = 256
VMEM_LIMIT = 56 * 1024 * 1024

TOK_TILE = 512
MOE_CAP = 160
COL_CHUNK = 512
ATT_TILE = 256
Q_SCALE = ATT_HEAD_DIM ** -0.5 * math.log2(math.e)
LAM_ROW = 0
BOUND_ROW = 1
ATT_BOUND_SLACK = 1.01
ATT_BOUND_LIMIT = 60.0
SCAN_T = 64
SCAN_LANES = 1024
ROUTE_W = LANES
ROUTE_OFF = N_GROUPS
GRP_LANE = 0
RANK_LANE = 1


def _lambda_init(layer_idx):
    return 0.8 - 0.6 * math.exp(-0.3 * layer_idx)


def _dot(a, b):
    return jnp.dot(a, b, preferred_element_type=F32)


def _const_spec(shape):
    nd = len(shape)
    return pl.BlockSpec(shape, lambda *_: (0,) * nd, pipeline_mode=pl.Buffered(1))


def _layer_weight(stacked, layer):
    shape = stacked.shape[1:]
    return pl.BlockSpec((None,) + shape, lambda *_: (layer,) + (0,) * len(shape), pipeline_mode=pl.Buffered(1))


def _round_once(pairs):
    @pl.when(pl.program_id(0) == 0)
    def _():
        for w_ref, w_scr in pairs:
            w_scr[...] = w_ref[...].astype(BF16)


def _params(*sem):
    return pltpu.CompilerParams(dimension_semantics=sem, vmem_limit_bytes=VMEM_LIMIT)


def _prep_kernel(lre_ref, lim_ref, lstep_ref, bre_ref, bim_ref, cre_ref, cim_ref, lqk_ref, qg_ref, kg_ref,
                 are_ref, aim_ref, bt_ref, ctre_ref, ctim_ref, lam_ref):
    lr = lre_ref[...]
    li = lim_ref[...]
    dt = jnp.exp(lstep_ref[...])
    mag = jnp.exp(lr * dt)
    ab_re = mag * jnp.cos(li * dt)
    ab_im = mag * jnp.sin(li * dt)
    den = lr * lr + li * li
    nr = ab_re - 1.0
    coef_re = (nr * lr + ab_im * li) / den
    coef_im = (ab_im * lr - nr * li) / den
    br = bre_ref[...]
    bi = bim_ref[...]
    are_ref[...] = ab_re
    aim_ref[...] = ab_im
    bb_re = coef_re * br - coef_im * bi
    bb_im = coef_re * bi + coef_im * br

    def spread(x, reps):
        w = x.shape[1]
        src = lax.broadcasted_iota(jnp.int32, (w, w * reps), 0)
        dst = lax.broadcasted_iota(jnp.int32, (w, w * reps), 1) & (w - 1)
        return _dot(x.astype(BF16), jnp.where(src == dst, 1.0, 0.0).astype(BF16))

    def same_group(shape, row0, row_shift, col0, col_shift):
        rows = (lax.broadcasted_iota(jnp.int32, shape, 0) + row0) >> row_shift
        cols = (lax.broadcasted_iota(jnp.int32, shape, 1) + col0) >> col_shift
        return rows == cols

    n_re = N_STATE // MXU_N
    ch_shift = SSM_GROUP.bit_length() - 1
    st_shift = SSM_STATE.bit_length() - 1
    for part, bb in enumerate((bb_re, bb_im)):
        for j in range(n_re):
            r0 = LANES * ((j * MXU_N // SSM_STATE * SSM_GROUP) // LANES)
            x = spread(bb[r0:r0 + LANES, :], MXU_N // SSM_STATE)
            keep = same_group(x.shape, r0, ch_shift, j * MXU_N, st_shift)
            bt_ref[part * n_re + j] = jnp.where(keep, x, 0.0).astype(BF16)
    half = N_STATE // 2
    for c_ref, ct_ref in ((cre_ref, ctre_ref), (cim_ref, ctim_ref)):
        for n in range(2):
            y = spread(c_ref[n * half:(n + 1) * half, :], MXU_N // SSM_GROUP)
            keep = same_group(y.shape, n * half, st_shift, n * MXU_N, ch_shift)
            ct_ref[n] = jnp.where(keep, y, 0.0).astype(BF16)

    lq = lqk_ref[...]
    s01 = jnp.sum(lq[0:1] * lq[1:2], axis=-1, keepdims=True)
    s23 = jnp.sum(lq[2:3] * lq[3:4], axis=-1, keepdims=True)
    lam_dyn = jnp.exp(s01) - jnp.exp(s23)
    g_max = jnp.max(jnp.abs(qg_ref[...]), keepdims=True) * jnp.max(jnp.abs(kg_ref[...]), keepdims=True)
    bound = g_max * (ATT_HEAD_DIM * Q_SCALE * ATT_BOUND_SLACK)
    row = lax.broadcasted_iota(jnp.int32, lam_ref.shape, 0)
    lam_ref[...] = jnp.where(row == LAM_ROW, lam_dyn, bound)


def _prep(lam_re, lam_im, log_step, b_re, b_im, c_re, c_im, lam_qk, q_gain, k_gain):
    depth = lam_re.shape[0]
    rows = SSM_WIDTH
    rep = lambda a: jnp.repeat(a, SSM_GROUP, axis=1)
    lstep = jnp.broadcast_to(log_step[:, :, None], lam_re.shape)
    tr_b = lambda b: jnp.transpose(b, (0, 1, 3, 2)).reshape(depth, rows, SSM_STATE)
    tr_c = lambda c: jnp.transpose(c, (0, 1, 3, 2)).reshape(depth, N_STATE, SSM_GROUP)
    lay = lambda *shape: pl.BlockSpec((None,) + shape, lambda i: (i,) + (0,) * len(shape))
    blk = lay(rows, SSM_STATE)
    out = jax.ShapeDtypeStruct((depth, rows, SSM_STATE), F32)
    n_bt = 2 * N_STATE // MXU_N
    return pl.pallas_call(
        _prep_kernel,
        grid=(depth,),
        in_specs=[blk, blk, blk, blk, blk, lay(N_STATE, SSM_GROUP), lay(N_STATE, SSM_GROUP),
                  lay(4, ATT_HEAD_DIM), lay(2, ATT_HEAD_DIM), lay(2, ATT_HEAD_DIM)],
        out_specs=[blk, blk, lay(n_bt, LANES, MXU_N), lay(2, N_STATE // 2, MXU_N), lay(2, N_STATE // 2, MXU_N),
                   lay(SUBLANES, LANES)],
        out_shape=[out, out, jax.ShapeDtypeStruct((depth, n_bt, LANES, MXU_N), BF16),
                   jax.ShapeDtypeStruct((depth, 2, N_STATE // 2, MXU_N), BF16),
                   jax.ShapeDtypeStruct((depth, 2, N_STATE // 2, MXU_N), BF16),
                   jax.ShapeDtypeStruct((depth, SUBLANES, LANES), F32)],
        compiler_params=_params("arbitrary"),
        name="prep",
    )(rep(lam_re), rep(lam_im), rep(lstep), tr_b(b_re), tr_b(b_im), tr_c(c_re), tr_c(c_im), lam_qk, q_gain, k_gain)


def _inproj_kernel(x_ref, g_ref, w_ref, qg_ref, kg_ref, bg_ref,
                   qt_ref, k_ref, vt_ref, u_hbm, gate_ref, w_scr, u_buf, u_sem, *, n_sb):
    _round_once([(w_ref, w_scr)])
    step = pl.program_id(0)

    def u_copy(i):
        s0 = pl.multiple_of((i % n_sb) * TOK_TILE, TOK_TILE)
        return pltpu.make_async_copy(u_buf, u_hbm.at[pl.ds(s0, TOK_TILE), i // n_sb], u_sem)

    xf = x_ref[...]
    ms = jnp.mean(xf * xf, axis=-1, keepdims=True)
    h = (xf * lax.rsqrt(ms + EPS) * g_ref[...]).astype(BF16)

    def head_norm_t(y, gain_col):
        rows = y.shape[0]
        y3 = y.T.reshape(COL_CHUNK // ATT_HEAD_DIM, ATT_HEAD_DIM, rows)
        ms = jnp.sum(y3 * y3, axis=1, keepdims=True) * (1.0 / ATT_HEAD_DIM)
        return (y3 * lax.rsqrt(ms + EPS)).reshape(COL_CHUNK, rows) * gain_col

    n_qk = ATT_QK_WIDTH // COL_CHUNK
    n_v = ATT_V_WIDTH // COL_CHUNK
    for c in range(IN_WIDTH // COL_CHUNK):
        y = _dot(h, w_scr[:, c * COL_CHUNK:(c + 1) * COL_CHUNK])
        if c < n_qk:
            qt_ref[c * COL_CHUNK:(c + 1) * COL_CHUNK, :] = (head_norm_t(y, qg_ref[...]) * Q_SCALE).astype(BF16)
        elif c < 2 * n_qk:
            cc = c - n_qk
            k_ref[:, cc * COL_CHUNK:(cc + 1) * COL_CHUNK] = head_norm_t(y, kg_ref[...]).T.astype(BF16)
        elif c < 2 * n_qk + n_v:
            cc = c - 2 * n_qk
            vt_ref[cc * COL_CHUNK:(cc + 1) * COL_CHUNK, :] = y.T.astype(BF16)
        elif c == 2 * n_qk + n_v:
            @pl.when(step > 0)
            def _():
                u_copy(step - 1).wait()
            u_buf[...] = y
        else:
            cc = c - (2 * n_qk + n_v + 1)
            b = bg_ref[:, cc * COL_CHUNK:(cc + 1) * COL_CHUNK]
            gate_ref[:, cc * COL_CHUNK:(cc + 1) * COL_CHUNK] = jax.nn.sigmoid(y + b).astype(BF16)

    u_copy(step).start()

    @pl.when(step == pl.num_programs(0) - 1)
    def _():
        u_copy(step).wait()


def _inproj(x2d, norm_g, w_in, qg, kg, b_gate, *, seq, layer):
    t = x2d.shape[0]
    n_sb = seq // TOK_TILE
    bsz = t // seq
    row = lambda w: pl.BlockSpec((TOK_TILE, w), lambda i: (i, 0))
    col = lambda w: pl.BlockSpec((w, TOK_TILE), lambda i: (i // n_sb, i % n_sb))
    return pl.pallas_call(
        functools.partial(_inproj_kernel, n_sb=n_sb),
        grid=(t // TOK_TILE,),
        in_specs=[row(D_MODEL), _const_spec((1, D_MODEL)), _layer_weight(w_in, layer),
                  _const_spec((COL_CHUNK, 1)), _const_spec((COL_CHUNK, 1)),
                  _const_spec((1, N_BRANCH * D_MODEL))],
        out_specs=[col(ATT_QK_WIDTH), row(ATT_QK_WIDTH), col(ATT_V_WIDTH),
                   pl.BlockSpec(memory_space=pl.ANY),
                   row(N_BRANCH * D_MODEL)],
        out_shape=[jax.ShapeDtypeStruct((bsz * ATT_QK_WIDTH, seq), BF16),
                   jax.ShapeDtypeStruct((t, ATT_QK_WIDTH), BF16),
                   jax.ShapeDtypeStruct((bsz * ATT_V_WIDTH, seq), BF16),
                   jax.ShapeDtypeStruct((seq, bsz, SSM_WIDTH), F32),
                   jax.ShapeDtypeStruct((t, N_BRANCH * D_MODEL), BF16)],
        scratch_shapes=[pltpu.VMEM((D_MODEL, IN_WIDTH), BF16), pltpu.VMEM((TOK_TILE, SSM_WIDTH), F32),
                        pltpu.SemaphoreType.DMA(())],
        compiler_params=_params("arbitrary"),
        name="inproj",
    )(x2d, norm_g, w_in, qg, kg, b_gate)


def _attn_kernel(sc_ref, qt_ref, k_ref, vt_ref, g_ref, o_ref, *, lam_init):
    tq = ATT_TILE
    seq = k_ref.shape[0]
    nq = seq // tq
    lam = sc_ref[LAM_ROW, 0] + lam_init
    bound = sc_ref[BOUND_ROW, 0]
    head_row = lax.broadcasted_iota(jnp.int32, (HEAD_W, tq), 0)
    key_pos = lax.broadcasted_iota(jnp.int32, (tq, 2 * tq), 0)
    qry_pos = lax.broadcasted_iota(jnp.int32, (tq, 2 * tq), 1) & (tq - 1)
    causal = key_pos <= qry_pos
    zero = jnp.zeros((HEAD_W, tq), BF16)

    def tile(qi, shift_by_bound):
        qt = qt_ref[:, qi * tq:(qi + 1) * tq]
        qst = jnp.concatenate([jnp.where(head_row < ATT_HEAD_DIM, qt, zero),
                               jnp.where(head_row >= ATT_HEAD_DIM, qt, zero)], axis=1)
        nk = qi * tq
        s_d = jnp.where(causal, _dot(k_ref[nk:nk + tq, :], qst), -jnp.inf)
        if qi > 0:
            s_m = _dot(k_ref[0:nk, :], qst)
        if shift_by_bound:
            m = bound
        else:
            m = jnp.max(s_d, axis=0, keepdims=True)
            if qi > 0:
                m = jnp.maximum(m, jnp.max(s_m, axis=0, keepdims=True))
        p_d = jnp.exp2(s_d - m)
        l = jnp.sum(p_d, axis=0, keepdims=True)
        acc = _dot(vt_ref[:, nk:nk + tq], p_d.astype(BF16))
        if qi > 0:
            p_m = jnp.exp2(s_m - m)
            l = l + jnp.sum(p_m, axis=0, keepdims=True)
            acc = acc + _dot(vt_ref[:, 0:nk], p_m.astype(BF16))
        accn = acc * (1.0 / l)
        ot = accn[:, :tq] - lam * accn[:, tq:]
        ms = jnp.mean(ot * ot, axis=0, keepdims=True)
        o = (ot * lax.rsqrt(ms + EPS)).T * g_ref[...] * (1.0 - lam_init)
        o_ref[qi * tq:(qi + 1) * tq, :] = o.astype(BF16)

    @pl.when(bound < ATT_BOUND_LIMIT)
    def _():
        for qi in range(nq):
            tile(qi, True)

    @pl.when(jnp.logical_not(bound < ATT_BOUND_LIMIT))
    def _():
        for qi in range(nq):
            tile(qi, False)


def _attention(scalars, qt, k, vt, subln_g, *, seq, lam_init):
    t = k.shape[0]
    bsz = t // seq
    fm = pl.BlockSpec((HEAD_W, seq), lambda b, h: (b * ATT_HEADS + h, 0))
    tm = pl.BlockSpec((seq, HEAD_W), lambda b, h: (b, h))
    return pl.pallas_call(
        functools.partial(_attn_kernel, lam_init=lam_init),
        grid=(bsz, ATT_HEADS),
        in_specs=[pl.BlockSpec(memory_space=pltpu.SMEM), fm, tm, fm, _const_spec((1, HEAD_W))],
        out_specs=tm,
        out_shape=jax.ShapeDtypeStruct((t, ATT_V_WIDTH), BF16),
        compiler_params=_params("parallel", "parallel"),
        name="attn",
    )(scalars, qt, k, vt, subln_g)


def _ssm_kernel(u_ref, bt_ref, cre_ref, cim_ref, are_ref, aim_ref, d_ref, z_hbm, st_scr, state_scr, z_buf, z_sem):
    n_t, bsz, _ = u_ref.shape
    rows = n_t * bsz
    chunk = pl.program_id(0)

    def z_copies(i):
        t0 = pl.multiple_of(i * n_t, n_t)
        return [pltpu.make_async_copy(z_buf.at[:, b], z_hbm.at[b, pl.ds(t0, n_t)], z_sem) for b in range(bsz)]

    @pl.when(chunk == 0)
    def _():
        state_scr[...] = jnp.zeros(state_scr.shape, F32)

    u = u_ref[...].reshape(rows, SSM_WIDTH)
    ub = u.astype(BF16)
    n_tiles = 2 * N_STATE // MXU_N
    for j in range(n_tiles):
        k0 = LANES * (((j % (n_tiles // 2)) * MXU_N // SSM_STATE * SSM_GROUP) // LANES)
        st_scr[:, j * MXU_N:(j + 1) * MXU_N] = _dot(ub[:, k0:k0 + LANES], bt_ref[j])

    for cc in range(N_STATE // SCAN_LANES):
        lo = cc * SCAN_LANES
        hi = N_STATE + lo
        a_r = jnp.broadcast_to(are_ref[:, lo:lo + SCAN_LANES], (bsz, SCAN_LANES))
        a_i = jnp.broadcast_to(aim_ref[:, lo:lo + SCAN_LANES], (bsz, SCAN_LANES))

        def step(t, carry, lo=lo, hi=hi, a_r=a_r, a_i=a_i):
            s_r, s_i = carry
            r0 = pl.multiple_of(t * bsz, bsz)
            n_r = a_r * s_r - a_i * s_i + st_scr[pl.ds(r0, bsz), lo:lo + SCAN_LANES]
            n_i = a_r * s_i + a_i * s_r + st_scr[pl.ds(r0, bsz), hi:hi + SCAN_LANES]
            st_scr[pl.ds(r0, bsz), lo:lo + SCAN_LANES] = n_r
            st_scr[pl.ds(r0, bsz), hi:hi + SCAN_LANES] = n_i
            return n_r, n_i

        s_r, s_i = lax.fori_loop(0, rows // bsz, step,
                                 (state_scr[:, lo:lo + SCAN_LANES], state_scr[:, hi:hi + SCAN_LANES]),
                                 unroll=8)
        state_scr[:, lo:lo + SCAN_LANES] = s_r
        state_scr[:, hi:hi + SCAN_LANES] = s_i

    @pl.when(chunk > 0)
    def _():
        for cp in z_copies(chunk - 1):
            cp.wait()
    half = N_STATE // 2
    for n in range(2):
        s_re = st_scr[:, n * half:(n + 1) * half].astype(BF16)
        s_im = st_scr[:, N_STATE + n * half:N_STATE + (n + 1) * half].astype(BF16)
        cols = slice(n * MXU_N, (n + 1) * MXU_N)
        y = _dot(s_re, cre_ref[n]) - _dot(s_im, cim_ref[n]) + d_ref[:, cols] * u[:, cols]
        z_buf[:, :, cols] = jax.nn.gelu(y).reshape(n_t, bsz, MXU_N)
    for cp in z_copies(chunk):
        cp.start()

    @pl.when(chunk == pl.num_programs(0) - 1)
    def _():
        for cp in z_copies(chunk):
            cp.wait()


def _ssm(u_tm, b_tiles, c_re_tiles, c_im_tiles, a_re, a_im, d_skip):
    seq, bsz, _ = u_tm.shape
    return pl.pallas_call(
        _ssm_kernel,
        grid=(seq // SCAN_T,),
        in_specs=[pl.BlockSpec((SCAN_T, bsz, SSM_WIDTH), lambda i: (i, 0, 0)),
                  _const_spec(b_tiles.shape), _const_spec(c_re_tiles.shape),
                  _const_spec(c_im_tiles.shape), _const_spec((1, N_STATE)), _const_spec((1, N_STATE)),
                  _const_spec((1, SSM_WIDTH))],
        out_specs=pl.BlockSpec(memory_space=pl.ANY),
        out_shape=jax.ShapeDtypeStruct((bsz, seq, SSM_WIDTH), F32),
        scratch_shapes=[pltpu.VMEM((SCAN_T * bsz, 2 * N_STATE), F32), pltpu.VMEM((bsz, 2 * N_STATE), F32),
                        pltpu.VMEM((SCAN_T, bsz, SSM_WIDTH), F32), pltpu.SemaphoreType.DMA(())],
        compiler_params=_params("arbitrary"),
        name="ssm",
    )(u_tm, b_tiles, c_re_tiles, c_im_tiles, a_re, a_im, d_skip)


def _route(logits, ltri):
    lane = lax.broadcasted_iota(jnp.int32, logits.shape, 1)
    lanef = lane.astype(F32)
    big = float(ROUTE_W)
    neg = -jnp.inf
    gl = jnp.where(lane < N_GROUPS, logits, neg)
    gmax = jnp.max(gl, axis=-1, keepdims=True)
    grp = jnp.min(jnp.where(gl == gmax, lanef, big), axis=-1, keepdims=True)
    p_grp = 1.0 / jnp.sum(jnp.exp(gl - gmax), axis=-1, keepdims=True)
    first = ROUTE_OFF + grp * EXPERTS_PER_GROUP
    in_grp = (lanef >= first) & (lanef < first + EXPERTS_PER_GROUP)
    el = jnp.where(in_grp, logits, neg)
    v1 = jnp.max(el, axis=-1, keepdims=True)
    i1 = jnp.min(jnp.where(el == v1, lanef, big), axis=-1, keepdims=True)
    el2 = jnp.where(lanef == i1, neg, el)
    v2 = jnp.max(el2, axis=-1, keepdims=True)
    i2 = jnp.min(jnp.where(el2 == v2, lanef, big), axis=-1, keepdims=True)
    e21 = jnp.exp(v2 - v1)
    w1 = p_grp / (1.0 + e21)
    w2 = p_grp * (e21 / (1.0 + e21))
    member = lanef == grp
    before = _dot(ltri, jnp.where(member, 1.0, 0.0).astype(BF16))
    rank = jnp.sum(jnp.where(member, before, 0.0), axis=-1, keepdims=True)
    counts = jnp.sum(jnp.where(member, 1.0, 0.0), axis=0, keepdims=True)
    return (jnp.where(lanef == i1, w1, 0.0) + jnp.where(lanef == i2, w2, 0.0)
            + jnp.where(lane == GRP_LANE, grp, 0.0) + jnp.where(lane == RANK_LANE, rank, 0.0)), counts


def _mix_kernel(x_ref, o_ref, z_ref, gate_ref, wau_ref, wglu_ref, bglu_ref, wsu_ref, wout_ref,
                fg_ref, wrh_ref, wrl_ref, br_ref, ltri_ref, x1_ref, hn_ref, route_ref, route_t_ref, cnt_ref,
                wau_scr, wglu_scr, wsu_scr, wout_scr):
    _round_once([(wau_ref, wau_scr), (wglu_ref, wglu_scr), (wsu_ref, wsu_scr), (wout_ref, wout_scr)])
    y_att = _dot(o_ref[...], wau_scr[...])
    z = z_ref[...]
    z = z * jax.nn.sigmoid(_dot(z.astype(BF16), wglu_scr[...]) + bglu_ref[...])
    y_ssm = _dot(z.astype(BF16), wsu_scr[...])
    mixed = (gate_ref[:, :D_MODEL].astype(F32) * y_att + gate_ref[:, D_MODEL:].astype(F32) * y_ssm)
    x1 = x_ref[...] + _dot(mixed.astype(BF16), wout_scr[...])
    x1_ref[...] = x1
    ms = jnp.mean(x1 * x1, axis=-1, keepdims=True)
    hn = x1 * lax.rsqrt(ms + EPS) * fg_ref[...]
    hn_hi = hn.astype(BF16)
    hn_ref[...] = hn_hi
    hn_lo = (hn - hn_hi.astype(F32)).astype(BF16)
    logits = (_dot(hn_hi, wrh_ref[...]) + _dot(hn_lo, wrh_ref[...]) + _dot(hn_hi, wrl_ref[...])
              + br_ref[...])
    route, counts = _route(logits, ltri_ref[...])
    route_ref[...] = route
    route_t_ref[...] = route.T[:SUBLANES, :]
    cnt_ref[...] = jnp.broadcast_to(counts, cnt_ref.shape)


def _mix(x2d, o, z, gates, wau, wglu, bglu, wsu, wout, ffn_g, wr_hi, wr_lo, br, ltri, *, layer):
    t = x2d.shape[0]
    row = lambda w: pl.BlockSpec((TOK_TILE, w), lambda i: (i, 0))
    return pl.pallas_call(
        _mix_kernel,
        grid=(t // TOK_TILE,),
        in_specs=[row(D_MODEL), row(ATT_V_WIDTH), row(SSM_WIDTH), row(N_BRANCH * D_MODEL),
                  _layer_weight(wau, layer), _layer_weight(wglu, layer), _const_spec(bglu.shape),
                  _layer_weight(wsu, layer), _layer_weight(wout, layer), _const_spec(ffn_g.shape),
                  _const_spec(wr_hi.shape), _const_spec(wr_lo.shape), _const_spec(br.shape),
                  _const_spec(ltri.shape)],
        out_specs=[row(D_MODEL), row(D_MODEL), row(ROUTE_W),
                   pl.BlockSpec((SUBLANES, TOK_TILE), lambda i: (0, i)),
                   pl.BlockSpec((None, SUBLANES, ROUTE_W), lambda i: (i, 0, 0))],
        out_shape=[jax.ShapeDtypeStruct((t, D_MODEL), F32),
                   jax.ShapeDtypeStruct((t, D_MODEL), BF16),
                   jax.ShapeDtypeStruct((t, ROUTE_W), F32),
                   jax.ShapeDtypeStruct((SUBLANES, t), F32),
                   jax.ShapeDtypeStruct((t // TOK_TILE, SUBLANES, ROUTE_W), F32)],
        scratch_shapes=[pltpu.VMEM(w.shape[1:], BF16) for w in (wau, wglu, wsu, wout)],
        compiler_params=_params("arbitrary"),
        name="mix",
    )(x2d, o, z, gates, wau, wglu, bglu, wsu, wout, ffn_g, wr_hi, wr_lo, br, ltri)


def _moe_kernel(cnt_ref, hn_ref, route_ref, route_t_ref, wg_hbm, wu_hbm, wd_hbm, out_ref,
                wg_f32, wu_f32, wd_f32, w_sem, wg_scr, wu_scr, wd_scr, *, layer):
    n = pl.program_id(0)

    def fetch(g):
        pairs = ((wg_hbm, wg_f32), (wu_hbm, wu_f32), (wd_hbm, wd_f32))
        return [pltpu.make_async_copy(src.at[layer, g], dst, w_sem.at[k]) for k, (src, dst) in enumerate(pairs)]

    @pl.when(pl.program_id(1) == 0)
    def _():
        @pl.when(n == 0)
        def _():
            for cp in fetch(n):
                cp.start()
        for cp in fetch(n):
            cp.wait()
        for j in range(EXPERTS_PER_GROUP):
            cols = slice(j * EXPERT_FF, (j + 1) * EXPERT_FF)
            wg_scr[:, cols] = wg_f32[j].astype(BF16)
            wu_scr[:, cols] = wu_f32[j].astype(BF16)
            wd_scr[cols, :] = wd_f32[j].astype(BF16)
        @pl.when(n + 1 < N_GROUPS)
        def _():
            for cp in fetch(n + 1):
                cp.start()

    tm = hn_ref.shape[0]
    nf = n.astype(F32)
    route = route_ref[...]
    lane = lax.broadcasted_iota(jnp.int32, route.shape, 1)
    grp_c = jnp.sum(jnp.where(lane == GRP_LANE, route, 0.0), axis=-1, keepdims=True)
    rank_c = jnp.sum(jnp.where(lane == RANK_LANE, route, 0.0), axis=-1, keepdims=True)
    grp_r = route_t_ref[GRP_LANE:GRP_LANE + 1, :]
    rank_r = route_t_ref[RANK_LANE:RANK_LANE + 1, :]
    r1 = route.astype(BF16)
    r2 = (route - r1.astype(F32)).astype(BF16)
    r3 = (route - r1.astype(F32) - r2.astype(F32)).astype(BF16)
    pieces = jnp.concatenate([r1, r2, r3], axis=1)

    def group_rows(base):
        slot_r = lax.broadcasted_iota(jnp.int32, (MOE_CAP, tm), 0).astype(F32) + base
        take = jnp.where((grp_r == nf) & (rank_r == slot_r), 1.0, 0.0).astype(BF16)
        slot_c = lax.broadcasted_iota(jnp.int32, (tm, MOE_CAP), 1).astype(F32) + base
        put = jnp.where((grp_c == nf) & (rank_c == slot_c), 1.0, 0.0).astype(BF16)
        xc = _dot(take, hn_ref[...]).astype(BF16)
        rc = _dot(take, pieces)
        rc = rc[:, :ROUTE_W] + rc[:, ROUTE_W:2 * ROUTE_W] + rc[:, 2 * ROUTE_W:]
        lane_c = lax.broadcasted_iota(jnp.int32, rc.shape, 1)
        first = ROUTE_OFF + n * EXPERTS_PER_GROUP
        act = jax.nn.silu(_dot(xc, wg_scr[...])) * _dot(xc, wu_scr[...])
        blocks = []
        for j in range(EXPERTS_PER_GROUP):
            w_j = jnp.sum(jnp.where(lane_c == first + j, rc, 0.0), axis=-1, keepdims=True)
            blocks.append((act[:, j * EXPERT_FF:(j + 1) * EXPERT_FF] * w_j).astype(BF16))
        y = _dot(jnp.concatenate(blocks, axis=1), wd_scr[...])
        return _dot(put, y.astype(BF16))

    out_ref[...] = group_rows(jnp.float32(0.0)).astype(BF16)

    n_pass = lax.div(cnt_ref[pl.program_id(1), n] + (MOE_CAP - 1), MOE_CAP)

    def extra_pass(k, carry):
        more = group_rows((k * MOE_CAP).astype(F32))
        out_ref[...] = (out_ref[...].astype(F32) + more).astype(BF16)
        return carry

    lax.fori_loop(1, n_pass, extra_pass, 0)


def _moe(counts, hn, route, route_t, w_gate, w_up, w_down, *, layer):
    t = hn.shape[0]
    row = lambda w: pl.BlockSpec((TOK_TILE, w), lambda n, i, cnt: (i, 0))
    up_shape = (EXPERTS_PER_GROUP, D_MODEL, EXPERT_FF)
    down_shape = (EXPERTS_PER_GROUP, EXPERT_FF, D_MODEL)
    ff = EXPERTS_PER_GROUP * EXPERT_FF
    return pl.pallas_call(
        functools.partial(_moe_kernel, layer=layer),
        grid_spec=pltpu.PrefetchScalarGridSpec(
            num_scalar_prefetch=1,
            grid=(N_GROUPS, t // TOK_TILE),
            in_specs=[row(D_MODEL), row(ROUTE_W), pl.BlockSpec((SUBLANES, TOK_TILE), lambda n, i, cnt: (0, i)),
                      pl.BlockSpec(memory_space=pl.ANY), pl.BlockSpec(memory_space=pl.ANY),
                      pl.BlockSpec(memory_space=pl.ANY)],
            out_specs=pl.BlockSpec((None, TOK_TILE, D_MODEL), lambda n, i, cnt: (n, i, 0)),
            scratch_shapes=[pltpu.VMEM(up_shape, F32), pltpu.VMEM(up_shape, F32), pltpu.VMEM(down_shape, F32),
                            pltpu.SemaphoreType.DMA((3,)),
                            pltpu.VMEM((D_MODEL, ff), BF16), pltpu.VMEM((D_MODEL, ff), BF16),
                            pltpu.VMEM((ff, D_MODEL), BF16)]),
        out_shape=jax.ShapeDtypeStruct((N_GROUPS, t, D_MODEL), BF16),
        compiler_params=_params("arbitrary", "arbitrary"),
        name="moe",
    )(counts, hn, route, route_t, w_gate, w_up, w_down)


def _ple_kernel(x1_ref, c_ref, p_ref, pg_ref, wpg_ref, wple_ref, out_ref, wpg_scr, wple_scr):
    _round_once([(wpg_ref, wpg_scr), (wple_ref, wple_scr)])
    x2 = x1_ref[...]
    for n in range(N_GROUPS):
        x2 = x2 + c_ref[n].astype(F32)
    ms = jnp.mean(x2 * x2, axis=-1, keepdims=True)
    hp = (x2 * lax.rsqrt(ms + EPS) * pg_ref[...]).astype(BF16)
    gate = jax.nn.sigmoid(_dot(hp, wpg_scr[...]))
    out_ref[...] = x2 + gate * _dot(p_ref[...].astype(BF16), wple_scr[...])


def _ple(x1, contrib, p, ple_g, wpg, wple, *, layer):
    t = x1.shape[0]
    n_sb = p.shape[2] // TOK_TILE
    row = lambda w: pl.BlockSpec((TOK_TILE, w), lambda i: (i, 0))
    return pl.pallas_call(
        _ple_kernel,
        grid=(t // TOK_TILE,),
        in_specs=[row(D_MODEL), pl.BlockSpec((N_GROUPS, TOK_TILE, D_MODEL), lambda i: (0, i, 0)),
                  pl.BlockSpec((None, None, TOK_TILE, PLE_DIM), lambda i: (layer, i // n_sb, i % n_sb, 0)),
                  _const_spec(ple_g.shape), _layer_weight(wpg, layer), _layer_weight(wple, layer)],
        out_specs=row(D_MODEL),
        out_shape=jax.ShapeDtypeStruct((t, D_MODEL), F32),
        scratch_shapes=[pltpu.VMEM(wpg.shape[1:], BF16), pltpu.VMEM(wple.shape[1:], BF16)],
        compiler_params=_params("arbitrary"),
        name="ple",
    )(x1, contrib, p, ple_g, wpg, wple)


def kernel(x, p, attn_norm_g, w_in, b_gate, q_norm_g, k_norm_g, lam_qk, subln_g, w_attn_up, ssm_lam_re, ssm_lam_im, ssm_log_step, ssm_b_re, ssm_b_im, ssm_c_re, ssm_c_im, ssm_d, w_glu, b_glu, w_ssm_up, w_out, ffn_norm_g, w_router_group, b_router_group, w_router_expert, b_router_expert, w_exp_gate, w_exp_up, w_exp_down, ple_norm_g, w_ple_gate, w_ple):
    bsz, seq, d = x.shape
    depth = w_in.shape[0]
    t = bsz * seq
    assert d == D_MODEL and seq % TOK_TILE == 0 and seq % ATT_TILE == 0
    assert seq % SCAN_T == 0 and bsz == SUBLANES
    tok = jnp.arange(TOK_TILE)
    ltri = (tok[None, :] < tok[:, None]).astype(BF16)

    a_re, a_im, b_tiles, c_re_tiles, c_im_tiles, scalars = _prep(
        ssm_lam_re, ssm_lam_im, ssm_log_step, ssm_b_re, ssm_b_im, ssm_c_re, ssm_c_im, lam_qk, q_norm_g, k_norm_g)
    x2d = x.reshape(t, d)
    for i in range(depth):
        lam_init = _lambda_init(i)
        tile_gain = lambda g: jnp.tile(g.reshape(HEAD_W, 1), (COL_CHUNK // HEAD_W, 1))
        qt, k, vt, u_tm, gates = _inproj(
            x2d, attn_norm_g[i].reshape(1, d), w_in,
            tile_gain(q_norm_g[i]), tile_gain(k_norm_g[i]), b_gate[i].reshape(1, -1), seq=seq, layer=i)

        o = _attention(scalars[i], qt, k, vt, subln_g[i].reshape(1, HEAD_W), seq=seq, lam_init=lam_init)

        z = _ssm(u_tm, b_tiles[i], c_re_tiles[i], c_im_tiles[i],
                 a_re[i, ::SSM_GROUP].reshape(1, N_STATE), a_im[i, ::SSM_GROUP].reshape(1, N_STATE),
                 ssm_d[i].reshape(1, SSM_WIDTH))

        w_r = jnp.concatenate(
            [w_router_group[i], jnp.transpose(w_router_expert[i], (1, 0, 2)).reshape(d, N_EXPERTS),
             jnp.zeros((d, ROUTE_W - N_GROUPS - N_EXPERTS), F32)], axis=1)
        w_r_hi = w_r.astype(BF16)
        w_r_lo = (w_r - w_r_hi.astype(F32)).astype(BF16)
        b_r = jnp.concatenate([b_router_group[i], b_router_expert[i].reshape(-1),
                               jnp.zeros((ROUTE_W - N_GROUPS - N_EXPERTS,), F32)]).reshape(1, ROUTE_W)
        x1, hn, route, route_t, counts = _mix(
            x2d, o, z.reshape(t, SSM_WIDTH), gates,
            w_attn_up, w_glu, b_glu[i].reshape(1, -1), w_ssm_up, w_out, ffn_norm_g[i].reshape(1, d),
            w_r_hi, w_r_lo, b_r, ltri, layer=i)

        contrib = _moe(counts[:, 0, :N_GROUPS].astype(jnp.int32), hn, route, route_t,
                       w_exp_gate, w_exp_up, w_exp_down, layer=i)
        x2d = _ple(x1, contrib, p, ple_norm_g[i].reshape(1, d), w_ple_gate, w_ple, layer=i)
    return x2d.reshape(bsz, seq, d)
```

```python
import functools
import math

import jax
import jax.numpy as jnp
from jax import lax
from jax.experimental import pallas as pl
from jax.experimental.pallas import tpu as pltpu

F32 = jnp.float32
BF16 = jnp.bfloat16

D_MODEL = 1024
ATT_HEADS = 8
ATT_HEAD_DIM = 64
HEAD_W = 2 * ATT_HEAD_DIM
ATT_QK_WIDTH = ATT_HEADS * HEAD_W
ATT_V_WIDTH = ATT_HEADS * HEAD_W
SSM_WIDTH = 512
SSM_GROUP = 16
SSM_GROUPS = SSM_WIDTH // SSM_GROUP
SSM_STATE = 64
N_STATE = SSM_GROUPS * SSM_STATE
N_BRANCH = 2
IN_WIDTH = 2 * ATT_QK_WIDTH + ATT_V_WIDTH + SSM_WIDTH + N_BRANCH * D_MODEL
N_GROUPS = 4
EXPERTS_PER_GROUP = 8
N_EXPERTS = N_GROUPS * EXPERTS_PER_GROUP
EXPERT_FF = 256
PLE_DIM = 256
EPS = 1e-6

LANES = 128
SUBLANES = 8
MXU_N = 256
VMEM_LIMIT = 56 * 1024 * 1024

TOK_TILE = 512
MOE_CAP = 160
COL_CHUNK = 512
ATT_TILE = 256
Q_SCALE = ATT_HEAD_DIM ** -0.5 * math.log2(math.e)
LAM_ROW = 0
BOUND_ROW = 1
ATT_BOUND_SLACK = 1.01
ATT_BOUND_LIMIT = 60.0
SCAN_T = 64
SCAN_LANES = 1024
ROUTE_W = LANES
ROUTE_OFF = N_GROUPS
GRP_LANE = 0
RANK_LANE = 1


def _lambda_init(layer_idx):
    return 0.8 - 0.6 * math.exp(-0.3 * layer_idx)


def _dot(a, b):
    return jnp.dot(a, b, preferred_element_type=F32)


def _const_spec(shape):
    nd = len(shape)
    return pl.BlockSpec(shape, lambda *_: (0,) * nd, pipeline_mode=pl.Buffered(1))


def _layer_weight(stacked, layer):
    shape = stacked.shape[1:]
    return pl.BlockSpec((None,) + shape, lambda *_: (layer,) + (0,) * len(shape), pipeline_mode=pl.Buffered(1))


def _round_once(pairs):
    @pl.when(pl.program_id(0) == 0)
    def _():
        for w_ref, w_scr in pairs:
            w_scr[...] = w_ref[...].astype(BF16)


def _params(*sem):
    return pltpu.CompilerParams(dimension_semantics=sem, vmem_limit_bytes=VMEM_LIMIT)


def _prep_kernel(lre_ref, lim_ref, lstep_ref, bre_ref, bim_ref, cre_ref, cim_ref, lqk_ref, qg_ref, kg_ref,
                 are_ref, aim_ref, bt_ref, ctre_ref, ctim_ref, lam_ref):
    lr = lre_ref[...]
    li = lim_ref[...]
    dt = jnp.exp(lstep_ref[...])
    mag = jnp.exp(lr * dt)
    ab_re = mag * jnp.cos(li * dt)
    ab_im = mag * jnp.sin(li * dt)
    den = lr * lr + li * li
    nr = ab_re - 1.0
    coef_re = (nr * lr + ab_im * li) / den
    coef_im = (ab_im * lr - nr * li) / den
    br = bre_ref[...]
    bi = bim_ref[...]
    are_ref[...] = ab_re
    aim_ref[...] = ab_im
    bb_re = coef_re * br - coef_im * bi
    bb_im = coef_re * bi + coef_im * br

    def spread(x, reps):
        w = x.shape[1]
        src = lax.broadcasted_iota(jnp.int32, (w, w * reps), 0)
        dst = lax.broadcasted_iota(jnp.int32, (w, w * reps), 1) & (w - 1)
        return _dot(x.astype(BF16), jnp.where(src == dst, 1.0, 0.0).astype(BF16))

    def same_group(shape, row0, row_shift, col0, col_shift):
        rows = (lax.broadcasted_iota(jnp.int32, shape, 0) + row0) >> row_shift
        cols = (lax.broadcasted_iota(jnp.int32, shape, 1) + col0) >> col_shift
        return rows == cols

    n_re = N_STATE // MXU_N
    ch_shift = SSM_GROUP.bit_length() - 1
    st_shift = SSM_STATE.bit_length() - 1
    for part, bb in enumerate((bb_re, bb_im)):
        for j in range(n_re):
            r0 = LANES * ((j * MXU_N // SSM_STATE * SSM_GROUP) // LANES)
            x = spread(bb[r0:r0 + LANES, :], MXU_N // SSM_STATE)
            keep = same_group(x.shape, r0, ch_shift, j * MXU_N, st_shift)
            bt_ref[part * n_re + j] = jnp.where(keep, x, 0.0).astype(BF16)
    half = N_STATE // 2
    for c_ref, ct_ref in ((cre_ref, ctre_ref), (cim_ref, ctim_ref)):
        for n in range(2):
            y = spread(c_ref[n * half:(n + 1) * half, :], MXU_N // SSM_GROUP)
            keep = same_group(y.shape, n * half, st_shift, n * MXU_N, ch_shift)
            ct_ref[n] = jnp.where(keep, y, 0.0).astype(BF16)

    lq = lqk_ref[...]
    s01 = jnp.sum(lq[0:1] * lq[1:2], axis=-1, keepdims=True)
    s23 = jnp.sum(lq[2:3] * lq[3:4], axis=-1, keepdims=True)
    lam_dyn = jnp.exp(s01) - jnp.exp(s23)
    g_max = jnp.max(jnp.abs(qg_ref[...]), keepdims=True) * jnp.max(jnp.abs(kg_ref[...]), keepdims=True)
    bound = g_max * (ATT_HEAD_DIM * Q_SCALE * ATT_BOUND_SLACK)
    row = lax.broadcasted_iota(jnp.int32, lam_ref.shape, 0)
    lam_ref[...] = jnp.where(row == LAM_ROW, lam_dyn, bound)


def _prep(lam_re, lam_im, log_step, b_re, b_im, c_re, c_im, lam_qk, q_gain, k_gain):
    depth = lam_re.shape[0]
    rows = SSM_WIDTH
    rep = lambda a: jnp.repeat(a, SSM_GROUP, axis=1)
    lstep = jnp.broadcast_to(log_step[:, :, None], lam_re.shape)
    tr_b = lambda b: jnp.transpose(b, (0, 1, 3, 2)).reshape(depth, rows, SSM_STATE)
    tr_c = lambda c: jnp.transpose(c, (0, 1, 3, 2)).reshape(depth, N_STATE, SSM_GROUP)
    lay = lambda *shape: pl.BlockSpec((None,) + shape, lambda i: (i,) + (0,) * len(shape))
    blk = lay(rows, SSM_STATE)
    out = jax.ShapeDtypeStruct((depth, rows, SSM_STATE), F32)
    n_bt = 2 * N_STATE // MXU_N
    return pl.pallas_call(
        _prep_kernel,
        grid=(depth,),
        in_specs=[blk, blk, blk, blk, blk, lay(N_STATE, SSM_GROUP), lay(N_STATE, SSM_GROUP),
                  lay(4, ATT_HEAD_DIM), lay(2, ATT_HEAD_DIM), lay(2, ATT_HEAD_DIM)],
        out_specs=[blk, blk, lay(n_bt, LANES, MXU_N), lay(2, N_STATE // 2, MXU_N), lay(2, N_STATE // 2, MXU_N),
                   lay(SUBLANES, LANES)],
        out_shape=[out, out, jax.ShapeDtypeStruct((depth, n_bt, LANES, MXU_N), BF16),
                   jax.ShapeDtypeStruct((depth, 2, N_STATE // 2, MXU_N), BF16),
                   jax.ShapeDtypeStruct((depth, 2, N_STATE // 2, MXU_N), BF16),
                   jax.ShapeDtypeStruct((depth, SUBLANES, LANES), F32)],
        compiler_params=_params("arbitrary"),
        name="prep",
    )(rep(lam_re), rep(lam_im), rep(lstep), tr_b(b_re), tr_b(b_im), tr_c(c_re), tr_c(c_im), lam_qk, q_gain, k_gain)


def _inproj_kernel(x_ref, g_ref, w_ref, qg_ref, kg_ref, bg_ref,
                   qt_ref, k_ref, vt_ref, u_hbm, gate_ref, w_scr, u_buf, u_sem, *, n_sb):
    _round_once([(w_ref, w_scr)])
    step = pl.program_id(0)

    def u_copy(i):
        s0 = pl.multiple_of((i % n_sb) * TOK_TILE, TOK_TILE)
        return pltpu.make_async_copy(u_buf, u_hbm.at[pl.ds(s0, TOK_TILE), i // n_sb], u_sem)

    xf = x_ref[...]
    ms = jnp.mean(xf * xf, axis=-1, keepdims=True)
    h = (xf * lax.rsqrt(ms + EPS) * g_ref[...]).astype(BF16)

    def head_norm_t(y, gain_col):
        rows = y.shape[0]
        y3 = y.T.reshape(COL_CHUNK // ATT_HEAD_DIM, ATT_HEAD_DIM, rows)
        ms = jnp.sum(y3 * y3, axis=1, keepdims=True) * (1.0 / ATT_HEAD_DIM)
        return (y3 * lax.rsqrt(ms + EPS)).reshape(COL_CHUNK, rows) * gain_col

    n_qk = ATT_QK_WIDTH // COL_CHUNK
    n_v = ATT_V_WIDTH // COL_CHUNK
    for c in range(IN_WIDTH // COL_CHUNK):
        y = _dot(h, w_scr[:, c * COL_CHUNK:(c + 1) * COL_CHUNK])
        if c < n_qk:
            qt_ref[c * COL_CHUNK:(c + 1) * COL_CHUNK, :] = (head_norm_t(y, qg_ref[...]) * Q_SCALE).astype(BF16)
        elif c < 2 * n_qk:
            cc = c - n_qk
            k_ref[:, cc * COL_CHUNK:(cc + 1) * COL_CHUNK] = head_norm_t(y, kg_ref[...]).T.astype(BF16)
        elif c < 2 * n_qk + n_v:
            cc = c - 2 * n_qk
            vt_ref[cc * COL_CHUNK:(cc + 1) * COL_CHUNK, :] = y.T.astype(BF16)
        elif c == 2 * n_qk + n_v:
            @pl.when(step > 0)
            def _():
                u_copy(step - 1).wait()
            u_buf[...] = y
        else:
            cc = c - (2 * n_qk + n_v + 1)
            b = bg_ref[:, cc * COL_CHUNK:(cc + 1) * COL_CHUNK]
            gate_ref[:, cc * COL_CHUNK:(cc + 1) * COL_CHUNK] = jax.nn.sigmoid(y + b).astype(BF16)

    u_copy(step).start()

    @pl.when(step == pl.num_programs(0) - 1)
    def _():
        u_copy(step).wait()


def _inproj(x2d, norm_g, w_in, qg, kg, b_gate, *, seq, layer):
    t = x2d.shape[0]
    n_sb = seq // TOK_TILE
    bsz = t // seq
    row = lambda w: pl.BlockSpec((TOK_TILE, w), lambda i: (i, 0))
    col = lambda w: pl.BlockSpec((w, TOK_TILE), lambda i: (i // n_sb, i % n_sb))
    return pl.pallas_call(
        functools.partial(_inproj_kernel, n_sb=n_sb),
        grid=(t // TOK_TILE,),
        in_specs=[row(D_MODEL), _const_spec((1, D_MODEL)), _layer_weight(w_in, layer),
                  _const_spec((COL_CHUNK, 1)), _const_spec((COL_CHUNK, 1)),
                  _const_spec((1, N_BRANCH * D_MODEL))],
        out_specs=[col(ATT_QK_WIDTH), row(ATT_QK_WIDTH), col(ATT_V_WIDTH),
                   pl.BlockSpec(memory_space=pl.ANY),
                   row(N_BRANCH * D_MODEL)],
        out_shape=[jax.ShapeDtypeStruct((bsz * ATT_QK_WIDTH, seq), BF16),
                   jax.ShapeDtypeStruct((t, ATT_QK_WIDTH), BF16),
                   jax.ShapeDtypeStruct((bsz * ATT_V_WIDTH, seq), BF16),
                   jax.ShapeDtypeStruct((seq, bsz, SSM_WIDTH), F32),
                   jax.ShapeDtypeStruct((t, N_BRANCH * D_MODEL), BF16)],
        scratch_shapes=[pltpu.VMEM((D_MODEL, IN_WIDTH), BF16), pltpu.VMEM((TOK_TILE, SSM_WIDTH), F32),
                        pltpu.SemaphoreType.DMA(())],
        compiler_params=_params("arbitrary"),
        name="inproj",
    )(x2d, norm_g, w_in, qg, kg, b_gate)


def _attn_kernel(sc_ref, qt_ref, k_ref, vt_ref, g_ref, o_ref, *, lam_init):
    tq = ATT_TILE
    seq = k_ref.shape[0]
    nq = seq // tq
    lam = sc_ref[LAM_ROW, 0] + lam_init
    bound = sc_ref[BOUND_ROW, 0]
    head_row = lax.broadcasted_iota(jnp.int32, (HEAD_W, tq), 0)
    key_pos = lax.broadcasted_iota(jnp.int32, (tq, 2 * tq), 0)
    qry_pos = lax.broadcasted_iota(jnp.int32, (tq, 2 * tq), 1) & (tq - 1)
    causal = key_pos <= qry_pos
    zero = jnp.zeros((HEAD_W, tq), BF16)

    def tile(qi, shift_by_bound):
        qt = qt_ref[:, qi * tq:(qi + 1) * tq]
        qst = jnp.concatenate([jnp.where(head_row < ATT_HEAD_DIM, qt, zero),
                               jnp.where(head_row >= ATT_HEAD_DIM, qt, zero)], axis=1)
        nk = qi * tq
        s_d = jnp.where(causal, _dot(k_ref[nk:nk + tq, :], qst), -jnp.inf)
        if qi > 0:
            s_m = _dot(k_ref[0:nk, :], qst)
        if shift_by_bound:
            m = bound
        else:
            m = jnp.max(s_d, axis=0, keepdims=True)
            if qi > 0:
                m = jnp.maximum(m, jnp.max(s_m, axis=0, keepdims=True))
        p_d = jnp.exp2(s_d - m)
        l = jnp.sum(p_d, axis=0, keepdims=True)
        acc = _dot(vt_ref[:, nk:nk + tq], p_d.astype(BF16))
        if qi > 0:
            p_m = jnp.exp2(s_m - m)
            l = l + jnp.sum(p_m, axis=0, keepdims=True)
            acc = acc + _dot(vt_ref[:, 0:nk], p_m.astype(BF16))
        accn = acc * (1.0 / l)
        ot = accn[:, :tq] - lam * accn[:, tq:]
        ms = jnp.mean(ot * ot, axis=0, keepdims=True)
        o = (ot * lax.rsqrt(ms + EPS)).T * g_ref[...] * (1.0 - lam_init)
        o_ref[qi * tq:(qi + 1) * tq, :] = o.astype(BF16)

    @pl.when(bound < ATT_BOUND_LIMIT)
    def _():
        for qi in range(nq):
            tile(qi, True)

    @pl.when(jnp.logical_not(bound < ATT_BOUND_LIMIT))
    def _():
        for qi in range(nq):
            tile(qi, False)


def _attention(scalars, qt, k, vt, subln_g, *, seq, lam_init):
    t = k.shape[0]
    bsz = t // seq
    fm = pl.BlockSpec((HEAD_W, seq), lambda b, h: (b * ATT_HEADS + h, 0))
    tm = pl.BlockSpec((seq, HEAD_W), lambda b, h: (b, h))
    return pl.pallas_call(
        functools.partial(_attn_kernel, lam_init=lam_init),
        grid=(bsz, ATT_HEADS),
        in_specs=[pl.BlockSpec(memory_space=pltpu.SMEM), fm, tm, fm, _const_spec((1, HEAD_W))],
        out_specs=tm,
        out_shape=jax.ShapeDtypeStruct((t, ATT_V_WIDTH), BF16),
        compiler_params=_params("parallel", "parallel"),
        name="attn",
    )(scalars, qt, k, vt, subln_g)


def _ssm_kernel(u_ref, bt_ref, cre_ref, cim_ref, are_ref, aim_ref, d_ref, z_hbm, st_scr, state_scr, z_buf, z_sem):
    n_t, bsz, _ = u_ref.shape
    rows = n_t * bsz
    chunk = pl.program_id(0)

    def z_copies(i):
        t0 = pl.multiple_of(i * n_t, n_t)
        return [pltpu.make_async_copy(z_buf.at[:, b], z_hbm.at[b, pl.ds(t0, n_t)], z_sem) for b in range(bsz)]

    @pl.when(chunk == 0)
    def _():
        state_scr[...] = jnp.zeros(state_scr.shape, F32)

    u = u_ref[...].reshape(rows, SSM_WIDTH)
    ub = u.astype(BF16)
    n_tiles = 2 * N_STATE // MXU_N
    for j in range(n_tiles):
        k0 = LANES * (((j % (n_tiles // 2)) * MXU_N // SSM_STATE * SSM_GROUP) // LANES)
        st_scr[:, j * MXU_N:(j + 1) * MXU_N] = _dot(ub[:, k0:k0 + LANES], bt_ref[j])

    for cc in range(N_STATE // SCAN_LANES):
        lo = cc * SCAN_LANES
        hi = N_STATE + lo
        a_r = jnp.broadcast_to(are_ref[:, lo:lo + SCAN_LANES], (bsz, SCAN_LANES))
        a_i = jnp.broadcast_to(aim_ref[:, lo:lo + SCAN_LANES], (bsz, SCAN_LANES))

        def step(t, carry, lo=lo, hi=hi, a_r=a_r, a_i=a_i):
            s_r, s_i = carry
            r0 = pl.multiple_of(t * bsz, bsz)
            n_r = a_r * s_r - a_i * s_i + st_scr[pl.ds(r0, bsz), lo:lo + SCAN_LANES]
            n_i = a_r * s_i + a_i * s_r + st_scr[pl.ds(r0, bsz), hi:hi + SCAN_LANES]
            st_scr[pl.ds(r0, bsz), lo:lo + SCAN_LANES] = n_r
            st_scr[pl.ds(r0, bsz), hi:hi + SCAN_LANES] = n_i
            return n_r, n_i

        s_r, s_i = lax.fori_loop(0, rows // bsz, step,
                                 (state_scr[:, lo:lo + SCAN_LANES], state_scr[:, hi:hi + SCAN_LANES]),
                                 unroll=8)
        state_scr[:, lo:lo + SCAN_LANES] = s_r
        state_scr[:, hi:hi + SCAN_LANES] = s_i

    @pl.when(chunk > 0)
    def _():
        for cp in z_copies(chunk - 1):
            cp.wait()
    half = N_STATE // 2
    for n in range(2):
        s_re = st_scr[:, n * half:(n + 1) * half].astype(BF16)
        s_im = st_scr[:, N_STATE + n * half:N_STATE + (n + 1) * half].astype(BF16)
        cols = slice(n * MXU_N, (n + 1) * MXU_N)
        y = _dot(s_re, cre_ref[n]) - _dot(s_im, cim_ref[n]) + d_ref[:, cols] * u[:, cols]
        z_buf[:, :, cols] = jax.nn.gelu(y).reshape(n_t, bsz, MXU_N)
    for cp in z_copies(chunk):
        cp.start()

    @pl.when(chunk == pl.num_programs(0) - 1)
    def _():
        for cp in z_copies(chunk):
            cp.wait()


def _ssm(u_tm, b_tiles, c_re_tiles, c_im_tiles, a_re, a_im, d_skip):
    seq, bsz, _ = u_tm.shape
    return pl.pallas_call(
        _ssm_kernel,
        grid=(seq // SCAN_T,),
        in_specs=[pl.BlockSpec((SCAN_T, bsz, SSM_WIDTH), lambda i: (i, 0, 0)),
                  _const_spec(b_tiles.shape), _const_spec(c_re_tiles.shape),
                  _const_spec(c_im_tiles.shape), _const_spec((1, N_STATE)), _const_spec((1, N_STATE)),
                  _const_spec((1, SSM_WIDTH))],
        out_specs=pl.BlockSpec(memory_space=pl.ANY),
        out_shape=jax.ShapeDtypeStruct((bsz, seq, SSM_WIDTH), F32),
        scratch_shapes=[pltpu.VMEM((SCAN_T * bsz, 2 * N_STATE), F32), pltpu.VMEM((bsz, 2 * N_STATE), F32),
                        pltpu.VMEM((SCAN_T, bsz, SSM_WIDTH), F32), pltpu.SemaphoreType.DMA(())],
        compiler_params=_params("arbitrary"),
        name="ssm",
    )(u_tm, b_tiles, c_re_tiles, c_im_tiles, a_re, a_im, d_skip)


def _route(logits, ltri):
    lane = lax.broadcasted_iota(jnp.int32, logits.shape, 1)
    lanef = lane.astype(F32)
    big = float(ROUTE_W)
    neg = -jnp.inf
    gl = jnp.where(lane < N_GROUPS, logits, neg)
    gmax = jnp.max(gl, axis=-1, keepdims=True)
    grp = jnp.min(jnp.where(gl == gmax, lanef, big), axis=-1, keepdims=True)
    p_grp = 1.0 / jnp.sum(jnp.exp(gl - gmax), axis=-1, keepdims=True)
    first = ROUTE_OFF + grp * EXPERTS_PER_GROUP
    in_grp = (lanef >= first) & (lanef < first + EXPERTS_PER_GROUP)
    el = jnp.where(in_grp, logits, neg)
    v1 = jnp.max(el, axis=-1, keepdims=True)
    i1 = jnp.min(jnp.where(el == v1, lanef, big), axis=-1, keepdims=True)
    el2 = jnp.where(lanef == i1, neg, el)
    v2 = jnp.max(el2, axis=-1, keepdims=True)
    i2 = jnp.min(jnp.where(el2 == v2, lanef, big), axis=-1, keepdims=True)
    e21 = jnp.exp(v2 - v1)
    w1 = p_grp / (1.0 + e21)
    w2 = p_grp * (e21 / (1.0 + e21))
    member = lanef == grp
    before = _dot(ltri, jnp.where(member, 1.0, 0.0).astype(BF16))
    rank = jnp.sum(jnp.where(member, before, 0.0), axis=-1, keepdims=True)
    counts = jnp.sum(jnp.where(member, 1.0, 0.0), axis=0, keepdims=True)
    return (jnp.where(lanef == i1, w1, 0.0) + jnp.where(lanef == i2, w2, 0.0)
            + jnp.where(lane == GRP_LANE, grp, 0.0) + jnp.where(lane == RANK_LANE, rank, 0.0)), counts


def _mix_kernel(x_ref, o_ref, z_ref, gate_ref, wau_ref, wglu_ref, bglu_ref, wsu_ref, wout_ref,
                fg_ref, wrh_ref, wrl_ref, br_ref, ltri_ref, x1_ref, hn_ref, route_ref, route_t_ref, cnt_ref,
                wau_scr, wglu_scr, wsu_scr, wout_scr):
    _round_once([(wau_ref, wau_scr), (wglu_ref, wglu_scr), (wsu_ref, wsu_scr), (wout_ref, wout_scr)])
    y_att = _dot(o_ref[...], wau_scr[...])
    z = z_ref[...]
    z = z * jax.nn.sigmoid(_dot(z.astype(BF16), wglu_scr[...]) + bglu_ref[...])
    y_ssm = _dot(z.astype(BF16), wsu_scr[...])
    mixed = (gate_ref[:, :D_MODEL].astype(F32) * y_att + gate_ref[:, D_MODEL:].astype(F32) * y_ssm)
    x1 = x_ref[...] + _dot(mixed.astype(BF16), wout_scr[...])
    x1_ref[...] = x1
    ms = jnp.mean(x1 * x1, axis=-1, keepdims=True)
    hn = x1 * lax.rsqrt(ms + EPS) * fg_ref[...]
    hn_hi = hn.astype(BF16)
    hn_ref[...] = hn_hi
    hn_lo = (hn - hn_hi.astype(F32)).astype(BF16)
    logits = (_dot(hn_hi, wrh_ref[...]) + _dot(hn_lo, wrh_ref[...]) + _dot(hn_hi, wrl_ref[...])
              + br_ref[...])
    route, counts = _route(logits, ltri_ref[...])
    route_ref[...] = route
    route_t_ref[...] = route.T[:SUBLANES, :]
    cnt_ref[...] = jnp.broadcast_to(counts, cnt_ref.shape)


def _mix(x2d, o, z, gates, wau, wglu, bglu, wsu, wout, ffn_g, wr_hi, wr_lo, br, ltri, *, layer):
    t = x2d.shape[0]
    row = lambda w: pl.BlockSpec((TOK_TILE, w), lambda i: (i, 0))
    return pl.pallas_call(
        _mix_kernel,
        grid=(t // TOK_TILE,),
        in_specs=[row(D_MODEL), row(ATT_V_WIDTH), row(SSM_WIDTH), row(N_BRANCH * D_MODEL),
                  _layer_weight(wau, layer), _layer_weight(wglu, layer), _const_spec(bglu.shape),
                  _layer_weight(wsu, layer), _layer_weight(wout, layer), _const_spec(ffn_g.shape),
                  _const_spec(wr_hi.shape), _const_spec(wr_lo.shape), _const_spec(br.shape),
                  _const_spec(ltri.shape)],
        out_specs=[row(D_MODEL), row(D_MODEL), row(ROUTE_W),
                   pl.BlockSpec((SUBLANES, TOK_TILE), lambda i: (0, i)),
                   pl.BlockSpec((None, SUBLANES, ROUTE_W), lambda i: (i, 0, 0))],
        out_shape=[jax.ShapeDtypeStruct((t, D_MODEL), F32),
                   jax.ShapeDtypeStruct((t, D_MODEL), BF16),
                   jax.ShapeDtypeStruct((t, ROUTE_W), F32),
                   jax.ShapeDtypeStruct((SUBLANES, t), F32),
                   jax.ShapeDtypeStruct((t // TOK_TILE, SUBLANES, ROUTE_W), F32)],
        scratch_shapes=[pltpu.VMEM(w.shape[1:], BF16) for w in (wau, wglu, wsu, wout)],
        compiler_params=_params("arbitrary"),
        name="mix",
    )(x2d, o, z, gates, wau, wglu, bglu, wsu, wout, ffn_g, wr_hi, wr_lo, br, ltri)


def _moe_kernel(cnt_ref, hn_ref, route_ref, route_t_ref, *rest, layer, overflow):
    if overflow:
        prev_ref, *rest = rest
    wg_hbm, wu_hbm, wd_hbm, out_ref, wg_f32, wu_f32, wd_f32, w_sem, wg_scr, wu_scr, wd_scr = rest
    n = pl.program_id(0)

    def fetch(g):
        pairs = ((wg_hbm, wg_f32), (wu_hbm, wu_f32), (wd_hbm, wd_f32))
        return [pltpu.make_async_copy(src.at[layer, g], dst, w_sem.at[k]) for k, (src, dst) in enumerate(pairs)]

    @pl.when(pl.program_id(1) == 0)
    def _():
        @pl.when(n == 0)
        def _():
            for cp in fetch(n):
                cp.start()
        for cp in fetch(n):
            cp.wait()
        for j in range(EXPERTS_PER_GROUP):
            cols = slice(j * EXPERT_FF, (j + 1) * EXPERT_FF)
            wg_scr[:, cols] = wg_f32[j].astype(BF16)
            wu_scr[:, cols] = wu_f32[j].astype(BF16)
            wd_scr[cols, :] = wd_f32[j].astype(BF16)
        @pl.when(n + 1 < N_GROUPS)
        def _():
            for cp in fetch(n + 1):
                cp.start()

    tm = hn_ref.shape[0]
    nf = n.astype(F32)
    route = route_ref[...]
    lane = lax.broadcasted_iota(jnp.int32, route.shape, 1)
    grp_c = jnp.sum(jnp.where(lane == GRP_LANE, route, 0.0), axis=-1, keepdims=True)
    rank_c = jnp.sum(jnp.where(lane == RANK_LANE, route, 0.0), axis=-1, keepdims=True)
    grp_r = route_t_ref[GRP_LANE:GRP_LANE + 1, :]
    rank_r = route_t_ref[RANK_LANE:RANK_LANE + 1, :]
    r1 = route.astype(BF16)
    r2 = (route - r1.astype(F32)).astype(BF16)
    r3 = (route - r1.astype(F32) - r2.astype(F32)).astype(BF16)
    pieces = jnp.concatenate([r1, r2, r3], axis=1)

    def group_rows(base):
        slot_r = lax.broadcasted_iota(jnp.int32, (MOE_CAP, tm), 0).astype(F32) + base
        take = jnp.where((grp_r == nf) & (rank_r == slot_r), 1.0, 0.0).astype(BF16)
        slot_c = lax.broadcasted_iota(jnp.int32, (tm, MOE_CAP), 1).astype(F32) + base
        put = jnp.where((grp_c == nf) & (rank_c == slot_c), 1.0, 0.0).astype(BF16)
        xc = _dot(take, hn_ref[...]).astype(BF16)
        rc = _dot(take, pieces)
        rc = rc[:, :ROUTE_W] + rc[:, ROUTE_W:2 * ROUTE_W] + rc[:, 2 * ROUTE_W:]
        lane_c = lax.broadcasted_iota(jnp.int32, rc.shape, 1)
        first = ROUTE_OFF + n * EXPERTS_PER_GROUP
        act = jax.nn.silu(_dot(xc, wg_scr[...])) * _dot(xc, wu_scr[...])
        blocks = []
        for j in range(EXPERTS_PER_GROUP):
            w_j = jnp.sum(jnp.where(lane_c == first + j, rc, 0.0), axis=-1, keepdims=True)
            blocks.append((act[:, j * EXPERT_FF:(j + 1) * EXPERT_FF] * w_j).astype(BF16))
        y = _dot(jnp.concatenate(blocks, axis=1), wd_scr[...])
        return _dot(put, y.astype(BF16))

    if not overflow:
        out_ref[...] = group_rows(jnp.float32(0.0)).astype(BF16)
        return

    out_ref[...] = prev_ref[...]
    n_pass = lax.div(cnt_ref[pl.program_id(1), n] + (MOE_CAP - 1), MOE_CAP)

    def extra_pass(k, carry):
        more = group_rows((k * MOE_CAP).astype(F32))
        out_ref[...] = (out_ref[...].astype(F32) + more).astype(BF16)
        return carry

    lax.fori_loop(1, n_pass, extra_pass, 0)


def _moe(counts, hn, route, route_t, w_gate, w_up, w_down, *, layer, prev=None):
    t = hn.shape[0]
    overflow = prev is not None
    mode = dict(pipeline_mode=pl.Buffered(1)) if overflow else {}
    row = lambda w: pl.BlockSpec((TOK_TILE, w), lambda n, i, cnt: (i, 0), **mode)
    out_blk = pl.BlockSpec((None, TOK_TILE, D_MODEL), lambda n, i, cnt: (n, i, 0))
    up_shape = (EXPERTS_PER_GROUP, D_MODEL, EXPERT_FF)
    down_shape = (EXPERTS_PER_GROUP, EXPERT_FF, D_MODEL)
    ff = EXPERTS_PER_GROUP * EXPERT_FF
    operands = (counts, hn, route, route_t) + ((prev,) if overflow else ()) + (w_gate, w_up, w_down)
    return pl.pallas_call(
        functools.partial(_moe_kernel, layer=layer, overflow=overflow),
        grid_spec=pltpu.PrefetchScalarGridSpec(
            num_scalar_prefetch=1,
            grid=(N_GROUPS, t // TOK_TILE),
            in_specs=[row(D_MODEL), row(ROUTE_W), pl.BlockSpec((SUBLANES, TOK_TILE), lambda n, i, cnt: (0, i))]
                     + ([pl.BlockSpec((None, TOK_TILE, D_MODEL), lambda n, i, cnt: (n, i, 0), **mode)]
                        if overflow else [])
                     + [pl.BlockSpec(memory_space=pl.ANY), pl.BlockSpec(memory_space=pl.ANY),
                        pl.BlockSpec(memory_space=pl.ANY)],
            out_specs=out_blk,
            scratch_shapes=[pltpu.VMEM(up_shape, F32), pltpu.VMEM(up_shape, F32), pltpu.VMEM(down_shape, F32),
                            pltpu.SemaphoreType.DMA((3,)),
                            pltpu.VMEM((D_MODEL, ff), BF16), pltpu.VMEM((D_MODEL, ff), BF16),
                            pltpu.VMEM((ff, D_MODEL), BF16)]),
        out_shape=jax.ShapeDtypeStruct((N_GROUPS, t, D_MODEL), BF16),
        input_output_aliases={4: 0} if overflow else {},
        compiler_params=_params("arbitrary", "arbitrary"),
        name="moe_overflow" if overflow else "moe",
    )(*operands)


def _ple_kernel(x1_ref, c_ref, p_ref, pg_ref, wpg_ref, wple_ref, out_ref, wpg_scr, wple_scr):
    _round_once([(wpg_ref, wpg_scr), (wple_ref, wple_scr)])
    x2 = x1_ref[...]
    for n in range(N_GROUPS):
        x2 = x2 + c_ref[n].astype(F32)
    ms = jnp.mean(x2 * x2, axis=-1, keepdims=True)
    hp = (x2 * lax.rsqrt(ms + EPS) * pg_ref[...]).astype(BF16)
    gate = jax.nn.sigmoid(_dot(hp, wpg_scr[...]))
    out_ref[...] = x2 + gate * _dot(p_ref[...].astype(BF16), wple_scr[...])


def _ple(x1, contrib, p, ple_g, wpg, wple, *, layer):
    t = x1.shape[0]
    n_sb = p.shape[2] // TOK_TILE
    row = lambda w: pl.BlockSpec((TOK_TILE, w), lambda i: (i, 0))
    return pl.pallas_call(
        _ple_kernel,
        grid=(t // TOK_TILE,),
        in_specs=[row(D_MODEL), pl.BlockSpec((N_GROUPS, TOK_TILE, D_MODEL), lambda i: (0, i, 0)),
                  pl.BlockSpec((None, None, TOK_TILE, PLE_DIM), lambda i: (layer, i // n_sb, i % n_sb, 0)),
                  _const_spec(ple_g.shape), _layer_weight(wpg, layer), _layer_weight(wple, layer)],
        out_specs=row(D_MODEL),
        out_shape=jax.ShapeDtypeStruct((t, D_MODEL), F32),
        scratch_shapes=[pltpu.VMEM(wpg.shape[1:], BF16), pltpu.VMEM(wple.shape[1:], BF16)],
        compiler_params=_params("arbitrary"),
        name="ple",
    )(x1, contrib, p, ple_g, wpg, wple)


def kernel(x, p, attn_norm_g, w_in, b_gate, q_norm_g, k_norm_g, lam_qk, subln_g, w_attn_up, ssm_lam_re, ssm_lam_im, ssm_log_step, ssm_b_re, ssm_b_im, ssm_c_re, ssm_c_im, ssm_d, w_glu, b_glu, w_ssm_up, w_out, ffn_norm_g, w_router_group, b_router_group, w_router_expert, b_router_expert, w_exp_gate, w_exp_up, w_exp_down, ple_norm_g, w_ple_gate, w_ple):
    bsz, seq, d = x.shape
    depth = w_in.shape[0]
    t = bsz * seq
    assert d == D_MODEL and seq % TOK_TILE == 0 and seq % ATT_TILE == 0
    assert seq % SCAN_T == 0 and bsz == SUBLANES
    tok = jnp.arange(TOK_TILE)
    ltri = (tok[None, :] < tok[:, None]).astype(BF16)

    a_re, a_im, b_tiles, c_re_tiles, c_im_tiles, scalars = _prep(
        ssm_lam_re, ssm_lam_im, ssm_log_step, ssm_b_re, ssm_b_im, ssm_c_re, ssm_c_im, lam_qk, q_norm_g, k_norm_g)
    x2d = x.reshape(t, d)
    for i in range(depth):
        lam_init = _lambda_init(i)
        tile_gain = lambda g: jnp.tile(g.reshape(HEAD_W, 1), (COL_CHUNK // HEAD_W, 1))
        qt, k, vt, u_tm, gates = _inproj(
            x2d, attn_norm_g[i].reshape(1, d), w_in,
            tile_gain(q_norm_g[i]), tile_gain(k_norm_g[i]), b_gate[i].reshape(1, -1), seq=seq, layer=i)

        o = _attention(scalars[i], qt, k, vt, subln_g[i].reshape(1, HEAD_W), seq=seq, lam_init=lam_init)

        z = _ssm(u_tm, b_tiles[i], c_re_tiles[i], c_im_tiles[i],
                 a_re[i, ::SSM_GROUP].reshape(1, N_STATE), a_im[i, ::SSM_GROUP].reshape(1, N_STATE),
                 ssm_d[i].reshape(1, SSM_WIDTH))

        w_r = jnp.concatenate(
            [w_router_group[i], jnp.transpose(w_router_expert[i], (1, 0, 2)).reshape(d, N_EXPERTS),
             jnp.zeros((d, ROUTE_W - N_GROUPS - N_EXPERTS), F32)], axis=1)
        w_r_hi = w_r.astype(BF16)
        w_r_lo = (w_r - w_r_hi.astype(F32)).astype(BF16)
        b_r = jnp.concatenate([b_router_group[i], b_router_expert[i].reshape(-1),
                               jnp.zeros((ROUTE_W - N_GROUPS - N_EXPERTS,), F32)]).reshape(1, ROUTE_W)
        x1, hn, route, route_t, counts = _mix(
            x2d, o, z.reshape(t, SSM_WIDTH), gates,
            w_attn_up, w_glu, b_glu[i].reshape(1, -1), w_ssm_up, w_out, ffn_norm_g[i].reshape(1, d),
            w_r_hi, w_r_lo, b_r, ltri, layer=i)

        counts = counts[:, 0, :N_GROUPS].astype(jnp.int32)
        moe_args = (counts, hn, route, route_t, w_exp_gate, w_exp_up, w_exp_down)
        contrib = _moe(*moe_args, layer=i)
        contrib = lax.cond(jnp.max(counts) > MOE_CAP,
                           lambda c: _moe(*moe_args, layer=i, prev=c), lambda c: c, contrib)
        x2d = _ple(x1, contrib, p, ple_norm_g[i].reshape(1, d), w_ple_gate, w_ple, layer=i)
    return x2d.reshape(bsz, seq, d)
```

```python
import functools
import math

import jax
import jax.numpy as jnp
from jax import lax
from jax.experimental import pallas as pl
from jax.experimental.pallas import tpu as pltpu

F32 = jnp.float32
BF16 = jnp.bfloat16

D_MODEL = 1024
ATT_HEADS = 8
ATT_HEAD_DIM = 64
HEAD_W = 2 * ATT_HEAD_DIM
ATT_QK_WIDTH = ATT_HEADS * HEAD_W
ATT_V_WIDTH = ATT_HEADS * HEAD_W
SSM_WIDTH = 512
SSM_GROUP = 16
SSM_GROUPS = SSM_WIDTH // SSM_GROUP
SSM_STATE = 64
N_STATE = SSM_GROUPS * SSM_STATE
N_BRANCH = 2
IN_WIDTH = 2 * ATT_QK_WIDTH + ATT_V_WIDTH + SSM_WIDTH + N_BRANCH * D_MODEL
N_GROUPS = 4
EXPERTS_PER_GROUP = 8
N_EXPERTS = N_GROUPS * EXPERTS_PER_GROUP
EXPERT_FF = 256
PLE_DIM = 256
EPS = 1e-6

LANES = 128
SUBLANES = 8
MXU_N = 256
VMEM_LIMIT = 56 * 1024 * 1024

TOK_TILE = 512
MOE_CAP = 176
COL_CHUNK = 512
ATT_TILE = 256
Q_SCALE = ATT_HEAD_DIM ** -0.5 * math.log2(math.e)
LAM_ROW = 0
BOUND_ROW = 1
ATT_BOUND_SLACK = 1.01
ATT_BOUND_LIMIT = 60.0
SCAN_T = 64
SCAN_LANES = 1024
ROUTE_W = LANES
ROUTE_OFF = N_GROUPS
GRP_LANE = 0
RANK_LANE = 1


def _lambda_init(layer_idx):
    return 0.8 - 0.6 * math.exp(-0.3 * layer_idx)


def _dot(a, b):
    return jnp.dot(a, b, preferred_element_type=F32)


def _const_spec(shape):
    nd = len(shape)
    return pl.BlockSpec(shape, lambda *_: (0,) * nd, pipeline_mode=pl.Buffered(1))


def _layer_weight(stacked, layer):
    shape = stacked.shape[1:]
    return pl.BlockSpec((None,) + shape, lambda *_: (layer,) + (0,) * len(shape), pipeline_mode=pl.Buffered(1))


def _round_once(pairs):
    @pl.when(pl.program_id(0) == 0)
    def _():
        for w_ref, w_scr in pairs:
            w_scr[...] = w_ref[...].astype(BF16)


def _params(*sem):
    return pltpu.CompilerParams(dimension_semantics=sem, vmem_limit_bytes=VMEM_LIMIT)


def _prep_kernel(lre_ref, lim_ref, lstep_ref, bre_ref, bim_ref, cre_ref, cim_ref, lqk_ref, qg_ref, kg_ref,
                 are_ref, aim_ref, bt_ref, ctre_ref, ctim_ref, lam_ref):
    lr = lre_ref[...]
    li = lim_ref[...]
    dt = jnp.exp(lstep_ref[...])
    mag = jnp.exp(lr * dt)
    ab_re = mag * jnp.cos(li * dt)
    ab_im = mag * jnp.sin(li * dt)
    den = lr * lr + li * li
    nr = ab_re - 1.0
    coef_re = (nr * lr + ab_im * li) / den
    coef_im = (ab_im * lr - nr * li) / den
    br = bre_ref[...]
    bi = bim_ref[...]
    are_ref[...] = ab_re
    aim_ref[...] = ab_im
    bb_re = coef_re * br - coef_im * bi
    bb_im = coef_re * bi + coef_im * br

    def spread(x, reps):
        w = x.shape[1]
        src = lax.broadcasted_iota(jnp.int32, (w, w * reps), 0)
        dst = lax.broadcasted_iota(jnp.int32, (w, w * reps), 1) & (w - 1)
        return _dot(x.astype(BF16), jnp.where(src == dst, 1.0, 0.0).astype(BF16))

    def same_group(shape, row0, row_shift, col0, col_shift):
        rows = (lax.broadcasted_iota(jnp.int32, shape, 0) + row0) >> row_shift
        cols = (lax.broadcasted_iota(jnp.int32, shape, 1) + col0) >> col_shift
        return rows == cols

    n_re = N_STATE // MXU_N
    ch_shift = SSM_GROUP.bit_length() - 1
    st_shift = SSM_STATE.bit_length() - 1
    for part, bb in enumerate((bb_re, bb_im)):
        for j in range(n_re):
            r0 = LANES * ((j * MXU_N // SSM_STATE * SSM_GROUP) // LANES)
            x = spread(bb[r0:r0 + LANES, :], MXU_N // SSM_STATE)
            keep = same_group(x.shape, r0, ch_shift, j * MXU_N, st_shift)
            bt_ref[part * n_re + j] = jnp.where(keep, x, 0.0).astype(BF16)
    half = N_STATE // 2
    for c_ref, ct_ref in ((cre_ref, ctre_ref), (cim_ref, ctim_ref)):
        for n in range(2):
            y = spread(c_ref[n * half:(n + 1) * half, :], MXU_N // SSM_GROUP)
            keep = same_group(y.shape, n * half, st_shift, n * MXU_N, ch_shift)
            ct_ref[n] = jnp.where(keep, y, 0.0).astype(BF16)

    lq = lqk_ref[...]
    s01 = jnp.sum(lq[0:1] * lq[1:2], axis=-1, keepdims=True)
    s23 = jnp.sum(lq[2:3] * lq[3:4], axis=-1, keepdims=True)
    lam_dyn = jnp.exp(s01) - jnp.exp(s23)
    g_max = jnp.max(jnp.abs(qg_ref[...]), keepdims=True) * jnp.max(jnp.abs(kg_ref[...]), keepdims=True)
    bound = g_max * (ATT_HEAD_DIM * Q_SCALE * ATT_BOUND_SLACK)
    row = lax.broadcasted_iota(jnp.int32, lam_ref.shape, 0)
    lam_ref[...] = jnp.where(row == LAM_ROW, lam_dyn, bound)


def _prep(lam_re, lam_im, log_step, b_re, b_im, c_re, c_im, lam_qk, q_gain, k_gain):
    depth = lam_re.shape[0]
    rows = SSM_WIDTH
    rep = lambda a: jnp.repeat(a, SSM_GROUP, axis=1)
    lstep = jnp.broadcast_to(log_step[:, :, None], lam_re.shape)
    tr_b = lambda b: jnp.transpose(b, (0, 1, 3, 2)).reshape(depth, rows, SSM_STATE)
    tr_c = lambda c: jnp.transpose(c, (0, 1, 3, 2)).reshape(depth, N_STATE, SSM_GROUP)
    lay = lambda *shape: pl.BlockSpec((None,) + shape, lambda i: (i,) + (0,) * len(shape))
    blk = lay(rows, SSM_STATE)
    out = jax.ShapeDtypeStruct((depth, rows, SSM_STATE), F32)
    n_bt = 2 * N_STATE // MXU_N
    return pl.pallas_call(
        _prep_kernel,
        grid=(depth,),
        in_specs=[blk, blk, blk, blk, blk, lay(N_STATE, SSM_GROUP), lay(N_STATE, SSM_GROUP),
                  lay(4, ATT_HEAD_DIM), lay(2, ATT_HEAD_DIM), lay(2, ATT_HEAD_DIM)],
        out_specs=[blk, blk, lay(n_bt, LANES, MXU_N), lay(2, N_STATE // 2, MXU_N), lay(2, N_STATE // 2, MXU_N),
                   lay(SUBLANES, LANES)],
        out_shape=[out, out, jax.ShapeDtypeStruct((depth, n_bt, LANES, MXU_N), BF16),
                   jax.ShapeDtypeStruct((depth, 2, N_STATE // 2, MXU_N), BF16),
                   jax.ShapeDtypeStruct((depth, 2, N_STATE // 2, MXU_N), BF16),
                   jax.ShapeDtypeStruct((depth, SUBLANES, LANES), F32)],
        compiler_params=_params("arbitrary"),
        name="prep",
    )(rep(lam_re), rep(lam_im), rep(lstep), tr_b(b_re), tr_b(b_im), tr_c(c_re), tr_c(c_im), lam_qk, q_gain, k_gain)


def _inproj_kernel(x_ref, g_ref, w_ref, qg_ref, kg_ref, bg_ref,
                   qt_ref, k_ref, vt_ref, u_hbm, gate_ref, w_scr, u_buf, u_sem, *, n_sb):
    _round_once([(w_ref, w_scr)])
    step = pl.program_id(0)

    def u_copy(i):
        s0 = pl.multiple_of((i % n_sb) * TOK_TILE, TOK_TILE)
        return pltpu.make_async_copy(u_buf, u_hbm.at[pl.ds(s0, TOK_TILE), i // n_sb], u_sem)

    xf = x_ref[...]
    ms = jnp.mean(xf * xf, axis=-1, keepdims=True)
    h = (xf * lax.rsqrt(ms + EPS) * g_ref[...]).astype(BF16)

    def head_norm_t(y, gain_col):
        rows = y.shape[0]
        y3 = y.T.reshape(COL_CHUNK // ATT_HEAD_DIM, ATT_HEAD_DIM, rows)
        ms = jnp.sum(y3 * y3, axis=1, keepdims=True) * (1.0 / ATT_HEAD_DIM)
        return (y3 * lax.rsqrt(ms + EPS)).reshape(COL_CHUNK, rows) * gain_col

    n_qk = ATT_QK_WIDTH // COL_CHUNK
    n_v = ATT_V_WIDTH // COL_CHUNK
    for c in range(IN_WIDTH // COL_CHUNK):
        y = _dot(h, w_scr[:, c * COL_CHUNK:(c + 1) * COL_CHUNK])
        if c < n_qk:
            qt_ref[c * COL_CHUNK:(c + 1) * COL_CHUNK, :] = (head_norm_t(y, qg_ref[...]) * Q_SCALE).astype(BF16)
        elif c < 2 * n_qk:
            cc = c - n_qk
            k_ref[:, cc * COL_CHUNK:(cc + 1) * COL_CHUNK] = head_norm_t(y, kg_ref[...]).T.astype(BF16)
        elif c < 2 * n_qk + n_v:
            cc = c - 2 * n_qk
            vt_ref[cc * COL_CHUNK:(cc + 1) * COL_CHUNK, :] = y.T.astype(BF16)
        elif c == 2 * n_qk + n_v:
            @pl.when(step > 0)
            def _():
                u_copy(step - 1).wait()
            u_buf[...] = y
        else:
            cc = c - (2 * n_qk + n_v + 1)
            b = bg_ref[:, cc * COL_CHUNK:(cc + 1) * COL_CHUNK]
            gate_ref[:, cc * COL_CHUNK:(cc + 1) * COL_CHUNK] = jax.nn.sigmoid(y + b).astype(BF16)

    u_copy(step).start()

    @pl.when(step == pl.num_programs(0) - 1)
    def _():
        u_copy(step).wait()


def _inproj(x2d, norm_g, w_in, qg, kg, b_gate, *, seq, layer):
    t = x2d.shape[0]
    n_sb = seq // TOK_TILE
    bsz = t // seq
    row = lambda w: pl.BlockSpec((TOK_TILE, w), lambda i: (i, 0))
    col = lambda w: pl.BlockSpec((w, TOK_TILE), lambda i: (i // n_sb, i % n_sb))
    return pl.pallas_call(
        functools.partial(_inproj_kernel, n_sb=n_sb),
        grid=(t // TOK_TILE,),
        in_specs=[row(D_MODEL), _const_spec((1, D_MODEL)), _layer_weight(w_in, layer),
                  _const_spec((COL_CHUNK, 1)), _const_spec((COL_CHUNK, 1)),
                  _const_spec((1, N_BRANCH * D_MODEL))],
        out_specs=[col(ATT_QK_WIDTH), row(ATT_QK_WIDTH), col(ATT_V_WIDTH),
                   pl.BlockSpec(memory_space=pl.ANY),
                   row(N_BRANCH * D_MODEL)],
        out_shape=[jax.ShapeDtypeStruct((bsz * ATT_QK_WIDTH, seq), BF16),
                   jax.ShapeDtypeStruct((t, ATT_QK_WIDTH), BF16),
                   jax.ShapeDtypeStruct((bsz * ATT_V_WIDTH, seq), BF16),
                   jax.ShapeDtypeStruct((seq, bsz, SSM_WIDTH), F32),
                   jax.ShapeDtypeStruct((t, N_BRANCH * D_MODEL), BF16)],
        scratch_shapes=[pltpu.VMEM((D_MODEL, IN_WIDTH), BF16), pltpu.VMEM((TOK_TILE, SSM_WIDTH), F32),
                        pltpu.SemaphoreType.DMA(())],
        compiler_params=_params("arbitrary"),
        name="inproj",
    )(x2d, norm_g, w_in, qg, kg, b_gate)


def _attn_kernel(sc_ref, qt_ref, k_ref, vt_ref, g_ref, o_ref, *, lam_init):
    tq = ATT_TILE
    seq = k_ref.shape[0]
    nq = seq // tq
    lam = sc_ref[LAM_ROW, 0] + lam_init
    bound = sc_ref[BOUND_ROW, 0]
    head_row = lax.broadcasted_iota(jnp.int32, (HEAD_W, tq), 0)
    key_pos = lax.broadcasted_iota(jnp.int32, (tq, 2 * tq), 0)
    qry_pos = lax.broadcasted_iota(jnp.int32, (tq, 2 * tq), 1) & (tq - 1)
    causal = key_pos <= qry_pos
    zero = jnp.zeros((HEAD_W, tq), BF16)

    def tile(qi, shift_by_bound):
        qt = qt_ref[:, qi * tq:(qi + 1) * tq]
        qst = jnp.concatenate([jnp.where(head_row < ATT_HEAD_DIM, qt, zero),
                               jnp.where(head_row >= ATT_HEAD_DIM, qt, zero)], axis=1)
        nk = qi * tq
        s_d = jnp.where(causal, _dot(k_ref[nk:nk + tq, :], qst), -jnp.inf)
        if qi > 0:
            s_m = _dot(k_ref[0:nk, :], qst)
        if shift_by_bound:
            m = bound
        else:
            m = jnp.max(s_d, axis=0, keepdims=True)
            if qi > 0:
                m = jnp.maximum(m, jnp.max(s_m, axis=0, keepdims=True))
        p_d = jnp.exp2(s_d - m)
        l = jnp.sum(p_d, axis=0, keepdims=True)
        acc = _dot(vt_ref[:, nk:nk + tq], p_d.astype(BF16))
        if qi > 0:
            p_m = jnp.exp2(s_m - m)
            l = l + jnp.sum(p_m, axis=0, keepdims=True)
            acc = acc + _dot(vt_ref[:, 0:nk], p_m.astype(BF16))
        accn = acc * (1.0 / l)
        ot = accn[:, :tq] - lam * accn[:, tq:]
        ms = jnp.mean(ot * ot, axis=0, keepdims=True)
        o = (ot * lax.rsqrt(ms + EPS)).T * g_ref[...] * (1.0 - lam_init)
        o_ref[qi * tq:(qi + 1) * tq, :] = o.astype(BF16)

    @pl.when(bound < ATT_BOUND_LIMIT)
    def _():
        for qi in range(nq):
            tile(qi, True)

    @pl.when(jnp.logical_not(bound < ATT_BOUND_LIMIT))
    def _():
        for qi in range(nq):
            tile(qi, False)


def _attention(scalars, qt, k, vt, subln_g, *, seq, lam_init):
    t = k.shape[0]
    bsz = t // seq
    fm = pl.BlockSpec((HEAD_W, seq), lambda b, h: (b * ATT_HEADS + h, 0))
    tm = pl.BlockSpec((seq, HEAD_W), lambda b, h: (b, h))
    return pl.pallas_call(
        functools.partial(_attn_kernel, lam_init=lam_init),
        grid=(bsz, ATT_HEADS),
        in_specs=[pl.BlockSpec(memory_space=pltpu.SMEM), fm, tm, fm, _const_spec((1, HEAD_W))],
        out_specs=tm,
        out_shape=jax.ShapeDtypeStruct((t, ATT_V_WIDTH), BF16),
        compiler_params=_params("parallel", "parallel"),
        name="attn",
    )(scalars, qt, k, vt, subln_g)


def _ssm_kernel(u_ref, bt_ref, cre_ref, cim_ref, are_ref, aim_ref, d_ref, z_hbm, st_scr, state_scr, z_buf, z_sem):
    n_t, bsz, _ = u_ref.shape
    rows = n_t * bsz
    chunk = pl.program_id(0)

    def z_copies(i):
        t0 = pl.multiple_of(i * n_t, n_t)
        return [pltpu.make_async_copy(z_buf.at[:, b], z_hbm.at[b, pl.ds(t0, n_t)], z_sem) for b in range(bsz)]

    @pl.when(chunk == 0)
    def _():
        state_scr[...] = jnp.zeros(state_scr.shape, F32)

    u = u_ref[...].reshape(rows, SSM_WIDTH)
    ub = u.astype(BF16)
    n_tiles = 2 * N_STATE // MXU_N
    for j in range(n_tiles):
        k0 = LANES * (((j % (n_tiles // 2)) * MXU_N // SSM_STATE * SSM_GROUP) // LANES)
        st_scr[:, j * MXU_N:(j + 1) * MXU_N] = _dot(ub[:, k0:k0 + LANES], bt_ref[j])

    for cc in range(N_STATE // SCAN_LANES):
        lo = cc * SCAN_LANES
        hi = N_STATE + lo
        a_r = jnp.broadcast_to(are_ref[:, lo:lo + SCAN_LANES], (bsz, SCAN_LANES))
        a_i = jnp.broadcast_to(aim_ref[:, lo:lo + SCAN_LANES], (bsz, SCAN_LANES))

        def step(t, carry, lo=lo, hi=hi, a_r=a_r, a_i=a_i):
            s_r, s_i = carry
            r0 = pl.multiple_of(t * bsz, bsz)
            n_r = a_r * s_r - a_i * s_i + st_scr[pl.ds(r0, bsz), lo:lo + SCAN_LANES]
            n_i = a_r * s_i + a_i * s_r + st_scr[pl.ds(r0, bsz), hi:hi + SCAN_LANES]
            st_scr[pl.ds(r0, bsz), lo:lo + SCAN_LANES] = n_r
            st_scr[pl.ds(r0, bsz), hi:hi + SCAN_LANES] = n_i
            return n_r, n_i

        s_r, s_i = lax.fori_loop(0, rows // bsz, step,
                                 (state_scr[:, lo:lo + SCAN_LANES], state_scr[:, hi:hi + SCAN_LANES]),
                                 unroll=8)
        state_scr[:, lo:lo + SCAN_LANES] = s_r
        state_scr[:, hi:hi + SCAN_LANES] = s_i

    @pl.when(chunk > 0)
    def _():
        for cp in z_copies(chunk - 1):
            cp.wait()
    half = N_STATE // 2
    for n in range(2):
        s_re = st_scr[:, n * half:(n + 1) * half].astype(BF16)
        s_im = st_scr[:, N_STATE + n * half:N_STATE + (n + 1) * half].astype(BF16)
        cols = slice(n * MXU_N, (n + 1) * MXU_N)
        y = _dot(s_re, cre_ref[n]) - _dot(s_im, cim_ref[n]) + d_ref[:, cols] * u[:, cols]
        z_buf[:, :, cols] = jax.nn.gelu(y).reshape(n_t, bsz, MXU_N)
    for cp in z_copies(chunk):
        cp.start()

    @pl.when(chunk == pl.num_programs(0) - 1)
    def _():
        for cp in z_copies(chunk):
            cp.wait()


def _ssm(u_tm, b_tiles, c_re_tiles, c_im_tiles, a_re, a_im, d_skip):
    seq, bsz, _ = u_tm.shape
    return pl.pallas_call(
        _ssm_kernel,
        grid=(seq // SCAN_T,),
        in_specs=[pl.BlockSpec((SCAN_T, bsz, SSM_WIDTH), lambda i: (i, 0, 0)),
                  _const_spec(b_tiles.shape), _const_spec(c_re_tiles.shape),
                  _const_spec(c_im_tiles.shape), _const_spec((1, N_STATE)), _const_spec((1, N_STATE)),
                  _const_spec((1, SSM_WIDTH))],
        out_specs=pl.BlockSpec(memory_space=pl.ANY),
        out_shape=jax.ShapeDtypeStruct((bsz, seq, SSM_WIDTH), F32),
        scratch_shapes=[pltpu.VMEM((SCAN_T * bsz, 2 * N_STATE), F32), pltpu.VMEM((bsz, 2 * N_STATE), F32),
                        pltpu.VMEM((SCAN_T, bsz, SSM_WIDTH), F32), pltpu.SemaphoreType.DMA(())],
        compiler_params=_params("arbitrary"),
        name="ssm",
    )(u_tm, b_tiles, c_re_tiles, c_im_tiles, a_re, a_im, d_skip)


def _route(logits, ltri):
    lane = lax.broadcasted_iota(jnp.int32, logits.shape, 1)
    lanef = lane.astype(F32)
    big = float(ROUTE_W)
    neg = -jnp.inf
    gl = jnp.where(lane < N_GROUPS, logits, neg)
    gmax = jnp.max(gl, axis=-1, keepdims=True)
    grp = jnp.min(jnp.where(gl == gmax, lanef, big), axis=-1, keepdims=True)
    p_grp = 1.0 / jnp.sum(jnp.exp(gl - gmax), axis=-1, keepdims=True)
    first = ROUTE_OFF + grp * EXPERTS_PER_GROUP
    in_grp = (lanef >= first) & (lanef < first + EXPERTS_PER_GROUP)
    el = jnp.where(in_grp, logits, neg)
    v1 = jnp.max(el, axis=-1, keepdims=True)
    i1 = jnp.min(jnp.where(el == v1, lanef, big), axis=-1, keepdims=True)
    el2 = jnp.where(lanef == i1, neg, el)
    v2 = jnp.max(el2, axis=-1, keepdims=True)
    i2 = jnp.min(jnp.where(el2 == v2, lanef, big), axis=-1, keepdims=True)
    e21 = jnp.exp(v2 - v1)
    w1 = p_grp / (1.0 + e21)
    w2 = p_grp * (e21 / (1.0 + e21))
    member = lanef == grp
    before = _dot(ltri, jnp.where(member, 1.0, 0.0).astype(BF16))
    rank = jnp.sum(jnp.where(member, before, 0.0), axis=-1, keepdims=True)
    counts = jnp.sum(jnp.where(member, 1.0, 0.0), axis=0, keepdims=True)
    return (jnp.where(lanef == i1, w1, 0.0) + jnp.where(lanef == i2, w2, 0.0)
            + jnp.where(lane == GRP_LANE, grp, 0.0) + jnp.where(lane == RANK_LANE, rank, 0.0)), counts


def _mix_kernel(x_ref, o_ref, z_ref, gate_ref, wau_ref, wglu_ref, bglu_ref, wsu_ref, wout_ref,
                fg_ref, wrh_ref, wrl_ref, br_ref, ltri_ref, x1_ref, hn_ref, route_ref, route_t_ref, cnt_ref,
                wau_scr, wglu_scr, wsu_scr, wout_scr):
    _round_once([(wau_ref, wau_scr), (wglu_ref, wglu_scr), (wsu_ref, wsu_scr), (wout_ref, wout_scr)])
    y_att = _dot(o_ref[...], wau_scr[...])
    z = z_ref[...]
    z = z * jax.nn.sigmoid(_dot(z.astype(BF16), wglu_scr[...]) + bglu_ref[...])
    y_ssm = _dot(z.astype(BF16), wsu_scr[...])
    mixed = (gate_ref[:, :D_MODEL].astype(F32) * y_att + gate_ref[:, D_MODEL:].astype(F32) * y_ssm)
    x1 = x_ref[...] + _dot(mixed.astype(BF16), wout_scr[...])
    x1_ref[...] = x1
    ms = jnp.mean(x1 * x1, axis=-1, keepdims=True)
    hn = x1 * lax.rsqrt(ms + EPS) * fg_ref[...]
    hn_hi = hn.astype(BF16)
    hn_ref[...] = hn_hi
    hn_lo = (hn - hn_hi.astype(F32)).astype(BF16)
    logits = (_dot(hn_hi, wrh_ref[...]) + _dot(hn_lo, wrh_ref[...]) + _dot(hn_hi, wrl_ref[...])
              + br_ref[...])
    route, counts = _route(logits, ltri_ref[...])
    route_ref[...] = route
    route_t_ref[...] = route.T[:SUBLANES, :]
    cnt_ref[...] = jnp.broadcast_to(counts, cnt_ref.shape)


def _mix(x2d, o, z, gates, wau, wglu, bglu, wsu, wout, ffn_g, wr_hi, wr_lo, br, ltri, *, layer):
    t = x2d.shape[0]
    row = lambda w: pl.BlockSpec((TOK_TILE, w), lambda i: (i, 0))
    return pl.pallas_call(
        _mix_kernel,
        grid=(t // TOK_TILE,),
        in_specs=[row(D_MODEL), row(ATT_V_WIDTH), row(SSM_WIDTH), row(N_BRANCH * D_MODEL),
                  _layer_weight(wau, layer), _layer_weight(wglu, layer), _const_spec(bglu.shape),
                  _layer_weight(wsu, layer), _layer_weight(wout, layer), _const_spec(ffn_g.shape),
                  _const_spec(wr_hi.shape), _const_spec(wr_lo.shape), _const_spec(br.shape),
                  _const_spec(ltri.shape)],
        out_specs=[row(D_MODEL), row(D_MODEL), row(ROUTE_W),
                   pl.BlockSpec((SUBLANES, TOK_TILE), lambda i: (0, i)),
                   pl.BlockSpec((None, SUBLANES, ROUTE_W), lambda i: (i, 0, 0))],
        out_shape=[jax.ShapeDtypeStruct((t, D_MODEL), F32),
                   jax.ShapeDtypeStruct((t, D_MODEL), BF16),
                   jax.ShapeDtypeStruct((t, ROUTE_W), F32),
                   jax.ShapeDtypeStruct((SUBLANES, t), F32),
                   jax.ShapeDtypeStruct((t // TOK_TILE, SUBLANES, ROUTE_W), F32)],
        scratch_shapes=[pltpu.VMEM(w.shape[1:], BF16) for w in (wau, wglu, wsu, wout)],
        compiler_params=_params("arbitrary"),
        name="mix",
    )(x2d, o, z, gates, wau, wglu, bglu, wsu, wout, ffn_g, wr_hi, wr_lo, br, ltri)


def _moe_kernel(cnt_ref, hn_ref, route_ref, route_t_ref, *rest, layer, overflow):
    if overflow:
        prev_ref, *rest = rest
    wg_hbm, wu_hbm, wd_hbm, out_ref, wg_f32, wu_f32, wd_f32, w_sem, wg_scr, wu_scr, wd_scr = rest
    n = pl.program_id(0)

    def fetch(g):
        pairs = ((wg_hbm, wg_f32), (wu_hbm, wu_f32), (wd_hbm, wd_f32))
        return [pltpu.make_async_copy(src.at[layer, g], dst, w_sem.at[k]) for k, (src, dst) in enumerate(pairs)]

    @pl.when(pl.program_id(1) == 0)
    def _():
        @pl.when(n == 0)
        def _():
            for cp in fetch(n):
                cp.start()
        for cp in fetch(n):
            cp.wait()
        for j in range(EXPERTS_PER_GROUP):
            cols = slice(j * EXPERT_FF, (j + 1) * EXPERT_FF)
            wg_scr[:, cols] = wg_f32[j].astype(BF16)
            wu_scr[:, cols] = wu_f32[j].astype(BF16)
            wd_scr[cols, :] = wd_f32[j].astype(BF16)
        @pl.when(n + 1 < N_GROUPS)
        def _():
            for cp in fetch(n + 1):
                cp.start()

    tm = hn_ref.shape[0]
    nf = n.astype(F32)
    route = route_ref[...]
    lane = lax.broadcasted_iota(jnp.int32, route.shape, 1)
    grp_c = jnp.sum(jnp.where(lane == GRP_LANE, route, 0.0), axis=-1, keepdims=True)
    rank_c = jnp.sum(jnp.where(lane == RANK_LANE, route, 0.0), axis=-1, keepdims=True)
    grp_r = route_t_ref[GRP_LANE:GRP_LANE + 1, :]
    rank_r = route_t_ref[RANK_LANE:RANK_LANE + 1, :]
    r1 = route.astype(BF16)
    r2 = (route - r1.astype(F32)).astype(BF16)
    r3 = (route - r1.astype(F32) - r2.astype(F32)).astype(BF16)
    pieces = jnp.concatenate([r1, r2, r3], axis=1)

    def group_rows(base):
        slot_r = lax.broadcasted_iota(jnp.int32, (MOE_CAP, tm), 0).astype(F32) + base
        take = jnp.where((grp_r == nf) & (rank_r == slot_r), 1.0, 0.0).astype(BF16)
        slot_c = lax.broadcasted_iota(jnp.int32, (tm, MOE_CAP), 1).astype(F32) + base
        put = jnp.where((grp_c == nf) & (rank_c == slot_c), 1.0, 0.0).astype(BF16)
        xc = _dot(take, hn_ref[...]).astype(BF16)
        rc = _dot(take, pieces)
        rc = rc[:, :ROUTE_W] + rc[:, ROUTE_W:2 * ROUTE_W] + rc[:, 2 * ROUTE_W:]
        lane_c = lax.broadcasted_iota(jnp.int32, rc.shape, 1)
        first = ROUTE_OFF + n * EXPERTS_PER_GROUP
        act = jax.nn.silu(_dot(xc, wg_scr[...])) * _dot(xc, wu_scr[...])
        blocks = []
        for j in range(EXPERTS_PER_GROUP):
            w_j = jnp.sum(jnp.where(lane_c == first + j, rc, 0.0), axis=-1, keepdims=True)
            blocks.append((act[:, j * EXPERT_FF:(j + 1) * EXPERT_FF] * w_j).astype(BF16))
        y = _dot(jnp.concatenate(blocks, axis=1), wd_scr[...])
        return _dot(put, y.astype(BF16))

    if not overflow:
        out_ref[...] = group_rows(jnp.float32(0.0)).astype(BF16)
        return

    out_ref[...] = prev_ref[...]
    n_pass = lax.div(cnt_ref[pl.program_id(1), n] + (MOE_CAP - 1), MOE_CAP)

    def extra_pass(k, carry):
        more = group_rows((k * MOE_CAP).astype(F32))
        out_ref[...] = (out_ref[...].astype(F32) + more).astype(BF16)
        return carry

    lax.fori_loop(1, n_pass, extra_pass, 0)


def _moe(counts, hn, route, route_t, w_gate, w_up, w_down, *, layer, prev=None):
    t = hn.shape[0]
    overflow = prev is not None
    mode = dict(pipeline_mode=pl.Buffered(1)) if overflow else {}
    row = lambda w: pl.BlockSpec((TOK_TILE, w), lambda n, i, cnt: (i, 0), **mode)
    out_blk = pl.BlockSpec((None, TOK_TILE, D_MODEL), lambda n, i, cnt: (n, i, 0))
    up_shape = (EXPERTS_PER_GROUP, D_MODEL, EXPERT_FF)
    down_shape = (EXPERTS_PER_GROUP, EXPERT_FF, D_MODEL)
    ff = EXPERTS_PER_GROUP * EXPERT_FF
    operands = (counts, hn, route, route_t) + ((prev,) if overflow else ()) + (w_gate, w_up, w_down)
    return pl.pallas_call(
        functools.partial(_moe_kernel, layer=layer, overflow=overflow),
        grid_spec=pltpu.PrefetchScalarGridSpec(
            num_scalar_prefetch=1,
            grid=(N_GROUPS, t // TOK_TILE),
            in_specs=[row(D_MODEL), row(ROUTE_W), pl.BlockSpec((SUBLANES, TOK_TILE), lambda n, i, cnt: (0, i))]
                     + ([pl.BlockSpec((None, TOK_TILE, D_MODEL), lambda n, i, cnt: (n, i, 0), **mode)]
                        if overflow else [])
                     + [pl.BlockSpec(memory_space=pl.ANY), pl.BlockSpec(memory_space=pl.ANY),
                        pl.BlockSpec(memory_space=pl.ANY)],
            out_specs=out_blk,
            scratch_shapes=[pltpu.VMEM(up_shape, F32), pltpu.VMEM(up_shape, F32), pltpu.VMEM(down_shape, F32),
                            pltpu.SemaphoreType.DMA((3,)),
                            pltpu.VMEM((D_MODEL, ff), BF16), pltpu.VMEM((D_MODEL, ff), BF16),
                            pltpu.VMEM((ff, D_MODEL), BF16)]),
        out_shape=jax.ShapeDtypeStruct((N_GROUPS, t, D_MODEL), BF16),
        input_output_aliases={4: 0} if overflow else {},
        compiler_params=_params("arbitrary", "arbitrary"),
        name="moe_overflow" if overflow else "moe",
    )(*operands)


def _ple_kernel(x1_ref, c_ref, p_ref, pg_ref, wpg_ref, wple_ref, out_ref, wpg_scr, wple_scr):
    _round_once([(wpg_ref, wpg_scr), (wple_ref, wple_scr)])
    x2 = x1_ref[...]
    for n in range(N_GROUPS):
        x2 = x2 + c_ref[n].astype(F32)
    ms = jnp.mean(x2 * x2, axis=-1, keepdims=True)
    hp = (x2 * lax.rsqrt(ms + EPS) * pg_ref[...]).astype(BF16)
    gate = jax.nn.sigmoid(_dot(hp, wpg_scr[...]))
    out_ref[...] = x2 + gate * _dot(p_ref[...].astype(BF16), wple_scr[...])


def _ple(x1, contrib, p, ple_g, wpg, wple, *, layer):
    t = x1.shape[0]
    n_sb = p.shape[2] // TOK_TILE
    row = lambda w: pl.BlockSpec((TOK_TILE, w), lambda i: (i, 0))
    return pl.pallas_call(
        _ple_kernel,
        grid=(t // TOK_TILE,),
        in_specs=[row(D_MODEL), pl.BlockSpec((N_GROUPS, TOK_TILE, D_MODEL), lambda i: (0, i, 0)),
                  pl.BlockSpec((None, None, TOK_TILE, PLE_DIM), lambda i: (layer, i // n_sb, i % n_sb, 0)),
                  _const_spec(ple_g.shape), _layer_weight(wpg, layer), _layer_weight(wple, layer)],
        out_specs=row(D_MODEL),
        out_shape=jax.ShapeDtypeStruct((t, D_MODEL), F32),
        scratch_shapes=[pltpu.VMEM(wpg.shape[1:], BF16), pltpu.VMEM(wple.shape[1:], BF16)],
        compiler_params=_params("arbitrary"),
        name="ple",
    )(x1, contrib, p, ple_g, wpg, wple)


def kernel(x, p, attn_norm_g, w_in, b_gate, q_norm_g, k_norm_g, lam_qk, subln_g, w_attn_up, ssm_lam_re, ssm_lam_im, ssm_log_step, ssm_b_re, ssm_b_im, ssm_c_re, ssm_c_im, ssm_d, w_glu, b_glu, w_ssm_up, w_out, ffn_norm_g, w_router_group, b_router_group, w_router_expert, b_router_expert, w_exp_gate, w_exp_up, w_exp_down, ple_norm_g, w_ple_gate, w_ple):
    bsz, seq, d = x.shape
    depth = w_in.shape[0]
    t = bsz * seq
    assert d == D_MODEL and seq % TOK_TILE == 0 and seq % ATT_TILE == 0
    assert seq % SCAN_T == 0 and bsz == SUBLANES
    tok = jnp.arange(TOK_TILE)
    ltri = (tok[None, :] < tok[:, None]).astype(BF16)

    a_re, a_im, b_tiles, c_re_tiles, c_im_tiles, scalars = _prep(
        ssm_lam_re, ssm_lam_im, ssm_log_step, ssm_b_re, ssm_b_im, ssm_c_re, ssm_c_im, lam_qk, q_norm_g, k_norm_g)
    x2d = x.reshape(t, d)
    for i in range(depth):
        lam_init = _lambda_init(i)
        tile_gain = lambda g: jnp.tile(g.reshape(HEAD_W, 1), (COL_CHUNK // HEAD_W, 1))
        qt, k, vt, u_tm, gates = _inproj(
            x2d, attn_norm_g[i].reshape(1, d), w_in,
            tile_gain(q_norm_g[i]), tile_gain(k_norm_g[i]), b_gate[i].reshape(1, -1), seq=seq, layer=i)

        o = _attention(scalars[i], qt, k, vt, subln_g[i].reshape(1, HEAD_W), seq=seq, lam_init=lam_init)

        z = _ssm(u_tm, b_tiles[i], c_re_tiles[i], c_im_tiles[i],
                 a_re[i, ::SSM_GROUP].reshape(1, N_STATE), a_im[i, ::SSM_GROUP].reshape(1, N_STATE),
                 ssm_d[i].reshape(1, SSM_WIDTH))

        w_r = jnp.concatenate(
            [w_router_group[i], jnp.transpose(w_router_expert[i], (1, 0, 2)).reshape(d, N_EXPERTS),
             jnp.zeros((d, ROUTE_W - N_GROUPS - N_EXPERTS), F32)], axis=1)
        w_r_hi = w_r.astype(BF16)
        w_r_lo = (w_r - w_r_hi.astype(F32)).astype(BF16)
        b_r = jnp.concatenate([b_router_group[i], b_router_expert[i].reshape(-1),
                               jnp.zeros((ROUTE_W - N_GROUPS - N_EXPERTS,), F32)]).reshape(1, ROUTE_W)
        x1, hn, route, route_t, counts = _mix(
            x2d, o, z.reshape(t, SSM_WIDTH), gates,
            w_attn_up, w_glu, b_glu[i].reshape(1, -1), w_ssm_up, w_out, ffn_norm_g[i].reshape(1, d),
            w_r_hi, w_r_lo, b_r, ltri, layer=i)

        counts = counts[:, 0, :N_GROUPS].astype(jnp.int32)
        moe_args = (counts, hn, route, route_t, w_exp_gate, w_exp_up, w_exp_down)
        contrib = _moe(*moe_args, layer=i)
        contrib = lax.cond(jnp.max(counts) > MOE_CAP,
                           lambda c: _moe(*moe_args, layer=i, prev=c), lambda c: c, contrib)
        x2d = _ple(x1, contrib, p, ple_norm_g[i].reshape(1, d), w_ple_gate, w_ple, layer=i)
    return x2d.reshape(bsz, seq, d)
```

```python
import functools
import math

import jax
import jax.numpy as jnp
from jax import lax
from jax.experimental import pallas as pl
from jax.experimental.pallas import tpu as pltpu

F32 = jnp.float32
BF16 = jnp.bfloat16

D_MODEL = 1024
ATT_HEADS = 8
ATT_HEAD_DIM = 64
HEAD_W = 2 * ATT_HEAD_DIM
ATT_QK_WIDTH = ATT_HEADS * HEAD_W
ATT_V_WIDTH = ATT_HEADS * HEAD_W
SSM_WIDTH = 512
SSM_GROUP = 16
SSM_GROUPS = SSM_WIDTH // SSM_GROUP
SSM_STATE = 64
N_STATE = SSM_GROUPS * SSM_STATE
N_BRANCH = 2
IN_WIDTH = 2 * ATT_QK_WIDTH + ATT_V_WIDTH + SSM_WIDTH + N_BRANCH * D_MODEL
N_GROUPS = 4
EXPERTS_PER_GROUP = 8
N_EXPERTS = N_GROUPS * EXPERTS_PER_GROUP
EXPERT_FF = 256
PLE_DIM = 256
EPS = 1e-6

LANES = 128
SUBLANES = 8
MXU_N = 256
VMEM_LIMIT = 56 * 1024 * 1024

TOK_TILE = 512
MOE_CAP = 160
MOE_TAIL = 48
COL_CHUNK = 512
ATT_TILE = 256
Q_SCALE = ATT_HEAD_DIM ** -0.5 * math.log2(math.e)
LAM_ROW = 0
BOUND_ROW = 1
ATT_BOUND_SLACK = 1.01
ATT_BOUND_LIMIT = 60.0
SCAN_T = 64
SCAN_LANES = 1024
ROUTE_W = LANES
ROUTE_OFF = N_GROUPS
GRP_LANE = 0
RANK_LANE = 1


def _lambda_init(layer_idx):
    return 0.8 - 0.6 * math.exp(-0.3 * layer_idx)


def _dot(a, b):
    return jnp.dot(a, b, preferred_element_type=F32)


def _const_spec(shape):
    nd = len(shape)
    return pl.BlockSpec(shape, lambda *_: (0,) * nd, pipeline_mode=pl.Buffered(1))


def _layer_weight(stacked, layer):
    shape = stacked.shape[1:]
    return pl.BlockSpec((None,) + shape, lambda *_: (layer,) + (0,) * len(shape), pipeline_mode=pl.Buffered(1))


def _round_once(pairs):
    @pl.when(pl.program_id(0) == 0)
    def _():
        for w_ref, w_scr in pairs:
            w_scr[...] = w_ref[...].astype(BF16)


def _params(*sem):
    return pltpu.CompilerParams(dimension_semantics=sem, vmem_limit_bytes=VMEM_LIMIT)


def _prep_kernel(lre_ref, lim_ref, lstep_ref, bre_ref, bim_ref, cre_ref, cim_ref, lqk_ref, qg_ref, kg_ref,
                 are_ref, aim_ref, bt_ref, ctre_ref, ctim_ref, lam_ref):
    lr = lre_ref[...]
    li = lim_ref[...]
    dt = jnp.exp(lstep_ref[...])
    mag = jnp.exp(lr * dt)
    ab_re = mag * jnp.cos(li * dt)
    ab_im = mag * jnp.sin(li * dt)
    den = lr * lr + li * li
    nr = ab_re - 1.0
    coef_re = (nr * lr + ab_im * li) / den
    coef_im = (ab_im * lr - nr * li) / den
    br = bre_ref[...]
    bi = bim_ref[...]
    are_ref[...] = ab_re
    aim_ref[...] = ab_im
    bb_re = coef_re * br - coef_im * bi
    bb_im = coef_re * bi + coef_im * br

    def spread(x, reps):
        w = x.shape[1]
        src = lax.broadcasted_iota(jnp.int32, (w, w * reps), 0)
        dst = lax.broadcasted_iota(jnp.int32, (w, w * reps), 1) & (w - 1)
        return _dot(x.astype(BF16), jnp.where(src == dst, 1.0, 0.0).astype(BF16))

    def same_group(shape, row0, row_shift, col0, col_shift):
        rows = (lax.broadcasted_iota(jnp.int32, shape, 0) + row0) >> row_shift
        cols = (lax.broadcasted_iota(jnp.int32, shape, 1) + col0) >> col_shift
        return rows == cols

    n_re = N_STATE // MXU_N
    ch_shift = SSM_GROUP.bit_length() - 1
    st_shift = SSM_STATE.bit_length() - 1
    for part, bb in enumerate((bb_re, bb_im)):
        for j in range(n_re):
            r0 = LANES * ((j * MXU_N // SSM_STATE * SSM_GROUP) // LANES)
            x = spread(bb[r0:r0 + LANES, :], MXU_N // SSM_STATE)
            keep = same_group(x.shape, r0, ch_shift, j * MXU_N, st_shift)
            bt_ref[part * n_re + j] = jnp.where(keep, x, 0.0).astype(BF16)
    half = N_STATE // 2
    for c_ref, ct_ref in ((cre_ref, ctre_ref), (cim_ref, ctim_ref)):
        for n in range(2):
            y = spread(c_ref[n * half:(n + 1) * half, :], MXU_N // SSM_GROUP)
            keep = same_group(y.shape, n * half, st_shift, n * MXU_N, ch_shift)
            ct_ref[n] = jnp.where(keep, y, 0.0).astype(BF16)

    lq = lqk_ref[...]
    s01 = jnp.sum(lq[0:1] * lq[1:2], axis=-1, keepdims=True)
    s23 = jnp.sum(lq[2:3] * lq[3:4], axis=-1, keepdims=True)
    lam_dyn = jnp.exp(s01) - jnp.exp(s23)
    g_max = jnp.max(jnp.abs(qg_ref[...]), keepdims=True) * jnp.max(jnp.abs(kg_ref[...]), keepdims=True)
    bound = g_max * (ATT_HEAD_DIM * Q_SCALE * ATT_BOUND_SLACK)
    row = lax.broadcasted_iota(jnp.int32, lam_ref.shape, 0)
    lam_ref[...] = jnp.where(row == LAM_ROW, lam_dyn, bound)


def _prep(lam_re, lam_im, log_step, b_re, b_im, c_re, c_im, lam_qk, q_gain, k_gain):
    depth = lam_re.shape[0]
    rows = SSM_WIDTH
    rep = lambda a: jnp.repeat(a, SSM_GROUP, axis=1)
    lstep = jnp.broadcast_to(log_step[:, :, None], lam_re.shape)
    tr_b = lambda b: jnp.transpose(b, (0, 1, 3, 2)).reshape(depth, rows, SSM_STATE)
    tr_c = lambda c: jnp.transpose(c, (0, 1, 3, 2)).reshape(depth, N_STATE, SSM_GROUP)
    lay = lambda *shape: pl.BlockSpec((None,) + shape, lambda i: (i,) + (0,) * len(shape))
    blk = lay(rows, SSM_STATE)
    out = jax.ShapeDtypeStruct((depth, rows, SSM_STATE), F32)
    n_bt = 2 * N_STATE // MXU_N
    return pl.pallas_call(
        _prep_kernel,
        grid=(depth,),
        in_specs=[blk, blk, blk, blk, blk, lay(N_STATE, SSM_GROUP), lay(N_STATE, SSM_GROUP),
                  lay(4, ATT_HEAD_DIM), lay(2, ATT_HEAD_DIM), lay(2, ATT_HEAD_DIM)],
        out_specs=[blk, blk, lay(n_bt, LANES, MXU_N), lay(2, N_STATE // 2, MXU_N), lay(2, N_STATE // 2, MXU_N),
                   lay(SUBLANES, LANES)],
        out_shape=[out, out, jax.ShapeDtypeStruct((depth, n_bt, LANES, MXU_N), BF16),
                   jax.ShapeDtypeStruct((depth, 2, N_STATE // 2, MXU_N), BF16),
                   jax.ShapeDtypeStruct((depth, 2, N_STATE // 2, MXU_N), BF16),
                   jax.ShapeDtypeStruct((depth, SUBLANES, LANES), F32)],
        compiler_params=_params("arbitrary"),
        name="prep",
    )(rep(lam_re), rep(lam_im), rep(lstep), tr_b(b_re), tr_b(b_im), tr_c(c_re), tr_c(c_im), lam_qk, q_gain, k_gain)


def _inproj_kernel(x_ref, g_ref, w_ref, qg_ref, kg_ref, bg_ref,
                   qt_ref, k_ref, vt_ref, u_hbm, gate_ref, w_scr, u_buf, u_sem, *, n_sb):
    _round_once([(w_ref, w_scr)])
    step = pl.program_id(0)

    def u_copy(i):
        s0 = pl.multiple_of((i % n_sb) * TOK_TILE, TOK_TILE)
        return pltpu.make_async_copy(u_buf, u_hbm.at[pl.ds(s0, TOK_TILE), i // n_sb], u_sem)

    xf = x_ref[...]
    ms = jnp.mean(xf * xf, axis=-1, keepdims=True)
    h = (xf * lax.rsqrt(ms + EPS) * g_ref[...]).astype(BF16)

    def head_norm_t(y, gain_col):
        rows = y.shape[0]
        y3 = y.T.reshape(COL_CHUNK // ATT_HEAD_DIM, ATT_HEAD_DIM, rows)
        ms = jnp.sum(y3 * y3, axis=1, keepdims=True) * (1.0 / ATT_HEAD_DIM)
        return (y3 * lax.rsqrt(ms + EPS)).reshape(COL_CHUNK, rows) * gain_col

    n_qk = ATT_QK_WIDTH // COL_CHUNK
    n_v = ATT_V_WIDTH // COL_CHUNK
    for c in range(IN_WIDTH // COL_CHUNK):
        y = _dot(h, w_scr[:, c * COL_CHUNK:(c + 1) * COL_CHUNK])
        if c < n_qk:
            qt_ref[c * COL_CHUNK:(c + 1) * COL_CHUNK, :] = (head_norm_t(y, qg_ref[...]) * Q_SCALE).astype(BF16)
        elif c < 2 * n_qk:
            cc = c - n_qk
            k_ref[:, cc * COL_CHUNK:(cc + 1) * COL_CHUNK] = head_norm_t(y, kg_ref[...]).T.astype(BF16)
        elif c < 2 * n_qk + n_v:
            cc = c - 2 * n_qk
            vt_ref[cc * COL_CHUNK:(cc + 1) * COL_CHUNK, :] = y.T.astype(BF16)
        elif c == 2 * n_qk + n_v:
            @pl.when(step > 0)
            def _():
                u_copy(step - 1).wait()
            u_buf[...] = y
        else:
            cc = c - (2 * n_qk + n_v + 1)
            b = bg_ref[:, cc * COL_CHUNK:(cc + 1) * COL_CHUNK]
            gate_ref[:, cc * COL_CHUNK:(cc + 1) * COL_CHUNK] = jax.nn.sigmoid(y + b).astype(BF16)

    u_copy(step).start()

    @pl.when(step == pl.num_programs(0) - 1)
    def _():
        u_copy(step).wait()


def _inproj(x2d, norm_g, w_in, qg, kg, b_gate, *, seq, layer):
    t = x2d.shape[0]
    n_sb = seq // TOK_TILE
    bsz = t // seq
    row = lambda w: pl.BlockSpec((TOK_TILE, w), lambda i: (i, 0))
    col = lambda w: pl.BlockSpec((w, TOK_TILE), lambda i: (i // n_sb, i % n_sb))
    return pl.pallas_call(
        functools.partial(_inproj_kernel, n_sb=n_sb),
        grid=(t // TOK_TILE,),
        in_specs=[row(D_MODEL), _const_spec((1, D_MODEL)), _layer_weight(w_in, layer),
                  _const_spec((COL_CHUNK, 1)), _const_spec((COL_CHUNK, 1)),
                  _const_spec((1, N_BRANCH * D_MODEL))],
        out_specs=[col(ATT_QK_WIDTH), row(ATT_QK_WIDTH), col(ATT_V_WIDTH),
                   pl.BlockSpec(memory_space=pl.ANY),
                   row(N_BRANCH * D_MODEL)],
        out_shape=[jax.ShapeDtypeStruct((bsz * ATT_QK_WIDTH, seq), BF16),
                   jax.ShapeDtypeStruct((t, ATT_QK_WIDTH), BF16),
                   jax.ShapeDtypeStruct((bsz * ATT_V_WIDTH, seq), BF16),
                   jax.ShapeDtypeStruct((seq, bsz, SSM_WIDTH), F32),
                   jax.ShapeDtypeStruct((t, N_BRANCH * D_MODEL), BF16)],
        scratch_shapes=[pltpu.VMEM((D_MODEL, IN_WIDTH), BF16), pltpu.VMEM((TOK_TILE, SSM_WIDTH), F32),
                        pltpu.SemaphoreType.DMA(())],
        compiler_params=_params("arbitrary"),
        name="inproj",
    )(x2d, norm_g, w_in, qg, kg, b_gate)


def _attn_kernel(sc_ref, qt_ref, k_ref, vt_ref, g_ref, o_ref, *, lam_init):
    tq = ATT_TILE
    seq = k_ref.shape[0]
    nq = seq // tq
    lam = sc_ref[LAM_ROW, 0] + lam_init
    bound = sc_ref[BOUND_ROW, 0]
    head_row = lax.broadcasted_iota(jnp.int32, (HEAD_W, tq), 0)
    key_pos = lax.broadcasted_iota(jnp.int32, (tq, 2 * tq), 0)
    qry_pos = lax.broadcasted_iota(jnp.int32, (tq, 2 * tq), 1) & (tq - 1)
    causal = key_pos <= qry_pos
    zero = jnp.zeros((HEAD_W, tq), BF16)

    def tile(qi, shift_by_bound):
        qt = qt_ref[:, qi * tq:(qi + 1) * tq]
        qst = jnp.concatenate([jnp.where(head_row < ATT_HEAD_DIM, qt, zero),
                               jnp.where(head_row >= ATT_HEAD_DIM, qt, zero)], axis=1)
        nk = qi * tq
        s_d = jnp.where(causal, _dot(k_ref[nk:nk + tq, :], qst), -jnp.inf)
        if qi > 0:
            s_m = _dot(k_ref[0:nk, :], qst)
        if shift_by_bound:
            m = bound
        else:
            m = jnp.max(s_d, axis=0, keepdims=True)
            if qi > 0:
                m = jnp.maximum(m, jnp.max(s_m, axis=0, keepdims=True))
        p_d = jnp.exp2(s_d - m)
        l = jnp.sum(p_d, axis=0, keepdims=True)
        acc = _dot(vt_ref[:, nk:nk + tq], p_d.astype(BF16))
        if qi > 0:
            p_m = jnp.exp2(s_m - m)
            l = l + jnp.sum(p_m, axis=0, keepdims=True)
            acc = acc + _dot(vt_ref[:, 0:nk], p_m.astype(BF16))
        accn = acc * (1.0 / l)
        ot = accn[:, :tq] - lam * accn[:, tq:]
        ms = jnp.mean(ot * ot, axis=0, keepdims=True)
        o = (ot * lax.rsqrt(ms + EPS)).T * g_ref[...] * (1.0 - lam_init)
        o_ref[qi * tq:(qi + 1) * tq, :] = o.astype(BF16)

    @pl.when(bound < ATT_BOUND_LIMIT)
    def _():
        for qi in range(nq):
            tile(qi, True)

    @pl.when(jnp.logical_not(bound < ATT_BOUND_LIMIT))
    def _():
        for qi in range(nq):
            tile(qi, False)


def _attention(scalars, qt, k, vt, subln_g, *, seq, lam_init):
    t = k.shape[0]
    bsz = t // seq
    fm = pl.BlockSpec((HEAD_W, seq), lambda b, h: (b * ATT_HEADS + h, 0))
    tm = pl.BlockSpec((seq, HEAD_W), lambda b, h: (b, h))
    return pl.pallas_call(
        functools.partial(_attn_kernel, lam_init=lam_init),
        grid=(bsz, ATT_HEADS),
        in_specs=[pl.BlockSpec(memory_space=pltpu.SMEM), fm, tm, fm, _const_spec((1, HEAD_W))],
        out_specs=tm,
        out_shape=jax.ShapeDtypeStruct((t, ATT_V_WIDTH), BF16),
        compiler_params=_params("parallel", "parallel"),
        name="attn",
    )(scalars, qt, k, vt, subln_g)


def _ssm_kernel(u_ref, bt_ref, cre_ref, cim_ref, are_ref, aim_ref, d_ref, z_hbm, st_scr, state_scr, z_buf, z_sem):
    n_t, bsz, _ = u_ref.shape
    rows = n_t * bsz
    chunk = pl.program_id(0)

    def z_copies(i):
        t0 = pl.multiple_of(i * n_t, n_t)
        return [pltpu.make_async_copy(z_buf.at[:, b], z_hbm.at[b, pl.ds(t0, n_t)], z_sem) for b in range(bsz)]

    @pl.when(chunk == 0)
    def _():
        state_scr[...] = jnp.zeros(state_scr.shape, F32)

    u = u_ref[...].reshape(rows, SSM_WIDTH)
    ub = u.astype(BF16)
    n_tiles = 2 * N_STATE // MXU_N
    for j in range(n_tiles):
        k0 = LANES * (((j % (n_tiles // 2)) * MXU_N // SSM_STATE * SSM_GROUP) // LANES)
        st_scr[:, j * MXU_N:(j + 1) * MXU_N] = _dot(ub[:, k0:k0 + LANES], bt_ref[j])

    for cc in range(N_STATE // SCAN_LANES):
        lo = cc * SCAN_LANES
        hi = N_STATE + lo
        a_r = jnp.broadcast_to(are_ref[:, lo:lo + SCAN_LANES], (bsz, SCAN_LANES))
        a_i = jnp.broadcast_to(aim_ref[:, lo:lo + SCAN_LANES], (bsz, SCAN_LANES))

        def step(t, carry, lo=lo, hi=hi, a_r=a_r, a_i=a_i):
            s_r, s_i = carry
            r0 = pl.multiple_of(t * bsz, bsz)
            n_r = a_r * s_r - a_i * s_i + st_scr[pl.ds(r0, bsz), lo:lo + SCAN_LANES]
            n_i = a_r * s_i + a_i * s_r + st_scr[pl.ds(r0, bsz), hi:hi + SCAN_LANES]
            st_scr[pl.ds(r0, bsz), lo:lo + SCAN_LANES] = n_r
            st_scr[pl.ds(r0, bsz), hi:hi + SCAN_LANES] = n_i
            return n_r, n_i

        s_r, s_i = lax.fori_loop(0, rows // bsz, step,
                                 (state_scr[:, lo:lo + SCAN_LANES], state_scr[:, hi:hi + SCAN_LANES]),
                                 unroll=8)
        state_scr[:, lo:lo + SCAN_LANES] = s_r
        state_scr[:, hi:hi + SCAN_LANES] = s_i

    @pl.when(chunk > 0)
    def _():
        for cp in z_copies(chunk - 1):
            cp.wait()
    half = N_STATE // 2
    for n in range(2):
        s_re = st_scr[:, n * half:(n + 1) * half].astype(BF16)
        s_im = st_scr[:, N_STATE + n * half:N_STATE + (n + 1) * half].astype(BF16)
        cols = slice(n * MXU_N, (n + 1) * MXU_N)
        y = _dot(s_re, cre_ref[n]) - _dot(s_im, cim_ref[n]) + d_ref[:, cols] * u[:, cols]
        z_buf[:, :, cols] = jax.nn.gelu(y).reshape(n_t, bsz, MXU_N)
    for cp in z_copies(chunk):
        cp.start()

    @pl.when(chunk == pl.num_programs(0) - 1)
    def _():
        for cp in z_copies(chunk):
            cp.wait()


def _ssm(u_tm, b_tiles, c_re_tiles, c_im_tiles, a_re, a_im, d_skip):
    seq, bsz, _ = u_tm.shape
    return pl.pallas_call(
        _ssm_kernel,
        grid=(seq // SCAN_T,),
        in_specs=[pl.BlockSpec((SCAN_T, bsz, SSM_WIDTH), lambda i: (i, 0, 0)),
                  _const_spec(b_tiles.shape), _const_spec(c_re_tiles.shape),
                  _const_spec(c_im_tiles.shape), _const_spec((1, N_STATE)), _const_spec((1, N_STATE)),
                  _const_spec((1, SSM_WIDTH))],
        out_specs=pl.BlockSpec(memory_space=pl.ANY),
        out_shape=jax.ShapeDtypeStruct((bsz, seq, SSM_WIDTH), F32),
        scratch_shapes=[pltpu.VMEM((SCAN_T * bsz, 2 * N_STATE), F32), pltpu.VMEM((bsz, 2 * N_STATE), F32),
                        pltpu.VMEM((SCAN_T, bsz, SSM_WIDTH), F32), pltpu.SemaphoreType.DMA(())],
        compiler_params=_params("arbitrary"),
        name="ssm",
    )(u_tm, b_tiles, c_re_tiles, c_im_tiles, a_re, a_im, d_skip)


def _route(logits, ltri):
    lane = lax.broadcasted_iota(jnp.int32, logits.shape, 1)
    lanef = lane.astype(F32)
    big = float(ROUTE_W)
    neg = -jnp.inf
    gl = jnp.where(lane < N_GROUPS, logits, neg)
    gmax = jnp.max(gl, axis=-1, keepdims=True)
    grp = jnp.min(jnp.where(gl == gmax, lanef, big), axis=-1, keepdims=True)
    p_grp = 1.0 / jnp.sum(jnp.exp(gl - gmax), axis=-1, keepdims=True)
    first = ROUTE_OFF + grp * EXPERTS_PER_GROUP
    in_grp = (lanef >= first) & (lanef < first + EXPERTS_PER_GROUP)
    el = jnp.where(in_grp, logits, neg)
    v1 = jnp.max(el, axis=-1, keepdims=True)
    i1 = jnp.min(jnp.where(el == v1, lanef, big), axis=-1, keepdims=True)
    el2 = jnp.where(lanef == i1, neg, el)
    v2 = jnp.max(el2, axis=-1, keepdims=True)
    i2 = jnp.min(jnp.where(el2 == v2, lanef, big), axis=-1, keepdims=True)
    e21 = jnp.exp(v2 - v1)
    w1 = p_grp / (1.0 + e21)
    w2 = p_grp * (e21 / (1.0 + e21))
    member = lanef == grp
    before = _dot(ltri, jnp.where(member, 1.0, 0.0).astype(BF16))
    rank = jnp.sum(jnp.where(member, before, 0.0), axis=-1, keepdims=True)
    counts = jnp.sum(jnp.where(member, 1.0, 0.0), axis=0, keepdims=True)
    return (jnp.where(lanef == i1, w1, 0.0) + jnp.where(lanef == i2, w2, 0.0)
            + jnp.where(lane == GRP_LANE, grp, 0.0) + jnp.where(lane == RANK_LANE, rank, 0.0)), counts


def _mix_kernel(x_ref, o_ref, z_ref, gate_ref, wau_ref, wglu_ref, bglu_ref, wsu_ref, wout_ref,
                fg_ref, wrh_ref, wrl_ref, br_ref, ltri_ref, x1_ref, hn_ref, route_ref, route_t_ref, cnt_ref,
                wau_scr, wglu_scr, wsu_scr, wout_scr):
    _round_once([(wau_ref, wau_scr), (wglu_ref, wglu_scr), (wsu_ref, wsu_scr), (wout_ref, wout_scr)])
    y_att = _dot(o_ref[...], wau_scr[...])
    z = z_ref[...]
    z = z * jax.nn.sigmoid(_dot(z.astype(BF16), wglu_scr[...]) + bglu_ref[...])
    y_ssm = _dot(z.astype(BF16), wsu_scr[...])
    mixed = (gate_ref[:, :D_MODEL].astype(F32) * y_att + gate_ref[:, D_MODEL:].astype(F32) * y_ssm)
    x1 = x_ref[...] + _dot(mixed.astype(BF16), wout_scr[...])
    x1_ref[...] = x1
    ms = jnp.mean(x1 * x1, axis=-1, keepdims=True)
    hn = x1 * lax.rsqrt(ms + EPS) * fg_ref[...]
    hn_hi = hn.astype(BF16)
    hn_ref[...] = hn_hi
    hn_lo = (hn - hn_hi.astype(F32)).astype(BF16)
    logits = (_dot(hn_hi, wrh_ref[...]) + _dot(hn_lo, wrh_ref[...]) + _dot(hn_hi, wrl_ref[...])
              + br_ref[...])
    route, counts = _route(logits, ltri_ref[...])
    route_ref[...] = route
    route_t_ref[...] = route.T[:SUBLANES, :]
    cnt_ref[...] = jnp.broadcast_to(counts, cnt_ref.shape)


def _mix(x2d, o, z, gates, wau, wglu, bglu, wsu, wout, ffn_g, wr_hi, wr_lo, br, ltri, *, layer):
    t = x2d.shape[0]
    row = lambda w: pl.BlockSpec((TOK_TILE, w), lambda i: (i, 0))
    return pl.pallas_call(
        _mix_kernel,
        grid=(t // TOK_TILE,),
        in_specs=[row(D_MODEL), row(ATT_V_WIDTH), row(SSM_WIDTH), row(N_BRANCH * D_MODEL),
                  _layer_weight(wau, layer), _layer_weight(wglu, layer), _const_spec(bglu.shape),
                  _layer_weight(wsu, layer), _layer_weight(wout, layer), _const_spec(ffn_g.shape),
                  _const_spec(wr_hi.shape), _const_spec(wr_lo.shape), _const_spec(br.shape),
                  _const_spec(ltri.shape)],
        out_specs=[row(D_MODEL), row(D_MODEL), row(ROUTE_W),
                   pl.BlockSpec((SUBLANES, TOK_TILE), lambda i: (0, i)),
                   pl.BlockSpec((None, SUBLANES, ROUTE_W), lambda i: (i, 0, 0))],
        out_shape=[jax.ShapeDtypeStruct((t, D_MODEL), F32),
                   jax.ShapeDtypeStruct((t, D_MODEL), BF16),
                   jax.ShapeDtypeStruct((t, ROUTE_W), F32),
                   jax.ShapeDtypeStruct((SUBLANES, t), F32),
                   jax.ShapeDtypeStruct((t // TOK_TILE, SUBLANES, ROUTE_W), F32)],
        scratch_shapes=[pltpu.VMEM(w.shape[1:], BF16) for w in (wau, wglu, wsu, wout)],
        compiler_params=_params("arbitrary"),
        name="mix",
    )(x2d, o, z, gates, wau, wglu, bglu, wsu, wout, ffn_g, wr_hi, wr_lo, br, ltri)


def _moe_kernel(cnt_ref, hn_ref, route_ref, route_t_ref, wg_hbm, wu_hbm, wd_hbm, out_ref,
                wg_f32, wu_f32, wd_f32, w_sem, wg_scr, wu_scr, wd_scr, *, layer):
    n = pl.program_id(0)

    def fetch(g):
        pairs = ((wg_hbm, wg_f32), (wu_hbm, wu_f32), (wd_hbm, wd_f32))
        return [pltpu.make_async_copy(src.at[layer, g], dst, w_sem.at[k]) for k, (src, dst) in enumerate(pairs)]

    @pl.when(pl.program_id(1) == 0)
    def _():
        @pl.when(n == 0)
        def _():
            for cp in fetch(n):
                cp.start()
        for cp in fetch(n):
            cp.wait()
        for j in range(EXPERTS_PER_GROUP):
            cols = slice(j * EXPERT_FF, (j + 1) * EXPERT_FF)
            wg_scr[:, cols] = wg_f32[j].astype(BF16)
            wu_scr[:, cols] = wu_f32[j].astype(BF16)
            wd_scr[cols, :] = wd_f32[j].astype(BF16)
        @pl.when(n + 1 < N_GROUPS)
        def _():
            for cp in fetch(n + 1):
                cp.start()

    tm = hn_ref.shape[0]
    nf = n.astype(F32)
    route = route_ref[...]
    lane = lax.broadcasted_iota(jnp.int32, route.shape, 1)
    grp_c = jnp.sum(jnp.where(lane == GRP_LANE, route, 0.0), axis=-1, keepdims=True)
    rank_c = jnp.sum(jnp.where(lane == RANK_LANE, route, 0.0), axis=-1, keepdims=True)
    grp_r = route_t_ref[GRP_LANE:GRP_LANE + 1, :]
    rank_r = route_t_ref[RANK_LANE:RANK_LANE + 1, :]
    r1 = route.astype(BF16)
    r2 = (route - r1.astype(F32)).astype(BF16)
    r3 = (route - r1.astype(F32) - r2.astype(F32)).astype(BF16)
    pieces = jnp.concatenate([r1, r2, r3], axis=1)

    def group_rows(base, cap):
        slot_r = lax.broadcasted_iota(jnp.int32, (cap, tm), 0).astype(F32) + base
        take = jnp.where((grp_r == nf) & (rank_r == slot_r), 1.0, 0.0).astype(BF16)
        slot_c = lax.broadcasted_iota(jnp.int32, (tm, cap), 1).astype(F32) + base
        put = jnp.where((grp_c == nf) & (rank_c == slot_c), 1.0, 0.0).astype(BF16)
        xc = _dot(take, hn_ref[...]).astype(BF16)
        rc = _dot(take, pieces)
        rc = rc[:, :ROUTE_W] + rc[:, ROUTE_W:2 * ROUTE_W] + rc[:, 2 * ROUTE_W:]
        lane_c = lax.broadcasted_iota(jnp.int32, rc.shape, 1)
        first = ROUTE_OFF + n * EXPERTS_PER_GROUP
        act = jax.nn.silu(_dot(xc, wg_scr[...])) * _dot(xc, wu_scr[...])
        blocks = []
        for j in range(EXPERTS_PER_GROUP):
            w_j = jnp.sum(jnp.where(lane_c == first + j, rc, 0.0), axis=-1, keepdims=True)
            blocks.append((act[:, j * EXPERT_FF:(j + 1) * EXPERT_FF] * w_j).astype(BF16))
        y = _dot(jnp.concatenate(blocks, axis=1), wd_scr[...])
        return _dot(put, y.astype(BF16))

    out_ref[...] = group_rows(jnp.float32(0.0), MOE_CAP).astype(BF16)

    over = jnp.maximum(cnt_ref[pl.program_id(1), n] - MOE_CAP, 0)
    n_tail = lax.div(over + (MOE_TAIL - 1), MOE_TAIL)

    def tail_pass(k, carry):
        more = group_rows((MOE_CAP + k * MOE_TAIL).astype(F32), MOE_TAIL)
        out_ref[...] = (out_ref[...].astype(F32) + more).astype(BF16)
        return carry

    lax.fori_loop(0, n_tail, tail_pass, 0)


def _moe(counts, hn, route, route_t, w_gate, w_up, w_down, *, layer):
    t = hn.shape[0]
    row = lambda w: pl.BlockSpec((TOK_TILE, w), lambda n, i, cnt: (i, 0))
    up_shape = (EXPERTS_PER_GROUP, D_MODEL, EXPERT_FF)
    down_shape = (EXPERTS_PER_GROUP, EXPERT_FF, D_MODEL)
    ff = EXPERTS_PER_GROUP * EXPERT_FF
    return pl.pallas_call(
        functools.partial(_moe_kernel, layer=layer),
        grid_spec=pltpu.PrefetchScalarGridSpec(
            num_scalar_prefetch=1,
            grid=(N_GROUPS, t // TOK_TILE),
            in_specs=[row(D_MODEL), row(ROUTE_W), pl.BlockSpec((SUBLANES, TOK_TILE), lambda n, i, cnt: (0, i)),
                      pl.BlockSpec(memory_space=pl.ANY), pl.BlockSpec(memory_space=pl.ANY),
                      pl.BlockSpec(memory_space=pl.ANY)],
            out_specs=pl.BlockSpec((None, TOK_TILE, D_MODEL), lambda n, i, cnt: (n, i, 0)),
            scratch_shapes=[pltpu.VMEM(up_shape, F32), pltpu.VMEM(up_shape, F32), pltpu.VMEM(down_shape, F32),
                            pltpu.SemaphoreType.DMA((3,)),
                            pltpu.VMEM((D_MODEL, ff), BF16), pltpu.VMEM((D_MODEL, ff), BF16),
                            pltpu.VMEM((ff, D_MODEL), BF16)]),
        out_shape=jax.ShapeDtypeStruct((N_GROUPS, t, D_MODEL), BF16),
        compiler_params=_params("arbitrary", "arbitrary"),
        name="moe",
    )(counts, hn, route, route_t, w_gate, w_up, w_down)


def _ple_kernel(x1_ref, c_ref, p_ref, pg_ref, wpg_ref, wple_ref, out_ref, wpg_scr, wple_scr):
    _round_once([(wpg_ref, wpg_scr), (wple_ref, wple_scr)])
    x2 = x1_ref[...]
    for n in range(N_GROUPS):
        x2 = x2 + c_ref[n].astype(F32)
    ms = jnp.mean(x2 * x2, axis=-1, keepdims=True)
    hp = (x2 * lax.rsqrt(ms + EPS) * pg_ref[...]).astype(BF16)
    gate = jax.nn.sigmoid(_dot(hp, wpg_scr[...]))
    out_ref[...] = x2 + gate * _dot(p_ref[...].astype(BF16), wple_scr[...])


def _ple(x1, contrib, p, ple_g, wpg, wple, *, layer):
    t = x1.shape[0]
    n_sb = p.shape[2] // TOK_TILE
    row = lambda w: pl.BlockSpec((TOK_TILE, w), lambda i: (i, 0))
    return pl.pallas_call(
        _ple_kernel,
        grid=(t // TOK_TILE,),
        in_specs=[row(D_MODEL), pl.BlockSpec((N_GROUPS, TOK_TILE, D_MODEL), lambda i: (0, i, 0)),
                  pl.BlockSpec((None, None, TOK_TILE, PLE_DIM), lambda i: (layer, i // n_sb, i % n_sb, 0)),
                  _const_spec(ple_g.shape), _layer_weight(wpg, layer), _layer_weight(wple, layer)],
        out_specs=row(D_MODEL),
        out_shape=jax.ShapeDtypeStruct((t, D_MODEL), F32),
        scratch_shapes=[pltpu.VMEM(wpg.shape[1:], BF16), pltpu.VMEM(wple.shape[1:], BF16)],
        compiler_params=_params("arbitrary"),
        name="ple",
    )(x1, contrib, p, ple_g, wpg, wple)


def kernel(x, p, attn_norm_g, w_in, b_gate, q_norm_g, k_norm_g, lam_qk, subln_g, w_attn_up, ssm_lam_re, ssm_lam_im, ssm_log_step, ssm_b_re, ssm_b_im, ssm_c_re, ssm_c_im, ssm_d, w_glu, b_glu, w_ssm_up, w_out, ffn_norm_g, w_router_group, b_router_group, w_router_expert, b_router_expert, w_exp_gate, w_exp_up, w_exp_down, ple_norm_g, w_ple_gate, w_ple):
    bsz, seq, d = x.shape
    depth = w_in.shape[0]
    t = bsz * seq
    assert d == D_MODEL and seq % TOK_TILE == 0 and seq % ATT_TILE == 0
    assert seq % SCAN_T == 0 and bsz == SUBLANES
    tok = jnp.arange(TOK_TILE)
    ltri = (tok[None, :] < tok[:, None]).astype(BF16)

    a_re, a_im, b_tiles, c_re_tiles, c_im_tiles, scalars = _prep(
        ssm_lam_re, ssm_lam_im, ssm_log_step, ssm_b_re, ssm_b_im, ssm_c_re, ssm_c_im, lam_qk, q_norm_g, k_norm_g)
    x2d = x.reshape(t, d)
    for i in range(depth):
        lam_init = _lambda_init(i)
        tile_gain = lambda g: jnp.tile(g.reshape(HEAD_W, 1), (COL_CHUNK // HEAD_W, 1))
        qt, k, vt, u_tm, gates = _inproj(
            x2d, attn_norm_g[i].reshape(1, d), w_in,
            tile_gain(q_norm_g[i]), tile_gain(k_norm_g[i]), b_gate[i].reshape(1, -1), seq=seq, layer=i)

        o = _attention(scalars[i], qt, k, vt, subln_g[i].reshape(1, HEAD_W), seq=seq, lam_init=lam_init)

        z = _ssm(u_tm, b_tiles[i], c_re_tiles[i], c_im_tiles[i],
                 a_re[i, ::SSM_GROUP].reshape(1, N_STATE), a_im[i, ::SSM_GROUP].reshape(1, N_STATE),
                 ssm_d[i].reshape(1, SSM_WIDTH))

        w_r = jnp.concatenate(
            [w_router_group[i], jnp.transpose(w_router_expert[i], (1, 0, 2)).reshape(d, N_EXPERTS),
             jnp.zeros((d, ROUTE_W - N_GROUPS - N_EXPERTS), F32)], axis=1)
        w_r_hi = w_r.astype(BF16)
        w_r_lo = (w_r - w_r_hi.astype(F32)).astype(BF16)
        b_r = jnp.concatenate([b_router_group[i], b_router_expert[i].reshape(-1),
                               jnp.zeros((ROUTE_W - N_GROUPS - N_EXPERTS,), F32)]).reshape(1, ROUTE_W)
        x1, hn, route, route_t, counts = _mix(
            x2d, o, z.reshape(t, SSM_WIDTH), gates,
            w_attn_up, w_glu, b_glu[i].reshape(1, -1), w_ssm_up, w_out, ffn_norm_g[i].reshape(1, d),
            w_r_hi, w_r_lo, b_r, ltri, layer=i)

        contrib = _moe(counts[:, 0, :N_GROUPS].astype(jnp.int32), hn, route, route_t,
                       w_exp_gate, w_exp_up, w_exp_down, layer=i)
        x2d = _ple(x1, contrib, p, ple_norm_g[i].reshape(1, d), w_ple_gate, w_ple, layer=i)
    return x2d.reshape(bsz, seq, d)
```
